```python
import math
import jax, jax.numpy as jnp
from jax import lax
import numpy as np

D_MODEL = 1024
BATCH = 8
SEQ = 2048
DEPTH = 1
DEC_BATCH = 128
DEC_SEQ = 4
PAST_LEN = 16384
PAGE_SIZE = 128

D_MIX = D_MODEL
HEAD_DIM = 64
D_ATTN = D_MIX // 2
N_HEADS = D_ATTN // HEAD_DIM
N_KV_HEADS = 2
N_REP = N_HEADS // N_KV_HEADS
D_KV = N_KV_HEADS * HEAD_DIM
D_SSM = D_MIX - D_ATTN
SSM_GROUP = 16
N_SSM_GROUPS = D_SSM // SSM_GROUP
SSM_STATE = 64
WINDOW = 128
BLOCK = WINDOW
NUM_BUCKETS = 32
MAX_DISTANCE = 128
D_FF = 2816
D_IN = D_ATTN + 2 * D_KV + D_SSM
RMS_EPS = 1e-6
NEG_INF = -1e30
DT_MIN = 0.001
DT_MAX = 0.1

kernel_name = 'hymba_swa_sink_s5_macaron_step'


def rmsnorm(x, g):
    xf = x.astype(jnp.float32)
    r = lax.rsqrt(jnp.mean(xf * xf, axis=-1, keepdims=True) + RMS_EPS)
    return (xf * r).astype(x.dtype) * g


def macaron_half_ffn(x, g, wg, wu, wd):
    h = rmsnorm(x, g)
    return x + 0.5 * ((jax.nn.silu(h @ wg) * (h @ wu)) @ wd)


def t5_bucket(d):
    d = jnp.maximum(d, 0)
    max_exact = NUM_BUCKETS // 2
    df = jnp.maximum(d, 1).astype(jnp.float32)
    large = max_exact + (jnp.log(df / max_exact) / math.log(MAX_DISTANCE / max_exact)
                         * (NUM_BUCKETS - max_exact)).astype(jnp.int32)
    large = jnp.minimum(large, NUM_BUCKETS - 1)
    return jnp.where(d < max_exact, d, large)


def rel_bias_for(d, rel_bias):
    b = rel_bias[t5_bucket(d)].astype(jnp.float32)
    return jnp.transpose(b, (2, 0, 1)).reshape(N_KV_HEADS, N_REP, d.shape[0], d.shape[1])


def sink_probs(logits, sinks):
    s = sinks.astype(jnp.float32).reshape(N_KV_HEADS, N_REP)[:, :, None, None]
    m = jnp.maximum(jnp.max(logits, axis=-1, keepdims=True), s)
    e = jnp.exp(logits - m)
    return e / (jnp.sum(e, axis=-1, keepdims=True) + jnp.exp(s - m))


def split_projection(h, w_in):
    b, l = h.shape[:2]
    p = h @ w_in
    q = p[..., :D_ATTN].reshape(b, l, N_HEADS, HEAD_DIM)
    o = D_ATTN
    k = p[..., o:o + D_KV].reshape(b, l, N_KV_HEADS, HEAD_DIM)
    o += D_KV
    v = p[..., o:o + D_KV].reshape(b, l, N_KV_HEADS, HEAD_DIM)
    o += D_KV
    u = p[..., o:]
    return q, k, v, u


def swa_prompt(q, k, v, rel_bias, sinks):
    b, l = q.shape[:2]
    nb = l // BLOCK
    qb = q.reshape(b, nb, BLOCK, N_KV_HEADS, N_REP, HEAD_DIM)

    def band(t):
        cur = t.reshape(b, nb, BLOCK, N_KV_HEADS, HEAD_DIM)
        prev = jnp.concatenate([jnp.zeros_like(cur[:, :1]), cur[:, :-1]], axis=1)
        return jnp.concatenate([prev, cur], axis=2)

    kb, vb = band(k), band(v)
    logits = jnp.einsum('bnqgrd,bnkgd->bngrqk', qb, kb).astype(jnp.float32) * (HEAD_DIM ** -0.5)
    qi = jnp.arange(BLOCK)[:, None]
    kj = jnp.arange(2 * BLOCK)[None, :]
    d = qi - kj + BLOCK
    blk = jnp.arange(nb)[:, None, None]
    valid = (d >= 0) & (d < WINDOW) & ((blk > 0) | (kj >= BLOCK))
    logits = logits + rel_bias_for(d, rel_bias)
    logits = jnp.where(valid[:, None, None], logits, NEG_INF)
    p = sink_probs(logits, sinks)
    out = jnp.einsum('bngrqk,bnkgd->bnqgrd', p.astype(vb.dtype), vb)
    return out.reshape(b, l, D_ATTN)


def swa_sample(q, k_new, v_new, cache_k, cache_v, rel_bias, sinks):
    db, t = q.shape[:2]
    w = cache_k.shape[1]
    k_all = jnp.concatenate([cache_k.astype(k_new.dtype), k_new], axis=1)
    v_all = jnp.concatenate([cache_v.astype(v_new.dtype), v_new], axis=1)
    d = jnp.arange(t)[:, None] - jnp.arange(w + t)[None, :] + w
    valid = (d >= 0) & (d < WINDOW)
    qg = q.reshape(db, t, N_KV_HEADS, N_REP, HEAD_DIM)
    logits = jnp.einsum('bqgrd,bkgd->bgrqk', qg, k_all).astype(jnp.float32) * (HEAD_DIM ** -0.5)
    logits = logits + rel_bias_for(d, rel_bias)
    logits = jnp.where(valid, logits, NEG_INF)
    p = sink_probs(logits, sinks)
    out = jnp.einsum('bgrqk,bkgd->bqgrd', p.astype(v_all.dtype), v_all).reshape(db, t, D_ATTN)
    return out, k_all[:, t:], v_all[:, t:]


def s5_block(u, x0, log_dt, a_re, a_im, b_re, b_im, c_re, c_im, d_skip, w_glu, b_glu):
    f32 = jnp.float32
    bsz, l = u.shape[:2]
    uf = u.astype(f32)
    ug = uf.reshape(bsz, l, N_SSM_GROUPS, SSM_GROUP)
    lam = lax.complex(a_re.astype(f32), a_im.astype(f32))
    dt = jnp.exp(log_dt.astype(f32))[:, None]
    lam_bar = jnp.exp(lam * dt)
    b_mat = lax.complex(b_re.astype(f32), b_im.astype(f32))
    b_bar = ((lam_bar - 1.0) / lam)[..., None] * b_mat
    bu = jnp.einsum('blgc,gpc->blgp', ug.astype(jnp.complex64), b_bar)
    bu = bu.at[:, 0].add(lam_bar * x0)
    a = jnp.broadcast_to(lam_bar, bu.shape)

    def combine(e1, e2):
        a1, b1 = e1
        a2, b2 = e2
        return a1 * a2, a2 * b1 + b2

    _, xs = lax.associative_scan(combine, (a, bu), axis=1)
    c_mat = lax.complex(c_re.astype(f32), c_im.astype(f32))
    y = jnp.real(jnp.einsum('gcp,blgp->blgc', c_mat, xs)).reshape(bsz, l, D_SSM)
    y = y + d_skip.astype(f32) * uf
    y = jax.nn.gelu(y)
    y = y * jax.nn.sigmoid(y @ w_glu.astype(f32) + b_glu.astype(f32))
    x_last = xs[:, -1]
    return y.astype(u.dtype), jnp.real(x_last), jnp.imag(x_last)


def setup_inputs(seed: int = 0) -> dict:
    key = jax.random.key(seed)
    ks = iter(jax.random.split(key, 40))
    f32 = jnp.float32

    def nrm(shape, scale):
        return jax.random.normal(next(ks), shape, f32) * scale

    w_buf = min(WINDOW, PAST_LEN)
    L, G, P, C = DEPTH, N_SSM_GROUPS, SSM_STATE, SSM_GROUP
    inp = {}
    inp['x_prompt'] = nrm((BATCH, SEQ, D_MODEL), 1.0)
    inp['x_sample'] = nrm((DEC_BATCH, DEC_SEQ, D_MODEL), 1.0)
    inp['cache_k'] = nrm((L, DEC_BATCH, w_buf, N_KV_HEADS, HEAD_DIM), 1.0)
    inp['cache_v'] = nrm((L, DEC_BATCH, w_buf, N_KV_HEADS, HEAD_DIM), 1.0)
    inp['state_ssm_re'] = nrm((L, DEC_BATCH, G, P), 0.3)
    inp['state_ssm_im'] = nrm((L, DEC_BATCH, G, P), 0.3)
    inp['rel_bias'] = nrm((NUM_BUCKETS, N_HEADS), 0.5)
    inp['ffn1_norm'] = 1.0 + nrm((L, D_MODEL), 0.01)
    inp['ffn1_w_gate'] = nrm((L, D_MODEL, D_FF), D_MODEL ** -0.5)
    inp['ffn1_w_up'] = nrm((L, D_MODEL, D_FF), D_MODEL ** -0.5)
    inp['ffn1_w_down'] = nrm((L, D_FF, D_MODEL), D_FF ** -0.5)
    inp['mix_norm'] = 1.0 + nrm((L, D_MODEL), 0.01)
    inp['w_in'] = nrm((L, D_MODEL, D_IN), D_MODEL ** -0.5)
    inp['sinks'] = nrm((L, N_HEADS), 0.5)
    inp['log_dt'] = jax.random.uniform(next(ks), (L, G), f32, math.log(DT_MIN), math.log(DT_MAX))
    inp['a_re'] = -0.5 + nrm((L, G, P), 0.01)
    inp['a_im'] = math.pi * jnp.arange(P, dtype=f32) + nrm((L, G, P), 0.01)
    inp['b_re'] = nrm((L, G, P, C), (2.0 * C) ** -0.5)
    inp['b_im'] = nrm((L, G, P, C), (2.0 * C) ** -0.5)
    inp['c_re'] = nrm((L, G, C, P), (2.0 * P) ** -0.5)
    inp['c_im'] = nrm((L, G, C, P), (2.0 * P) ** -0.5)
    inp['d_skip'] = nrm((L, D_SSM), 0.5)
    inp['w_glu'] = nrm((L, D_SSM, D_SSM), D_SSM ** -0.5)
    inp['b_glu'] = nrm((L, D_SSM), 0.01)
    inp['w_out'] = nrm((L, D_MIX, D_MODEL), D_MIX ** -0.5)
    inp['ffn2_norm'] = 1.0 + nrm((L, D_MODEL), 0.01)
    inp['ffn2_w_gate'] = nrm((L, D_MODEL, D_FF), D_MODEL ** -0.5)
    inp['ffn2_w_up'] = nrm((L, D_MODEL, D_FF), D_MODEL ** -0.5)
    inp['ffn2_w_down'] = nrm((L, D_FF, D_MODEL), D_FF ** -0.5)
    inp['final_norm'] = 1.0 + nrm((D_MODEL,), 0.01)
    return inp


def reference(x_prompt, x_sample, cache_k, cache_v, state_ssm_re, state_ssm_im, rel_bias,
              ffn1_norm, ffn1_w_gate, ffn1_w_up, ffn1_w_down, mix_norm, w_in, sinks,
              log_dt, a_re, a_im, b_re, b_im, c_re, c_im, d_skip, w_glu, b_glu, w_out,
              ffn2_norm, ffn2_w_gate, ffn2_w_up, ffn2_w_down, final_norm):
    y_p, y_s = x_prompt, x_sample
    k_p_l, v_p_l, re_p_l, im_p_l = [], [], [], []
    k_s_l, v_s_l, re_s_l, im_s_l = [], [], [], []
    for i in range(DEPTH):
        ffn1 = (ffn1_norm[i], ffn1_w_gate[i], ffn1_w_up[i], ffn1_w_down[i])
        ffn2 = (ffn2_norm[i], ffn2_w_gate[i], ffn2_w_up[i], ffn2_w_down[i])
        ssm_w = (log_dt[i], a_re[i], a_im[i], b_re[i], b_im[i], c_re[i], c_im[i],
                 d_skip[i], w_glu[i], b_glu[i])

        y_p = macaron_half_ffn(y_p, *ffn1)
        q, k, v, u = split_projection(rmsnorm(y_p, mix_norm[i]), w_in[i])
        attn = swa_prompt(q, k, v, rel_bias, sinks[i])
        x0 = jnp.zeros((y_p.shape[0], N_SSM_GROUPS, SSM_STATE), jnp.complex64)
        ssm, s_re, s_im = s5_block(u, x0, *ssm_w)
        y_p = y_p + jnp.concatenate([attn, ssm], axis=-1) @ w_out[i]
        y_p = macaron_half_ffn(y_p, *ffn2)
        w_p = min(WINDOW, k.shape[1])
        k_p_l.append(k[:, k.shape[1] - w_p:])
        v_p_l.append(v[:, v.shape[1] - w_p:])
        re_p_l.append(s_re)
        im_p_l.append(s_im)

        y_s = macaron_half_ffn(y_s, *ffn1)
        q, k, v, u = split_projection(rmsnorm(y_s, mix_norm[i]), w_in[i])
        attn, k_buf, v_buf = swa_sample(q, k, v, cache_k[i], cache_v[i], rel_bias, sinks[i])
        x0 = lax.complex(state_ssm_re[i].astype(jnp.float32), state_ssm_im[i].astype(jnp.float32))
        ssm, s_re, s_im = s5_block(u, x0, *ssm_w)
        y_s = y_s + jnp.concatenate([attn, ssm], axis=-1) @ w_out[i]
        y_s = macaron_half_ffn(y_s, *ffn2)
        k_s_l.append(k_buf)
        v_s_l.append(v_buf)
        re_s_l.append(s_re)
        im_s_l.append(s_im)

    y_prompt = rmsnorm(y_p, final_norm)
    y_sample = rmsnorm(y_s, final_norm)
    return (y_prompt, y_sample,
            jnp.stack(k_p_l), jnp.stack(v_p_l), jnp.stack(re_p_l), jnp.stack(im_p_l),
            jnp.stack(k_s_l), jnp.stack(v_s_l), jnp.stack(re_s_l), jnp.stack(im_s_l))
```

```python
import functools
import math

import jax
import jax.numpy as jnp
from jax import lax
from jax.experimental import pallas as pl
from jax.experimental.pallas import tpu as pltpu

F32 = jnp.float32
BF16 = jnp.bfloat16

D_MODEL = 1024
HEAD_DIM = 64
D_ATTN = 512
N_HEADS = 8
N_KV_HEADS = 2
N_REP = 4
D_KV = 128
D_SSM = 512
SSM_GROUP = 16
N_SSM_GROUPS = 32
SSM_STATE = 64
WINDOW = 128
NUM_BUCKETS = 32
MAX_DISTANCE = 128
D_FF = 2816
D_IN = D_ATTN + 2 * D_KV + D_SSM
RMS_EPS = 1e-6
NEG_INF = -1e30

LANES = 128
SUBLANES = 8
MXU_DIM = 256
HALF_GROUPS = N_SSM_GROUPS // 2
HALF_STATE = HALF_GROUPS * SSM_STATE
HALF_U = HALF_GROUPS * SSM_GROUP
U_SLABS = D_SSM // LANES
PITCH = WINDOW + SUBLANES
VMEM_LIMIT = 60 * 1024 * 1024


def _const_spec(shape):
    nd = len(shape)
    return pl.BlockSpec(shape, lambda *_: (0,) * nd, pipeline_mode=pl.Buffered(1))


def _smem_spec():
    return pl.BlockSpec(memory_space=pltpu.SMEM)


def _rmsnorm(x, g):
    r = lax.rsqrt(jnp.mean(x * x, axis=-1, keepdims=True) + RMS_EPS)
    return (x * r) * g


def _dot(a, b):
    return jnp.dot(a, b, preferred_element_type=F32)


def _dot_nt(a, b):
    return lax.dot_general(a, b, (((1,), (1,)), ((), ())), preferred_element_type=F32)


def _ffn_kernel(x_ref, g_ref, wg_ref, wu_ref, wd_ref, fg_ref, o_ref, *, final_norm):
    x = x_ref[...]
    h = _rmsnorm(x, g_ref[...]).astype(BF16)
    acc = None
    for c in range(D_FF // MXU_DIM):
        sl = slice(c * MXU_DIM, (c + 1) * MXU_DIM)
        gate = _dot(h, wg_ref[:, sl])
        up = _dot(h, wu_ref[:, sl])
        a = (gate * jax.nn.sigmoid(gate) * up).astype(BF16)
        part = _dot(a, wd_ref[sl, :])
        acc = part if acc is None else acc + part
    y = x + 0.5 * acc
    if final_norm:
        y = _rmsnorm(y, fg_ref[...])
    o_ref[...] = y


def _ffn(x, g, wg, wu, wd, fg, *, final_norm, tm):
    n = x.shape[0]
    return pl.pallas_call(
        functools.partial(_ffn_kernel, final_norm=final_norm),
        grid=(n // tm,),
        in_specs=[
            pl.BlockSpec((tm, D_MODEL), lambda i: (i, 0)),
            _const_spec((1, D_MODEL)),
            _const_spec((D_MODEL, D_FF)),
            _const_spec((D_MODEL, D_FF)),
            _const_spec((D_FF, D_MODEL)),
            _const_spec((1, D_MODEL)),
        ],
        out_specs=pl.BlockSpec((tm, D_MODEL), lambda i: (i, 0)),
        out_shape=jax.ShapeDtypeStruct((n, D_MODEL), F32),
        compiler_params=pltpu.CompilerParams(
            dimension_semantics=("arbitrary",), vmem_limit_bytes=VMEM_LIMIT),
        name="ffn_final" if final_norm else "ffn",
    )(x, g, wg, wu, wd, fg)


def _t5_bucket(d):
    d = jnp.maximum(d, 0)
    max_exact = NUM_BUCKETS // 2
    df = jnp.maximum(d, 1).astype(F32)
    large = max_exact + (jnp.log(df / max_exact) / math.log(MAX_DISTANCE / max_exact)
                         * (NUM_BUCKETS - max_exact)).astype(jnp.int32)
    large = jnp.minimum(large, NUM_BUCKETS - 1)
    return jnp.where(d < max_exact, d, large)


def _masked_bias(relb_ref, head, d, valid):
    bucket = _t5_bucket(d)
    b = jnp.zeros(d.shape, F32)
    for k in range(NUM_BUCKETS):
        b = jnp.where(bucket == k, relb_ref[k, head], b)
    return jnp.where(valid, b, NEG_INF)


def _kv_lane_mask(t, g):
    lane = lax.broadcasted_iota(jnp.int32, t.shape, 1)
    return jnp.where((lane >= g * HEAD_DIM) & (lane < (g + 1) * HEAD_DIM), t, jnp.zeros_like(t))


def _stack_queries(q_chunks, g):
    qs = []
    for r in range(N_REP):
        qc = q_chunks[2 * g + r // 2]
        if r % 2 != g:
            qc = pltpu.roll(qc, HEAD_DIM, 1)
        qs.append(qc)
    return (jnp.concatenate(qs, axis=0) * (HEAD_DIM ** -0.5)).astype(BF16)


def _sink_column(sinks_ref, g, rows):
    return jnp.concatenate(
        [jnp.full((rows, 1), sinks_ref[N_REP * g + r], F32) for r in range(N_REP)], axis=0)


def _unstack_heads(o_by_g, rows):
    lane = lax.broadcasted_iota(jnp.int32, (rows, LANES), 1)
    chunks = []
    for c in range(D_ATTN // LANES):
        g = c // 2
        halves = []
        for half in range(2):
            r = 2 * (c % 2) + half
            piece = o_by_g[g][r * rows:(r + 1) * rows]
            if half != g:
                piece = pltpu.roll(piece, HEAD_DIM, 1)
            halves.append(piece)
        chunks.append(jnp.where(lane < HEAD_DIM, halves[0], halves[1]))
    return chunks


def _glu_tail(y, u, dskip_ref, wglu_ref, bglu_ref):
    y = y + dskip_ref[...] * u
    y = jax.nn.gelu(y)
    z = _dot(y.astype(BF16), wglu_ref[...]) + bglu_ref[...]
    return y * jax.nn.sigmoid(z)


def _mixp_kernel(relb_ref, sinks_ref, x_ref, g_ref, win_ref, wout_ref, lam_ref, bd_ref, cd_ref,
                 dskip_ref, wglu_ref, bglu_ref,
                 o_ref, kp_ref, vp_ref, sre_ref, sim_ref,
                 p_s, kband, vband, bias_s, us, uperm, xs, yperm, state, attn_s,
                 *, nb, blk):
    i = pl.program_id(0)
    rows = nb * blk

    @pl.when(i == 0)
    def _():
        kband[...] = jnp.zeros(kband.shape, BF16)
        vband[...] = jnp.zeros(vband.shape, BF16)
        state[...] = jnp.zeros(state.shape, F32)

    @pl.when(i <= 1)
    def _():
        qi = lax.broadcasted_iota(jnp.int32, (blk, 2 * blk), 0)
        kj = lax.broadcasted_iota(jnp.int32, (blk, 2 * blk), 1)
        d = qi - kj + blk
        first_key = jnp.where(i > 0, 0, blk)
        valid = (d >= 0) & (d < WINDOW) & (kj >= first_key)
        for h in range(N_HEADS):
            r = h % N_REP
            bias_s[h // N_REP, r * blk:(r + 1) * blk, :] = _masked_bias(relb_ref, h, d, valid)

    x = x_ref[...].reshape(rows, D_MODEL)
    h = _rmsnorm(x, g_ref[...]).astype(BF16)
    p_s[...] = _dot(h, win_ref[...])

    k3 = p_s[:, D_ATTN:D_ATTN + D_KV].reshape(nb, blk, D_KV)
    v3 = p_s[:, D_ATTN + D_KV:D_ATTN + 2 * D_KV].reshape(nb, blk, D_KV)
    kp_ref[...] = k3
    vp_ref[...] = v3
    kband[:, blk:2 * blk, :] = k3.astype(BF16)
    vband[:, blk:2 * blk, :] = v3.astype(BF16)

    def attn_body(b, carry):
        row0 = pl.multiple_of(b * blk, blk)
        kb = kband[b]
        vb = vband[b]
        q_chunks = [p_s[pl.ds(row0, blk), c * LANES:(c + 1) * LANES] for c in range(D_ATTN // LANES)]
        o_by_g = []
        for g in range(N_KV_HEADS):
            q = _stack_queries(q_chunks, g)
            logits = _dot_nt(q, _kv_lane_mask(kb, g)) + bias_s[g]
            s = _sink_column(sinks_ref, g, blk)
            m = jnp.maximum(jnp.max(logits, axis=-1, keepdims=True), s)
            e = jnp.exp(logits - m)
            denom = jnp.sum(e, axis=-1, keepdims=True) + jnp.exp(s - m)
            o_by_g.append(_dot(e.astype(BF16), vb) / denom)
        for c, chunk in enumerate(_unstack_heads(o_by_g, blk)):
            attn_s[pl.ds(row0, blk), c * LANES:(c + 1) * LANES] = chunk
        kband[b, 0:blk, :] = kb[blk:2 * blk]
        vband[b, 0:blk, :] = vb[blk:2 * blk]
        return carry

    lax.fori_loop(0, nb, attn_body, 0)

    u_off = D_ATTN + 2 * D_KV
    for s in range(U_SLABS):
        for b in range(nb):
            us[s, b * PITCH:b * PITCH + blk, :] = (
                p_s[b * blk:(b + 1) * blk, u_off + s * LANES:u_off + (s + 1) * LANES])

    def perm_body(t, carry):
        r0 = pl.multiple_of(t * nb, nb)
        for s in range(U_SLABS):
            uperm[pl.ds(r0, nb), s * LANES:(s + 1) * LANES] = us[s, pl.ds(t, nb, stride=PITCH), :]
        return carry

    lax.fori_loop(0, blk, perm_body, 0, unroll=8)

    for hf in range(2):
        xs[...] = _dot(uperm[:, hf * HALF_U:(hf + 1) * HALF_U].astype(BF16), bd_ref[hf])
        lr = jnp.broadcast_to(lam_ref[2 * hf:2 * hf + 1, :], (nb, HALF_STATE))
        li = jnp.broadcast_to(lam_ref[2 * hf + 1:2 * hf + 2, :], (nb, HALF_STATE))
        base = 2 * hf * HALF_STATE
        xr0 = state[:, base:base + HALF_STATE]
        xi0 = state[:, base + HALF_STATE:base + 2 * HALF_STATE]

        def scan_body(t, carry):
            xr, xi = carry
            r0 = pl.multiple_of(t * nb, nb)
            br = xs[pl.ds(r0, nb), 0:HALF_STATE]
            bi = xs[pl.ds(r0, nb), HALF_STATE:2 * HALF_STATE]
            nr = lr * xr - li * xi + br
            ni = lr * xi + li * xr + bi
            xs[pl.ds(r0, nb), 0:HALF_STATE] = nr
            xs[pl.ds(r0, nb), HALF_STATE:2 * HALF_STATE] = ni
            return nr, ni

        xr, xi = lax.fori_loop(0, blk, scan_body, (xr0, xi0), unroll=2)
        state[:, base:base + HALF_STATE] = xr
        state[:, base + HALF_STATE:base + 2 * HALF_STATE] = xi
        sre_ref[:, hf * HALF_STATE:(hf + 1) * HALF_STATE] = xr
        sim_ref[:, hf * HALF_STATE:(hf + 1) * HALF_STATE] = xi
        yperm[:, hf * HALF_U:(hf + 1) * HALF_U] = _dot(xs[...].astype(BF16), cd_ref[hf])

    def unperm_body(t, carry):
        r0 = pl.multiple_of(t * nb, nb)
        for s in range(U_SLABS):
            us[s, pl.ds(t, nb, stride=PITCH), :] = yperm[pl.ds(r0, nb), s * LANES:(s + 1) * LANES]
        return carry

    lax.fori_loop(0, blk, unperm_body, 0, unroll=8)

    y = jnp.concatenate(
        [jnp.concatenate([us[s, b * PITCH:b * PITCH + blk, :] for b in range(nb)], axis=0)
         for s in range(U_SLABS)], axis=1)
    ssm = _glu_tail(y, p_s[:, u_off:u_off + D_SSM], dskip_ref, wglu_ref, bglu_ref)

    mix = (_dot(attn_s[...].astype(BF16), wout_ref[0:D_ATTN, :])
           + _dot(ssm.astype(BF16), wout_ref[D_ATTN:D_ATTN + D_SSM, :]))
    o_ref[...] = (x + mix).reshape(nb, blk, D_MODEL)


def _mix_prompt(x, relb, sinks, g, win, wout, lam, bd, cd, dskip, wglu, bglu):
    nb, seq, _ = x.shape
    blk = WINDOW
    rows = nb * blk
    const2 = lambda shape: _const_spec(shape)
    return pl.pallas_call(
        functools.partial(_mixp_kernel, nb=nb, blk=blk),
        grid=(seq // blk,),
        in_specs=[
            _smem_spec(), _smem_spec(),
            pl.BlockSpec((nb, blk, D_MODEL), lambda i: (0, i, 0)),
            const2((1, D_MODEL)),
            const2((D_MODEL, D_IN)),
            const2((D_ATTN + D_SSM, D_MODEL)),
            const2((4, HALF_STATE)),
            const2((2, HALF_U, 2 * HALF_STATE)),
            const2((2, 2 * HALF_STATE, HALF_U)),
            const2((1, D_SSM)),
            const2((D_SSM, D_SSM)),
            const2((1, D_SSM)),
        ],
        out_specs=[
            pl.BlockSpec((nb, blk, D_MODEL), lambda i: (0, i, 0)),
            pl.BlockSpec((nb, blk, D_KV), lambda i: (0, 0, 0)),
            pl.BlockSpec((nb, blk, D_KV), lambda i: (0, 0, 0)),
            pl.BlockSpec((nb, 2 * HALF_STATE), lambda i: (0, 0)),
            pl.BlockSpec((nb, 2 * HALF_STATE), lambda i: (0, 0)),
        ],
        out_shape=[
            jax.ShapeDtypeStruct((nb, seq, D_MODEL), F32),
            jax.ShapeDtypeStruct((nb, blk, D_KV), F32),
            jax.ShapeDtypeStruct((nb, blk, D_KV), F32),
            jax.ShapeDtypeStruct((nb, 2 * HALF_STATE), F32),
            jax.ShapeDtypeStruct((nb, 2 * HALF_STATE), F32),
        ],
        scratch_shapes=[
            pltpu.VMEM((rows, D_IN), F32),
            pltpu.VMEM((nb, 2 * blk, D_KV), BF16),
            pltpu.VMEM((nb, 2 * blk, D_KV), BF16),
            pltpu.VMEM((N_KV_HEADS, N_REP * blk, 2 * blk), F32),
            pltpu.VMEM((U_SLABS, nb * PITCH, LANES), F32),
            pltpu.VMEM((rows, D_SSM), F32),
            pltpu.VMEM((rows, 2 * HALF_STATE), F32),
            pltpu.VMEM((rows, D_SSM), F32),
            pltpu.VMEM((nb, 4 * HALF_STATE), F32),
            pltpu.VMEM((rows, D_ATTN), F32),
        ],
        compiler_params=pltpu.CompilerParams(
            dimension_semantics=("arbitrary",), vmem_limit_bytes=VMEM_LIMIT),
        name="mix_prompt",
    )(relb, sinks, x, g, win, wout, lam, bd, cd, dskip, wglu, bglu)


def _mixs_kernel(relb_ref, sinks_ref, x_ref, g_ref, win_ref, wout_ref, ck_ref, cv_ref,
                 sre0_ref, sim0_ref, lam_ref, bd_ref, cd_ref, dskip_ref, wglu_ref, bglu_ref,
                 o_ref, ks_ref, vs_ref, sre_ref, sim_ref,
                 p_s, attn_s, ssm_s, bias_c, bias_n, us,
                 *, nseq, t_new, sb, wbuf):
    i = pl.program_id(0)
    nsteps = pl.num_programs(0)
    rows = nseq * t_new
    brow = sb * t_new
    u_off = D_ATTN + 2 * D_KV
    tshift = t_new.bit_length() - 1
    sshift = sb.bit_length() - 1
    wshift = wbuf.bit_length() - 1

    @pl.when(i == 0)
    def _():
        x = x_ref[...]
        h = _rmsnorm(x, g_ref[...]).astype(BF16)
        p_s[...] = _dot(h, win_ref[...])

        for s in range(U_SLABS):
            us[s] = p_s[:, u_off + s * LANES:u_off + (s + 1) * LANES]
        uperm = jnp.concatenate(
            [jnp.concatenate([us[s, pl.ds(t, nseq, stride=t_new), :] for t in range(t_new)], axis=0)
             for s in range(U_SLABS)], axis=1)
        yparts = []
        for hf in range(2):
            bu = _dot(uperm[:, hf * HALF_U:(hf + 1) * HALF_U].astype(BF16), bd_ref[hf])
            lr = lam_ref[2 * hf:2 * hf + 1, :]
            li = lam_ref[2 * hf + 1:2 * hf + 2, :]
            xr = sre0_ref[:, hf * HALF_STATE:(hf + 1) * HALF_STATE]
            xi = sim0_ref[:, hf * HALF_STATE:(hf + 1) * HALF_STATE]
            states = []
            for t in range(t_new):
                br = bu[t * nseq:(t + 1) * nseq, 0:HALF_STATE]
                bi = bu[t * nseq:(t + 1) * nseq, HALF_STATE:2 * HALF_STATE]
                xr, xi = lr * xr - li * xi + br, lr * xi + li * xr + bi
                states.append(jnp.concatenate([xr, xi], axis=1).astype(BF16))
            sre_ref[:, hf * HALF_STATE:(hf + 1) * HALF_STATE] = xr
            sim_ref[:, hf * HALF_STATE:(hf + 1) * HALF_STATE] = xi
            yparts.append(_dot(jnp.concatenate(states, axis=0), cd_ref[hf]))
        yperm = jnp.concatenate(yparts, axis=1)
        for s in range(U_SLABS):
            for t in range(t_new):
                us[s, pl.ds(t, nseq, stride=t_new), :] = (
                    yperm[t * nseq:(t + 1) * nseq, s * LANES:(s + 1) * LANES])
        y = jnp.concatenate([us[s] for s in range(U_SLABS)], axis=1)
        ssm_s[...] = _glu_tail(y, p_s[:, u_off:u_off + D_SSM], dskip_ref, wglu_ref, bglu_ref)

        ncol = sb * wbuf
        rho = lax.broadcasted_iota(jnp.int32, (brow, ncol), 0)
        kap = lax.broadcasted_iota(jnp.int32, (brow, ncol), 1)
        tq = rho & (t_new - 1)
        bq = rho >> tshift
        bk = kap >> wshift
        jk = kap & (wbuf - 1)
        d_c = tq - jk + wbuf
        valid_c = (bq == bk) & (d_c >= 0) & (d_c < WINDOW)
        rho_n = lax.broadcasted_iota(jnp.int32, (brow, LANES), 0)
        kap_n = lax.broadcasted_iota(jnp.int32, (brow, LANES), 1)
        tq_n = rho_n & (t_new - 1)
        d_n = tq_n - (kap_n & (t_new - 1))
        valid_n = ((rho_n >> tshift) == (kap_n >> tshift)) & (d_n >= 0) & (kap_n < brow)
        for hd in range(N_HEADS):
            g, r = hd // N_REP, hd % N_REP
            bias_c[g, r * brow:(r + 1) * brow, :] = _masked_bias(relb_ref, hd, d_c, valid_c)
            bias_n[g, r * brow:(r + 1) * brow, :] = _masked_bias(relb_ref, hd, d_n, valid_n)

    row0 = pl.multiple_of(i * brow, brow)
    ck = ck_ref[...]
    cv = cv_ref[...]
    kn = p_s[pl.ds(row0, brow), D_ATTN:D_ATTN + D_KV]
    vn = p_s[pl.ds(row0, brow), D_ATTN + D_KV:D_ATTN + 2 * D_KV]
    pad = jnp.zeros((LANES - brow, D_KV), F32)
    kn_pad = jnp.concatenate([kn, pad], axis=0).astype(BF16)
    vn_pad = jnp.concatenate([vn, pad], axis=0).astype(BF16)
    kc = ck.reshape(sb * wbuf, D_KV).astype(BF16)
    vc = cv.reshape(sb * wbuf, D_KV).astype(BF16)
    q_chunks = [p_s[pl.ds(row0, brow), c * LANES:(c + 1) * LANES] for c in range(D_ATTN // LANES)]
    o_by_g = []
    for g in range(N_KV_HEADS):
        q = _stack_queries(q_chunks, g)
        lc = _dot_nt(q, _kv_lane_mask(kc, g)) + bias_c[g]
        ln = _dot_nt(q, _kv_lane_mask(kn_pad, g)) + bias_n[g]
        s = _sink_column(sinks_ref, g, brow)
        m = jnp.maximum(jnp.maximum(jnp.max(lc, axis=-1, keepdims=True),
                                    jnp.max(ln, axis=-1, keepdims=True)), s)
        ec = jnp.exp(lc - m)
        en = jnp.exp(ln - m)
        denom = (jnp.sum(ec, axis=-1, keepdims=True) + jnp.sum(en, axis=-1, keepdims=True)
                 + jnp.exp(s - m))
        o = _dot(ec.astype(BF16), vc) + _dot(en.astype(BF16), vn_pad)
        o_by_g.append(o / denom)
    for c, chunk in enumerate(_unstack_heads(o_by_g, brow)):
        attn_s[pl.ds(row0, brow), c * LANES:(c + 1) * LANES] = chunk

    ks_ref[:, 0:wbuf - t_new, :] = ck[:, t_new:wbuf, :]
    vs_ref[:, 0:wbuf - t_new, :] = cv[:, t_new:wbuf, :]
    ks_ref[:, wbuf - t_new:wbuf, :] = kn.reshape(sb, t_new, D_KV)
    vs_ref[:, wbuf - t_new:wbuf, :] = vn.reshape(sb, t_new, D_KV)

    @pl.when(i == nsteps - 1)
    def _():
        mix = (_dot(attn_s[...].astype(BF16), wout_ref[0:D_ATTN, :])
               + _dot(ssm_s[...].astype(BF16), wout_ref[D_ATTN:D_ATTN + D_SSM, :]))
        o_ref[...] = x_ref[...] + mix


def _mix_sample(x, relb, sinks, g, win, wout, ck, cv, sre0, sim0, lam, bd, cd, dskip, wglu, bglu,
                *, nseq, t_new):
    rows = nseq * t_new
    wbuf = ck.shape[1]
    sb = SUBLANES
    brow = sb * t_new
    return pl.pallas_call(
        functools.partial(_mixs_kernel, nseq=nseq, t_new=t_new, sb=sb, wbuf=wbuf),
        grid=(nseq // sb,),
        in_specs=[
            _smem_spec(), _smem_spec(),
            _const_spec((rows, D_MODEL)),
            _const_spec((1, D_MODEL)),
            _const_spec((D_MODEL, D_IN)),
            _const_spec((D_ATTN + D_SSM, D_MODEL)),
            pl.BlockSpec((sb, wbuf, D_KV), lambda i: (i, 0, 0)),
            pl.BlockSpec((sb, wbuf, D_KV), lambda i: (i, 0, 0)),
            _const_spec((nseq, 2 * HALF_STATE)),
            _const_spec((nseq, 2 * HALF_STATE)),
            _const_spec((4, HALF_STATE)),
            _const_spec((2, HALF_U, 2 * HALF_STATE)),
            _const_spec((2, 2 * HALF_STATE, HALF_U)),
            _const_spec((1, D_SSM)),
            _const_spec((D_SSM, D_SSM)),
            _const_spec((1, D_SSM)),
        ],
        out_specs=[
            pl.BlockSpec((rows, D_MODEL), lambda i: (0, 0)),
            pl.BlockSpec((sb, wbuf, D_KV), lambda i: (i, 0, 0)),
            pl.BlockSpec((sb, wbuf, D_KV), lambda i: (i, 0, 0)),
            pl.BlockSpec((nseq, 2 * HALF_STATE), lambda i: (0, 0)),
            pl.BlockSpec((nseq, 2 * HALF_STATE), lambda i: (0, 0)),
        ],
        out_shape=[
            jax.ShapeDtypeStruct((rows, D_MODEL), F32),
            jax.ShapeDtypeStruct((nseq, wbuf, D_KV), F32),
            jax.ShapeDtypeStruct((nseq, wbuf, D_KV), F32),
            jax.ShapeDtypeStruct((nseq, 2 * HALF_STATE), F32),
            jax.ShapeDtypeStruct((nseq, 2 * HALF_STATE), F32),
        ],
        scratch_shapes=[
            pltpu.VMEM((rows, D_IN), F32),
            pltpu.VMEM((rows, D_ATTN), F32),
            pltpu.VMEM((rows, D_SSM), F32),
            pltpu.VMEM((N_KV_HEADS, N_REP * brow, sb * wbuf), F32),
            pltpu.VMEM((N_KV_HEADS, N_REP * brow, LANES), F32),
            pltpu.VMEM((U_SLABS, rows, LANES), F32),
        ],
        compiler_params=pltpu.CompilerParams(
            dimension_semantics=("arbitrary",), vmem_limit_bytes=VMEM_LIMIT),
        name="mix_sample",
    )(relb, sinks, x, g, win, wout, ck, cv, sre0, sim0, lam, bd, cd, dskip, wglu, bglu)


def _s5_operators(log_dt, a_re, a_im, b_re, b_im, c_re, c_im):
    dt = jnp.exp(log_dt)[:, None]
    mag = jnp.exp(a_re * dt)
    lb_re = mag * jnp.cos(a_im * dt)
    lb_im = mag * jnp.sin(a_im * dt)
    den = a_re * a_re + a_im * a_im
    nr = lb_re - 1.0
    q_re = (nr * a_re + lb_im * a_im) / den
    q_im = (lb_im * a_re - nr * a_im) / den
    bb_re = q_re[..., None] * b_re - q_im[..., None] * b_im
    bb_im = q_re[..., None] * b_im + q_im[..., None] * b_re
    eye = jnp.eye(HALF_GROUPS, dtype=F32)
    lam, bd, cd = [], [], []
    for hf in range(2):
        gs = slice(hf * HALF_GROUPS, (hf + 1) * HALF_GROUPS)
        lam += [lb_re[gs].reshape(1, HALF_STATE), lb_im[gs].reshape(1, HALF_STATE)]
        m_re = jnp.einsum('gpc,gk->gckp', bb_re[gs], eye).reshape(HALF_U, HALF_STATE)
        m_im = jnp.einsum('gpc,gk->gckp', bb_im[gs], eye).reshape(HALF_U, HALF_STATE)
        bd.append(jnp.concatenate([m_re, m_im], axis=1))
        n_re = jnp.einsum('gcp,gk->gpkc', c_re[gs], eye).reshape(HALF_STATE, HALF_U)
        n_im = jnp.einsum('gcp,gk->gpkc', c_im[gs], eye).reshape(HALF_STATE, HALF_U)
        cd.append(jnp.concatenate([n_re, -n_im], axis=0))
    return (jnp.concatenate(lam, axis=0), jnp.stack(bd).astype(BF16), jnp.stack(cd).astype(BF16))


def kernel(x_prompt, x_sample, cache_k, cache_v, state_ssm_re, state_ssm_im, rel_bias,
           ffn1_norm, ffn1_w_gate, ffn1_w_up, ffn1_w_down, mix_norm, w_in, sinks,
           log_dt, a_re, a_im, b_re, b_im, c_re, c_im, d_skip, w_glu, b_glu, w_out,
           ffn2_norm, ffn2_w_gate, ffn2_w_up, ffn2_w_down, final_norm):
    depth = w_in.shape[0]
    assert depth == 1
    batch, seq, _ = x_prompt.shape
    nseq, t_new, _ = x_sample.shape
    wbuf = cache_k.shape[2]
    fg = final_norm.reshape(1, D_MODEL)

    l = 0
    ffn1 = (ffn1_norm[l].reshape(1, D_MODEL), ffn1_w_gate[l].astype(BF16),
            ffn1_w_up[l].astype(BF16), ffn1_w_down[l].astype(BF16), fg)
    ffn2 = (ffn2_norm[l].reshape(1, D_MODEL), ffn2_w_gate[l].astype(BF16),
            ffn2_w_up[l].astype(BF16), ffn2_w_down[l].astype(BF16), fg)
    lam, bd, cd = _s5_operators(log_dt[l], a_re[l], a_im[l], b_re[l], b_im[l], c_re[l], c_im[l])
    mix_w = (mix_norm[l].reshape(1, D_MODEL), w_in[l].astype(BF16), w_out[l].astype(BF16))
    ssm_w = (lam, bd, cd, d_skip[l].reshape(1, D_SSM), w_glu[l].astype(BF16),
             b_glu[l].reshape(1, D_SSM))
    sinks_l = sinks[l]

    xp = x_prompt.reshape(batch * seq, D_MODEL)
    y = _ffn(xp, *ffn1, final_norm=False, tm=512)
    y, kp, vp, sre_p, sim_p = _mix_prompt(
        y.reshape(batch, seq, D_MODEL), rel_bias, sinks_l, *mix_w, *ssm_w)
    y_prompt = _ffn(y.reshape(batch * seq, D_MODEL), *ffn2, final_norm=True, tm=512)
    y_prompt = y_prompt.reshape(batch, seq, D_MODEL)

    xs = x_sample.reshape(nseq * t_new, D_MODEL)
    ys = _ffn(xs, *ffn1, final_norm=False, tm=nseq * t_new)
    ys, ks, vs, sre_s, sim_s = _mix_sample(
        ys, rel_bias, sinks_l, *mix_w,
        cache_k[l].reshape(nseq, wbuf, D_KV), cache_v[l].reshape(nseq, wbuf, D_KV),
        state_ssm_re[l].reshape(nseq, 2 * HALF_STATE), state_ssm_im[l].reshape(nseq, 2 * HALF_STATE),
        *ssm_w, nseq=nseq, t_new=t_new)
    y_sample = _ffn(ys, *ffn2, final_norm=True, tm=nseq * t_new).reshape(nseq, t_new, D_MODEL)

    kv_p = (1, batch, WINDOW, N_KV_HEADS, HEAD_DIM)
    st_p = (1, batch, N_SSM_GROUPS, SSM_STATE)
    kv_s = (1, nseq, wbuf, N_KV_HEADS, HEAD_DIM)
    st_s = (1, nseq, N_SSM_GROUPS, SSM_STATE)
    return (y_prompt, y_sample,
            kp.reshape(kv_p), vp.reshape(kv_p), sre_p.reshape(st_p), sim_p.reshape(st_p),
            ks.reshape(kv_s), vs.reshape(kv_s), sre_s.reshape(st_s), sim_s.reshape(st_s))
```

```python
import functools
import math

import jax
import jax.numpy as jnp
from jax import lax
from jax.experimental import pallas as pl
from jax.experimental.pallas import tpu as pltpu

F32 = jnp.float32
BF16 = jnp.bfloat16

D_MODEL = 1024
HEAD_DIM = 64
D_ATTN = 512
N_HEADS = 8
N_KV_HEADS = 2
N_REP = 4
D_KV = 128
D_SSM = 512
SSM_GROUP = 16
N_SSM_GROUPS = 32
SSM_STATE = 64
WINDOW = 128
NUM_BUCKETS = 32
MAX_DISTANCE = 128
D_FF = 2816
D_IN = D_ATTN + 2 * D_KV + D_SSM
RMS_EPS = 1e-6
NEG_INF = -1e30

LANES = 128
SUBLANES = 8
MXU_DIM = 256
HALF_GROUPS = N_SSM_GROUPS // 2
HALF_STATE = HALF_GROUPS * SSM_STATE
HALF_U = HALF_GROUPS * SSM_GROUP
U_SLABS = D_SSM // LANES
PITCH = WINDOW + SUBLANES
VMEM_LIMIT = 60 * 1024 * 1024


def _const_spec(shape):
    nd = len(shape)
    return pl.BlockSpec(shape, lambda *_: (0,) * nd, pipeline_mode=pl.Buffered(1))


def _smem_spec():
    return pl.BlockSpec(memory_space=pltpu.SMEM)


def _rmsnorm(x, g):
    r = lax.rsqrt(jnp.mean(x * x, axis=-1, keepdims=True) + RMS_EPS)
    return (x * r) * g


def _dot(a, b):
    return jnp.dot(a, b, preferred_element_type=F32)


def _dot_nt(a, b):
    return lax.dot_general(a, b, (((1,), (1,)), ((), ())), preferred_element_type=F32)


def _ffn_kernel(x_ref, g_ref, wg_ref, wu_ref, wd_ref, fg_ref, o_ref, *, final_norm):
    x = x_ref[...]
    h = _rmsnorm(x, g_ref[...]).astype(BF16)
    acc = None
    for c in range(D_FF // MXU_DIM):
        sl = slice(c * MXU_DIM, (c + 1) * MXU_DIM)
        gate = _dot(h, wg_ref[:, sl])
        up = _dot(h, wu_ref[:, sl])
        a = (gate * jax.nn.sigmoid(gate) * up).astype(BF16)
        part = _dot(a, wd_ref[sl, :])
        acc = part if acc is None else acc + part
    y = x + 0.5 * acc
    if final_norm:
        y = _rmsnorm(y, fg_ref[...])
    o_ref[...] = y


def _ffn(x, g, wg, wu, wd, fg, *, final_norm, tm):
    n = x.shape[0]
    return pl.pallas_call(
        functools.partial(_ffn_kernel, final_norm=final_norm),
        grid=(n // tm,),
        in_specs=[
            pl.BlockSpec((tm, D_MODEL), lambda i: (i, 0)),
            _const_spec((1, D_MODEL)),
            _const_spec((D_MODEL, D_FF)),
            _const_spec((D_MODEL, D_FF)),
            _const_spec((D_FF, D_MODEL)),
            _const_spec((1, D_MODEL)),
        ],
        out_specs=pl.BlockSpec((tm, D_MODEL), lambda i: (i, 0)),
        out_shape=jax.ShapeDtypeStruct((n, D_MODEL), F32),
        compiler_params=pltpu.CompilerParams(
            dimension_semantics=("arbitrary",), vmem_limit_bytes=VMEM_LIMIT),
        name="ffn_final" if final_norm else "ffn",
    )(x, g, wg, wu, wd, fg)


def _t5_bucket(d):
    d = jnp.maximum(d, 0)
    max_exact = NUM_BUCKETS // 2
    df = jnp.maximum(d, 1).astype(F32)
    large = max_exact + (jnp.log(df / max_exact) / math.log(MAX_DISTANCE / max_exact)
                         * (NUM_BUCKETS - max_exact)).astype(jnp.int32)
    large = jnp.minimum(large, NUM_BUCKETS - 1)
    return jnp.where(d < max_exact, d, large)


def _masked_bias(relb_ref, head, d, valid):
    bucket = _t5_bucket(d)
    b = jnp.zeros(d.shape, F32)
    for k in range(NUM_BUCKETS):
        b = jnp.where(bucket == k, relb_ref[k, head], b)
    return jnp.where(valid, b, NEG_INF)


def _kv_lane_mask(t, g):
    lane = lax.broadcasted_iota(jnp.int32, t.shape, 1)
    return jnp.where((lane >= g * HEAD_DIM) & (lane < (g + 1) * HEAD_DIM), t, jnp.zeros_like(t))


def _stack_queries(q_chunks, g):
    qs = []
    for r in range(N_REP):
        qc = q_chunks[2 * g + r // 2]
        if r % 2 != g:
            qc = pltpu.roll(qc, HEAD_DIM, 1)
        qs.append(qc)
    return (jnp.concatenate(qs, axis=0) * (HEAD_DIM ** -0.5)).astype(BF16)


def _sink_column(sinks_ref, g, rows):
    return jnp.concatenate(
        [jnp.full((rows, 1), sinks_ref[N_REP * g + r], F32) for r in range(N_REP)], axis=0)


def _unstack_heads(o_by_g, rows):
    lane = lax.broadcasted_iota(jnp.int32, (rows, LANES), 1)
    chunks = []
    for c in range(D_ATTN // LANES):
        g = c // 2
        halves = []
        for half in range(2):
            r = 2 * (c % 2) + half
            piece = o_by_g[g][r * rows:(r + 1) * rows]
            if half != g:
                piece = pltpu.roll(piece, HEAD_DIM, 1)
            halves.append(piece)
        chunks.append(jnp.where(lane < HEAD_DIM, halves[0], halves[1]))
    return chunks


def _glu_tail(y, u, dskip_ref, wglu_ref, bglu_ref):
    y = y + dskip_ref[...] * u
    y = jax.nn.gelu(y)
    z = _dot(y.astype(BF16), wglu_ref[...]) + bglu_ref[...]
    return y * jax.nn.sigmoid(z)


def _mixp_kernel(relb_ref, sinks_ref, x_ref, g_ref, win_ref, wout_ref, lam_ref, bd_ref, cd_ref,
                 dskip_ref, wglu_ref, bglu_ref,
                 o_ref, kp_ref, vp_ref, sre_ref, sim_ref,
                 p_s, kband, vband_t, bias_t, us, uperm, xs, yperm, state, attn_s,
                 *, nb, blk):
    i = pl.program_id(0)
    rows = nb * blk

    @pl.when(i == 0)
    def _():
        kband[...] = jnp.zeros(kband.shape, BF16)
        vband_t[...] = jnp.zeros(vband_t.shape, BF16)
        state[...] = jnp.zeros(state.shape, F32)

    @pl.when(i <= 1)
    def _():
        kj = lax.broadcasted_iota(jnp.int32, (2 * blk, blk), 0)
        qi = lax.broadcasted_iota(jnp.int32, (2 * blk, blk), 1)
        d = qi - kj + blk
        first_key = jnp.where(i > 0, 0, blk)
        valid = (d >= 0) & (d < WINDOW) & (kj >= first_key)
        for h in range(N_HEADS):
            r = h % N_REP
            bias_t[h // N_REP, :, r * blk:(r + 1) * blk] = _masked_bias(relb_ref, h, d, valid)

    x = x_ref[...].reshape(rows, D_MODEL)
    h = _rmsnorm(x, g_ref[...]).astype(BF16)
    p_s[...] = _dot(h, win_ref[...])

    k3 = p_s[:, D_ATTN:D_ATTN + D_KV].reshape(nb, blk, D_KV)
    kp_ref[...] = k3
    vp_ref[...] = p_s[:, D_ATTN + D_KV:D_ATTN + 2 * D_KV].reshape(nb, blk, D_KV)
    kband[:, blk:2 * blk, :] = k3.astype(BF16)

    def attn_body(b, carry):
        row0 = pl.multiple_of(b * blk, blk)
        kb = kband[b]
        v_cur_t = p_s[pl.ds(row0, blk), D_ATTN + D_KV:D_ATTN + 2 * D_KV].T.astype(BF16)
        vband_t[b, :, blk:2 * blk] = v_cur_t
        vt = vband_t[b]
        feat = lax.broadcasted_iota(jnp.int32, vt.shape, 0)
        q_chunks = [p_s[pl.ds(row0, blk), c * LANES:(c + 1) * LANES] for c in range(D_ATTN // LANES)]
        o_by_g = []
        for g in range(N_KV_HEADS):
            q = _stack_queries(q_chunks, g)
            st = _dot_nt(_kv_lane_mask(kb, g), q) + bias_t[g]
            s = jnp.concatenate(
                [jnp.full((1, blk), sinks_ref[N_REP * g + r], F32) for r in range(N_REP)], axis=1)
            m = jnp.maximum(jnp.max(st, axis=0, keepdims=True), s)
            e = jnp.exp(st - m).astype(BF16)
            in_g = (feat >= g * HEAD_DIM) & (feat < (g + 1) * HEAD_DIM)
            ot = _dot(jnp.where(in_g, vt, jnp.ones_like(vt)), e)
            other = (1 - g) * HEAD_DIM
            denom = ot[other:other + 1, :] + jnp.exp(s - m)
            o_by_g.append(ot[g * HEAD_DIM:(g + 1) * HEAD_DIM, :] / denom)
        for c in range(D_ATTN // LANES):
            og = o_by_g[c // 2]
            cc = 2 * (c % 2)
            chunk_t = jnp.concatenate(
                [og[:, cc * blk:(cc + 1) * blk], og[:, (cc + 1) * blk:(cc + 2) * blk]], axis=0)
            attn_s[pl.ds(row0, blk), c * LANES:(c + 1) * LANES] = chunk_t.T
        kband[b, 0:blk, :] = kb[blk:2 * blk]
        vband_t[b, :, 0:blk] = v_cur_t
        return carry

    lax.fori_loop(0, nb, attn_body, 0)

    u_off = D_ATTN + 2 * D_KV
    for s in range(U_SLABS):
        for b in range(nb):
            us[s, b * PITCH:b * PITCH + blk, :] = (
                p_s[b * blk:(b + 1) * blk, u_off + s * LANES:u_off + (s + 1) * LANES])

    def perm_body(t, carry):
        r0 = pl.multiple_of(t * nb, nb)
        for s in range(U_SLABS):
            uperm[pl.ds(r0, nb), s * LANES:(s + 1) * LANES] = us[s, pl.ds(t, nb, stride=PITCH), :]
        return carry

    lax.fori_loop(0, blk, perm_body, 0, unroll=8)

    for hf in range(2):
        xs[...] = _dot(uperm[:, hf * HALF_U:(hf + 1) * HALF_U].astype(BF16), bd_ref[hf])
        lr = jnp.broadcast_to(lam_ref[2 * hf:2 * hf + 1, :], (nb, HALF_STATE))
        li = jnp.broadcast_to(lam_ref[2 * hf + 1:2 * hf + 2, :], (nb, HALF_STATE))
        base = 2 * hf * HALF_STATE
        xr0 = state[:, base:base + HALF_STATE]
        xi0 = state[:, base + HALF_STATE:base + 2 * HALF_STATE]

        def scan_body(t, carry):
            xr, xi = carry
            r0 = pl.multiple_of(t * nb, nb)
            br = xs[pl.ds(r0, nb), 0:HALF_STATE]
            bi = xs[pl.ds(r0, nb), HALF_STATE:2 * HALF_STATE]
            nr = lr * xr - li * xi + br
            ni = lr * xi + li * xr + bi
            xs[pl.ds(r0, nb), 0:HALF_STATE] = nr
            xs[pl.ds(r0, nb), HALF_STATE:2 * HALF_STATE] = ni
            return nr, ni

        xr, xi = lax.fori_loop(0, blk, scan_body, (xr0, xi0), unroll=2)
        state[:, base:base + HALF_STATE] = xr
        state[:, base + HALF_STATE:base + 2 * HALF_STATE] = xi
        sre_ref[:, hf * HALF_STATE:(hf + 1) * HALF_STATE] = xr
        sim_ref[:, hf * HALF_STATE:(hf + 1) * HALF_STATE] = xi
        yperm[:, hf * HALF_U:(hf + 1) * HALF_U] = _dot(xs[...].astype(BF16), cd_ref[hf])

    def unperm_body(t, carry):
        r0 = pl.multiple_of(t * nb, nb)
        for s in range(U_SLABS):
            us[s, pl.ds(t, nb, stride=PITCH), :] = yperm[pl.ds(r0, nb), s * LANES:(s + 1) * LANES]
        return carry

    lax.fori_loop(0, blk, unperm_body, 0, unroll=8)

    y = jnp.concatenate(
        [jnp.concatenate([us[s, b * PITCH:b * PITCH + blk, :] for b in range(nb)], axis=0)
         for s in range(U_SLABS)], axis=1)
    ssm = _glu_tail(y, p_s[:, u_off:u_off + D_SSM], dskip_ref, wglu_ref, bglu_ref)

    mix = (_dot(attn_s[...].astype(BF16), wout_ref[0:D_ATTN, :])
           + _dot(ssm.astype(BF16), wout_ref[D_ATTN:D_ATTN + D_SSM, :]))
    o_ref[...] = (x + mix).reshape(nb, blk, D_MODEL)


def _mix_prompt(x, relb, sinks, g, win, wout, lam, bd, cd, dskip, wglu, bglu):
    nb, seq, _ = x.shape
    blk = WINDOW
    rows = nb * blk
    const2 = lambda shape: _const_spec(shape)
    return pl.pallas_call(
        functools.partial(_mixp_kernel, nb=nb, blk=blk),
        grid=(seq // blk,),
        in_specs=[
            _smem_spec(), _smem_spec(),
            pl.BlockSpec((nb, blk, D_MODEL), lambda i: (0, i, 0)),
            const2((1, D_MODEL)),
            const2((D_MODEL, D_IN)),
            const2((D_ATTN + D_SSM, D_MODEL)),
            const2((4, HALF_STATE)),
            const2((2, HALF_U, 2 * HALF_STATE)),
            const2((2, 2 * HALF_STATE, HALF_U)),
            const2((1, D_SSM)),
            const2((D_SSM, D_SSM)),
            const2((1, D_SSM)),
        ],
        out_specs=[
            pl.BlockSpec((nb, blk, D_MODEL), lambda i: (0, i, 0)),
            pl.BlockSpec((nb, blk, D_KV), lambda i: (0, 0, 0)),
            pl.BlockSpec((nb, blk, D_KV), lambda i: (0, 0, 0)),
            pl.BlockSpec((nb, 2 * HALF_STATE), lambda i: (0, 0)),
            pl.BlockSpec((nb, 2 * HALF_STATE), lambda i: (0, 0)),
        ],
        out_shape=[
            jax.ShapeDtypeStruct((nb, seq, D_MODEL), F32),
            jax.ShapeDtypeStruct((nb, blk, D_KV), F32),
            jax.ShapeDtypeStruct((nb, blk, D_KV), F32),
            jax.ShapeDtypeStruct((nb, 2 * HALF_STATE), F32),
            jax.ShapeDtypeStruct((nb, 2 * HALF_STATE), F32),
        ],
        scratch_shapes=[
            pltpu.VMEM((rows, D_IN), F32),
            pltpu.VMEM((nb, 2 * blk, D_KV), BF16),
            pltpu.VMEM((nb, D_KV, 2 * blk), BF16),
            pltpu.VMEM((N_KV_HEADS, 2 * blk, N_REP * blk), F32),
            pltpu.VMEM((U_SLABS, nb * PITCH, LANES), F32),
            pltpu.VMEM((rows, D_SSM), F32),
            pltpu.VMEM((rows, 2 * HALF_STATE), F32),
            pltpu.VMEM((rows, D_SSM), F32),
            pltpu.VMEM((nb, 4 * HALF_STATE), F32),
            pltpu.VMEM((rows, D_ATTN), F32),
        ],
        compiler_params=pltpu.CompilerParams(
            dimension_semantics=("arbitrary",), vmem_limit_bytes=VMEM_LIMIT),
        name="mix_prompt",
    )(relb, sinks, x, g, win, wout, lam, bd, cd, dskip, wglu, bglu)


def _mixs_kernel(relb_ref, sinks_ref, x_ref, g_ref, win_ref, wout_ref, ck_ref, cv_ref,
                 sre0_ref, sim0_ref, lam_ref, bd_ref, cd_ref, dskip_ref, wglu_ref, bglu_ref,
                 o_ref, ks_ref, vs_ref, sre_ref, sim_ref,
                 p_s, attn_s, ssm_s, bias_c, bias_n, us,
                 *, nseq, t_new, sb, wbuf):
    i = pl.program_id(0)
    nsteps = pl.num_programs(0)
    rows = nseq * t_new
    brow = sb * t_new
    u_off = D_ATTN + 2 * D_KV
    tshift = t_new.bit_length() - 1
    sshift = sb.bit_length() - 1
    wshift = wbuf.bit_length() - 1

    @pl.when(i == 0)
    def _():
        x = x_ref[...]
        h = _rmsnorm(x, g_ref[...]).astype(BF16)
        p_s[...] = _dot(h, win_ref[...])

        for s in range(U_SLABS):
            us[s] = p_s[:, u_off + s * LANES:u_off + (s + 1) * LANES]
        uperm = jnp.concatenate(
            [jnp.concatenate([us[s, pl.ds(t, nseq, stride=t_new), :] for t in range(t_new)], axis=0)
             for s in range(U_SLABS)], axis=1)
        yparts = []
        for hf in range(2):
            bu = _dot(uperm[:, hf * HALF_U:(hf + 1) * HALF_U].astype(BF16), bd_ref[hf])
            lr = lam_ref[2 * hf:2 * hf + 1, :]
            li = lam_ref[2 * hf + 1:2 * hf + 2, :]
            xr = sre0_ref[:, hf * HALF_STATE:(hf + 1) * HALF_STATE]
            xi = sim0_ref[:, hf * HALF_STATE:(hf + 1) * HALF_STATE]
            states = []
            for t in range(t_new):
                br = bu[t * nseq:(t + 1) * nseq, 0:HALF_STATE]
                bi = bu[t * nseq:(t + 1) * nseq, HALF_STATE:2 * HALF_STATE]
                xr, xi = lr * xr - li * xi + br, lr * xi + li * xr + bi
                states.append(jnp.concatenate([xr, xi], axis=1).astype(BF16))
            sre_ref[:, hf * HALF_STATE:(hf + 1) * HALF_STATE] = xr
            sim_ref[:, hf * HALF_STATE:(hf + 1) * HALF_STATE] = xi
            yparts.append(_dot(jnp.concatenate(states, axis=0), cd_ref[hf]))
        yperm = jnp.concatenate(yparts, axis=1)
        for s in range(U_SLABS):
            for t in range(t_new):
                us[s, pl.ds(t, nseq, stride=t_new), :] = (
                    yperm[t * nseq:(t + 1) * nseq, s * LANES:(s + 1) * LANES])
        y = jnp.concatenate([us[s] for s in range(U_SLABS)], axis=1)
        ssm_s[...] = _glu_tail(y, p_s[:, u_off:u_off + D_SSM], dskip_ref, wglu_ref, bglu_ref)

        ncol = sb * wbuf
        rho = lax.broadcasted_iota(jnp.int32, (brow, ncol), 0)
        kap = lax.broadcasted_iota(jnp.int32, (brow, ncol), 1)
        tq = rho & (t_new - 1)
        bq = rho >> tshift
        bk = kap >> wshift
        jk = kap & (wbuf - 1)
        d_c = tq - jk + wbuf
        valid_c = (bq == bk) & (d_c >= 0) & (d_c < WINDOW)
        rho_n = lax.broadcasted_iota(jnp.int32, (brow, LANES), 0)
        kap_n = lax.broadcasted_iota(jnp.int32, (brow, LANES), 1)
        tq_n = rho_n & (t_new - 1)
        d_n = tq_n - (kap_n & (t_new - 1))
        valid_n = ((rho_n >> tshift) == (kap_n >> tshift)) & (d_n >= 0) & (kap_n < brow)
        for hd in range(N_HEADS):
            g, r = hd // N_REP, hd % N_REP
            bias_c[g, r * brow:(r + 1) * brow, :] = _masked_bias(relb_ref, hd, d_c, valid_c)
            bias_n[g, r * brow:(r + 1) * brow, :] = _masked_bias(relb_ref, hd, d_n, valid_n)

    row0 = pl.multiple_of(i * brow, brow)
    ck = ck_ref[...]
    cv = cv_ref[...]
    kn = p_s[pl.ds(row0, brow), D_ATTN:D_ATTN + D_KV]
    vn = p_s[pl.ds(row0, brow), D_ATTN + D_KV:D_ATTN + 2 * D_KV]
    pad = jnp.zeros((LANES - brow, D_KV), F32)
    kn_pad = jnp.concatenate([kn, pad], axis=0).astype(BF16)
    vn_pad = jnp.concatenate([vn, pad], axis=0).astype(BF16)
    kc = ck.reshape(sb * wbuf, D_KV).astype(BF16)
    vc = cv.reshape(sb * wbuf, D_KV).astype(BF16)
    q_chunks = [p_s[pl.ds(row0, brow), c * LANES:(c + 1) * LANES] for c in range(D_ATTN // LANES)]
    o_by_g = []
    for g in range(N_KV_HEADS):
        q = _stack_queries(q_chunks, g)
        lc = _dot_nt(q, _kv_lane_mask(kc, g)) + bias_c[g]
        ln = _dot_nt(q, _kv_lane_mask(kn_pad, g)) + bias_n[g]
        s = _sink_column(sinks_ref, g, brow)
        m = jnp.maximum(jnp.maximum(jnp.max(lc, axis=-1, keepdims=True),
                                    jnp.max(ln, axis=-1, keepdims=True)), s)
        ec = jnp.exp(lc - m)
        en = jnp.exp(ln - m)
        denom = (jnp.sum(ec, axis=-1, keepdims=True) + jnp.sum(en, axis=-1, keepdims=True)
                 + jnp.exp(s - m))
        o = _dot(ec.astype(BF16), vc) + _dot(en.astype(BF16), vn_pad)
        o_by_g.append(o / denom)
    for c, chunk in enumerate(_unstack_heads(o_by_g, brow)):
        attn_s[pl.ds(row0, brow), c * LANES:(c + 1) * LANES] = chunk

    ks_ref[:, 0:wbuf - t_new, :] = ck[:, t_new:wbuf, :]
    vs_ref[:, 0:wbuf - t_new, :] = cv[:, t_new:wbuf, :]
    ks_ref[:, wbuf - t_new:wbuf, :] = kn.reshape(sb, t_new, D_KV)
    vs_ref[:, wbuf - t_new:wbuf, :] = vn.reshape(sb, t_new, D_KV)

    @pl.when(i == nsteps - 1)
    def _():
        mix = (_dot(attn_s[...].astype(BF16), wout_ref[0:D_ATTN, :])
               + _dot(ssm_s[...].astype(BF16), wout_ref[D_ATTN:D_ATTN + D_SSM, :]))
        o_ref[...] = x_ref[...] + mix


def _mix_sample(x, relb, sinks, g, win, wout, ck, cv, sre0, sim0, lam, bd, cd, dskip, wglu, bglu,
                *, nseq, t_new):
    rows = nseq * t_new
    wbuf = ck.shape[1]
    sb = SUBLANES
    brow = sb * t_new
    return pl.pallas_call(
        functools.partial(_mixs_kernel, nseq=nseq, t_new=t_new, sb=sb, wbuf=wbuf),
        grid=(nseq // sb,),
        in_specs=[
            _smem_spec(), _smem_spec(),
            _const_spec((rows, D_MODEL)),
            _const_spec((1, D_MODEL)),
            _const_spec((D_MODEL, D_IN)),
            _const_spec((D_ATTN + D_SSM, D_MODEL)),
            pl.BlockSpec((sb, wbuf, D_KV), lambda i: (i, 0, 0)),
            pl.BlockSpec((sb, wbuf, D_KV), lambda i: (i, 0, 0)),
            _const_spec((nseq, 2 * HALF_STATE)),
            _const_spec((nseq, 2 * HALF_STATE)),
            _const_spec((4, HALF_STATE)),
            _const_spec((2, HALF_U, 2 * HALF_STATE)),
            _const_spec((2, 2 * HALF_STATE, HALF_U)),
            _const_spec((1, D_SSM)),
            _const_spec((D_SSM, D_SSM)),
            _const_spec((1, D_SSM)),
        ],
        out_specs=[
            pl.BlockSpec((rows, D_MODEL), lambda i: (0, 0)),
            pl.BlockSpec((sb, wbuf, D_KV), lambda i: (i, 0, 0)),
            pl.BlockSpec((sb, wbuf, D_KV), lambda i: (i, 0, 0)),
            pl.BlockSpec((nseq, 2 * HALF_STATE), lambda i: (0, 0)),
            pl.BlockSpec((nseq, 2 * HALF_STATE), lambda i: (0, 0)),
        ],
        out_shape=[
            jax.ShapeDtypeStruct((rows, D_MODEL), F32),
            jax.ShapeDtypeStruct((nseq, wbuf, D_KV), F32),
            jax.ShapeDtypeStruct((nseq, wbuf, D_KV), F32),
            jax.ShapeDtypeStruct((nseq, 2 * HALF_STATE), F32),
            jax.ShapeDtypeStruct((nseq, 2 * HALF_STATE), F32),
        ],
        scratch_shapes=[
            pltpu.VMEM((rows, D_IN), F32),
            pltpu.VMEM((rows, D_ATTN), F32),
            pltpu.VMEM((rows, D_SSM), F32),
            pltpu.VMEM((N_KV_HEADS, N_REP * brow, sb * wbuf), F32),
            pltpu.VMEM((N_KV_HEADS, N_REP * brow, LANES), F32),
            pltpu.VMEM((U_SLABS, rows, LANES), F32),
        ],
        compiler_params=pltpu.CompilerParams(
            dimension_semantics=("arbitrary",), vmem_limit_bytes=VMEM_LIMIT),
        name="mix_sample",
    )(relb, sinks, x, g, win, wout, ck, cv, sre0, sim0, lam, bd, cd, dskip, wglu, bglu)


def _s5_operators(log_dt, a_re, a_im, b_re, b_im, c_re, c_im):
    dt = jnp.exp(log_dt)[:, None]
    mag = jnp.exp(a_re * dt)
    lb_re = mag * jnp.cos(a_im * dt)
    lb_im = mag * jnp.sin(a_im * dt)
    den = a_re * a_re + a_im * a_im
    nr = lb_re - 1.0
    q_re = (nr * a_re + lb_im * a_im) / den
    q_im = (lb_im * a_re - nr * a_im) / den
    bb_re = q_re[..., None] * b_re - q_im[..., None] * b_im
    bb_im = q_re[..., None] * b_im + q_im[..., None] * b_re
    eye = jnp.eye(HALF_GROUPS, dtype=F32)
    lam, bd, cd = [], [], []
    for hf in range(2):
        gs = slice(hf * HALF_GROUPS, (hf + 1) * HALF_GROUPS)
        lam += [lb_re[gs].reshape(1, HALF_STATE), lb_im[gs].reshape(1, HALF_STATE)]
        m_re = jnp.einsum('gpc,gk->gckp', bb_re[gs], eye).reshape(HALF_U, HALF_STATE)
        m_im = jnp.einsum('gpc,gk->gckp', bb_im[gs], eye).reshape(HALF_U, HALF_STATE)
        bd.append(jnp.concatenate([m_re, m_im], axis=1))
        n_re = jnp.einsum('gcp,gk->gpkc', c_re[gs], eye).reshape(HALF_STATE, HALF_U)
        n_im = jnp.einsum('gcp,gk->gpkc', c_im[gs], eye).reshape(HALF_STATE, HALF_U)
        cd.append(jnp.concatenate([n_re, -n_im], axis=0))
    return (jnp.concatenate(lam, axis=0), jnp.stack(bd).astype(BF16), jnp.stack(cd).astype(BF16))


def kernel(x_prompt, x_sample, cache_k, cache_v, state_ssm_re, state_ssm_im, rel_bias,
           ffn1_norm, ffn1_w_gate, ffn1_w_up, ffn1_w_down, mix_norm, w_in, sinks,
           log_dt, a_re, a_im, b_re, b_im, c_re, c_im, d_skip, w_glu, b_glu, w_out,
           ffn2_norm, ffn2_w_gate, ffn2_w_up, ffn2_w_down, final_norm):
    depth = w_in.shape[0]
    assert depth == 1
    batch, seq, _ = x_prompt.shape
    nseq, t_new, _ = x_sample.shape
    wbuf = cache_k.shape[2]
    fg = final_norm.reshape(1, D_MODEL)

    l = 0
    ffn1 = (ffn1_norm[l].reshape(1, D_MODEL), ffn1_w_gate[l].astype(BF16),
            ffn1_w_up[l].astype(BF16), ffn1_w_down[l].astype(BF16), fg)
    ffn2 = (ffn2_norm[l].reshape(1, D_MODEL), ffn2_w_gate[l].astype(BF16),
            ffn2_w_up[l].astype(BF16), ffn2_w_down[l].astype(BF16), fg)
    lam, bd, cd = _s5_operators(log_dt[l], a_re[l], a_im[l], b_re[l], b_im[l], c_re[l], c_im[l])
    mix_w = (mix_norm[l].reshape(1, D_MODEL), w_in[l].astype(BF16), w_out[l].astype(BF16))
    ssm_w = (lam, bd, cd, d_skip[l].reshape(1, D_SSM), w_glu[l].astype(BF16),
             b_glu[l].reshape(1, D_SSM))
    sinks_l = sinks[l]

    xp = x_prompt.reshape(batch * seq, D_MODEL)
    y = _ffn(xp, *ffn1, final_norm=False, tm=512)
    y, kp, vp, sre_p, sim_p = _mix_prompt(
        y.reshape(batch, seq, D_MODEL), rel_bias, sinks_l, *mix_w, *ssm_w)
    y_prompt = _ffn(y.reshape(batch * seq, D_MODEL), *ffn2, final_norm=True, tm=512)
    y_prompt = y_prompt.reshape(batch, seq, D_MODEL)

    xs = x_sample.reshape(nseq * t_new, D_MODEL)
    ys = _ffn(xs, *ffn1, final_norm=False, tm=nseq * t_new)
    ys, ks, vs, sre_s, sim_s = _mix_sample(
        ys, rel_bias, sinks_l, *mix_w,
        cache_k[l].reshape(nseq, wbuf, D_KV), cache_v[l].reshape(nseq, wbuf, D_KV),
        state_ssm_re[l].reshape(nseq, 2 * HALF_STATE), state_ssm_im[l].reshape(nseq, 2 * HALF_STATE),
        *ssm_w, nseq=nseq, t_new=t_new)
    y_sample = _ffn(ys, *ffn2, final_norm=True, tm=nseq * t_new).reshape(nseq, t_new, D_MODEL)

    kv_p = (1, batch, WINDOW, N_KV_HEADS, HEAD_DIM)
    st_p = (1, batch, N_SSM_GROUPS, SSM_STATE)
    kv_s = (1, nseq, wbuf, N_KV_HEADS, HEAD_DIM)
    st_s = (1, nseq, N_SSM_GROUPS, SSM_STATE)
    return (y_prompt, y_sample,
            kp.reshape(kv_p), vp.reshape(kv_p), sre_p.reshape(st_p), sim_p.reshape(st_p),
            ks.reshape(kv_s), vs.reshape(kv_s), sre_s.reshape(st_s), sim_s.reshape(st_s))
```

```python
import functools
import math

import jax
import jax.numpy as jnp
from jax import lax
from jax.experimental import pallas as pl
from jax.experimental.pallas import tpu as pltpu

F32 = jnp.float32
BF16 = jnp.bfloat16

D_MODEL = 1024
HEAD_DIM = 64
D_ATTN = 512
N_HEADS = 8
N_KV_HEADS = 2
N_REP = 4
D_KV = 128
D_SSM = 512
SSM_GROUP = 16
N_SSM_GROUPS = 32
SSM_STATE = 64
WINDOW = 128
NUM_BUCKETS = 32
MAX_DISTANCE = 128
D_FF = 2816
D_IN = D_ATTN + 2 * D_KV + D_SSM
RMS_EPS = 1e-6
NEG_INF = -1e30

LANES = 128
SUBLANES = 8
MXU_DIM = 256
HALF_GROUPS = N_SSM_GROUPS // 2
HALF_STATE = HALF_GROUPS * SSM_STATE
HALF_U = HALF_GROUPS * SSM_GROUP
U_SLABS = D_SSM // LANES
PITCH = WINDOW + SUBLANES
ATTN_GROUP = 2
VMEM_LIMIT = 60 * 1024 * 1024


def _const_spec(shape):
    nd = len(shape)
    return pl.BlockSpec(shape, lambda *_: (0,) * nd, pipeline_mode=pl.Buffered(1))


def _smem_spec():
    return pl.BlockSpec(memory_space=pltpu.SMEM)


def _rmsnorm(x, g):
    r = lax.rsqrt(jnp.mean(x * x, axis=-1, keepdims=True) + RMS_EPS)
    return (x * r) * g


def _dot(a, b):
    return jnp.dot(a, b, preferred_element_type=F32)


def _dot_nt(a, b):
    return lax.dot_general(a, b, (((1,), (1,)), ((), ())), preferred_element_type=F32)


def _ffn_tile(x, g_ref, wg_ref, wu_ref, wd_ref, fg_ref, final_norm):
    h = _rmsnorm(x, g_ref[...]).astype(BF16)
    acc = None
    for c in range(D_FF // MXU_DIM):
        sl = slice(c * MXU_DIM, (c + 1) * MXU_DIM)
        gate = _dot(h, wg_ref[:, sl].astype(BF16))
        up = _dot(h, wu_ref[:, sl].astype(BF16))
        a = (gate * jax.nn.sigmoid(gate) * up).astype(BF16)
        part = _dot(a, wd_ref[sl, :].astype(BF16))
        acc = part if acc is None else acc + part
    y = x + 0.5 * acc
    if final_norm:
        y = _rmsnorm(y, fg_ref[...])
    return y


def _ffn_kernel(xp_ref, xs_ref, g_ref, wg_ref, wu_ref, wd_ref, fg_ref, op_ref, os_ref,
                *, final_norm, prompt_steps):
    i = pl.program_id(0)
    weights = (g_ref, wg_ref, wu_ref, wd_ref, fg_ref)

    @pl.when(i < prompt_steps)
    def _():
        op_ref[...] = _ffn_tile(xp_ref[...], *weights, final_norm)

    @pl.when(i == prompt_steps)
    def _():
        os_ref[...] = _ffn_tile(xs_ref[...], *weights, final_norm)


def _ffn(xp, xs, g, wg, wu, wd, fg, *, final_norm):
    n_p, n_s = xp.shape[0], xs.shape[0]
    tm = n_s
    prompt_steps = n_p // tm
    assert prompt_steps * tm == n_p
    prompt_block = lambda i: (jnp.minimum(i, prompt_steps - 1), 0)
    return pl.pallas_call(
        functools.partial(_ffn_kernel, final_norm=final_norm, prompt_steps=prompt_steps),
        grid=(prompt_steps + 1,),
        in_specs=[
            pl.BlockSpec((tm, D_MODEL), prompt_block),
            _const_spec((n_s, D_MODEL)),
            _const_spec((1, D_MODEL)),
            _const_spec((D_MODEL, D_FF)),
            _const_spec((D_MODEL, D_FF)),
            _const_spec((D_FF, D_MODEL)),
            _const_spec((1, D_MODEL)),
        ],
        out_specs=[
            pl.BlockSpec((tm, D_MODEL), prompt_block),
            pl.BlockSpec((n_s, D_MODEL), lambda i: (0, 0)),
        ],
        out_shape=[
            jax.ShapeDtypeStruct((n_p, D_MODEL), F32),
            jax.ShapeDtypeStruct((n_s, D_MODEL), F32),
        ],
        compiler_params=pltpu.CompilerParams(
            dimension_semantics=("arbitrary",), vmem_limit_bytes=VMEM_LIMIT),
        name="ffn_final" if final_norm else "ffn",
    )(xp, xs, g, wg, wu, wd, fg)


def _t5_bucket(d):
    d = jnp.maximum(d, 0)
    max_exact = NUM_BUCKETS // 2
    df = jnp.maximum(d, 1).astype(F32)
    large = max_exact + (jnp.log(df / max_exact) / math.log(MAX_DISTANCE / max_exact)
                         * (NUM_BUCKETS - max_exact)).astype(jnp.int32)
    large = jnp.minimum(large, NUM_BUCKETS - 1)
    return jnp.where(d < max_exact, d, large)


def _masked_bias(relb_ref, head, d, valid):
    bucket = _t5_bucket(d)
    b = jnp.zeros(d.shape, F32)
    for k in range(NUM_BUCKETS):
        b = jnp.where(bucket == k, relb_ref[k, head], b)
    return jnp.where(valid, b, NEG_INF)


def _kv_lane_mask(t, g):
    lane = lax.broadcasted_iota(jnp.int32, t.shape, 1)
    return jnp.where((lane >= g * HEAD_DIM) & (lane < (g + 1) * HEAD_DIM), t, jnp.zeros_like(t))


def _stack_queries(q_chunks, g):
    qs = []
    for r in range(N_REP):
        qc = q_chunks[2 * g + r // 2]
        if r % 2 != g:
            qc = pltpu.roll(qc, HEAD_DIM, 1)
        qs.append(qc)
    return (jnp.concatenate(qs, axis=0) * (HEAD_DIM ** -0.5)).astype(BF16)


def _sink_column(sinks_ref, g, rows):
    return jnp.concatenate(
        [jnp.full((rows, 1), sinks_ref[N_REP * g + r], F32) for r in range(N_REP)], axis=0)


def _unstack_heads(o_by_g, rows):
    lane = lax.broadcasted_iota(jnp.int32, (rows, LANES), 1)
    chunks = []
    for c in range(D_ATTN // LANES):
        g = c // 2
        halves = []
        for half in range(2):
            r = 2 * (c % 2) + half
            piece = o_by_g[g][r * rows:(r + 1) * rows]
            if half != g:
                piece = pltpu.roll(piece, HEAD_DIM, 1)
            halves.append(piece)
        chunks.append(jnp.where(lane < HEAD_DIM, halves[0], halves[1]))
    return chunks


def _glu_tail(y, u, dskip_ref, wglu_ref, bglu_ref):
    y = y + dskip_ref[...] * u
    y = jax.nn.gelu(y)
    z = _dot(y.astype(BF16), wglu_ref[...]) + bglu_ref[...]
    return y * jax.nn.sigmoid(z)


def _mixp_kernel(relb_ref, sinks_ref, x_ref, g_ref, win_ref, wout_ref, lam_ref, bd_ref, cd_ref,
                 dskip_ref, wglu_ref, bglu_ref,
                 o_ref, kp_ref, vp_ref, sre_ref, sim_ref,
                 p_s, kband, vband_t, bias_t, us, uperm, xs, yperm, state, attn_s,
                 *, nb, blk):
    i = pl.program_id(0)
    rows = nb * blk

    @pl.when(i == 0)
    def _():
        kband[...] = jnp.zeros(kband.shape, BF16)
        vband_t[...] = jnp.zeros(vband_t.shape, BF16)
        state[...] = jnp.zeros(state.shape, F32)

    @pl.when(i <= 1)
    def _():
        kj = lax.broadcasted_iota(jnp.int32, (2 * blk, blk), 0)
        qi = lax.broadcasted_iota(jnp.int32, (2 * blk, blk), 1)
        d = qi - kj + blk
        first_key = jnp.where(i > 0, 0, blk)
        valid = (d >= 0) & (d < WINDOW) & (kj >= first_key)
        for h in range(N_HEADS):
            r = h % N_REP
            bias_t[h // N_REP, :, r * blk:(r + 1) * blk] = _masked_bias(relb_ref, h, d, valid)

    x = x_ref[...].reshape(rows, D_MODEL)
    h = _rmsnorm(x, g_ref[...]).astype(BF16)
    p_s[...] = _dot(h, win_ref[...])

    k3 = p_s[:, D_ATTN:D_ATTN + D_KV].reshape(nb, blk, D_KV)
    kp_ref[...] = k3
    vp_ref[...] = p_s[:, D_ATTN + D_KV:D_ATTN + 2 * D_KV].reshape(nb, blk, D_KV)
    kband[:, blk:2 * blk, :] = k3.astype(BF16)
    for b in range(nb):
        vband_t[b, :, blk:2 * blk] = (
            p_s[b * blk:(b + 1) * blk, D_ATTN + D_KV:D_ATTN + 2 * D_KV].T.astype(BF16))

    def attn_body(bp, carry):
        scores = []
        for bb in range(ATTN_GROUP):
            b = bp * ATTN_GROUP + bb
            row0 = pl.multiple_of(b * blk, blk)
            kb = kband[b]
            q_chunks = [p_s[pl.ds(row0, blk), c * LANES:(c + 1) * LANES]
                        for c in range(D_ATTN // LANES)]
            for g in range(N_KV_HEADS):
                q = _stack_queries(q_chunks, g)
                scores.append(_dot_nt(_kv_lane_mask(kb, g), q))
        for bb in range(ATTN_GROUP):
            b = bp * ATTN_GROUP + bb
            row0 = pl.multiple_of(b * blk, blk)
            vt = vband_t[b]
            feat = lax.broadcasted_iota(jnp.int32, vt.shape, 0)
            o_by_g = []
            for g in range(N_KV_HEADS):
                st = scores[bb * N_KV_HEADS + g] + bias_t[g]
                s = jnp.concatenate(
                    [jnp.full((1, blk), sinks_ref[N_REP * g + r], F32) for r in range(N_REP)],
                    axis=1)
                m = jnp.maximum(jnp.max(st, axis=0, keepdims=True), s)
                e = jnp.exp(st - m).astype(BF16)
                in_g = (feat >= g * HEAD_DIM) & (feat < (g + 1) * HEAD_DIM)
                ot = _dot(jnp.where(in_g, vt, jnp.ones_like(vt)), e)
                other = (1 - g) * HEAD_DIM
                denom = ot[other:other + 1, :] + jnp.exp(s - m)
                o_by_g.append(ot[g * HEAD_DIM:(g + 1) * HEAD_DIM, :] / denom)
            for c in range(D_ATTN // LANES):
                og = o_by_g[c // 2]
                cc = 2 * (c % 2)
                chunk_t = jnp.concatenate(
                    [og[:, cc * blk:(cc + 1) * blk], og[:, (cc + 1) * blk:(cc + 2) * blk]], axis=0)
                attn_s[pl.ds(row0, blk), c * LANES:(c + 1) * LANES] = chunk_t.T
        return carry

    lax.fori_loop(0, nb // ATTN_GROUP, attn_body, 0)
    kband[:, 0:blk, :] = kband[:, blk:2 * blk, :]
    vband_t[:, :, 0:blk] = vband_t[:, :, blk:2 * blk]

    u_off = D_ATTN + 2 * D_KV
    for s in range(U_SLABS):
        for b in range(nb):
            us[s, b * PITCH:b * PITCH + blk, :] = (
                p_s[b * blk:(b + 1) * blk, u_off + s * LANES:u_off + (s + 1) * LANES])

    def perm_body(t, carry):
        r0 = pl.multiple_of(t * nb, nb)
        for s in range(U_SLABS):
            uperm[pl.ds(r0, nb), s * LANES:(s + 1) * LANES] = us[s, pl.ds(t, nb, stride=PITCH), :]
        return carry

    lax.fori_loop(0, blk, perm_body, 0, unroll=8)

    for hf in range(2):
        xs[...] = _dot(uperm[:, hf * HALF_U:(hf + 1) * HALF_U].astype(BF16), bd_ref[hf])
        lr = jnp.broadcast_to(lam_ref[2 * hf:2 * hf + 1, :], (nb, HALF_STATE))
        li = jnp.broadcast_to(lam_ref[2 * hf + 1:2 * hf + 2, :], (nb, HALF_STATE))
        base = 2 * hf * HALF_STATE
        xr0 = state[:, base:base + HALF_STATE]
        xi0 = state[:, base + HALF_STATE:base + 2 * HALF_STATE]

        def scan_body(t, carry):
            xr, xi = carry
            r0 = pl.multiple_of(t * nb, nb)
            br = xs[pl.ds(r0, nb), 0:HALF_STATE]
            bi = xs[pl.ds(r0, nb), HALF_STATE:2 * HALF_STATE]
            nr = lr * xr - li * xi + br
            ni = lr * xi + li * xr + bi
            xs[pl.ds(r0, nb), 0:HALF_STATE] = nr
            xs[pl.ds(r0, nb), HALF_STATE:2 * HALF_STATE] = ni
            return nr, ni

        xr, xi = lax.fori_loop(0, blk, scan_body, (xr0, xi0), unroll=2)
        state[:, base:base + HALF_STATE] = xr
        state[:, base + HALF_STATE:base + 2 * HALF_STATE] = xi
        sre_ref[:, hf * HALF_STATE:(hf + 1) * HALF_STATE] = xr
        sim_ref[:, hf * HALF_STATE:(hf + 1) * HALF_STATE] = xi
        yperm[:, hf * HALF_U:(hf + 1) * HALF_U] = _dot(xs[...].astype(BF16), cd_ref[hf])

    def unperm_body(t, carry):
        r0 = pl.multiple_of(t * nb, nb)
        for s in range(U_SLABS):
            us[s, pl.ds(t, nb, stride=PITCH), :] = yperm[pl.ds(r0, nb), s * LANES:(s + 1) * LANES]
        return carry

    lax.fori_loop(0, blk, unperm_body, 0, unroll=8)

    y = jnp.concatenate(
        [jnp.concatenate([us[s, b * PITCH:b * PITCH + blk, :] for b in range(nb)], axis=0)
         for s in range(U_SLABS)], axis=1)
    ssm = _glu_tail(y, p_s[:, u_off:u_off + D_SSM], dskip_ref, wglu_ref, bglu_ref)

    mix = (_dot(attn_s[...].astype(BF16), wout_ref[0:D_ATTN, :])
           + _dot(ssm.astype(BF16), wout_ref[D_ATTN:D_ATTN + D_SSM, :]))
    o_ref[...] = (x + mix).reshape(nb, blk, D_MODEL)


def _mix_prompt(x, relb, sinks, g, win, wout, lam, bd, cd, dskip, wglu, bglu):
    nb, seq, _ = x.shape
    blk = WINDOW
    rows = nb * blk
    const2 = lambda shape: _const_spec(shape)
    return pl.pallas_call(
        functools.partial(_mixp_kernel, nb=nb, blk=blk),
        grid=(seq // blk,),
        in_specs=[
            _smem_spec(), _smem_spec(),
            pl.BlockSpec((nb, blk, D_MODEL), lambda i: (0, i, 0)),
            const2((1, D_MODEL)),
            const2((D_MODEL, D_IN)),
            const2((D_ATTN + D_SSM, D_MODEL)),
            const2((4, HALF_STATE)),
            const2((2, HALF_U, 2 * HALF_STATE)),
            const2((2, 2 * HALF_STATE, HALF_U)),
            const2((1, D_SSM)),
            const2((D_SSM, D_SSM)),
            const2((1, D_SSM)),
        ],
        out_specs=[
            pl.BlockSpec((nb, blk, D_MODEL), lambda i: (0, i, 0)),
            pl.BlockSpec((nb, blk, D_KV), lambda i: (0, 0, 0)),
            pl.BlockSpec((nb, blk, D_KV), lambda i: (0, 0, 0)),
            pl.BlockSpec((nb, 2 * HALF_STATE), lambda i: (0, 0)),
            pl.BlockSpec((nb, 2 * HALF_STATE), lambda i: (0, 0)),
        ],
        out_shape=[
            jax.ShapeDtypeStruct((nb, seq, D_MODEL), F32),
            jax.ShapeDtypeStruct((nb, blk, D_KV), F32),
            jax.ShapeDtypeStruct((nb, blk, D_KV), F32),
            jax.ShapeDtypeStruct((nb, 2 * HALF_STATE), F32),
            jax.ShapeDtypeStruct((nb, 2 * HALF_STATE), F32),
        ],
        scratch_shapes=[
            pltpu.VMEM((rows, D_IN), F32),
            pltpu.VMEM((nb, 2 * blk, D_KV), BF16),
            pltpu.VMEM((nb, D_KV, 2 * blk), BF16),
            pltpu.VMEM((N_KV_HEADS, 2 * blk, N_REP * blk), F32),
            pltpu.VMEM((U_SLABS, nb * PITCH, LANES), F32),
            pltpu.VMEM((rows, D_SSM), F32),
            pltpu.VMEM((rows, 2 * HALF_STATE), F32),
            pltpu.VMEM((rows, D_SSM), F32),
            pltpu.VMEM((nb, 4 * HALF_STATE), F32),
            pltpu.VMEM((rows, D_ATTN), F32),
        ],
        compiler_params=pltpu.CompilerParams(
            dimension_semantics=("arbitrary",), vmem_limit_bytes=VMEM_LIMIT),
        name="mix_prompt",
    )(relb, sinks, x, g, win, wout, lam, bd, cd, dskip, wglu, bglu)


def _mixs_kernel(relb_ref, sinks_ref, x_ref, g_ref, win_ref, wout_ref, ck_ref, cv_ref,
                 sre0_ref, sim0_ref, lam_ref, bd_ref, cd_ref, dskip_ref, wglu_ref, bglu_ref,
                 o_ref, ks_ref, vs_ref, sre_ref, sim_ref,
                 p_s, attn_s, ssm_s, bias_c, bias_n, us,
                 *, nseq, t_new, sb, wbuf):
    i = pl.program_id(0)
    nsteps = pl.num_programs(0)
    rows = nseq * t_new
    brow = sb * t_new
    u_off = D_ATTN + 2 * D_KV
    tshift = t_new.bit_length() - 1
    sshift = sb.bit_length() - 1
    wshift = wbuf.bit_length() - 1

    @pl.when(i == 0)
    def _():
        x = x_ref[...]
        h = _rmsnorm(x, g_ref[...]).astype(BF16)
        p_s[...] = _dot(h, win_ref[...])

        for s in range(U_SLABS):
            us[s] = p_s[:, u_off + s * LANES:u_off + (s + 1) * LANES]
        uperm = jnp.concatenate(
            [jnp.concatenate([us[s, pl.ds(t, nseq, stride=t_new), :] for t in range(t_new)], axis=0)
             for s in range(U_SLABS)], axis=1)
        yparts = []
        for hf in range(2):
            bu = _dot(uperm[:, hf * HALF_U:(hf + 1) * HALF_U].astype(BF16), bd_ref[hf])
            lr = lam_ref[2 * hf:2 * hf + 1, :]
            li = lam_ref[2 * hf + 1:2 * hf + 2, :]
            xr = sre0_ref[:, hf * HALF_STATE:(hf + 1) * HALF_STATE]
            xi = sim0_ref[:, hf * HALF_STATE:(hf + 1) * HALF_STATE]
            states = []
            for t in range(t_new):
                br = bu[t * nseq:(t + 1) * nseq, 0:HALF_STATE]
                bi = bu[t * nseq:(t + 1) * nseq, HALF_STATE:2 * HALF_STATE]
                xr, xi = lr * xr - li * xi + br, lr * xi + li * xr + bi
                states.append(jnp.concatenate([xr, xi], axis=1).astype(BF16))
            sre_ref[:, hf * HALF_STATE:(hf + 1) * HALF_STATE] = xr
            sim_ref[:, hf * HALF_STATE:(hf + 1) * HALF_STATE] = xi
            yparts.append(_dot(jnp.concatenate(states, axis=0), cd_ref[hf]))
        yperm = jnp.concatenate(yparts, axis=1)
        for s in range(U_SLABS):
            for t in range(t_new):
                us[s, pl.ds(t, nseq, stride=t_new), :] = (
                    yperm[t * nseq:(t + 1) * nseq, s * LANES:(s + 1) * LANES])
        y = jnp.concatenate([us[s] for s in range(U_SLABS)], axis=1)
        ssm_s[...] = _glu_tail(y, p_s[:, u_off:u_off + D_SSM], dskip_ref, wglu_ref, bglu_ref)

        ncol = sb * wbuf
        rho = lax.broadcasted_iota(jnp.int32, (brow, ncol), 0)
        kap = lax.broadcasted_iota(jnp.int32, (brow, ncol), 1)
        tq = rho & (t_new - 1)
        bq = rho >> tshift
        bk = kap >> wshift
        jk = kap & (wbuf - 1)
        d_c = tq - jk + wbuf
        valid_c = (bq == bk) & (d_c >= 0) & (d_c < WINDOW)
        rho_n = lax.broadcasted_iota(jnp.int32, (brow, LANES), 0)
        kap_n = lax.broadcasted_iota(jnp.int32, (brow, LANES), 1)
        tq_n = rho_n & (t_new - 1)
        d_n = tq_n - (kap_n & (t_new - 1))
        valid_n = ((rho_n >> tshift) == (kap_n >> tshift)) & (d_n >= 0) & (kap_n < brow)
        for hd in range(N_HEADS):
            g, r = hd // N_REP, hd % N_REP
            bias_c[g, r * brow:(r + 1) * brow, :] = _masked_bias(relb_ref, hd, d_c, valid_c)
            bias_n[g, r * brow:(r + 1) * brow, :] = _masked_bias(relb_ref, hd, d_n, valid_n)

    row0 = pl.multiple_of(i * brow, brow)
    ck = ck_ref[...]
    cv = cv_ref[...]
    kn = p_s[pl.ds(row0, brow), D_ATTN:D_ATTN + D_KV]
    vn = p_s[pl.ds(row0, brow), D_ATTN + D_KV:D_ATTN + 2 * D_KV]
    pad = jnp.zeros((LANES - brow, D_KV), F32)
    kn_pad = jnp.concatenate([kn, pad], axis=0).astype(BF16)
    vn_pad = jnp.concatenate([vn, pad], axis=0).astype(BF16)
    kc = ck.reshape(sb * wbuf, D_KV).astype(BF16)
    vc = cv.reshape(sb * wbuf, D_KV).astype(BF16)
    q_chunks = [p_s[pl.ds(row0, brow), c * LANES:(c + 1) * LANES] for c in range(D_ATTN // LANES)]
    o_by_g = []
    for g in range(N_KV_HEADS):
        q = _stack_queries(q_chunks, g)
        lc = _dot_nt(q, _kv_lane_mask(kc, g)) + bias_c[g]
        ln = _dot_nt(q, _kv_lane_mask(kn_pad, g)) + bias_n[g]
        s = _sink_column(sinks_ref, g, brow)
        m = jnp.maximum(jnp.maximum(jnp.max(lc, axis=-1, keepdims=True),
                                    jnp.max(ln, axis=-1, keepdims=True)), s)
        ec = jnp.exp(lc - m)
        en = jnp.exp(ln - m)
        denom = (jnp.sum(ec, axis=-1, keepdims=True) + jnp.sum(en, axis=-1, keepdims=True)
                 + jnp.exp(s - m))
        o = _dot(ec.astype(BF16), vc) + _dot(en.astype(BF16), vn_pad)
        o_by_g.append(o / denom)
    for c, chunk in enumerate(_unstack_heads(o_by_g, brow)):
        attn_s[pl.ds(row0, brow), c * LANES:(c + 1) * LANES] = chunk

    ks_ref[:, 0:wbuf - t_new, :] = ck[:, t_new:wbuf, :]
    vs_ref[:, 0:wbuf - t_new, :] = cv[:, t_new:wbuf, :]
    ks_ref[:, wbuf - t_new:wbuf, :] = kn.reshape(sb, t_new, D_KV)
    vs_ref[:, wbuf - t_new:wbuf, :] = vn.reshape(sb, t_new, D_KV)

    @pl.when(i == nsteps - 1)
    def _():
        mix = (_dot(attn_s[...].astype(BF16), wout_ref[0:D_ATTN, :])
               + _dot(ssm_s[...].astype(BF16), wout_ref[D_ATTN:D_ATTN + D_SSM, :]))
        o_ref[...] = x_ref[...] + mix


def _mix_sample(x, relb, sinks, g, win, wout, ck, cv, sre0, sim0, lam, bd, cd, dskip, wglu, bglu,
                *, nseq, t_new):
    rows = nseq * t_new
    wbuf = ck.shape[1]
    sb = SUBLANES
    brow = sb * t_new
    return pl.pallas_call(
        functools.partial(_mixs_kernel, nseq=nseq, t_new=t_new, sb=sb, wbuf=wbuf),
        grid=(nseq // sb,),
        in_specs=[
            _smem_spec(), _smem_spec(),
            _const_spec((rows, D_MODEL)),
            _const_spec((1, D_MODEL)),
            _const_spec((D_MODEL, D_IN)),
            _const_spec((D_ATTN + D_SSM, D_MODEL)),
            pl.BlockSpec((sb, wbuf, D_KV), lambda i: (i, 0, 0)),
            pl.BlockSpec((sb, wbuf, D_KV), lambda i: (i, 0, 0)),
            _const_spec((nseq, 2 * HALF_STATE)),
            _const_spec((nseq, 2 * HALF_STATE)),
            _const_spec((4, HALF_STATE)),
            _const_spec((2, HALF_U, 2 * HALF_STATE)),
            _const_spec((2, 2 * HALF_STATE, HALF_U)),
            _const_spec((1, D_SSM)),
            _const_spec((D_SSM, D_SSM)),
            _const_spec((1, D_SSM)),
        ],
        out_specs=[
            pl.BlockSpec((rows, D_MODEL), lambda i: (0, 0)),
            pl.BlockSpec((sb, wbuf, D_KV), lambda i: (i, 0, 0)),
            pl.BlockSpec((sb, wbuf, D_KV), lambda i: (i, 0, 0)),
            pl.BlockSpec((nseq, 2 * HALF_STATE), lambda i: (0, 0)),
            pl.BlockSpec((nseq, 2 * HALF_STATE), lambda i: (0, 0)),
        ],
        out_shape=[
            jax.ShapeDtypeStruct((rows, D_MODEL), F32),
            jax.ShapeDtypeStruct((nseq, wbuf, D_KV), F32),
            jax.ShapeDtypeStruct((nseq, wbuf, D_KV), F32),
            jax.ShapeDtypeStruct((nseq, 2 * HALF_STATE), F32),
            jax.ShapeDtypeStruct((nseq, 2 * HALF_STATE), F32),
        ],
        scratch_shapes=[
            pltpu.VMEM((rows, D_IN), F32),
            pltpu.VMEM((rows, D_ATTN), F32),
            pltpu.VMEM((rows, D_SSM), F32),
            pltpu.VMEM((N_KV_HEADS, N_REP * brow, sb * wbuf), F32),
            pltpu.VMEM((N_KV_HEADS, N_REP * brow, LANES), F32),
            pltpu.VMEM((U_SLABS, rows, LANES), F32),
        ],
        compiler_params=pltpu.CompilerParams(
            dimension_semantics=("arbitrary",), vmem_limit_bytes=VMEM_LIMIT),
        name="mix_sample",
    )(relb, sinks, x, g, win, wout, ck, cv, sre0, sim0, lam, bd, cd, dskip, wglu, bglu)


def _s5_operators(log_dt, a_re, a_im, b_re, b_im, c_re, c_im):
    dt = jnp.exp(log_dt)[:, None]
    mag = jnp.exp(a_re * dt)
    lb_re = mag * jnp.cos(a_im * dt)
    lb_im = mag * jnp.sin(a_im * dt)
    den = a_re * a_re + a_im * a_im
    nr = lb_re - 1.0
    q_re = (nr * a_re + lb_im * a_im) / den
    q_im = (lb_im * a_re - nr * a_im) / den
    bb_re = q_re[..., None] * b_re - q_im[..., None] * b_im
    bb_im = q_re[..., None] * b_im + q_im[..., None] * b_re
    eye = jnp.eye(HALF_GROUPS, dtype=F32)
    lam, bd, cd = [], [], []
    for hf in range(2):
        gs = slice(hf * HALF_GROUPS, (hf + 1) * HALF_GROUPS)
        lam += [lb_re[gs].reshape(1, HALF_STATE), lb_im[gs].reshape(1, HALF_STATE)]
        m_re = jnp.einsum('gpc,gk->gckp', bb_re[gs], eye).reshape(HALF_U, HALF_STATE)
        m_im = jnp.einsum('gpc,gk->gckp', bb_im[gs], eye).reshape(HALF_U, HALF_STATE)
        bd.append(jnp.concatenate([m_re, m_im], axis=1))
        n_re = jnp.einsum('gcp,gk->gpkc', c_re[gs], eye).reshape(HALF_STATE, HALF_U)
        n_im = jnp.einsum('gcp,gk->gpkc', c_im[gs], eye).reshape(HALF_STATE, HALF_U)
        cd.append(jnp.concatenate([n_re, -n_im], axis=0))
    return (jnp.concatenate(lam, axis=0), jnp.stack(bd).astype(BF16), jnp.stack(cd).astype(BF16))


def kernel(x_prompt, x_sample, cache_k, cache_v, state_ssm_re, state_ssm_im, rel_bias,
           ffn1_norm, ffn1_w_gate, ffn1_w_up, ffn1_w_down, mix_norm, w_in, sinks,
           log_dt, a_re, a_im, b_re, b_im, c_re, c_im, d_skip, w_glu, b_glu, w_out,
           ffn2_norm, ffn2_w_gate, ffn2_w_up, ffn2_w_down, final_norm):
    depth = w_in.shape[0]
    assert depth == 1
    batch, seq, _ = x_prompt.shape
    nseq, t_new, _ = x_sample.shape
    wbuf = cache_k.shape[2]
    fg = final_norm.reshape(1, D_MODEL)

    l = 0
    ffn1 = (ffn1_norm[l].reshape(1, D_MODEL), ffn1_w_gate[l], ffn1_w_up[l], ffn1_w_down[l], fg)
    ffn2 = (ffn2_norm[l].reshape(1, D_MODEL), ffn2_w_gate[l], ffn2_w_up[l], ffn2_w_down[l], fg)
    lam, bd, cd = _s5_operators(log_dt[l], a_re[l], a_im[l], b_re[l], b_im[l], c_re[l], c_im[l])
    mix_w = (mix_norm[l].reshape(1, D_MODEL), w_in[l].astype(BF16), w_out[l].astype(BF16))
    ssm_w = (lam, bd, cd, d_skip[l].reshape(1, D_SSM), w_glu[l].astype(BF16),
             b_glu[l].reshape(1, D_SSM))
    sinks_l = sinks[l]

    xp = x_prompt.reshape(batch * seq, D_MODEL)
    xs = x_sample.reshape(nseq * t_new, D_MODEL)
    yp, ys = _ffn(xp, xs, *ffn1, final_norm=False)
    yp, kp, vp, sre_p, sim_p = _mix_prompt(
        yp.reshape(batch, seq, D_MODEL), rel_bias, sinks_l, *mix_w, *ssm_w)
    ys, ks, vs, sre_s, sim_s = _mix_sample(
        ys, rel_bias, sinks_l, *mix_w,
        cache_k[l].reshape(nseq, wbuf, D_KV), cache_v[l].reshape(nseq, wbuf, D_KV),
        state_ssm_re[l].reshape(nseq, 2 * HALF_STATE), state_ssm_im[l].reshape(nseq, 2 * HALF_STATE),
        *ssm_w, nseq=nseq, t_new=t_new)
    y_prompt, y_sample = _ffn(yp.reshape(batch * seq, D_MODEL), ys, *ffn2, final_norm=True)
    y_prompt = y_prompt.reshape(batch, seq, D_MODEL)
    y_sample = y_sample.reshape(nseq, t_new, D_MODEL)

    kv_p = (1, batch, WINDOW, N_KV_HEADS, HEAD_DIM)
    st_p = (1, batch, N_SSM_GROUPS, SSM_STATE)
    kv_s = (1, nseq, wbuf, N_KV_HEADS, HEAD_DIM)
    st_s = (1, nseq, N_SSM_GROUPS, SSM_STATE)
    return (y_prompt, y_sample,
            kp.reshape(kv_p), vp.reshape(kv_p), sre_p.reshape(st_p), sim_p.reshape(st_p),
            ks.reshape(kv_s), vs.reshape(kv_s), sre_s.reshape(st_s), sim_s.reshape(st_s))
```

```python
import functools
import math

import jax
import jax.numpy as jnp
from jax import lax
from jax.experimental import pallas as pl
from jax.experimental.pallas import tpu as pltpu

F32 = jnp.float32
BF16 = jnp.bfloat16

D_MODEL = 1024
HEAD_DIM = 64
D_ATTN = 512
N_HEADS = 8
N_KV_HEADS = 2
N_REP = 4
D_KV = 128
D_SSM = 512
SSM_GROUP = 16
N_SSM_GROUPS = 32
SSM_STATE = 64
WINDOW = 128
NUM_BUCKETS = 32
MAX_DISTANCE = 128
D_FF = 2816
D_IN = D_ATTN + 2 * D_KV + D_SSM
RMS_EPS = 1e-6
NEG_INF = -1e30

LANES = 128
SUBLANES = 8
MXU_DIM = 256
HALF_GROUPS = N_SSM_GROUPS // 2
HALF_STATE = HALF_GROUPS * SSM_STATE
HALF_U = HALF_GROUPS * SSM_GROUP
U_SLABS = D_SSM // LANES
PITCH = WINDOW + SUBLANES
SCAN_STEPS = 16
RING = 3
VMEM_LIMIT = 60 * 1024 * 1024


def _const_spec(shape):
    nd = len(shape)
    return pl.BlockSpec(shape, lambda *_: (0,) * nd, pipeline_mode=pl.Buffered(1))


def _smem_spec():
    return pl.BlockSpec(memory_space=pltpu.SMEM)


def _rmsnorm(x, g):
    r = lax.rsqrt(jnp.mean(x * x, axis=-1, keepdims=True) + RMS_EPS)
    return (x * r) * g


def _dot(a, b):
    return jnp.dot(a, b, preferred_element_type=F32)


def _dot_nt(a, b):
    return lax.dot_general(a, b, (((1,), (1,)), ((), ())), preferred_element_type=F32)


def _ffn_tile(x, g_ref, wg_ref, wu_ref, wd_ref, fg_ref, final_norm):
    h = _rmsnorm(x, g_ref[...]).astype(BF16)
    acc = None
    for c in range(D_FF // MXU_DIM):
        sl = slice(c * MXU_DIM, (c + 1) * MXU_DIM)
        gate = _dot(h, wg_ref[:, sl].astype(BF16))
        up = _dot(h, wu_ref[:, sl].astype(BF16))
        a = (gate * jax.nn.sigmoid(gate) * up).astype(BF16)
        part = _dot(a, wd_ref[sl, :].astype(BF16))
        acc = part if acc is None else acc + part
    y = x + 0.5 * acc
    if final_norm:
        y = _rmsnorm(y, fg_ref[...])
    return y


def _ffn_kernel(xp_ref, xs_ref, g_ref, wg_ref, wu_ref, wd_ref, fg_ref, op_ref, os_ref,
                *, final_norm, prompt_steps):
    i = pl.program_id(0)
    weights = (g_ref, wg_ref, wu_ref, wd_ref, fg_ref)

    @pl.when(i < prompt_steps)
    def _():
        op_ref[...] = _ffn_tile(xp_ref[...], *weights, final_norm)

    @pl.when(i == prompt_steps)
    def _():
        os_ref[...] = _ffn_tile(xs_ref[...], *weights, final_norm)


def _ffn(xp, xs, g, wg, wu, wd, fg, *, final_norm):
    n_p, n_s = xp.shape[0], xs.shape[0]
    tm = n_s
    prompt_steps = n_p // tm
    assert prompt_steps * tm == n_p
    prompt_block = lambda i: (jnp.minimum(i, prompt_steps - 1), 0)
    return pl.pallas_call(
        functools.partial(_ffn_kernel, final_norm=final_norm, prompt_steps=prompt_steps),
        grid=(prompt_steps + 1,),
        in_specs=[
            pl.BlockSpec((tm, D_MODEL), prompt_block),
            _const_spec((n_s, D_MODEL)),
            _const_spec((1, D_MODEL)),
            _const_spec((D_MODEL, D_FF)),
            _const_spec((D_MODEL, D_FF)),
            _const_spec((D_FF, D_MODEL)),
            _const_spec((1, D_MODEL)),
        ],
        out_specs=[
            pl.BlockSpec((tm, D_MODEL), prompt_block),
            pl.BlockSpec((n_s, D_MODEL), lambda i: (0, 0)),
        ],
        out_shape=[
            jax.ShapeDtypeStruct((n_p, D_MODEL), F32),
            jax.ShapeDtypeStruct((n_s, D_MODEL), F32),
        ],
        compiler_params=pltpu.CompilerParams(
            dimension_semantics=("arbitrary",), vmem_limit_bytes=VMEM_LIMIT),
        name="ffn_final" if final_norm else "ffn",
    )(xp, xs, g, wg, wu, wd, fg)


def _t5_bucket(d):
    d = jnp.maximum(d, 0)
    max_exact = NUM_BUCKETS // 2
    df = jnp.maximum(d, 1).astype(F32)
    large = max_exact + (jnp.log(df / max_exact) / math.log(MAX_DISTANCE / max_exact)
                         * (NUM_BUCKETS - max_exact)).astype(jnp.int32)
    large = jnp.minimum(large, NUM_BUCKETS - 1)
    return jnp.where(d < max_exact, d, large)


def _masked_bias(relb_ref, head, d, valid):
    bucket = _t5_bucket(d)
    b = jnp.zeros(d.shape, F32)
    for k in range(NUM_BUCKETS):
        b = jnp.where(bucket == k, relb_ref[k, head], b)
    return jnp.where(valid, b, NEG_INF)


def _kv_lane_mask(t, g):
    lane = lax.broadcasted_iota(jnp.int32, t.shape, 1)
    return jnp.where((lane >= g * HEAD_DIM) & (lane < (g + 1) * HEAD_DIM), t, jnp.zeros_like(t))


def _stack_queries(q_chunks, g):
    qs = []
    for r in range(N_REP):
        qc = q_chunks[2 * g + r // 2]
        if r % 2 != g:
            qc = pltpu.roll(qc, HEAD_DIM, 1)
        qs.append(qc)
    return (jnp.concatenate(qs, axis=0) * (HEAD_DIM ** -0.5)).astype(BF16)


def _sink_column(sinks_ref, g, rows):
    return jnp.concatenate(
        [jnp.full((rows, 1), sinks_ref[N_REP * g + r], F32) for r in range(N_REP)], axis=0)


def _unstack_heads(o_by_g, rows):
    lane = lax.broadcasted_iota(jnp.int32, (rows, LANES), 1)
    chunks = []
    for c in range(D_ATTN // LANES):
        g = c // 2
        halves = []
        for half in range(2):
            r = 2 * (c % 2) + half
            piece = o_by_g[g][r * rows:(r + 1) * rows]
            if half != g:
                piece = pltpu.roll(piece, HEAD_DIM, 1)
            halves.append(piece)
        chunks.append(jnp.where(lane < HEAD_DIM, halves[0], halves[1]))
    return chunks


def _glu_tail(y, u, dskip_ref, wglu_ref, bglu_ref):
    y = y + dskip_ref[...] * u
    y = jax.nn.gelu(y)
    z = _dot(y.astype(BF16), wglu_ref[...]) + bglu_ref[...]
    return y * jax.nn.sigmoid(z)


def _mixp_kernel(relb_ref, sinks_ref, x_ref, g_ref, win_ref, wout_ref, lam_ref, bd_ref, cd_ref,
                 dskip_ref, wglu_ref, bglu_ref,
                 o_ref, kp_ref, vp_ref, sre_ref, sim_ref,
                 p_s, kband, vband_t, bias_t, us, ys, ring, u_ring, state, attn_s,
                 *, nb, blk):
    i = pl.program_id(0)
    rows = nb * blk

    @pl.when(i == 0)
    def _():
        kband[...] = jnp.zeros(kband.shape, BF16)
        vband_t[...] = jnp.zeros(vband_t.shape, BF16)
        state[...] = jnp.zeros(state.shape, F32)

    @pl.when(i <= 1)
    def _():
        kj = lax.broadcasted_iota(jnp.int32, (2 * blk, blk), 0)
        qi = lax.broadcasted_iota(jnp.int32, (2 * blk, blk), 1)
        d = qi - kj + blk
        first_key = jnp.where(i > 0, 0, blk)
        valid = (d >= 0) & (d < WINDOW) & (kj >= first_key)
        for h in range(N_HEADS):
            r = h % N_REP
            bias_t[h // N_REP, :, r * blk:(r + 1) * blk] = _masked_bias(relb_ref, h, d, valid)

    x = x_ref[...].reshape(rows, D_MODEL)
    h = _rmsnorm(x, g_ref[...]).astype(BF16)
    p_s[...] = _dot(h, win_ref[...])

    k3 = p_s[:, D_ATTN:D_ATTN + D_KV].reshape(nb, blk, D_KV)
    kp_ref[...] = k3
    vp_ref[...] = p_s[:, D_ATTN + D_KV:D_ATTN + 2 * D_KV].reshape(nb, blk, D_KV)
    kband[:, blk:2 * blk, :] = k3.astype(BF16)

    u_off = D_ATTN + 2 * D_KV
    for s in range(U_SLABS):
        for b in range(nb):
            us[s, b * PITCH:b * PITCH + blk, :] = (
                p_s[b * blk:(b + 1) * blk, u_off + s * LANES:u_off + (s + 1) * LANES])

    def attention(b):
        rows_b = slice(b * blk, (b + 1) * blk)
        kb = kband[b]
        vband_t[b, :, blk:2 * blk] = p_s[rows_b, D_ATTN + D_KV:D_ATTN + 2 * D_KV].T.astype(BF16)
        vt = vband_t[b]
        feat = lax.broadcasted_iota(jnp.int32, vt.shape, 0)
        q_chunks = [p_s[rows_b, c * LANES:(c + 1) * LANES] for c in range(D_ATTN // LANES)]
        o_by_g = []
        for g in range(N_KV_HEADS):
            q = _stack_queries(q_chunks, g)
            st = _dot_nt(_kv_lane_mask(kb, g), q) + bias_t[g]
            s = jnp.concatenate(
                [jnp.full((1, blk), sinks_ref[N_REP * g + r], F32) for r in range(N_REP)], axis=1)
            m = jnp.maximum(jnp.max(st, axis=0, keepdims=True), s)
            e = jnp.exp(st - m).astype(BF16)
            in_g = (feat >= g * HEAD_DIM) & (feat < (g + 1) * HEAD_DIM)
            ot = _dot(jnp.where(in_g, vt, jnp.ones_like(vt)), e)
            other = (1 - g) * HEAD_DIM
            denom = ot[other:other + 1, :] + jnp.exp(s - m)
            o_by_g.append(ot[g * HEAD_DIM:(g + 1) * HEAD_DIM, :] / denom)
        for c in range(D_ATTN // LANES):
            og = o_by_g[c // 2]
            cc = 2 * (c % 2)
            chunk_t = jnp.concatenate(
                [og[:, cc * blk:(cc + 1) * blk], og[:, (cc + 1) * blk:(cc + 2) * blk]], axis=0)
            attn_s[rows_b, c * LANES:(c + 1) * LANES] = chunk_t.T

    def scan_project_in(j):
        slot = j % RING
        steps = []
        for t in range(SCAN_STEPS):
            steps.append(jnp.concatenate(
                [us[s, pl.ds(j * SCAN_STEPS + t, nb, stride=PITCH), :] for s in range(U_SLABS)],
                axis=1))
        u_chunk = jnp.concatenate(steps, axis=0)
        u_ring[slot] = u_chunk
        ub = u_chunk.astype(BF16)
        for hf in range(2):
            ring[slot, :, 2 * hf * HALF_STATE:2 * (hf + 1) * HALF_STATE] = _dot(
                ub[:, hf * HALF_U:(hf + 1) * HALF_U], bd_ref[hf])

    def scan_recurrence(j):
        slot = j % RING
        for hf in range(2):
            lr = jnp.broadcast_to(lam_ref[2 * hf:2 * hf + 1, :], (nb, HALF_STATE))
            li = jnp.broadcast_to(lam_ref[2 * hf + 1:2 * hf + 2, :], (nb, HALF_STATE))
            re = slice(2 * hf * HALF_STATE, (2 * hf + 1) * HALF_STATE)
            im = slice((2 * hf + 1) * HALF_STATE, (2 * hf + 2) * HALF_STATE)
            xr = state[:, re]
            xi = state[:, im]
            for t in range(SCAN_STEPS):
                rows_t = slice(t * nb, (t + 1) * nb)
                xr, xi = (lr * xr - li * xi + ring[slot, rows_t, re],
                          lr * xi + li * xr + ring[slot, rows_t, im])
                ring[slot, rows_t, re] = xr
                ring[slot, rows_t, im] = xi
            state[:, re] = xr
            state[:, im] = xi

    def scan_project_out(j):
        slot = j % RING
        y = jnp.concatenate(
            [_dot(ring[slot, :, 2 * hf * HALF_STATE:2 * (hf + 1) * HALF_STATE].astype(BF16),
                  cd_ref[hf]) for hf in range(2)], axis=1)
        ssm = _glu_tail(y, u_ring[slot], dskip_ref, wglu_ref, bglu_ref)
        for t in range(SCAN_STEPS):
            for s in range(U_SLABS):
                ys[s, pl.ds(j * SCAN_STEPS + t, nb, stride=PITCH), :] = (
                    ssm[t * nb:(t + 1) * nb, s * LANES:(s + 1) * LANES])

    n_chunks = blk // SCAN_STEPS
    for jj in range(max(n_chunks, nb) + 2):
        if 2 <= jj < n_chunks + 2:
            scan_project_out(jj - 2)
        if 1 <= jj < n_chunks + 1:
            scan_recurrence(jj - 1)
        if jj < n_chunks:
            scan_project_in(jj)
        if jj < nb:
            attention(jj)

    kband[:, 0:blk, :] = kband[:, blk:2 * blk, :]
    vband_t[:, :, 0:blk] = vband_t[:, :, blk:2 * blk]
    for hf in range(2):
        sre_ref[:, hf * HALF_STATE:(hf + 1) * HALF_STATE] = (
            state[:, 2 * hf * HALF_STATE:(2 * hf + 1) * HALF_STATE])
        sim_ref[:, hf * HALF_STATE:(hf + 1) * HALF_STATE] = (
            state[:, (2 * hf + 1) * HALF_STATE:(2 * hf + 2) * HALF_STATE])

    ssm = jnp.concatenate(
        [jnp.concatenate([ys[s, b * PITCH:b * PITCH + blk, :] for b in range(nb)], axis=0)
         for s in range(U_SLABS)], axis=1)
    mix = (_dot(attn_s[...].astype(BF16), wout_ref[0:D_ATTN, :])
           + _dot(ssm.astype(BF16), wout_ref[D_ATTN:D_ATTN + D_SSM, :]))
    o_ref[...] = (x + mix).reshape(nb, blk, D_MODEL)


def _mix_prompt(x, relb, sinks, g, win, wout, lam, bd, cd, dskip, wglu, bglu):
    nb, seq, _ = x.shape
    blk = WINDOW
    rows = nb * blk
    const2 = lambda shape: _const_spec(shape)
    return pl.pallas_call(
        functools.partial(_mixp_kernel, nb=nb, blk=blk),
        grid=(seq // blk,),
        in_specs=[
            _smem_spec(), _smem_spec(),
            pl.BlockSpec((nb, blk, D_MODEL), lambda i: (0, i, 0)),
            const2((1, D_MODEL)),
            const2((D_MODEL, D_IN)),
            const2((D_ATTN + D_SSM, D_MODEL)),
            const2((4, HALF_STATE)),
            const2((2, HALF_U, 2 * HALF_STATE)),
            const2((2, 2 * HALF_STATE, HALF_U)),
            const2((1, D_SSM)),
            const2((D_SSM, D_SSM)),
            const2((1, D_SSM)),
        ],
        out_specs=[
            pl.BlockSpec((nb, blk, D_MODEL), lambda i: (0, i, 0)),
            pl.BlockSpec((nb, blk, D_KV), lambda i: (0, 0, 0)),
            pl.BlockSpec((nb, blk, D_KV), lambda i: (0, 0, 0)),
            pl.BlockSpec((nb, 2 * HALF_STATE), lambda i: (0, 0)),
            pl.BlockSpec((nb, 2 * HALF_STATE), lambda i: (0, 0)),
        ],
        out_shape=[
            jax.ShapeDtypeStruct((nb, seq, D_MODEL), F32),
            jax.ShapeDtypeStruct((nb, blk, D_KV), F32),
            jax.ShapeDtypeStruct((nb, blk, D_KV), F32),
            jax.ShapeDtypeStruct((nb, 2 * HALF_STATE), F32),
            jax.ShapeDtypeStruct((nb, 2 * HALF_STATE), F32),
        ],
        scratch_shapes=[
            pltpu.VMEM((rows, D_IN), F32),
            pltpu.VMEM((nb, 2 * blk, D_KV), BF16),
            pltpu.VMEM((nb, D_KV, 2 * blk), BF16),
            pltpu.VMEM((N_KV_HEADS, 2 * blk, N_REP * blk), F32),
            pltpu.VMEM((U_SLABS, nb * PITCH, LANES), F32),
            pltpu.VMEM((U_SLABS, nb * PITCH, LANES), F32),
            pltpu.VMEM((RING, SCAN_STEPS * nb, 4 * HALF_STATE), F32),
            pltpu.VMEM((RING, SCAN_STEPS * nb, D_SSM), F32),
            pltpu.VMEM((nb, 4 * HALF_STATE), F32),
            pltpu.VMEM((rows, D_ATTN), F32),
        ],
        compiler_params=pltpu.CompilerParams(
            dimension_semantics=("arbitrary",), vmem_limit_bytes=VMEM_LIMIT),
        name="mix_prompt",
    )(relb, sinks, x, g, win, wout, lam, bd, cd, dskip, wglu, bglu)


def _mixs_kernel(relb_ref, sinks_ref, x_ref, g_ref, win_ref, wout_ref, ck_ref, cv_ref,
                 sre0_ref, sim0_ref, lam_ref, bd_ref, cd_ref, dskip_ref, wglu_ref, bglu_ref,
                 o_ref, ks_ref, vs_ref, sre_ref, sim_ref,
                 p_s, attn_s, ssm_s, bias_c, bias_n, us,
                 *, nseq, t_new, sb, wbuf):
    i = pl.program_id(0)
    nsteps = pl.num_programs(0)
    rows = nseq * t_new
    brow = sb * t_new
    u_off = D_ATTN + 2 * D_KV
    tshift = t_new.bit_length() - 1
    sshift = sb.bit_length() - 1
    wshift = wbuf.bit_length() - 1

    @pl.when(i == 0)
    def _():
        x = x_ref[...]
        h = _rmsnorm(x, g_ref[...]).astype(BF16)
        p_s[...] = _dot(h, win_ref[...])

        for s in range(U_SLABS):
            us[s] = p_s[:, u_off + s * LANES:u_off + (s + 1) * LANES]
        uperm = jnp.concatenate(
            [jnp.concatenate([us[s, pl.ds(t, nseq, stride=t_new), :] for t in range(t_new)], axis=0)
             for s in range(U_SLABS)], axis=1)
        yparts = []
        for hf in range(2):
            bu = _dot(uperm[:, hf * HALF_U:(hf + 1) * HALF_U].astype(BF16), bd_ref[hf])
            lr = lam_ref[2 * hf:2 * hf + 1, :]
            li = lam_ref[2 * hf + 1:2 * hf + 2, :]
            xr = sre0_ref[:, hf * HALF_STATE:(hf + 1) * HALF_STATE]
            xi = sim0_ref[:, hf * HALF_STATE:(hf + 1) * HALF_STATE]
            states = []
            for t in range(t_new):
                br = bu[t * nseq:(t + 1) * nseq, 0:HALF_STATE]
                bi = bu[t * nseq:(t + 1) * nseq, HALF_STATE:2 * HALF_STATE]
                xr, xi = lr * xr - li * xi + br, lr * xi + li * xr + bi
                states.append(jnp.concatenate([xr, xi], axis=1).astype(BF16))
            sre_ref[:, hf * HALF_STATE:(hf + 1) * HALF_STATE] = xr
            sim_ref[:, hf * HALF_STATE:(hf + 1) * HALF_STATE] = xi
            yparts.append(_dot(jnp.concatenate(states, axis=0), cd_ref[hf]))
        yperm = jnp.concatenate(yparts, axis=1)
        for s in range(U_SLABS):
            for t in range(t_new):
                us[s, pl.ds(t, nseq, stride=t_new), :] = (
                    yperm[t * nseq:(t + 1) * nseq, s * LANES:(s + 1) * LANES])
        y = jnp.concatenate([us[s] for s in range(U_SLABS)], axis=1)
        ssm_s[...] = _glu_tail(y, p_s[:, u_off:u_off + D_SSM], dskip_ref, wglu_ref, bglu_ref)

        ncol = sb * wbuf
        rho = lax.broadcasted_iota(jnp.int32, (brow, ncol), 0)
        kap = lax.broadcasted_iota(jnp.int32, (brow, ncol), 1)
        tq = rho & (t_new - 1)
        bq = rho >> tshift
        bk = kap >> wshift
        jk = kap & (wbuf - 1)
        d_c = tq - jk + wbuf
        valid_c = (bq == bk) & (d_c >= 0) & (d_c < WINDOW)
        rho_n = lax.broadcasted_iota(jnp.int32, (brow, LANES), 0)
        kap_n = lax.broadcasted_iota(jnp.int32, (brow, LANES), 1)
        tq_n = rho_n & (t_new - 1)
        d_n = tq_n - (kap_n & (t_new - 1))
        valid_n = ((rho_n >> tshift) == (kap_n >> tshift)) & (d_n >= 0) & (kap_n < brow)
        for hd in range(N_HEADS):
            g, r = hd // N_REP, hd % N_REP
            bias_c[g, r * brow:(r + 1) * brow, :] = _masked_bias(relb_ref, hd, d_c, valid_c)
            bias_n[g, r * brow:(r + 1) * brow, :] = _masked_bias(relb_ref, hd, d_n, valid_n)

    row0 = pl.multiple_of(i * brow, brow)
    ck = ck_ref[...]
    cv = cv_ref[...]
    kn = p_s[pl.ds(row0, brow), D_ATTN:D_ATTN + D_KV]
    vn = p_s[pl.ds(row0, brow), D_ATTN + D_KV:D_ATTN + 2 * D_KV]
    pad = jnp.zeros((LANES - brow, D_KV), F32)
    kn_pad = jnp.concatenate([kn, pad], axis=0).astype(BF16)
    vn_pad = jnp.concatenate([vn, pad], axis=0).astype(BF16)
    kc = ck.reshape(sb * wbuf, D_KV).astype(BF16)
    vc = cv.reshape(sb * wbuf, D_KV).astype(BF16)
    q_chunks = [p_s[pl.ds(row0, brow), c * LANES:(c + 1) * LANES] for c in range(D_ATTN // LANES)]
    o_by_g = []
    for g in range(N_KV_HEADS):
        q = _stack_queries(q_chunks, g)
        lc = _dot_nt(q, _kv_lane_mask(kc, g)) + bias_c[g]
        ln = _dot_nt(q, _kv_lane_mask(kn_pad, g)) + bias_n[g]
        s = _sink_column(sinks_ref, g, brow)
        m = jnp.maximum(jnp.maximum(jnp.max(lc, axis=-1, keepdims=True),
                                    jnp.max(ln, axis=-1, keepdims=True)), s)
        ec = jnp.exp(lc - m)
        en = jnp.exp(ln - m)
        denom = (jnp.sum(ec, axis=-1, keepdims=True) + jnp.sum(en, axis=-1, keepdims=True)
                 + jnp.exp(s - m))
        o = _dot(ec.astype(BF16), vc) + _dot(en.astype(BF16), vn_pad)
        o_by_g.append(o / denom)
    for c, chunk in enumerate(_unstack_heads(o_by_g, brow)):
        attn_s[pl.ds(row0, brow), c * LANES:(c + 1) * LANES] = chunk

    ks_ref[:, 0:wbuf - t_new, :] = ck[:, t_new:wbuf, :]
    vs_ref[:, 0:wbuf - t_new, :] = cv[:, t_new:wbuf, :]
    ks_ref[:, wbuf - t_new:wbuf, :] = kn.reshape(sb, t_new, D_KV)
    vs_ref[:, wbuf - t_new:wbuf, :] = vn.reshape(sb, t_new, D_KV)

    @pl.when(i == nsteps - 1)
    def _():
        mix = (_dot(attn_s[...].astype(BF16), wout_ref[0:D_ATTN, :])
               + _dot(ssm_s[...].astype(BF16), wout_ref[D_ATTN:D_ATTN + D_SSM, :]))
        o_ref[...] = x_ref[...] + mix


def _mix_sample(x, relb, sinks, g, win, wout, ck, cv, sre0, sim0, lam, bd, cd, dskip, wglu, bglu,
                *, nseq, t_new):
    rows = nseq * t_new
    wbuf = ck.shape[1]
    sb = SUBLANES
    brow = sb * t_new
    return pl.pallas_call(
        functools.partial(_mixs_kernel, nseq=nseq, t_new=t_new, sb=sb, wbuf=wbuf),
        grid=(nseq // sb,),
        in_specs=[
            _smem_spec(), _smem_spec(),
            _const_spec((rows, D_MODEL)),
            _const_spec((1, D_MODEL)),
            _const_spec((D_MODEL, D_IN)),
            _const_spec((D_ATTN + D_SSM, D_MODEL)),
            pl.BlockSpec((sb, wbuf, D_KV), lambda i: (i, 0, 0)),
            pl.BlockSpec((sb, wbuf, D_KV), lambda i: (i, 0, 0)),
            _const_spec((nseq, 2 * HALF_STATE)),
            _const_spec((nseq, 2 * HALF_STATE)),
            _const_spec((4, HALF_STATE)),
            _const_spec((2, HALF_U, 2 * HALF_STATE)),
            _const_spec((2, 2 * HALF_STATE, HALF_U)),
            _const_spec((1, D_SSM)),
            _const_spec((D_SSM, D_SSM)),
            _const_spec((1, D_SSM)),
        ],
        out_specs=[
            pl.BlockSpec((rows, D_MODEL), lambda i: (0, 0)),
            pl.BlockSpec((sb, wbuf, D_KV), lambda i: (i, 0, 0)),
            pl.BlockSpec((sb, wbuf, D_KV), lambda i: (i, 0, 0)),
            pl.BlockSpec((nseq, 2 * HALF_STATE), lambda i: (0, 0)),
            pl.BlockSpec((nseq, 2 * HALF_STATE), lambda i: (0, 0)),
        ],
        out_shape=[
            jax.ShapeDtypeStruct((rows, D_MODEL), F32),
            jax.ShapeDtypeStruct((nseq, wbuf, D_KV), F32),
            jax.ShapeDtypeStruct((nseq, wbuf, D_KV), F32),
            jax.ShapeDtypeStruct((nseq, 2 * HALF_STATE), F32),
            jax.ShapeDtypeStruct((nseq, 2 * HALF_STATE), F32),
        ],
        scratch_shapes=[
            pltpu.VMEM((rows, D_IN), F32),
            pltpu.VMEM((rows, D_ATTN), F32),
            pltpu.VMEM((rows, D_SSM), F32),
            pltpu.VMEM((N_KV_HEADS, N_REP * brow, sb * wbuf), F32),
            pltpu.VMEM((N_KV_HEADS, N_REP * brow, LANES), F32),
            pltpu.VMEM((U_SLABS, rows, LANES), F32),
        ],
        compiler_params=pltpu.CompilerParams(
            dimension_semantics=("arbitrary",), vmem_limit_bytes=VMEM_LIMIT),
        name="mix_sample",
    )(relb, sinks, x, g, win, wout, ck, cv, sre0, sim0, lam, bd, cd, dskip, wglu, bglu)


def _s5_operators(log_dt, a_re, a_im, b_re, b_im, c_re, c_im):
    dt = jnp.exp(log_dt)[:, None]
    mag = jnp.exp(a_re * dt)
    lb_re = mag * jnp.cos(a_im * dt)
    lb_im = mag * jnp.sin(a_im * dt)
    den = a_re * a_re + a_im * a_im
    nr = lb_re - 1.0
    q_re = (nr * a_re + lb_im * a_im) / den
    q_im = (lb_im * a_re - nr * a_im) / den
    bb_re = q_re[..., None] * b_re - q_im[..., None] * b_im
    bb_im = q_re[..., None] * b_im + q_im[..., None] * b_re
    eye = jnp.eye(HALF_GROUPS, dtype=F32)
    lam, bd, cd = [], [], []
    for hf in range(2):
        gs = slice(hf * HALF_GROUPS, (hf + 1) * HALF_GROUPS)
        lam += [lb_re[gs].reshape(1, HALF_STATE), lb_im[gs].reshape(1, HALF_STATE)]
        m_re = jnp.einsum('gpc,gk->gckp', bb_re[gs], eye).reshape(HALF_U, HALF_STATE)
        m_im = jnp.einsum('gpc,gk->gckp', bb_im[gs], eye).reshape(HALF_U, HALF_STATE)
        bd.append(jnp.concatenate([m_re, m_im], axis=1))
        n_re = jnp.einsum('gcp,gk->gpkc', c_re[gs], eye).reshape(HALF_STATE, HALF_U)
        n_im = jnp.einsum('gcp,gk->gpkc', c_im[gs], eye).reshape(HALF_STATE, HALF_U)
        cd.append(jnp.concatenate([n_re, -n_im], axis=0))
    return (jnp.concatenate(lam, axis=0), jnp.stack(bd).astype(BF16), jnp.stack(cd).astype(BF16))


def kernel(x_prompt, x_sample, cache_k, cache_v, state_ssm_re, state_ssm_im, rel_bias,
           ffn1_norm, ffn1_w_gate, ffn1_w_up, ffn1_w_down, mix_norm, w_in, sinks,
           log_dt, a_re, a_im, b_re, b_im, c_re, c_im, d_skip, w_glu, b_glu, w_out,
           ffn2_norm, ffn2_w_gate, ffn2_w_up, ffn2_w_down, final_norm):
    depth = w_in.shape[0]
    assert depth == 1
    batch, seq, _ = x_prompt.shape
    nseq, t_new, _ = x_sample.shape
    wbuf = cache_k.shape[2]
    fg = final_norm.reshape(1, D_MODEL)

    l = 0
    ffn1 = (ffn1_norm[l].reshape(1, D_MODEL), ffn1_w_gate[l], ffn1_w_up[l], ffn1_w_down[l], fg)
    ffn2 = (ffn2_norm[l].reshape(1, D_MODEL), ffn2_w_gate[l], ffn2_w_up[l], ffn2_w_down[l], fg)
    lam, bd, cd = _s5_operators(log_dt[l], a_re[l], a_im[l], b_re[l], b_im[l], c_re[l], c_im[l])
    mix_w = (mix_norm[l].reshape(1, D_MODEL), w_in[l].astype(BF16), w_out[l].astype(BF16))
    ssm_w = (lam, bd, cd, d_skip[l].reshape(1, D_SSM), w_glu[l].astype(BF16),
             b_glu[l].reshape(1, D_SSM))
    sinks_l = sinks[l]

    xp = x_prompt.reshape(batch * seq, D_MODEL)
    xs = x_sample.reshape(nseq * t_new, D_MODEL)
    yp, ys = _ffn(xp, xs, *ffn1, final_norm=False)
    yp, kp, vp, sre_p, sim_p = _mix_prompt(
        yp.reshape(batch, seq, D_MODEL), rel_bias, sinks_l, *mix_w, *ssm_w)
    ys, ks, vs, sre_s, sim_s = _mix_sample(
        ys, rel_bias, sinks_l, *mix_w,
        cache_k[l].reshape(nseq, wbuf, D_KV), cache_v[l].reshape(nseq, wbuf, D_KV),
        state_ssm_re[l].reshape(nseq, 2 * HALF_STATE), state_ssm_im[l].reshape(nseq, 2 * HALF_STATE),
        *ssm_w, nseq=nseq, t_new=t_new)
    y_prompt, y_sample = _ffn(yp.reshape(batch * seq, D_MODEL), ys, *ffn2, final_norm=True)
    y_prompt = y_prompt.reshape(batch, seq, D_MODEL)
    y_sample = y_sample.reshape(nseq, t_new, D_MODEL)

    kv_p = (1, batch, WINDOW, N_KV_HEADS, HEAD_DIM)
    st_p = (1, batch, N_SSM_GROUPS, SSM_STATE)
    kv_s = (1, nseq, wbuf, N_KV_HEADS, HEAD_DIM)
    st_s = (1, nseq, N_SSM_GROUPS, SSM_STATE)
    return (y_prompt, y_sample,
            kp.reshape(kv_p), vp.reshape(kv_p), sre_p.reshape(st_p), sim_p.reshape(st_p),
            ks.reshape(kv_s), vs.reshape(kv_s), sre_s.reshape(st_s), sim_s.reshape(st_s))
```

```python
import functools
import math

import jax
import jax.numpy as jnp
from jax import lax
from jax.experimental import pallas as pl
from jax.experimental.pallas import tpu as pltpu

F32 = jnp.float32
BF16 = jnp.bfloat16

D_MODEL = 1024
HEAD_DIM = 64
D_ATTN = 512
N_HEADS = 8
N_KV_HEADS = 2
N_REP = 4
D_KV = 128
D_SSM = 512
SSM_GROUP = 16
N_SSM_GROUPS = 32
SSM_STATE = 64
WINDOW = 128
NUM_BUCKETS = 32
MAX_DISTANCE = 128
D_FF = 2816
D_IN = D_ATTN + 2 * D_KV + D_SSM
RMS_EPS = 1e-6
NEG_INF = -1e30

LANES = 128
SUBLANES = 8
MXU_DIM = 256
HALF_GROUPS = N_SSM_GROUPS // 2
HALF_STATE = HALF_GROUPS * SSM_STATE
HALF_U = HALF_GROUPS * SSM_GROUP
U_SLABS = D_SSM // LANES
PITCH = WINDOW + SUBLANES
SCAN_STEPS = 32
RING = 3
VMEM_LIMIT = 60 * 1024 * 1024


def _const_spec(shape):
    nd = len(shape)
    return pl.BlockSpec(shape, lambda *_: (0,) * nd, pipeline_mode=pl.Buffered(1))


def _smem_spec():
    return pl.BlockSpec(memory_space=pltpu.SMEM)


def _rmsnorm(x, g):
    r = lax.rsqrt(jnp.mean(x * x, axis=-1, keepdims=True) + RMS_EPS)
    return (x * r) * g


def _dot(a, b):
    return jnp.dot(a, b, preferred_element_type=F32)


def _dot_nt(a, b):
    return lax.dot_general(a, b, (((1,), (1,)), ((), ())), preferred_element_type=F32)


def _transpose_blocks(a):
    r, c = a.shape
    return jnp.concatenate(
        [jnp.concatenate([a[i * LANES:(i + 1) * LANES, j * LANES:(j + 1) * LANES].T
                          for i in range(r // LANES)], axis=1)
         for j in range(c // LANES)], axis=0)


def _ffn_tile(x, g_ref, wg_ref, wu_ref, wd_ref, fg_ref, final_norm):
    h = _rmsnorm(x, g_ref[...]).astype(BF16)
    acc = None
    for c in range(D_FF // MXU_DIM):
        sl = slice(c * MXU_DIM, (c + 1) * MXU_DIM)
        gate = _dot(h, wg_ref[:, sl].astype(BF16))
        up = _dot(h, wu_ref[:, sl].astype(BF16))
        a = (gate * jax.nn.sigmoid(gate) * up).astype(BF16)
        part = _dot(a, wd_ref[sl, :].astype(BF16))
        acc = part if acc is None else acc + part
    y = x + 0.5 * acc
    if final_norm:
        y = _rmsnorm(y, fg_ref[...])
    return y


def _ffn_kernel(xp_ref, xs_ref, g_ref, wg_ref, wu_ref, wd_ref, fg_ref, op_ref, os_ref,
                *, final_norm, prompt_steps):
    i = pl.program_id(0)
    weights = (g_ref, wg_ref, wu_ref, wd_ref, fg_ref)

    @pl.when(i < prompt_steps)
    def _():
        op_ref[...] = _ffn_tile(xp_ref[...], *weights, final_norm)

    @pl.when(i == prompt_steps)
    def _():
        os_ref[...] = _ffn_tile(xs_ref[...], *weights, final_norm)


def _ffn(xp, xs, g, wg, wu, wd, fg, *, final_norm):
    n_p, n_s = xp.shape[0], xs.shape[0]
    tm = n_s
    prompt_steps = n_p // tm
    assert prompt_steps * tm == n_p
    prompt_block = lambda i: (jnp.minimum(i, prompt_steps - 1), 0)
    return pl.pallas_call(
        functools.partial(_ffn_kernel, final_norm=final_norm, prompt_steps=prompt_steps),
        grid=(prompt_steps + 1,),
        in_specs=[
            pl.BlockSpec((tm, D_MODEL), prompt_block),
            _const_spec((n_s, D_MODEL)),
            _const_spec((1, D_MODEL)),
            _const_spec((D_MODEL, D_FF)),
            _const_spec((D_MODEL, D_FF)),
            _const_spec((D_FF, D_MODEL)),
            _const_spec((1, D_MODEL)),
        ],
        out_specs=[
            pl.BlockSpec((tm, D_MODEL), prompt_block),
            pl.BlockSpec((n_s, D_MODEL), lambda i: (0, 0)),
        ],
        out_shape=[
            jax.ShapeDtypeStruct((n_p, D_MODEL), F32),
            jax.ShapeDtypeStruct((n_s, D_MODEL), F32),
        ],
        compiler_params=pltpu.CompilerParams(
            dimension_semantics=("arbitrary",), vmem_limit_bytes=VMEM_LIMIT),
        name="ffn_final" if final_norm else "ffn",
    )(xp, xs, g, wg, wu, wd, fg)


def _t5_bucket(d):
    d = jnp.maximum(d, 0)
    max_exact = NUM_BUCKETS // 2
    df = jnp.maximum(d, 1).astype(F32)
    large = max_exact + (jnp.log(df / max_exact) / math.log(MAX_DISTANCE / max_exact)
                         * (NUM_BUCKETS - max_exact)).astype(jnp.int32)
    large = jnp.minimum(large, NUM_BUCKETS - 1)
    return jnp.where(d < max_exact, d, large)


def _masked_bias(relb_ref, head, d, valid):
    bucket = _t5_bucket(d)
    b = jnp.zeros(d.shape, F32)
    for k in range(NUM_BUCKETS):
        b = jnp.where(bucket == k, relb_ref[head, k], b)
    return jnp.where(valid, b, NEG_INF)


def _kv_lane_mask(t, g):
    lane = lax.broadcasted_iota(jnp.int32, t.shape, 1)
    return jnp.where((lane >= g * HEAD_DIM) & (lane < (g + 1) * HEAD_DIM), t, jnp.zeros_like(t))


def _stack_queries(q_chunks, g):
    qs = []
    for r in range(N_REP):
        qc = q_chunks[2 * g + r // 2]
        if r % 2 != g:
            qc = pltpu.roll(qc, HEAD_DIM, 1)
        qs.append(qc)
    return (jnp.concatenate(qs, axis=0) * (HEAD_DIM ** -0.5)).astype(BF16)


def _sink_column(sinks_ref, g, rows):
    return jnp.concatenate(
        [jnp.full((rows, 1), sinks_ref[N_REP * g + r], F32) for r in range(N_REP)], axis=0)


def _unstack_heads(o_by_g, rows):
    lane = lax.broadcasted_iota(jnp.int32, (rows, LANES), 1)
    chunks = []
    for c in range(D_ATTN // LANES):
        g = c // 2
        halves = []
        for half in range(2):
            r = 2 * (c % 2) + half
            piece = o_by_g[g][r * rows:(r + 1) * rows]
            if half != g:
                piece = pltpu.roll(piece, HEAD_DIM, 1)
            halves.append(piece)
        chunks.append(jnp.where(lane < HEAD_DIM, halves[0], halves[1]))
    return chunks


def _glu_tail(y, u, dskip_ref, wglu_ref, bglu_ref):
    y = y + dskip_ref[...] * u
    y = jax.nn.gelu(y)
    z = _dot(y.astype(BF16), wglu_ref[...]) + bglu_ref[...]
    return y * jax.nn.sigmoid(z)


def _mixp_kernel(relb_ref, sinks_ref, x_ref, g_ref, win_ref, wout_ref, lam_ref, bd_ref, cd_ref,
                 dskip_ref, wglu_ref, bglu_ref,
                 o_ref, kp_ref, vp_ref, sre_ref, sim_ref,
                 p_s, kband, vband_t, bias_t, us, ys, state, attn_s, *ring_refs, nb, blk):
    i = pl.program_id(0)
    rows = nb * blk
    rings, u_rings = ring_refs[:RING], ring_refs[RING:]

    @pl.when(i == 0)
    def _():
        kband[...] = jnp.zeros(kband.shape, BF16)
        vband_t[...] = jnp.zeros(vband_t.shape, BF16)
        state[...] = jnp.zeros(state.shape, F32)

    @pl.when(i <= 1)
    def _():
        kj = lax.broadcasted_iota(jnp.int32, (2 * blk, blk), 0)
        qi = lax.broadcasted_iota(jnp.int32, (2 * blk, blk), 1)
        d = qi - kj + blk
        first_key = jnp.where(i > 0, 0, blk)
        valid = (d >= 0) & (d < WINDOW) & (kj >= first_key)
        for h in range(N_HEADS):
            r = h % N_REP
            bias_t[h // N_REP, :, r * blk:(r + 1) * blk] = _masked_bias(relb_ref, h, d, valid)

    x = x_ref[...].reshape(rows, D_MODEL)
    h = _rmsnorm(x, g_ref[...]).astype(BF16)
    p_s[...] = _dot(h, win_ref[...])

    kband[:, blk:2 * blk, :] = (
        p_s[:, D_ATTN:D_ATTN + D_KV].reshape(nb, blk, D_KV).astype(BF16))

    @pl.when(i == pl.num_programs(0) - 1)
    def _():
        for b in range(nb):
            rows_b = slice(b * blk, (b + 1) * blk)
            kp_ref[b] = p_s[rows_b, D_ATTN:D_ATTN + D_KV].T
            vp_ref[b] = p_s[rows_b, D_ATTN + D_KV:D_ATTN + 2 * D_KV].T

    u_off = D_ATTN + 2 * D_KV
    for s in range(U_SLABS):
        for b in range(nb):
            us[s, b * PITCH:b * PITCH + blk, :] = (
                p_s[b * blk:(b + 1) * blk, u_off + s * LANES:u_off + (s + 1) * LANES])

    def attention(b):
        rows_b = slice(b * blk, (b + 1) * blk)
        kb = kband[b]
        vband_t[b, :, blk:2 * blk] = p_s[rows_b, D_ATTN + D_KV:D_ATTN + 2 * D_KV].T.astype(BF16)
        vt = vband_t[b]
        feat = lax.broadcasted_iota(jnp.int32, vt.shape, 0)
        q_chunks = [p_s[rows_b, c * LANES:(c + 1) * LANES] for c in range(D_ATTN // LANES)]
        o_by_g = []
        for g in range(N_KV_HEADS):
            q = _stack_queries(q_chunks, g)
            st = _dot_nt(_kv_lane_mask(kb, g), q) + bias_t[g]
            s = jnp.concatenate(
                [jnp.full((1, blk), sinks_ref[N_REP * g + r], F32) for r in range(N_REP)], axis=1)
            m = jnp.maximum(jnp.max(st, axis=0, keepdims=True), s)
            e = jnp.exp(st - m).astype(BF16)
            in_g = (feat >= g * HEAD_DIM) & (feat < (g + 1) * HEAD_DIM)
            ot = _dot(jnp.where(in_g, vt, jnp.ones_like(vt)), e)
            other = (1 - g) * HEAD_DIM
            denom = ot[other:other + 1, :] + jnp.exp(s - m)
            o_by_g.append(ot[g * HEAD_DIM:(g + 1) * HEAD_DIM, :] / denom)
        for c in range(D_ATTN // LANES):
            og = o_by_g[c // 2]
            cc = 2 * (c % 2)
            chunk_t = jnp.concatenate(
                [og[:, cc * blk:(cc + 1) * blk], og[:, (cc + 1) * blk:(cc + 2) * blk]], axis=0)
            attn_s[rows_b, c * LANES:(c + 1) * LANES] = chunk_t.T

    def scan_project_in(j):
        slot = j % RING
        steps = []
        for t in range(SCAN_STEPS):
            steps.append(jnp.concatenate(
                [us[s, pl.ds(j * SCAN_STEPS + t, nb, stride=PITCH), :] for s in range(U_SLABS)],
                axis=1))
        u_chunk = jnp.concatenate(steps, axis=0)
        u_rings[slot][...] = u_chunk
        ub = u_chunk.astype(BF16)
        for hf in range(2):
            rings[slot][:, 2 * hf * HALF_STATE:2 * (hf + 1) * HALF_STATE] = _dot(
                ub[:, hf * HALF_U:(hf + 1) * HALF_U], bd_ref[hf])

    def scan_recurrence(j):
        slot = j % RING
        for hf in range(2):
            lr = jnp.broadcast_to(lam_ref[2 * hf:2 * hf + 1, :], (nb, HALF_STATE))
            li = jnp.broadcast_to(lam_ref[2 * hf + 1:2 * hf + 2, :], (nb, HALF_STATE))
            re = slice(2 * hf * HALF_STATE, (2 * hf + 1) * HALF_STATE)
            im = slice((2 * hf + 1) * HALF_STATE, (2 * hf + 2) * HALF_STATE)
            xr = state[:, re]
            xi = state[:, im]
            for t in range(SCAN_STEPS):
                rows_t = slice(t * nb, (t + 1) * nb)
                xr, xi = (lr * xr - li * xi + rings[slot][rows_t, re],
                          lr * xi + li * xr + rings[slot][rows_t, im])
                rings[slot][rows_t, re] = xr
                rings[slot][rows_t, im] = xi
            state[:, re] = xr
            state[:, im] = xi

    def scan_project_out(j):
        slot = j % RING
        y = jnp.concatenate(
            [_dot(rings[slot][:, 2 * hf * HALF_STATE:2 * (hf + 1) * HALF_STATE].astype(BF16),
                  cd_ref[hf]) for hf in range(2)], axis=1)
        ssm = _glu_tail(y, u_rings[slot][...], dskip_ref, wglu_ref, bglu_ref)
        for t in range(SCAN_STEPS):
            for s in range(U_SLABS):
                ys[s, pl.ds(j * SCAN_STEPS + t, nb, stride=PITCH), :] = (
                    ssm[t * nb:(t + 1) * nb, s * LANES:(s + 1) * LANES])

    n_chunks = blk // SCAN_STEPS
    n_iters = n_chunks + 2
    for jj in range(n_iters):
        if 2 <= jj:
            scan_project_out(jj - 2)
        if 1 <= jj < n_chunks + 1:
            scan_recurrence(jj - 1)
        if jj < n_chunks:
            scan_project_in(jj)
        for b in range(nb):
            if b * n_iters // nb == jj:
                attention(b)

    kband[:, 0:blk, :] = kband[:, blk:2 * blk, :]
    vband_t[:, :, 0:blk] = vband_t[:, :, blk:2 * blk]
    for hf in range(2):
        sre_ref[:, hf * HALF_STATE:(hf + 1) * HALF_STATE] = (
            state[:, 2 * hf * HALF_STATE:(2 * hf + 1) * HALF_STATE])
        sim_ref[:, hf * HALF_STATE:(hf + 1) * HALF_STATE] = (
            state[:, (2 * hf + 1) * HALF_STATE:(2 * hf + 2) * HALF_STATE])

    ssm = jnp.concatenate(
        [jnp.concatenate([ys[s, b * PITCH:b * PITCH + blk, :] for b in range(nb)], axis=0)
         for s in range(U_SLABS)], axis=1)
    mix = (_dot(attn_s[...].astype(BF16), wout_ref[0:D_ATTN, :])
           + _dot(ssm.astype(BF16), wout_ref[D_ATTN:D_ATTN + D_SSM, :]))
    o_ref[...] = (x + mix).reshape(nb, blk, D_MODEL)


def _mix_prompt(x, relb, sinks, g, win, wout, lam, bd, cd, dskip, wglu, bglu):
    nb, seq, _ = x.shape
    blk = WINDOW
    rows = nb * blk
    const2 = lambda shape: _const_spec(shape)
    return pl.pallas_call(
        functools.partial(_mixp_kernel, nb=nb, blk=blk),
        grid=(seq // blk,),
        in_specs=[
            _smem_spec(), _smem_spec(),
            pl.BlockSpec((nb, blk, D_MODEL), lambda i: (0, i, 0)),
            const2((1, D_MODEL)),
            const2((D_MODEL, D_IN)),
            const2((D_ATTN + D_SSM, D_MODEL)),
            const2((4, HALF_STATE)),
            const2((2, HALF_U, 2 * HALF_STATE)),
            const2((2, 2 * HALF_STATE, HALF_U)),
            const2((1, D_SSM)),
            const2((D_SSM, D_SSM)),
            const2((1, D_SSM)),
        ],
        out_specs=[
            pl.BlockSpec((nb, blk, D_MODEL), lambda i: (0, i, 0)),
            pl.BlockSpec((nb, blk, D_KV), lambda i: (0, 0, 0)),
            pl.BlockSpec((nb, blk, D_KV), lambda i: (0, 0, 0)),
            pl.BlockSpec((nb, 2 * HALF_STATE), lambda i: (0, 0)),
            pl.BlockSpec((nb, 2 * HALF_STATE), lambda i: (0, 0)),
        ],
        out_shape=[
            jax.ShapeDtypeStruct((nb, seq, D_MODEL), F32),
            jax.ShapeDtypeStruct((nb, blk, D_KV), F32),
            jax.ShapeDtypeStruct((nb, blk, D_KV), F32),
            jax.ShapeDtypeStruct((nb, 2 * HALF_STATE), F32),
            jax.ShapeDtypeStruct((nb, 2 * HALF_STATE), F32),
        ],
        scratch_shapes=[
            pltpu.VMEM((rows, D_IN), F32),
            pltpu.VMEM((nb, 2 * blk, D_KV), BF16),
            pltpu.VMEM((nb, D_KV, 2 * blk), BF16),
            pltpu.VMEM((N_KV_HEADS, 2 * blk, N_REP * blk), F32),
            pltpu.VMEM((U_SLABS, nb * PITCH, LANES), F32),
            pltpu.VMEM((U_SLABS, nb * PITCH, LANES), F32),
            pltpu.VMEM((nb, 4 * HALF_STATE), F32),
            pltpu.VMEM((rows, D_ATTN), F32),
        ] + [pltpu.VMEM((SCAN_STEPS * nb, 4 * HALF_STATE), F32)] * RING
          + [pltpu.VMEM((SCAN_STEPS * nb, D_SSM), F32)] * RING,

        compiler_params=pltpu.CompilerParams(
            dimension_semantics=("arbitrary",), vmem_limit_bytes=VMEM_LIMIT),
        name="mix_prompt",
    )(relb, sinks, x, g, win, wout, lam, bd, cd, dskip, wglu, bglu)


def _mixs_kernel(relb_ref, sinks_ref, x_ref, g_ref, win_ref, wout_ref, ck_ref, cv_ref,
                 sre0_ref, sim0_ref, lam_ref, bd_ref, cd_ref, dskip_ref, wglu_ref, bglu_ref,
                 o_ref, ks_ref, vs_ref, sre_ref, sim_ref,
                 p_s, attn_s, ssm_s, bias_c, bias_n, us,
                 *, nseq, t_new, sb, wbuf):
    i = pl.program_id(0)
    nsteps = pl.num_programs(0)
    rows = nseq * t_new
    brow = sb * t_new
    u_off = D_ATTN + 2 * D_KV
    tshift = t_new.bit_length() - 1
    sshift = sb.bit_length() - 1
    wshift = wbuf.bit_length() - 1

    @pl.when(i == 0)
    def _():
        x = x_ref[...]
        h = _rmsnorm(x, g_ref[...]).astype(BF16)
        p_s[...] = _dot(h, win_ref[...])

        for s in range(U_SLABS):
            us[s] = p_s[:, u_off + s * LANES:u_off + (s + 1) * LANES]
        uperm = jnp.concatenate(
            [jnp.concatenate([us[s, pl.ds(t, nseq, stride=t_new), :] for t in range(t_new)], axis=0)
             for s in range(U_SLABS)], axis=1)
        yparts = []
        for hf in range(2):
            bu = _dot(uperm[:, hf * HALF_U:(hf + 1) * HALF_U].astype(BF16), bd_ref[hf])
            lr = lam_ref[2 * hf:2 * hf + 1, :]
            li = lam_ref[2 * hf + 1:2 * hf + 2, :]
            xr = _transpose_blocks(sre0_ref[hf * HALF_STATE:(hf + 1) * HALF_STATE, :])
            xi = _transpose_blocks(sim0_ref[hf * HALF_STATE:(hf + 1) * HALF_STATE, :])
            states = []
            for t in range(t_new):
                br = bu[t * nseq:(t + 1) * nseq, 0:HALF_STATE]
                bi = bu[t * nseq:(t + 1) * nseq, HALF_STATE:2 * HALF_STATE]
                xr, xi = lr * xr - li * xi + br, lr * xi + li * xr + bi
                states.append(jnp.concatenate([xr, xi], axis=1).astype(BF16))
            sre_ref[hf * HALF_STATE:(hf + 1) * HALF_STATE, :] = _transpose_blocks(xr)
            sim_ref[hf * HALF_STATE:(hf + 1) * HALF_STATE, :] = _transpose_blocks(xi)
            yparts.append(_dot(jnp.concatenate(states, axis=0), cd_ref[hf]))
        yperm = jnp.concatenate(yparts, axis=1)
        for s in range(U_SLABS):
            for t in range(t_new):
                us[s, pl.ds(t, nseq, stride=t_new), :] = (
                    yperm[t * nseq:(t + 1) * nseq, s * LANES:(s + 1) * LANES])
        y = jnp.concatenate([us[s] for s in range(U_SLABS)], axis=1)
        ssm_s[...] = _glu_tail(y, p_s[:, u_off:u_off + D_SSM], dskip_ref, wglu_ref, bglu_ref)

        ncol = sb * wbuf
        rho = lax.broadcasted_iota(jnp.int32, (brow, ncol), 0)
        kap = lax.broadcasted_iota(jnp.int32, (brow, ncol), 1)
        tq = rho & (t_new - 1)
        bq = rho >> tshift
        bk = kap >> wshift
        jk = kap & (wbuf - 1)
        d_c = tq - jk + wbuf
        valid_c = (bq == bk) & (d_c >= 0) & (d_c < WINDOW)
        rho_n = lax.broadcasted_iota(jnp.int32, (brow, LANES), 0)
        kap_n = lax.broadcasted_iota(jnp.int32, (brow, LANES), 1)
        tq_n = rho_n & (t_new - 1)
        d_n = tq_n - (kap_n & (t_new - 1))
        valid_n = ((rho_n >> tshift) == (kap_n >> tshift)) & (d_n >= 0) & (kap_n < brow)
        for hd in range(N_HEADS):
            g, r = hd // N_REP, hd % N_REP
            bias_c[g, r * brow:(r + 1) * brow, :] = _masked_bias(relb_ref, hd, d_c, valid_c)
            bias_n[g, r * brow:(r + 1) * brow, :] = _masked_bias(relb_ref, hd, d_n, valid_n)

    row0 = pl.multiple_of(i * brow, brow)
    kc_t = jnp.concatenate([ck_ref[b] for b in range(sb)], axis=1).astype(BF16)
    vc_t = jnp.concatenate([cv_ref[b] for b in range(sb)], axis=1).astype(BF16)
    pad = jnp.zeros((LANES - brow, D_KV), F32)
    kn_pad = jnp.concatenate([p_s[pl.ds(row0, brow), D_ATTN:D_ATTN + D_KV], pad], axis=0)
    vn_pad = jnp.concatenate(
        [p_s[pl.ds(row0, brow), D_ATTN + D_KV:D_ATTN + 2 * D_KV], pad], axis=0)
    kn_b = kn_pad.astype(BF16)
    vn_b = vn_pad.astype(BF16)
    q_chunks = [p_s[pl.ds(row0, brow), c * LANES:(c + 1) * LANES] for c in range(D_ATTN // LANES)]
    o_by_g = []
    for g in range(N_KV_HEADS):
        q = _kv_lane_mask(_stack_queries(q_chunks, g), g)
        lc = _dot(q, kc_t) + bias_c[g]
        ln = _dot_nt(q, kn_b) + bias_n[g]
        s = _sink_column(sinks_ref, g, brow)
        m = jnp.maximum(jnp.maximum(jnp.max(lc, axis=-1, keepdims=True),
                                    jnp.max(ln, axis=-1, keepdims=True)), s)
        ec = jnp.exp(lc - m)
        en = jnp.exp(ln - m)
        denom = (jnp.sum(ec, axis=-1, keepdims=True) + jnp.sum(en, axis=-1, keepdims=True)
                 + jnp.exp(s - m))
        o = _dot_nt(ec.astype(BF16), vc_t) + _dot(en.astype(BF16), vn_b)
        o_by_g.append(o / denom)
    for c, chunk in enumerate(_unstack_heads(o_by_g, brow)):
        attn_s[pl.ds(row0, brow), c * LANES:(c + 1) * LANES] = chunk

    lane = lax.broadcasted_iota(jnp.int32, (D_KV, wbuf), 1)
    keep = wbuf - t_new
    for new_pad, old_ref, out_ref in ((kn_pad, ck_ref, ks_ref), (vn_pad, cv_ref, vs_ref)):
        new_t = new_pad.T
        for b in range(sb):
            shifted = pltpu.roll(old_ref[b], keep, 1)
            appended = pltpu.roll(new_t, (keep - b * t_new) % wbuf, 1)
            out_ref[b] = jnp.where(lane >= keep, appended, shifted)

    @pl.when(i == nsteps - 1)
    def _():
        mix = (_dot(attn_s[...].astype(BF16), wout_ref[0:D_ATTN, :])
               + _dot(ssm_s[...].astype(BF16), wout_ref[D_ATTN:D_ATTN + D_SSM, :]))
        o_ref[...] = x_ref[...] + mix


def _mix_sample(x, relb, sinks, g, win, wout, ck, cv, sre0, sim0, lam, bd, cd, dskip, wglu, bglu,
                *, nseq, t_new):
    rows = nseq * t_new
    wbuf = ck.shape[2]
    assert wbuf == LANES
    sb = SUBLANES
    brow = sb * t_new
    return pl.pallas_call(
        functools.partial(_mixs_kernel, nseq=nseq, t_new=t_new, sb=sb, wbuf=wbuf),
        grid=(nseq // sb,),
        in_specs=[
            _smem_spec(), _smem_spec(),
            _const_spec((rows, D_MODEL)),
            _const_spec((1, D_MODEL)),
            _const_spec((D_MODEL, D_IN)),
            _const_spec((D_ATTN + D_SSM, D_MODEL)),
            pl.BlockSpec((sb, D_KV, wbuf), lambda i: (i, 0, 0)),
            pl.BlockSpec((sb, D_KV, wbuf), lambda i: (i, 0, 0)),
            _const_spec((2 * HALF_STATE, nseq)),
            _const_spec((2 * HALF_STATE, nseq)),
            _const_spec((4, HALF_STATE)),
            _const_spec((2, HALF_U, 2 * HALF_STATE)),
            _const_spec((2, 2 * HALF_STATE, HALF_U)),
            _const_spec((1, D_SSM)),
            _const_spec((D_SSM, D_SSM)),
            _const_spec((1, D_SSM)),
        ],
        out_specs=[
            pl.BlockSpec((rows, D_MODEL), lambda i: (0, 0)),
            pl.BlockSpec((sb, D_KV, wbuf), lambda i: (i, 0, 0)),
            pl.BlockSpec((sb, D_KV, wbuf), lambda i: (i, 0, 0)),
            pl.BlockSpec((2 * HALF_STATE, nseq), lambda i: (0, 0)),
            pl.BlockSpec((2 * HALF_STATE, nseq), lambda i: (0, 0)),
        ],
        out_shape=[
            jax.ShapeDtypeStruct((rows, D_MODEL), F32),
            jax.ShapeDtypeStruct((nseq, D_KV, wbuf), F32),
            jax.ShapeDtypeStruct((nseq, D_KV, wbuf), F32),
            jax.ShapeDtypeStruct((2 * HALF_STATE, nseq), F32),
            jax.ShapeDtypeStruct((2 * HALF_STATE, nseq), F32),
        ],
        scratch_shapes=[
            pltpu.VMEM((rows, D_IN), F32),
            pltpu.VMEM((rows, D_ATTN), F32),
            pltpu.VMEM((rows, D_SSM), F32),
            pltpu.VMEM((N_KV_HEADS, N_REP * brow, sb * wbuf), F32),
            pltpu.VMEM((N_KV_HEADS, N_REP * brow, LANES), F32),
            pltpu.VMEM((U_SLABS, rows, LANES), F32),
        ],
        compiler_params=pltpu.CompilerParams(
            dimension_semantics=("arbitrary",), vmem_limit_bytes=VMEM_LIMIT),
        name="mix_sample",
    )(relb, sinks, x, g, win, wout, ck, cv, sre0, sim0, lam, bd, cd, dskip, wglu, bglu)


def _s5_operators(log_dt, a_re, a_im, b_re, b_im, c_re, c_im):
    dt = jnp.exp(log_dt)[:, None]
    mag = jnp.exp(a_re * dt)
    lb_re = mag * jnp.cos(a_im * dt)
    lb_im = mag * jnp.sin(a_im * dt)
    den = a_re * a_re + a_im * a_im
    nr = lb_re - 1.0
    q_re = (nr * a_re + lb_im * a_im) / den
    q_im = (lb_im * a_re - nr * a_im) / den
    bb_re = q_re[..., None] * b_re - q_im[..., None] * b_im
    bb_im = q_re[..., None] * b_im + q_im[..., None] * b_re
    eye = jnp.eye(HALF_GROUPS, dtype=F32)
    lam, bd, cd = [], [], []
    for hf in range(2):
        gs = slice(hf * HALF_GROUPS, (hf + 1) * HALF_GROUPS)
        lam += [lb_re[gs].reshape(1, HALF_STATE), lb_im[gs].reshape(1, HALF_STATE)]
        m_re = jnp.einsum('gpc,gk->gckp', bb_re[gs], eye).reshape(HALF_U, HALF_STATE)
        m_im = jnp.einsum('gpc,gk->gckp', bb_im[gs], eye).reshape(HALF_U, HALF_STATE)
        bd.append(jnp.concatenate([m_re, m_im], axis=1))
        n_re = jnp.einsum('gcp,gk->gpkc', c_re[gs], eye).reshape(HALF_STATE, HALF_U)
        n_im = jnp.einsum('gcp,gk->gpkc', c_im[gs], eye).reshape(HALF_STATE, HALF_U)
        cd.append(jnp.concatenate([n_re, -n_im], axis=0))
    return (jnp.concatenate(lam, axis=0), jnp.stack(bd).astype(BF16), jnp.stack(cd).astype(BF16))


def kernel(x_prompt, x_sample, cache_k, cache_v, state_ssm_re, state_ssm_im, rel_bias,
           ffn1_norm, ffn1_w_gate, ffn1_w_up, ffn1_w_down, mix_norm, w_in, sinks,
           log_dt, a_re, a_im, b_re, b_im, c_re, c_im, d_skip, w_glu, b_glu, w_out,
           ffn2_norm, ffn2_w_gate, ffn2_w_up, ffn2_w_down, final_norm):
    depth = w_in.shape[0]
    assert depth == 1
    batch, seq, _ = x_prompt.shape
    nseq, t_new, _ = x_sample.shape
    wbuf = cache_k.shape[2]
    fg = final_norm.reshape(1, D_MODEL)

    l = 0
    ffn1 = (ffn1_norm[l].reshape(1, D_MODEL), ffn1_w_gate[l], ffn1_w_up[l], ffn1_w_down[l], fg)
    ffn2 = (ffn2_norm[l].reshape(1, D_MODEL), ffn2_w_gate[l], ffn2_w_up[l], ffn2_w_down[l], fg)
    lam, bd, cd = _s5_operators(log_dt[l], a_re[l], a_im[l], b_re[l], b_im[l], c_re[l], c_im[l])
    mix_w = (mix_norm[l].reshape(1, D_MODEL), w_in[l].astype(BF16), w_out[l].astype(BF16))
    ssm_w = (lam, bd, cd, d_skip[l].reshape(1, D_SSM), w_glu[l].astype(BF16),
             b_glu[l].reshape(1, D_SSM))
    sinks_l = sinks[l]

    def window_in(c):
        return jnp.transpose(c, (0, 2, 3, 1)).reshape(c.shape[0], D_KV, c.shape[1])

    def window_out(w):
        n, _, pos = w.shape
        return jnp.transpose(w.reshape(n, N_KV_HEADS, HEAD_DIM, pos), (0, 3, 1, 2))[None]

    def state_in(s):
        return jnp.transpose(s, (1, 2, 0)).reshape(2 * HALF_STATE, s.shape[0])

    def state_out(s):
        return jnp.transpose(s.reshape(N_SSM_GROUPS, SSM_STATE, s.shape[1]), (2, 0, 1))[None]

    relb_t = rel_bias.T
    xp = x_prompt.reshape(batch * seq, D_MODEL)
    xs = x_sample.reshape(nseq * t_new, D_MODEL)
    yp, ys = _ffn(xp, xs, *ffn1, final_norm=False)
    yp, kp, vp, sre_p, sim_p = _mix_prompt(
        yp.reshape(batch, seq, D_MODEL), relb_t, sinks_l, *mix_w, *ssm_w)
    ys, ks, vs, sre_s, sim_s = _mix_sample(
        ys, relb_t, sinks_l, *mix_w,
        window_in(cache_k[l]), window_in(cache_v[l]),
        state_in(state_ssm_re[l]), state_in(state_ssm_im[l]),
        *ssm_w, nseq=nseq, t_new=t_new)
    y_prompt, y_sample = _ffn(yp.reshape(batch * seq, D_MODEL), ys, *ffn2, final_norm=True)
    y_prompt = y_prompt.reshape(batch, seq, D_MODEL)
    y_sample = y_sample.reshape(nseq, t_new, D_MODEL)

    st_p = (1, batch, N_SSM_GROUPS, SSM_STATE)
    return (y_prompt, y_sample,
            window_out(kp), window_out(vp), sre_p.reshape(st_p), sim_p.reshape(st_p),
            window_out(ks), window_out(vs), state_out(sre_s), state_out(sim_s))
```

```python
import functools
import math

import jax
import jax.numpy as jnp
from jax import lax
from jax.experimental import pallas as pl
from jax.experimental.pallas import tpu as pltpu

F32 = jnp.float32
BF16 = jnp.bfloat16

D_MODEL = 1024
HEAD_DIM = 64
D_ATTN = 512
N_HEADS = 8
N_KV_HEADS = 2
N_REP = 4
D_KV = 128
D_SSM = 512
SSM_GROUP = 16
N_SSM_GROUPS = 32
SSM_STATE = 64
WINDOW = 128
NUM_BUCKETS = 32
MAX_DISTANCE = 128
D_FF = 2816
D_IN = D_ATTN + 2 * D_KV + D_SSM
RMS_EPS = 1e-6
NEG_INF = -1e30

LANES = 128
SUBLANES = 8
MXU_DIM = 256
FF_CHUNKS = D_FF // MXU_DIM
HALF_GROUPS = N_SSM_GROUPS // 2
HALF_STATE = HALF_GROUPS * SSM_STATE
HALF_U = HALF_GROUPS * SSM_GROUP
U_SLABS = D_SSM // LANES
PITCH = WINDOW + SUBLANES
SCAN_STEPS = 32
RING = 3
VMEM_LIMIT = 60 * 1024 * 1024


def _const_spec(shape):
    nd = len(shape)
    return pl.BlockSpec(shape, lambda *_: (0,) * nd, pipeline_mode=pl.Buffered(1))


def _smem_spec():
    return pl.BlockSpec(memory_space=pltpu.SMEM)


def _rmsnorm(x, g):
    r = lax.rsqrt(jnp.mean(x * x, axis=-1, keepdims=True) + RMS_EPS)
    return (x * r) * g


def _dot(a, b):
    return jnp.dot(a, b, preferred_element_type=F32)


def _dot_nt(a, b):
    return lax.dot_general(a, b, (((1,), (1,)), ((), ())), preferred_element_type=F32)


def _transpose_blocks(a):
    r, c = a.shape
    return jnp.concatenate(
        [jnp.concatenate([a[i * LANES:(i + 1) * LANES, j * LANES:(j + 1) * LANES].T
                          for i in range(r // LANES)], axis=1)
         for j in range(c // LANES)], axis=0)


def _ffn_tile(x, g_ref, wg_ref, wu_ref, wd_ref, fg_ref, final_norm, before_chunk=None):
    h = _rmsnorm(x, g_ref[...]).astype(BF16)
    acc = None
    for c in range(FF_CHUNKS):
        if before_chunk is not None:
            before_chunk(c)
        sl = slice(c * MXU_DIM, (c + 1) * MXU_DIM)
        gate = _dot(h, wg_ref[:, sl].astype(BF16))
        up = _dot(h, wu_ref[:, sl].astype(BF16))
        a = (gate * jax.nn.sigmoid(gate) * up).astype(BF16)
        part = _dot(a, wd_ref[sl, :].astype(BF16))
        acc = part if acc is None else acc + part
    y = x + 0.5 * acc
    if final_norm:
        y = _rmsnorm(y, fg_ref[...])
    return y


def _ffn_kernel(xs_ref, xp_ref, g_ref, wg_hbm, wu_hbm, wd_hbm, fg_ref, os_ref, op_ref,
                wg_s, wu_s, wd_s, sems, *, final_norm):
    i = pl.program_id(0)
    weights = (g_ref, wg_s, wu_s, wd_s, fg_ref)

    def chunk_copies(c):
        sl = slice(c * MXU_DIM, (c + 1) * MXU_DIM)
        return (pltpu.make_async_copy(wg_hbm.at[:, sl], wg_s.at[:, sl], sems.at[0, c]),
                pltpu.make_async_copy(wu_hbm.at[:, sl], wu_s.at[:, sl], sems.at[1, c]),
                pltpu.make_async_copy(wd_hbm.at[sl, :], wd_s.at[sl, :], sems.at[2, c]))

    @pl.when(i == 0)
    def _():
        for c in range(FF_CHUNKS):
            for cp in chunk_copies(c):
                cp.start()

        def wait_chunk(c):
            for cp in chunk_copies(c):
                cp.wait()

        os_ref[...] = _ffn_tile(xs_ref[...], *weights, final_norm, wait_chunk)

    @pl.when(i > 0)
    def _():
        op_ref[...] = _ffn_tile(xp_ref[...], *weights, final_norm)


def _ffn(xp, xs, g, wg, wu, wd, fg, *, final_norm):
    n_p, n_s = xp.shape[0], xs.shape[0]
    tm = n_s
    prompt_steps = n_p // tm
    assert prompt_steps * tm == n_p
    prompt_block = lambda i: (jnp.maximum(i - 1, 0), 0)
    hbm = pl.BlockSpec(memory_space=pl.ANY)
    return pl.pallas_call(
        functools.partial(_ffn_kernel, final_norm=final_norm),
        grid=(prompt_steps + 1,),
        in_specs=[
            _const_spec((n_s, D_MODEL)),
            pl.BlockSpec((tm, D_MODEL), prompt_block),
            _const_spec((1, D_MODEL)),
            hbm, hbm, hbm,
            _const_spec((1, D_MODEL)),
        ],
        out_specs=[
            pl.BlockSpec((n_s, D_MODEL), lambda i: (0, 0)),
            pl.BlockSpec((tm, D_MODEL), prompt_block),
        ],
        out_shape=[
            jax.ShapeDtypeStruct((n_s, D_MODEL), F32),
            jax.ShapeDtypeStruct((n_p, D_MODEL), F32),
        ],
        scratch_shapes=[
            pltpu.VMEM((D_MODEL, D_FF), F32),
            pltpu.VMEM((D_MODEL, D_FF), F32),
            pltpu.VMEM((D_FF, D_MODEL), F32),
            pltpu.SemaphoreType.DMA((3, FF_CHUNKS)),
        ],
        compiler_params=pltpu.CompilerParams(
            dimension_semantics=("arbitrary",), vmem_limit_bytes=VMEM_LIMIT),
        name="ffn_final" if final_norm else "ffn",
    )(xs, xp, g, wg, wu, wd, fg)


def _t5_bucket(d):
    d = jnp.maximum(d, 0)
    max_exact = NUM_BUCKETS // 2
    df = jnp.maximum(d, 1).astype(F32)
    large = max_exact + (jnp.log(df / max_exact) / math.log(MAX_DISTANCE / max_exact)
                         * (NUM_BUCKETS - max_exact)).astype(jnp.int32)
    large = jnp.minimum(large, NUM_BUCKETS - 1)
    return jnp.where(d < max_exact, d, large)


def _masked_bias(relb_ref, head, d, valid):
    bucket = _t5_bucket(d)
    b = jnp.zeros(d.shape, F32)
    for k in range(NUM_BUCKETS):
        b = jnp.where(bucket == k, relb_ref[head, k], b)
    return jnp.where(valid, b, NEG_INF)


def _kv_lane_mask(t, g):
    lane = lax.broadcasted_iota(jnp.int32, t.shape, 1)
    return jnp.where((lane >= g * HEAD_DIM) & (lane < (g + 1) * HEAD_DIM), t, jnp.zeros_like(t))


def _stack_queries(q_chunks, g):
    qs = []
    for r in range(N_REP):
        qc = q_chunks[2 * g + r // 2]
        if r % 2 != g:
            qc = pltpu.roll(qc, HEAD_DIM, 1)
        qs.append(qc)
    return (jnp.concatenate(qs, axis=0) * (HEAD_DIM ** -0.5)).astype(BF16)


def _sink_column(sinks_ref, g, rows):
    return jnp.concatenate(
        [jnp.full((rows, 1), sinks_ref[N_REP * g + r], F32) for r in range(N_REP)], axis=0)


def _unstack_heads(o_by_g, rows):
    lane = lax.broadcasted_iota(jnp.int32, (rows, LANES), 1)
    chunks = []
    for c in range(D_ATTN // LANES):
        g = c // 2
        halves = []
        for half in range(2):
            r = 2 * (c % 2) + half
            piece = o_by_g[g][r * rows:(r + 1) * rows]
            if half != g:
                piece = pltpu.roll(piece, HEAD_DIM, 1)
            halves.append(piece)
        chunks.append(jnp.where(lane < HEAD_DIM, halves[0], halves[1]))
    return chunks


def _glu_tail(y, u, dskip_ref, wglu_ref, bglu_ref):
    y = y + dskip_ref[...] * u
    y = jax.nn.gelu(y)
    z = _dot(y.astype(BF16), wglu_ref[...]) + bglu_ref[...]
    return y * jax.nn.sigmoid(z)


def _mixp_kernel(relb_ref, sinks_ref, x_ref, g_ref, win_ref, wout_ref, lam_ref, bd_ref, cd_ref,
                 dskip_ref, wglu_ref, bglu_ref,
                 o_ref, kp_ref, vp_ref, sre_ref, sim_ref,
                 p_s, kband, vband_t, bias_t, us, ys, state, attn_s, *ring_refs, nb, blk):
    i = pl.program_id(0)
    rows = nb * blk
    rings, u_rings = ring_refs[:RING], ring_refs[RING:]

    @pl.when(i == 0)
    def _():
        kband[...] = jnp.zeros(kband.shape, BF16)
        vband_t[...] = jnp.zeros(vband_t.shape, BF16)
        state[...] = jnp.zeros(state.shape, F32)

    @pl.when(i <= 1)
    def _():
        kj = lax.broadcasted_iota(jnp.int32, (2 * blk, blk), 0)
        qi = lax.broadcasted_iota(jnp.int32, (2 * blk, blk), 1)
        d = qi - kj + blk
        first_key = jnp.where(i > 0, 0, blk)
        valid = (d >= 0) & (d < WINDOW) & (kj >= first_key)
        for h in range(N_HEADS):
            r = h % N_REP
            bias_t[h // N_REP, :, r * blk:(r + 1) * blk] = _masked_bias(relb_ref, h, d, valid)

    x = x_ref[...].reshape(rows, D_MODEL)
    h = _rmsnorm(x, g_ref[...]).astype(BF16)
    p_s[...] = _dot(h, win_ref[...])

    kband[:, blk:2 * blk, :] = (
        p_s[:, D_ATTN:D_ATTN + D_KV].reshape(nb, blk, D_KV).astype(BF16))

    @pl.when(i == pl.num_programs(0) - 1)
    def _():
        for b in range(nb):
            rows_b = slice(b * blk, (b + 1) * blk)
            kp_ref[b] = p_s[rows_b, D_ATTN:D_ATTN + D_KV].T
            vp_ref[b] = p_s[rows_b, D_ATTN + D_KV:D_ATTN + 2 * D_KV].T

    u_off = D_ATTN + 2 * D_KV
    for s in range(U_SLABS):
        for b in range(nb):
            us[s, b * PITCH:b * PITCH + blk, :] = (
                p_s[b * blk:(b + 1) * blk, u_off + s * LANES:u_off + (s + 1) * LANES])

    def attention(b):
        rows_b = slice(b * blk, (b + 1) * blk)
        kb = kband[b]
        vband_t[b, :, blk:2 * blk] = p_s[rows_b, D_ATTN + D_KV:D_ATTN + 2 * D_KV].T.astype(BF16)
        vt = vband_t[b]
        feat = lax.broadcasted_iota(jnp.int32, vt.shape, 0)
        q_chunks = [p_s[rows_b, c * LANES:(c + 1) * LANES] for c in range(D_ATTN // LANES)]
        o_by_g = []
        for g in range(N_KV_HEADS):
            q = _stack_queries(q_chunks, g)
            st = _dot_nt(_kv_lane_mask(kb, g), q) + bias_t[g]
            s = jnp.concatenate(
                [jnp.full((1, blk), sinks_ref[N_REP * g + r], F32) for r in range(N_REP)], axis=1)
            m = jnp.maximum(jnp.max(st, axis=0, keepdims=True), s)
            e = jnp.exp(st - m).astype(BF16)
            in_g = (feat >= g * HEAD_DIM) & (feat < (g + 1) * HEAD_DIM)
            ot = _dot(jnp.where(in_g, vt, jnp.ones_like(vt)), e)
            other = (1 - g) * HEAD_DIM
            denom = ot[other:other + 1, :] + jnp.exp(s - m)
            o_by_g.append(ot[g * HEAD_DIM:(g + 1) * HEAD_DIM, :] / denom)
        for c in range(D_ATTN // LANES):
            og = o_by_g[c // 2]
            cc = 2 * (c % 2)
            chunk_t = jnp.concatenate(
                [og[:, cc * blk:(cc + 1) * blk], og[:, (cc + 1) * blk:(cc + 2) * blk]], axis=0)
            attn_s[rows_b, c * LANES:(c + 1) * LANES] = chunk_t.T

    def scan_project_in(j):
        slot = j % RING
        steps = []
        for t in range(SCAN_STEPS):
            steps.append(jnp.concatenate(
                [us[s, pl.ds(j * SCAN_STEPS + t, nb, stride=PITCH), :] for s in range(U_SLABS)],
                axis=1))
        u_chunk = jnp.concatenate(steps, axis=0)
        u_rings[slot][...] = u_chunk
        ub = u_chunk.astype(BF16)
        for hf in range(2):
            rings[slot][:, 2 * hf * HALF_STATE:2 * (hf + 1) * HALF_STATE] = _dot(
                ub[:, hf * HALF_U:(hf + 1) * HALF_U], bd_ref[hf])

    def scan_recurrence(j):
        slot = j % RING
        for hf in range(2):
            lr = jnp.broadcast_to(lam_ref[2 * hf:2 * hf + 1, :], (nb, HALF_STATE))
            li = jnp.broadcast_to(lam_ref[2 * hf + 1:2 * hf + 2, :], (nb, HALF_STATE))
            re = slice(2 * hf * HALF_STATE, (2 * hf + 1) * HALF_STATE)
            im = slice((2 * hf + 1) * HALF_STATE, (2 * hf + 2) * HALF_STATE)
            xr = state[:, re]
            xi = state[:, im]
            for t in range(SCAN_STEPS):
                rows_t = slice(t * nb, (t + 1) * nb)
                xr, xi = (lr * xr - li * xi + rings[slot][rows_t, re],
                          lr * xi + li * xr + rings[slot][rows_t, im])
                rings[slot][rows_t, re] = xr
                rings[slot][rows_t, im] = xi
            state[:, re] = xr
            state[:, im] = xi

    def scan_project_out(j):
        slot = j % RING
        y = jnp.concatenate(
            [_dot(rings[slot][:, 2 * hf * HALF_STATE:2 * (hf + 1) * HALF_STATE].astype(BF16),
                  cd_ref[hf]) for hf in range(2)], axis=1)
        ssm = _glu_tail(y, u_rings[slot][...], dskip_ref, wglu_ref, bglu_ref)
        for t in range(SCAN_STEPS):
            for s in range(U_SLABS):
                ys[s, pl.ds(j * SCAN_STEPS + t, nb, stride=PITCH), :] = (
                    ssm[t * nb:(t + 1) * nb, s * LANES:(s + 1) * LANES])

    n_chunks = blk // SCAN_STEPS
    n_iters = n_chunks + 2
    for jj in range(n_iters):
        if 2 <= jj:
            scan_project_out(jj - 2)
        if 1 <= jj < n_chunks + 1:
            scan_recurrence(jj - 1)
        if jj < n_chunks:
            scan_project_in(jj)
        for b in range(nb):
            if b * n_iters // nb == jj:
                attention(b)

    kband[:, 0:blk, :] = kband[:, blk:2 * blk, :]
    vband_t[:, :, 0:blk] = vband_t[:, :, blk:2 * blk]
    for hf in range(2):
        sre_ref[:, hf * HALF_STATE:(hf + 1) * HALF_STATE] = (
            state[:, 2 * hf * HALF_STATE:(2 * hf + 1) * HALF_STATE])
        sim_ref[:, hf * HALF_STATE:(hf + 1) * HALF_STATE] = (
            state[:, (2 * hf + 1) * HALF_STATE:(2 * hf + 2) * HALF_STATE])

    ssm = jnp.concatenate(
        [jnp.concatenate([ys[s, b * PITCH:b * PITCH + blk, :] for b in range(nb)], axis=0)
         for s in range(U_SLABS)], axis=1)
    mix = (_dot(attn_s[...].astype(BF16), wout_ref[0:D_ATTN, :])
           + _dot(ssm.astype(BF16), wout_ref[D_ATTN:D_ATTN + D_SSM, :]))
    o_ref[...] = (x + mix).reshape(nb, blk, D_MODEL)


def _mix_prompt(x, relb, sinks, g, win, wout, lam, bd, cd, dskip, wglu, bglu):
    nb, seq, _ = x.shape
    blk = WINDOW
    rows = nb * blk
    const2 = lambda shape: _const_spec(shape)
    return pl.pallas_call(
        functools.partial(_mixp_kernel, nb=nb, blk=blk),
        grid=(seq // blk,),
        in_specs=[
            _smem_spec(), _smem_spec(),
            pl.BlockSpec((nb, blk, D_MODEL), lambda i: (0, i, 0)),
            const2((1, D_MODEL)),
            const2((D_MODEL, D_IN)),
            const2((D_ATTN + D_SSM, D_MODEL)),
            const2((4, HALF_STATE)),
            const2((2, HALF_U, 2 * HALF_STATE)),
            const2((2, 2 * HALF_STATE, HALF_U)),
            const2((1, D_SSM)),
            const2((D_SSM, D_SSM)),
            const2((1, D_SSM)),
        ],
        out_specs=[
            pl.BlockSpec((nb, blk, D_MODEL), lambda i: (0, i, 0)),
            pl.BlockSpec((nb, blk, D_KV), lambda i: (0, 0, 0)),
            pl.BlockSpec((nb, blk, D_KV), lambda i: (0, 0, 0)),
            pl.BlockSpec((nb, 2 * HALF_STATE), lambda i: (0, 0)),
            pl.BlockSpec((nb, 2 * HALF_STATE), lambda i: (0, 0)),
        ],
        out_shape=[
            jax.ShapeDtypeStruct((nb, seq, D_MODEL), F32),
            jax.ShapeDtypeStruct((nb, blk, D_KV), F32),
            jax.ShapeDtypeStruct((nb, blk, D_KV), F32),
            jax.ShapeDtypeStruct((nb, 2 * HALF_STATE), F32),
            jax.ShapeDtypeStruct((nb, 2 * HALF_STATE), F32),
        ],
        scratch_shapes=[
            pltpu.VMEM((rows, D_IN), F32),
            pltpu.VMEM((nb, 2 * blk, D_KV), BF16),
            pltpu.VMEM((nb, D_KV, 2 * blk), BF16),
            pltpu.VMEM((N_KV_HEADS, 2 * blk, N_REP * blk), F32),
            pltpu.VMEM((U_SLABS, nb * PITCH, LANES), F32),
            pltpu.VMEM((U_SLABS, nb * PITCH, LANES), F32),
            pltpu.VMEM((nb, 4 * HALF_STATE), F32),
            pltpu.VMEM((rows, D_ATTN), F32),
        ] + [pltpu.VMEM((SCAN_STEPS * nb, 4 * HALF_STATE), F32)] * RING
          + [pltpu.VMEM((SCAN_STEPS * nb, D_SSM), F32)] * RING,

        compiler_params=pltpu.CompilerParams(
            dimension_semantics=("arbitrary",), vmem_limit_bytes=VMEM_LIMIT),
        name="mix_prompt",
    )(relb, sinks, x, g, win, wout, lam, bd, cd, dskip, wglu, bglu)


def _mixs_kernel(relb_ref, sinks_ref, x_ref, g_ref, win_ref, wout_ref, ck_ref, cv_ref,
                 sre0_ref, sim0_ref, lam_ref, bd_ref, cd_ref, dskip_ref, wglu_ref, bglu_ref,
                 o_ref, ks_ref, vs_ref, sre_ref, sim_ref,
                 p_s, attn_s, ssm_s, bias_c, bias_n, us,
                 *, nseq, t_new, sb, wbuf):
    i = pl.program_id(0)
    nsteps = pl.num_programs(0)
    rows = nseq * t_new
    brow = sb * t_new
    u_off = D_ATTN + 2 * D_KV
    tshift = t_new.bit_length() - 1
    sshift = sb.bit_length() - 1
    wshift = wbuf.bit_length() - 1

    @pl.when(i == 0)
    def _():
        x = x_ref[...]
        h = _rmsnorm(x, g_ref[...]).astype(BF16)
        p_s[...] = _dot(h, win_ref[...])

        for s in range(U_SLABS):
            us[s] = p_s[:, u_off + s * LANES:u_off + (s + 1) * LANES]
        uperm = jnp.concatenate(
            [jnp.concatenate([us[s, pl.ds(t, nseq, stride=t_new), :] for t in range(t_new)], axis=0)
             for s in range(U_SLABS)], axis=1)
        yparts = []
        for hf in range(2):
            bu = _dot(uperm[:, hf * HALF_U:(hf + 1) * HALF_U].astype(BF16), bd_ref[hf])
            lr = lam_ref[2 * hf:2 * hf + 1, :]
            li = lam_ref[2 * hf + 1:2 * hf + 2, :]
            xr = _transpose_blocks(sre0_ref[hf * HALF_STATE:(hf + 1) * HALF_STATE, :])
            xi = _transpose_blocks(sim0_ref[hf * HALF_STATE:(hf + 1) * HALF_STATE, :])
            states = []
            for t in range(t_new):
                br = bu[t * nseq:(t + 1) * nseq, 0:HALF_STATE]
                bi = bu[t * nseq:(t + 1) * nseq, HALF_STATE:2 * HALF_STATE]
                xr, xi = lr * xr - li * xi + br, lr * xi + li * xr + bi
                states.append(jnp.concatenate([xr, xi], axis=1).astype(BF16))
            sre_ref[hf * HALF_STATE:(hf + 1) * HALF_STATE, :] = _transpose_blocks(xr)
            sim_ref[hf * HALF_STATE:(hf + 1) * HALF_STATE, :] = _transpose_blocks(xi)
            yparts.append(_dot(jnp.concatenate(states, axis=0), cd_ref[hf]))
        yperm = jnp.concatenate(yparts, axis=1)
        for s in range(U_SLABS):
            for t in range(t_new):
                us[s, pl.ds(t, nseq, stride=t_new), :] = (
                    yperm[t * nseq:(t + 1) * nseq, s * LANES:(s + 1) * LANES])
        y = jnp.concatenate([us[s] for s in range(U_SLABS)], axis=1)
        ssm_s[...] = _glu_tail(y, p_s[:, u_off:u_off + D_SSM], dskip_ref, wglu_ref, bglu_ref)

        ncol = sb * wbuf
        rho = lax.broadcasted_iota(jnp.int32, (brow, ncol), 0)
        kap = lax.broadcasted_iota(jnp.int32, (brow, ncol), 1)
        tq = rho & (t_new - 1)
        bq = rho >> tshift
        bk = kap >> wshift
        jk = kap & (wbuf - 1)
        d_c = tq - jk + wbuf
        valid_c = (bq == bk) & (d_c >= 0) & (d_c < WINDOW)
        rho_n = lax.broadcasted_iota(jnp.int32, (brow, LANES), 0)
        kap_n = lax.broadcasted_iota(jnp.int32, (brow, LANES), 1)
        tq_n = rho_n & (t_new - 1)
        d_n = tq_n - (kap_n & (t_new - 1))
        valid_n = ((rho_n >> tshift) == (kap_n >> tshift)) & (d_n >= 0) & (kap_n < brow)
        for hd in range(N_HEADS):
            g, r = hd // N_REP, hd % N_REP
            bias_c[g, r * brow:(r + 1) * brow, :] = _masked_bias(relb_ref, hd, d_c, valid_c)
            bias_n[g, r * brow:(r + 1) * brow, :] = _masked_bias(relb_ref, hd, d_n, valid_n)

    row0 = pl.multiple_of(i * brow, brow)
    kc_t = jnp.concatenate([ck_ref[b] for b in range(sb)], axis=1).astype(BF16)
    vc_t = jnp.concatenate([cv_ref[b] for b in range(sb)], axis=1).astype(BF16)
    pad = jnp.zeros((LANES - brow, D_KV), F32)
    kn_pad = jnp.concatenate([p_s[pl.ds(row0, brow), D_ATTN:D_ATTN + D_KV], pad], axis=0)
    vn_pad = jnp.concatenate(
        [p_s[pl.ds(row0, brow), D_ATTN + D_KV:D_ATTN + 2 * D_KV], pad], axis=0)
    kn_b = kn_pad.astype(BF16)
    vn_b = vn_pad.astype(BF16)
    q_chunks = [p_s[pl.ds(row0, brow), c * LANES:(c + 1) * LANES] for c in range(D_ATTN // LANES)]
    o_by_g = []
    for g in range(N_KV_HEADS):
        q = _kv_lane_mask(_stack_queries(q_chunks, g), g)
        lc = _dot(q, kc_t) + bias_c[g]
        ln = _dot_nt(q, kn_b) + bias_n[g]
        s = _sink_column(sinks_ref, g, brow)
        m = jnp.maximum(jnp.maximum(jnp.max(lc, axis=-1, keepdims=True),
                                    jnp.max(ln, axis=-1, keepdims=True)), s)
        ec = jnp.exp(lc - m)
        en = jnp.exp(ln - m)
        denom = (jnp.sum(ec, axis=-1, keepdims=True) + jnp.sum(en, axis=-1, keepdims=True)
                 + jnp.exp(s - m))
        o = _dot_nt(ec.astype(BF16), vc_t) + _dot(en.astype(BF16), vn_b)
        o_by_g.append(o / denom)
    for c, chunk in enumerate(_unstack_heads(o_by_g, brow)):
        attn_s[pl.ds(row0, brow), c * LANES:(c + 1) * LANES] = chunk

    lane = lax.broadcasted_iota(jnp.int32, (D_KV, wbuf), 1)
    keep = wbuf - t_new
    for new_pad, old_ref, out_ref in ((kn_pad, ck_ref, ks_ref), (vn_pad, cv_ref, vs_ref)):
        new_t = new_pad.T
        for b in range(sb):
            shifted = pltpu.roll(old_ref[b], keep, 1)
            appended = pltpu.roll(new_t, (keep - b * t_new) % wbuf, 1)
            out_ref[b] = jnp.where(lane >= keep, appended, shifted)

    @pl.when(i == nsteps - 1)
    def _():
        mix = (_dot(attn_s[...].astype(BF16), wout_ref[0:D_ATTN, :])
               + _dot(ssm_s[...].astype(BF16), wout_ref[D_ATTN:D_ATTN + D_SSM, :]))
        o_ref[...] = x_ref[...] + mix


def _mix_sample(x, relb, sinks, g, win, wout, ck, cv, sre0, sim0, lam, bd, cd, dskip, wglu, bglu,
                *, nseq, t_new):
    rows = nseq * t_new
    wbuf = ck.shape[2]
    assert wbuf == LANES
    sb = SUBLANES
    brow = sb * t_new
    return pl.pallas_call(
        functools.partial(_mixs_kernel, nseq=nseq, t_new=t_new, sb=sb, wbuf=wbuf),
        grid=(nseq // sb,),
        in_specs=[
            _smem_spec(), _smem_spec(),
            _const_spec((rows, D_MODEL)),
            _const_spec((1, D_MODEL)),
            _const_spec((D_MODEL, D_IN)),
            _const_spec((D_ATTN + D_SSM, D_MODEL)),
            pl.BlockSpec((sb, D_KV, wbuf), lambda i: (i, 0, 0)),
            pl.BlockSpec((sb, D_KV, wbuf), lambda i: (i, 0, 0)),
            _const_spec((2 * HALF_STATE, nseq)),
            _const_spec((2 * HALF_STATE, nseq)),
            _const_spec((4, HALF_STATE)),
            _const_spec((2, HALF_U, 2 * HALF_STATE)),
            _const_spec((2, 2 * HALF_STATE, HALF_U)),
            _const_spec((1, D_SSM)),
            _const_spec((D_SSM, D_SSM)),
            _const_spec((1, D_SSM)),
        ],
        out_specs=[
            pl.BlockSpec((rows, D_MODEL), lambda i: (0, 0)),
            pl.BlockSpec((sb, D_KV, wbuf), lambda i: (i, 0, 0)),
            pl.BlockSpec((sb, D_KV, wbuf), lambda i: (i, 0, 0)),
            pl.BlockSpec((2 * HALF_STATE, nseq), lambda i: (0, 0)),
            pl.BlockSpec((2 * HALF_STATE, nseq), lambda i: (0, 0)),
        ],
        out_shape=[
            jax.ShapeDtypeStruct((rows, D_MODEL), F32),
            jax.ShapeDtypeStruct((nseq, D_KV, wbuf), F32),
            jax.ShapeDtypeStruct((nseq, D_KV, wbuf), F32),
            jax.ShapeDtypeStruct((2 * HALF_STATE, nseq), F32),
            jax.ShapeDtypeStruct((2 * HALF_STATE, nseq), F32),
        ],
        scratch_shapes=[
            pltpu.VMEM((rows, D_IN), F32),
            pltpu.VMEM((rows, D_ATTN), F32),
            pltpu.VMEM((rows, D_SSM), F32),
            pltpu.VMEM((N_KV_HEADS, N_REP * brow, sb * wbuf), F32),
            pltpu.VMEM((N_KV_HEADS, N_REP * brow, LANES), F32),
            pltpu.VMEM((U_SLABS, rows, LANES), F32),
        ],
        compiler_params=pltpu.CompilerParams(
            dimension_semantics=("arbitrary",), vmem_limit_bytes=VMEM_LIMIT),
        name="mix_sample",
    )(relb, sinks, x, g, win, wout, ck, cv, sre0, sim0, lam, bd, cd, dskip, wglu, bglu)


def _s5_operators(log_dt, a_re, a_im, b_re, b_im, c_re, c_im):
    dt = jnp.exp(log_dt)[:, None]
    mag = jnp.exp(a_re * dt)
    lb_re = mag * jnp.cos(a_im * dt)
    lb_im = mag * jnp.sin(a_im * dt)
    den = a_re * a_re + a_im * a_im
    nr = lb_re - 1.0
    q_re = (nr * a_re + lb_im * a_im) / den
    q_im = (lb_im * a_re - nr * a_im) / den
    bb_re = q_re[..., None] * b_re - q_im[..., None] * b_im
    bb_im = q_re[..., None] * b_im + q_im[..., None] * b_re
    eye = jnp.eye(HALF_GROUPS, dtype=F32)
    lam, bd, cd = [], [], []
    for hf in range(2):
        gs = slice(hf * HALF_GROUPS, (hf + 1) * HALF_GROUPS)
        lam += [lb_re[gs].reshape(1, HALF_STATE), lb_im[gs].reshape(1, HALF_STATE)]
        m_re = jnp.einsum('gpc,gk->gckp', bb_re[gs], eye).reshape(HALF_U, HALF_STATE)
        m_im = jnp.einsum('gpc,gk->gckp', bb_im[gs], eye).reshape(HALF_U, HALF_STATE)
        bd.append(jnp.concatenate([m_re, m_im], axis=1))
        n_re = jnp.einsum('gcp,gk->gpkc', c_re[gs], eye).reshape(HALF_STATE, HALF_U)
        n_im = jnp.einsum('gcp,gk->gpkc', c_im[gs], eye).reshape(HALF_STATE, HALF_U)
        cd.append(jnp.concatenate([n_re, -n_im], axis=0))
    return (jnp.concatenate(lam, axis=0), jnp.stack(bd).astype(BF16), jnp.stack(cd).astype(BF16))


def kernel(x_prompt, x_sample, cache_k, cache_v, state_ssm_re, state_ssm_im, rel_bias,
           ffn1_norm, ffn1_w_gate, ffn1_w_up, ffn1_w_down, mix_norm, w_in, sinks,
           log_dt, a_re, a_im, b_re, b_im, c_re, c_im, d_skip, w_glu, b_glu, w_out,
           ffn2_norm, ffn2_w_gate, ffn2_w_up, ffn2_w_down, final_norm):
    depth = w_in.shape[0]
    assert depth == 1
    batch, seq, _ = x_prompt.shape
    nseq, t_new, _ = x_sample.shape
    wbuf = cache_k.shape[2]
    fg = final_norm.reshape(1, D_MODEL)

    l = 0
    ffn1 = (ffn1_norm[l].reshape(1, D_MODEL), ffn1_w_gate[l], ffn1_w_up[l], ffn1_w_down[l], fg)
    ffn2 = (ffn2_norm[l].reshape(1, D_MODEL), ffn2_w_gate[l], ffn2_w_up[l], ffn2_w_down[l], fg)
    lam, bd, cd = _s5_operators(log_dt[l], a_re[l], a_im[l], b_re[l], b_im[l], c_re[l], c_im[l])
    mix_w = (mix_norm[l].reshape(1, D_MODEL), w_in[l].astype(BF16), w_out[l].astype(BF16))
    ssm_w = (lam, bd, cd, d_skip[l].reshape(1, D_SSM), w_glu[l].astype(BF16),
             b_glu[l].reshape(1, D_SSM))
    sinks_l = sinks[l]

    def window_in(c):
        return jnp.transpose(c, (0, 2, 3, 1)).reshape(c.shape[0], D_KV, c.shape[1])

    def window_out(w):
        n, _, pos = w.shape
        return jnp.transpose(w.reshape(n, N_KV_HEADS, HEAD_DIM, pos), (0, 3, 1, 2))[None]

    def state_in(s):
        return jnp.transpose(s, (1, 2, 0)).reshape(2 * HALF_STATE, s.shape[0])

    def state_out(s):
        return jnp.transpose(s.reshape(N_SSM_GROUPS, SSM_STATE, s.shape[1]), (2, 0, 1))[None]

    relb_t = rel_bias.T
    xp = x_prompt.reshape(batch * seq, D_MODEL)
    xs = x_sample.reshape(nseq * t_new, D_MODEL)
    ys, yp = _ffn(xp, xs, *ffn1, final_norm=False)
    yp, kp, vp, sre_p, sim_p = _mix_prompt(
        yp.reshape(batch, seq, D_MODEL), relb_t, sinks_l, *mix_w, *ssm_w)
    ys, ks, vs, sre_s, sim_s = _mix_sample(
        ys, relb_t, sinks_l, *mix_w,
        window_in(cache_k[l]), window_in(cache_v[l]),
        state_in(state_ssm_re[l]), state_in(state_ssm_im[l]),
        *ssm_w, nseq=nseq, t_new=t_new)
    y_sample, y_prompt = _ffn(yp.reshape(batch * seq, D_MODEL), ys, *ffn2, final_norm=True)
    y_prompt = y_prompt.reshape(batch, seq, D_MODEL)
    y_sample = y_sample.reshape(nseq, t_new, D_MODEL)

    st_p = (1, batch, N_SSM_GROUPS, SSM_STATE)
    return (y_prompt, y_sample,
            window_out(kp), window_out(vp), sre_p.reshape(st_p), sim_p.reshape(st_p),
            window_out(ks), window_out(vs), state_out(sre_s), state_out(sim_s))
```

```python
import functools
import math

import jax
import jax.numpy as jnp
from jax import lax
from jax.experimental import pallas as pl
from jax.experimental.pallas import tpu as pltpu

F32 = jnp.float32
BF16 = jnp.bfloat16

D_MODEL = 1024
HEAD_DIM = 64
D_ATTN = 512
N_HEADS = 8
N_KV_HEADS = 2
N_REP = 4
D_KV = 128
D_SSM = 512
SSM_GROUP = 16
N_SSM_GROUPS = 32
SSM_STATE = 64
WINDOW = 128
NUM_BUCKETS = 32
MAX_DISTANCE = 128
D_FF = 2816
D_IN = D_ATTN + 2 * D_KV + D_SSM
RMS_EPS = 1e-6
NEG_INF = -1e30

LANES = 128
SUBLANES = 8
MXU_DIM = 256
FF_CHUNKS = D_FF // MXU_DIM
HALF_GROUPS = N_SSM_GROUPS // 2
HALF_STATE = HALF_GROUPS * SSM_STATE
HALF_U = HALF_GROUPS * SSM_GROUP
U_SLABS = D_SSM // LANES
PITCH = WINDOW + SUBLANES
SCAN_STEPS = 32
RING = 3
SAMPLE_BLOCKS = 1
VMEM_LIMIT = 60 * 1024 * 1024


def _const_spec(shape):
    nd = len(shape)
    return pl.BlockSpec(shape, lambda *_: (0,) * nd, pipeline_mode=pl.Buffered(1))


def _smem_spec():
    return pl.BlockSpec(memory_space=pltpu.SMEM)


def _rmsnorm(x, g):
    r = lax.rsqrt(jnp.mean(x * x, axis=-1, keepdims=True) + RMS_EPS)
    return (x * r) * g


def _dot(a, b):
    return jnp.dot(a, b, preferred_element_type=F32)


def _dot_nt(a, b):
    return lax.dot_general(a, b, (((1,), (1,)), ((), ())), preferred_element_type=F32)


def _transpose_blocks(a):
    r, c = a.shape
    return jnp.concatenate(
        [jnp.concatenate([a[i * LANES:(i + 1) * LANES, j * LANES:(j + 1) * LANES].T
                          for i in range(r // LANES)], axis=1)
         for j in range(c // LANES)], axis=0)


def _ffn_tile(x, g_ref, wg_ref, wu_ref, wd_ref, fg_ref, final_norm, before_chunk=None):
    h = _rmsnorm(x, g_ref[...]).astype(BF16)
    acc = None
    for c in range(FF_CHUNKS):
        if before_chunk is not None:
            before_chunk(c)
        sl = slice(c * MXU_DIM, (c + 1) * MXU_DIM)
        gate = _dot(h, wg_ref[:, sl].astype(BF16))
        up = _dot(h, wu_ref[:, sl].astype(BF16))
        a = (gate * jax.nn.sigmoid(gate) * up).astype(BF16)
        part = _dot(a, wd_ref[sl, :].astype(BF16))
        acc = part if acc is None else acc + part
    y = x + 0.5 * acc
    if final_norm:
        y = _rmsnorm(y, fg_ref[...])
    return y


def _ffn_kernel(xs_ref, xp_ref, g_ref, wg_hbm, wu_hbm, wd_hbm, fg_ref, os_ref, op_ref,
                wg_s, wu_s, wd_s, sems, *, final_norm):
    i = pl.program_id(0)
    weights = (g_ref, wg_s, wu_s, wd_s, fg_ref)

    def chunk_copies(c):
        sl = slice(c * MXU_DIM, (c + 1) * MXU_DIM)
        return (pltpu.make_async_copy(wg_hbm.at[:, sl], wg_s.at[:, sl], sems.at[0, c]),
                pltpu.make_async_copy(wu_hbm.at[:, sl], wu_s.at[:, sl], sems.at[1, c]),
                pltpu.make_async_copy(wd_hbm.at[sl, :], wd_s.at[sl, :], sems.at[2, c]))

    @pl.when(i == 0)
    def _():
        for c in range(FF_CHUNKS):
            for cp in chunk_copies(c):
                cp.start()

        def wait_chunk(c):
            for cp in chunk_copies(c):
                cp.wait()

        os_ref[...] = _ffn_tile(xs_ref[...], *weights, final_norm, wait_chunk)

    @pl.when(i > 0)
    def _():
        op_ref[...] = _ffn_tile(xp_ref[...], *weights, final_norm)


def _ffn(xp, xs, g, wg, wu, wd, fg, *, final_norm):
    n_p, n_s = xp.shape[0], xs.shape[0]
    tm = n_s
    prompt_steps = n_p // tm
    assert prompt_steps * tm == n_p
    prompt_block = lambda i: (jnp.maximum(i - 1, 0), 0)
    hbm = pl.BlockSpec(memory_space=pl.ANY)
    return pl.pallas_call(
        functools.partial(_ffn_kernel, final_norm=final_norm),
        grid=(prompt_steps + 1,),
        in_specs=[
            _const_spec((n_s, D_MODEL)),
            pl.BlockSpec((tm, D_MODEL), prompt_block),
            _const_spec((1, D_MODEL)),
            hbm, hbm, hbm,
            _const_spec((1, D_MODEL)),
        ],
        out_specs=[
            pl.BlockSpec((n_s, D_MODEL), lambda i: (0, 0)),
            pl.BlockSpec((tm, D_MODEL), prompt_block),
        ],
        out_shape=[
            jax.ShapeDtypeStruct((n_s, D_MODEL), F32),
            jax.ShapeDtypeStruct((n_p, D_MODEL), F32),
        ],
        scratch_shapes=[
            pltpu.VMEM((D_MODEL, D_FF), F32),
            pltpu.VMEM((D_MODEL, D_FF), F32),
            pltpu.VMEM((D_FF, D_MODEL), F32),
            pltpu.SemaphoreType.DMA((3, FF_CHUNKS)),
        ],
        compiler_params=pltpu.CompilerParams(
            dimension_semantics=("arbitrary",), vmem_limit_bytes=VMEM_LIMIT),
        name="ffn_final" if final_norm else "ffn",
    )(xs, xp, g, wg, wu, wd, fg)


def _t5_bucket(d):
    d = jnp.maximum(d, 0)
    max_exact = NUM_BUCKETS // 2
    df = jnp.maximum(d, 1).astype(F32)
    large = max_exact + (jnp.log(df / max_exact) / math.log(MAX_DISTANCE / max_exact)
                         * (NUM_BUCKETS - max_exact)).astype(jnp.int32)
    large = jnp.minimum(large, NUM_BUCKETS - 1)
    return jnp.where(d < max_exact, d, large)


def _masked_bias(relb_ref, head, d, valid):
    bucket = _t5_bucket(d)
    b = jnp.zeros(d.shape, F32)
    for k in range(NUM_BUCKETS):
        b = jnp.where(bucket == k, relb_ref[head, k], b)
    return jnp.where(valid, b, NEG_INF)


def _kv_lane_mask(t, g):
    lane = lax.broadcasted_iota(jnp.int32, t.shape, 1)
    return jnp.where((lane >= g * HEAD_DIM) & (lane < (g + 1) * HEAD_DIM), t, jnp.zeros_like(t))


def _stack_queries(q_chunks, g):
    qs = []
    for r in range(N_REP):
        qc = q_chunks[2 * g + r // 2]
        if r % 2 != g:
            qc = pltpu.roll(qc, HEAD_DIM, 1)
        qs.append(qc)
    return (jnp.concatenate(qs, axis=0) * (HEAD_DIM ** -0.5)).astype(BF16)


def _sink_column(sinks_ref, g, rows):
    return jnp.concatenate(
        [jnp.full((rows, 1), sinks_ref[N_REP * g + r], F32) for r in range(N_REP)], axis=0)


def _unstack_heads(o_by_g, rows):
    lane = lax.broadcasted_iota(jnp.int32, (rows, LANES), 1)
    chunks = []
    for c in range(D_ATTN // LANES):
        g = c // 2
        halves = []
        for half in range(2):
            r = 2 * (c % 2) + half
            piece = o_by_g[g][r * rows:(r + 1) * rows]
            if half != g:
                piece = pltpu.roll(piece, HEAD_DIM, 1)
            halves.append(piece)
        chunks.append(jnp.where(lane < HEAD_DIM, halves[0], halves[1]))
    return chunks


def _glu_tail(y, u, dskip_ref, wglu_ref, bglu_ref):
    y = y + dskip_ref[...] * u
    y = jax.nn.gelu(y)
    z = _dot(y.astype(BF16), wglu_ref[...]) + bglu_ref[...]
    return y * jax.nn.sigmoid(z)


def _mixp_kernel(relb_ref, sinks_ref, x_ref, g_ref, win_ref, wout_ref, lam_ref, bd_ref, cd_ref,
                 dskip_ref, wglu_ref, bglu_ref,
                 o_ref, kp_ref, vp_ref, sre_ref, sim_ref,
                 p_s, kband, vband_t, bias_t, us, ys, state, attn_s, *ring_refs, nb, blk):
    i = pl.program_id(0)
    rows = nb * blk
    rings, u_rings = ring_refs[:RING], ring_refs[RING:]

    @pl.when(i == 0)
    def _():
        kband[...] = jnp.zeros(kband.shape, BF16)
        vband_t[...] = jnp.zeros(vband_t.shape, BF16)
        state[...] = jnp.zeros(state.shape, F32)

    @pl.when(i <= 1)
    def _():
        kj = lax.broadcasted_iota(jnp.int32, (2 * blk, blk), 0)
        qi = lax.broadcasted_iota(jnp.int32, (2 * blk, blk), 1)
        d = qi - kj + blk
        first_key = jnp.where(i > 0, 0, blk)
        valid = (d >= 0) & (d < WINDOW) & (kj >= first_key)
        for h in range(N_HEADS):
            r = h % N_REP
            bias_t[h // N_REP, :, r * blk:(r + 1) * blk] = _masked_bias(relb_ref, h, d, valid)

    h = _rmsnorm(x_ref[...].reshape(rows, D_MODEL), g_ref[...]).astype(BF16)
    p_s[...] = _dot(h, win_ref[...])

    kband[:, blk:2 * blk, :] = (
        p_s[:, D_ATTN:D_ATTN + D_KV].reshape(nb, blk, D_KV).astype(BF16))

    @pl.when(i == pl.num_programs(0) - 1)
    def _():
        for b in range(nb):
            rows_b = slice(b * blk, (b + 1) * blk)
            kp_ref[b] = p_s[rows_b, D_ATTN:D_ATTN + D_KV].T
            vp_ref[b] = p_s[rows_b, D_ATTN + D_KV:D_ATTN + 2 * D_KV].T

    u_off = D_ATTN + 2 * D_KV
    for s in range(U_SLABS):
        for b in range(nb):
            us[s, b * PITCH:b * PITCH + blk, :] = (
                p_s[b * blk:(b + 1) * blk, u_off + s * LANES:u_off + (s + 1) * LANES])

    def attention(b):
        rows_b = slice(b * blk, (b + 1) * blk)
        kb = kband[b]
        vband_t[b, :, blk:2 * blk] = p_s[rows_b, D_ATTN + D_KV:D_ATTN + 2 * D_KV].T.astype(BF16)
        vt = vband_t[b]
        feat = lax.broadcasted_iota(jnp.int32, vt.shape, 0)
        q_chunks = [p_s[rows_b, c * LANES:(c + 1) * LANES] for c in range(D_ATTN // LANES)]
        o_by_g = []
        for g in range(N_KV_HEADS):
            q = _stack_queries(q_chunks, g)
            st = _dot_nt(_kv_lane_mask(kb, g), q) + bias_t[g]
            s = jnp.concatenate(
                [jnp.full((1, blk), sinks_ref[N_REP * g + r], F32) for r in range(N_REP)], axis=1)
            m = jnp.maximum(jnp.max(st, axis=0, keepdims=True), s)
            e = jnp.exp(st - m).astype(BF16)
            in_g = (feat >= g * HEAD_DIM) & (feat < (g + 1) * HEAD_DIM)
            ot = _dot(jnp.where(in_g, vt, jnp.ones_like(vt)), e)
            other = (1 - g) * HEAD_DIM
            denom = ot[other:other + 1, :] + jnp.exp(s - m)
            o_by_g.append(ot[g * HEAD_DIM:(g + 1) * HEAD_DIM, :] / denom)
        for c in range(D_ATTN // LANES):
            og = o_by_g[c // 2]
            cc = 2 * (c % 2)
            chunk_t = jnp.concatenate(
                [og[:, cc * blk:(cc + 1) * blk], og[:, (cc + 1) * blk:(cc + 2) * blk]], axis=0)
            attn_s[rows_b, c * LANES:(c + 1) * LANES] = chunk_t.T

    def scan_project_in(j):
        slot = j % RING
        steps = []
        for t in range(SCAN_STEPS):
            steps.append(jnp.concatenate(
                [us[s, pl.ds(j * SCAN_STEPS + t, nb, stride=PITCH), :] for s in range(U_SLABS)],
                axis=1))
        u_chunk = jnp.concatenate(steps, axis=0)
        u_rings[slot][...] = u_chunk
        ub = u_chunk.astype(BF16)
        for hf in range(2):
            rings[slot][:, 2 * hf * HALF_STATE:2 * (hf + 1) * HALF_STATE] = _dot(
                ub[:, hf * HALF_U:(hf + 1) * HALF_U], bd_ref[hf])

    def scan_recurrence(j):
        slot = j % RING
        for hf in range(2):
            lr = jnp.broadcast_to(lam_ref[2 * hf:2 * hf + 1, :], (nb, HALF_STATE))
            li = jnp.broadcast_to(lam_ref[2 * hf + 1:2 * hf + 2, :], (nb, HALF_STATE))
            re = slice(2 * hf * HALF_STATE, (2 * hf + 1) * HALF_STATE)
            im = slice((2 * hf + 1) * HALF_STATE, (2 * hf + 2) * HALF_STATE)
            xr = state[:, re]
            xi = state[:, im]
            for t in range(SCAN_STEPS):
                rows_t = slice(t * nb, (t + 1) * nb)
                xr, xi = (lr * xr - li * xi + rings[slot][rows_t, re],
                          lr * xi + li * xr + rings[slot][rows_t, im])
                rings[slot][rows_t, re] = xr
                rings[slot][rows_t, im] = xi
            state[:, re] = xr
            state[:, im] = xi

    def scan_project_out(j):
        slot = j % RING
        y = jnp.concatenate(
            [_dot(rings[slot][:, 2 * hf * HALF_STATE:2 * (hf + 1) * HALF_STATE].astype(BF16),
                  cd_ref[hf]) for hf in range(2)], axis=1)
        ssm = _glu_tail(y, u_rings[slot][...], dskip_ref, wglu_ref, bglu_ref)
        for t in range(SCAN_STEPS):
            for s in range(U_SLABS):
                ys[s, pl.ds(j * SCAN_STEPS + t, nb, stride=PITCH), :] = (
                    ssm[t * nb:(t + 1) * nb, s * LANES:(s + 1) * LANES])

    n_chunks = blk // SCAN_STEPS
    n_iters = n_chunks + 2
    for jj in range(n_iters):
        if 2 <= jj:
            scan_project_out(jj - 2)
        if 1 <= jj < n_chunks + 1:
            scan_recurrence(jj - 1)
        if jj < n_chunks:
            scan_project_in(jj)
        for b in range(nb):
            if b * n_iters // nb == jj:
                attention(b)

    kband[:, 0:blk, :] = kband[:, blk:2 * blk, :]
    vband_t[:, :, 0:blk] = vband_t[:, :, blk:2 * blk]
    for hf in range(2):
        sre_ref[:, hf * HALF_STATE:(hf + 1) * HALF_STATE] = (
            state[:, 2 * hf * HALF_STATE:(2 * hf + 1) * HALF_STATE])
        sim_ref[:, hf * HALF_STATE:(hf + 1) * HALF_STATE] = (
            state[:, (2 * hf + 1) * HALF_STATE:(2 * hf + 2) * HALF_STATE])

    ssm = jnp.concatenate(
        [jnp.concatenate([ys[s, b * PITCH:b * PITCH + blk, :] for b in range(nb)], axis=0)
         for s in range(U_SLABS)], axis=1)
    mix = (_dot(attn_s[...].astype(BF16), wout_ref[0:D_ATTN, :])
           + _dot(ssm.astype(BF16), wout_ref[D_ATTN:D_ATTN + D_SSM, :]))
    o_ref[...] = x_ref[...] + mix.reshape(nb, blk, D_MODEL)


def _mix_prompt(x, relb, sinks, g, win, wout, lam, bd, cd, dskip, wglu, bglu):
    nb, seq, _ = x.shape
    blk = WINDOW
    rows = nb * blk
    const2 = lambda shape: _const_spec(shape)
    return pl.pallas_call(
        functools.partial(_mixp_kernel, nb=nb, blk=blk),
        grid=(seq // blk,),
        in_specs=[
            _smem_spec(), _smem_spec(),
            pl.BlockSpec((nb, blk, D_MODEL), lambda i: (0, i, 0)),
            const2((1, D_MODEL)),
            const2((D_MODEL, D_IN)),
            const2((D_ATTN + D_SSM, D_MODEL)),
            const2((4, HALF_STATE)),
            const2((2, HALF_U, 2 * HALF_STATE)),
            const2((2, 2 * HALF_STATE, HALF_U)),
            const2((1, D_SSM)),
            const2((D_SSM, D_SSM)),
            const2((1, D_SSM)),
        ],
        out_specs=[
            pl.BlockSpec((nb, blk, D_MODEL), lambda i: (0, i, 0)),
            pl.BlockSpec((nb, blk, D_KV), lambda i: (0, 0, 0)),
            pl.BlockSpec((nb, blk, D_KV), lambda i: (0, 0, 0)),
            pl.BlockSpec((nb, 2 * HALF_STATE), lambda i: (0, 0)),
            pl.BlockSpec((nb, 2 * HALF_STATE), lambda i: (0, 0)),
        ],
        out_shape=[
            jax.ShapeDtypeStruct((nb, seq, D_MODEL), F32),
            jax.ShapeDtypeStruct((nb, blk, D_KV), F32),
            jax.ShapeDtypeStruct((nb, blk, D_KV), F32),
            jax.ShapeDtypeStruct((nb, 2 * HALF_STATE), F32),
            jax.ShapeDtypeStruct((nb, 2 * HALF_STATE), F32),
        ],
        scratch_shapes=[
            pltpu.VMEM((rows, D_IN), F32),
            pltpu.VMEM((nb, 2 * blk, D_KV), BF16),
            pltpu.VMEM((nb, D_KV, 2 * blk), BF16),
            pltpu.VMEM((N_KV_HEADS, 2 * blk, N_REP * blk), F32),
            pltpu.VMEM((U_SLABS, nb * PITCH, LANES), F32),
            pltpu.VMEM((U_SLABS, nb * PITCH, LANES), F32),
            pltpu.VMEM((nb, 4 * HALF_STATE), F32),
            pltpu.VMEM((rows, D_ATTN), F32),
        ] + [pltpu.VMEM((SCAN_STEPS * nb, 4 * HALF_STATE), F32)] * RING
          + [pltpu.VMEM((SCAN_STEPS * nb, D_SSM), F32)] * RING,

        compiler_params=pltpu.CompilerParams(
            dimension_semantics=("arbitrary",), vmem_limit_bytes=VMEM_LIMIT),
        name="mix_prompt",
    )(relb, sinks, x, g, win, wout, lam, bd, cd, dskip, wglu, bglu)


def _mixs_kernel(relb_ref, sinks_ref, x_ref, g_ref, win_ref, wout_ref, ck_ref, cv_ref,
                 sre0_ref, sim0_ref, lam_ref, bd_ref, cd_ref, dskip_ref, wglu_ref, bglu_ref,
                 o_ref, ks_ref, vs_ref, sre_ref, sim_ref,
                 p_s, attn_s, ssm_s, bias_c, bias_n, us,
                 *, nseq, t_new, sb, wbuf):
    i = pl.program_id(0)
    nsteps = pl.num_programs(0)
    rows = nseq * t_new
    brow = sb * t_new
    u_off = D_ATTN + 2 * D_KV
    tshift = t_new.bit_length() - 1
    wshift = wbuf.bit_length() - 1

    @pl.when(i == 0)
    def _():
        x = x_ref[...]
        h = _rmsnorm(x, g_ref[...]).astype(BF16)
        p_s[...] = _dot(h, win_ref[...])

        for s in range(U_SLABS):
            us[s] = p_s[:, u_off + s * LANES:u_off + (s + 1) * LANES]
        uperm = jnp.concatenate(
            [jnp.concatenate([us[s, pl.ds(t, nseq, stride=t_new), :] for t in range(t_new)], axis=0)
             for s in range(U_SLABS)], axis=1)
        yparts = []
        for hf in range(2):
            bu = _dot(uperm[:, hf * HALF_U:(hf + 1) * HALF_U].astype(BF16), bd_ref[hf])
            lr = lam_ref[2 * hf:2 * hf + 1, :]
            li = lam_ref[2 * hf + 1:2 * hf + 2, :]
            xr = _transpose_blocks(sre0_ref[hf * HALF_STATE:(hf + 1) * HALF_STATE, :])
            xi = _transpose_blocks(sim0_ref[hf * HALF_STATE:(hf + 1) * HALF_STATE, :])
            states = []
            for t in range(t_new):
                br = bu[t * nseq:(t + 1) * nseq, 0:HALF_STATE]
                bi = bu[t * nseq:(t + 1) * nseq, HALF_STATE:2 * HALF_STATE]
                xr, xi = lr * xr - li * xi + br, lr * xi + li * xr + bi
                states.append(jnp.concatenate([xr, xi], axis=1).astype(BF16))
            sre_ref[hf * HALF_STATE:(hf + 1) * HALF_STATE, :] = _transpose_blocks(xr)
            sim_ref[hf * HALF_STATE:(hf + 1) * HALF_STATE, :] = _transpose_blocks(xi)
            yparts.append(_dot(jnp.concatenate(states, axis=0), cd_ref[hf]))
        yperm = jnp.concatenate(yparts, axis=1)
        for s in range(U_SLABS):
            for t in range(t_new):
                us[s, pl.ds(t, nseq, stride=t_new), :] = (
                    yperm[t * nseq:(t + 1) * nseq, s * LANES:(s + 1) * LANES])
        y = jnp.concatenate([us[s] for s in range(U_SLABS)], axis=1)
        ssm_s[...] = _glu_tail(y, p_s[:, u_off:u_off + D_SSM], dskip_ref, wglu_ref, bglu_ref)

        ncol = sb * wbuf
        rho = lax.broadcasted_iota(jnp.int32, (brow, ncol), 0)
        kap = lax.broadcasted_iota(jnp.int32, (brow, ncol), 1)
        same_seq = (rho >> tshift) == (kap >> wshift)
        rho_w = lax.broadcasted_iota(jnp.int32, (brow, wbuf), 0)
        d_c = (rho_w & (t_new - 1)) - lax.broadcasted_iota(jnp.int32, (brow, wbuf), 1) + wbuf
        valid_c = (d_c >= 0) & (d_c < WINDOW)
        rho_n = lax.broadcasted_iota(jnp.int32, (brow, LANES), 0)
        kap_n = lax.broadcasted_iota(jnp.int32, (brow, LANES), 1)
        tq_n = rho_n & (t_new - 1)
        d_n = tq_n - (kap_n & (t_new - 1))
        valid_n = ((rho_n >> tshift) == (kap_n >> tshift)) & (d_n >= 0) & (kap_n < brow)
        for hd in range(N_HEADS):
            g, r = hd // N_REP, hd % N_REP
            tile = _masked_bias(relb_ref, hd, d_c, valid_c)
            bias_c[g, r * brow:(r + 1) * brow, :] = jnp.where(
                same_seq, jnp.concatenate([tile] * sb, axis=1), NEG_INF)
            bias_n[g, r * brow:(r + 1) * brow, :] = _masked_bias(relb_ref, hd, d_n, valid_n)

    def seq_block(k):
        first = k * sb
        row0 = pl.multiple_of((i * SAMPLE_BLOCKS + k) * brow, brow)
        kc_t = jnp.concatenate([ck_ref[first + b] for b in range(sb)], axis=1).astype(BF16)
        vc_t = jnp.concatenate([cv_ref[first + b] for b in range(sb)], axis=1).astype(BF16)
        pad = jnp.zeros((LANES - brow, D_KV), F32)
        kn_pad = jnp.concatenate([p_s[pl.ds(row0, brow), D_ATTN:D_ATTN + D_KV], pad], axis=0)
        vn_pad = jnp.concatenate(
            [p_s[pl.ds(row0, brow), D_ATTN + D_KV:D_ATTN + 2 * D_KV], pad], axis=0)
        kn_b = kn_pad.astype(BF16)
        vn_b = vn_pad.astype(BF16)
        q_chunks = [p_s[pl.ds(row0, brow), c * LANES:(c + 1) * LANES]
                    for c in range(D_ATTN // LANES)]
        o_by_g = []
        for g in range(N_KV_HEADS):
            q = _kv_lane_mask(_stack_queries(q_chunks, g), g)
            lc = _dot(q, kc_t) + bias_c[g]
            ln = _dot_nt(q, kn_b) + bias_n[g]
            s = _sink_column(sinks_ref, g, brow)
            m = jnp.maximum(jnp.maximum(jnp.max(lc, axis=-1, keepdims=True),
                                        jnp.max(ln, axis=-1, keepdims=True)), s)
            ec = jnp.exp(lc - m)
            en = jnp.exp(ln - m)
            denom = (jnp.sum(ec, axis=-1, keepdims=True) + jnp.sum(en, axis=-1, keepdims=True)
                     + jnp.exp(s - m))
            o = _dot_nt(ec.astype(BF16), vc_t) + _dot(en.astype(BF16), vn_b)
            o_by_g.append(o / denom)
        for c, chunk in enumerate(_unstack_heads(o_by_g, brow)):
            attn_s[pl.ds(row0, brow), c * LANES:(c + 1) * LANES] = chunk

        lane = lax.broadcasted_iota(jnp.int32, (D_KV, wbuf), 1)
        keep = wbuf - t_new
        for new_pad, old_ref, out_ref in ((kn_pad, ck_ref, ks_ref), (vn_pad, cv_ref, vs_ref)):
            new_t = new_pad.T
            for b in range(sb):
                shifted = pltpu.roll(old_ref[first + b], keep, 1)
                appended = pltpu.roll(new_t, (keep - b * t_new) % wbuf, 1)
                out_ref[first + b] = jnp.where(lane >= keep, appended, shifted)

    for k in range(SAMPLE_BLOCKS):
        seq_block(k)

    @pl.when(i == nsteps - 1)
    def _():
        mix = (_dot(attn_s[...].astype(BF16), wout_ref[0:D_ATTN, :])
               + _dot(ssm_s[...].astype(BF16), wout_ref[D_ATTN:D_ATTN + D_SSM, :]))
        o_ref[...] = x_ref[...] + mix


def _mix_sample(x, relb, sinks, g, win, wout, ck, cv, sre0, sim0, lam, bd, cd, dskip, wglu, bglu,
                *, nseq, t_new):
    rows = nseq * t_new
    wbuf = ck.shape[2]
    assert wbuf == LANES
    sb = SUBLANES
    brow = sb * t_new
    return pl.pallas_call(
        functools.partial(_mixs_kernel, nseq=nseq, t_new=t_new, sb=sb, wbuf=wbuf),
        grid=(nseq // (SAMPLE_BLOCKS * sb),),
        in_specs=[
            _smem_spec(), _smem_spec(),
            _const_spec((rows, D_MODEL)),
            _const_spec((1, D_MODEL)),
            _const_spec((D_MODEL, D_IN)),
            _const_spec((D_ATTN + D_SSM, D_MODEL)),
            pl.BlockSpec((SAMPLE_BLOCKS * sb, D_KV, wbuf), lambda i: (i, 0, 0)),
            pl.BlockSpec((SAMPLE_BLOCKS * sb, D_KV, wbuf), lambda i: (i, 0, 0)),
            _const_spec((2 * HALF_STATE, nseq)),
            _const_spec((2 * HALF_STATE, nseq)),
            _const_spec((4, HALF_STATE)),
            _const_spec((2, HALF_U, 2 * HALF_STATE)),
            _const_spec((2, 2 * HALF_STATE, HALF_U)),
            _const_spec((1, D_SSM)),
            _const_spec((D_SSM, D_SSM)),
            _const_spec((1, D_SSM)),
        ],
        out_specs=[
            pl.BlockSpec((rows, D_MODEL), lambda i: (0, 0)),
            pl.BlockSpec((SAMPLE_BLOCKS * sb, D_KV, wbuf), lambda i: (i, 0, 0)),
            pl.BlockSpec((SAMPLE_BLOCKS * sb, D_KV, wbuf), lambda i: (i, 0, 0)),
            pl.BlockSpec((2 * HALF_STATE, nseq), lambda i: (0, 0)),
            pl.BlockSpec((2 * HALF_STATE, nseq), lambda i: (0, 0)),
        ],
        out_shape=[
            jax.ShapeDtypeStruct((rows, D_MODEL), F32),
            jax.ShapeDtypeStruct((nseq, D_KV, wbuf), F32),
            jax.ShapeDtypeStruct((nseq, D_KV, wbuf), F32),
            jax.ShapeDtypeStruct((2 * HALF_STATE, nseq), F32),
            jax.ShapeDtypeStruct((2 * HALF_STATE, nseq), F32),
        ],
        scratch_shapes=[
            pltpu.VMEM((rows, D_IN), F32),
            pltpu.VMEM((rows, D_ATTN), F32),
            pltpu.VMEM((rows, D_SSM), F32),
            pltpu.VMEM((N_KV_HEADS, N_REP * brow, sb * wbuf), F32),
            pltpu.VMEM((N_KV_HEADS, N_REP * brow, LANES), F32),
            pltpu.VMEM((U_SLABS, rows, LANES), F32),
        ],
        compiler_params=pltpu.CompilerParams(
            dimension_semantics=("arbitrary",), vmem_limit_bytes=VMEM_LIMIT),
        name="mix_sample",
    )(relb, sinks, x, g, win, wout, ck, cv, sre0, sim0, lam, bd, cd, dskip, wglu, bglu)


def _s5_operators(log_dt, a_re, a_im, b_re, b_im, c_re, c_im):
    dt = jnp.exp(log_dt)[:, None]
    mag = jnp.exp(a_re * dt)
    lb_re = mag * jnp.cos(a_im * dt)
    lb_im = mag * jnp.sin(a_im * dt)
    den = a_re * a_re + a_im * a_im
    nr = lb_re - 1.0
    q_re = (nr * a_re + lb_im * a_im) / den
    q_im = (lb_im * a_re - nr * a_im) / den
    bb_re = q_re[..., None] * b_re - q_im[..., None] * b_im
    bb_im = q_re[..., None] * b_im + q_im[..., None] * b_re
    eye = jnp.eye(HALF_GROUPS, dtype=BF16)
    lam = jnp.stack([lb_re.reshape(2, HALF_STATE), lb_im.reshape(2, HALF_STATE)], axis=1)
    bb = jnp.stack([bb_re, bb_im]).astype(BF16).reshape(2, 2, HALF_GROUPS, SSM_STATE, SSM_GROUP)
    bd = jnp.einsum('rhgpc,gk->hgcrkp', bb, eye).reshape(2, HALF_U, 2 * HALF_STATE)
    cc = jnp.stack([c_re, -c_im]).astype(BF16).reshape(2, 2, HALF_GROUPS, SSM_GROUP, SSM_STATE)
    cd = jnp.einsum('rhgcp,gk->hrgpkc', cc, eye).reshape(2, 2 * HALF_STATE, HALF_U)
    return lam.reshape(4, HALF_STATE), bd, cd


def kernel(x_prompt, x_sample, cache_k, cache_v, state_ssm_re, state_ssm_im, rel_bias,
           ffn1_norm, ffn1_w_gate, ffn1_w_up, ffn1_w_down, mix_norm, w_in, sinks,
           log_dt, a_re, a_im, b_re, b_im, c_re, c_im, d_skip, w_glu, b_glu, w_out,
           ffn2_norm, ffn2_w_gate, ffn2_w_up, ffn2_w_down, final_norm):
    depth = w_in.shape[0]
    assert depth == 1
    batch, seq, _ = x_prompt.shape
    nseq, t_new, _ = x_sample.shape
    wbuf = cache_k.shape[2]
    fg = final_norm.reshape(1, D_MODEL)

    l = 0
    ffn1 = (ffn1_norm[l].reshape(1, D_MODEL), ffn1_w_gate[l], ffn1_w_up[l], ffn1_w_down[l], fg)
    ffn2 = (ffn2_norm[l].reshape(1, D_MODEL), ffn2_w_gate[l], ffn2_w_up[l], ffn2_w_down[l], fg)
    lam, bd, cd = _s5_operators(log_dt[l], a_re[l], a_im[l], b_re[l], b_im[l], c_re[l], c_im[l])
    mix_w = (mix_norm[l].reshape(1, D_MODEL), w_in[l].astype(BF16), w_out[l].astype(BF16))
    ssm_w = (lam, bd, cd, d_skip[l].reshape(1, D_SSM), w_glu[l].astype(BF16),
             b_glu[l].reshape(1, D_SSM))
    sinks_l = sinks[l]

    def window_in(c):
        return jnp.transpose(c, (0, 2, 3, 1)).reshape(c.shape[0], D_KV, c.shape[1])

    def window_out(w):
        n, _, pos = w.shape
        return jnp.transpose(w.reshape(n, N_KV_HEADS, HEAD_DIM, pos), (0, 3, 1, 2))[None]

    def state_in(s):
        return jnp.transpose(s, (1, 2, 0)).reshape(2 * HALF_STATE, s.shape[0])

    def state_out(s):
        return jnp.transpose(s.reshape(N_SSM_GROUPS, SSM_STATE, s.shape[1]), (2, 0, 1))[None]

    relb_t = rel_bias.T
    xp = x_prompt.reshape(batch * seq, D_MODEL)
    xs = x_sample.reshape(nseq * t_new, D_MODEL)
    ys, yp = _ffn(xp, xs, *ffn1, final_norm=False)
    yp, kp, vp, sre_p, sim_p = _mix_prompt(
        yp.reshape(batch, seq, D_MODEL), relb_t, sinks_l, *mix_w, *ssm_w)
    ys, ks, vs, sre_s, sim_s = _mix_sample(
        ys, relb_t, sinks_l, *mix_w,
        window_in(cache_k[l]), window_in(cache_v[l]),
        state_in(state_ssm_re[l]), state_in(state_ssm_im[l]),
        *ssm_w, nseq=nseq, t_new=t_new)
    y_sample, y_prompt = _ffn(yp.reshape(batch * seq, D_MODEL), ys, *ffn2, final_norm=True)
    y_prompt = y_prompt.reshape(batch, seq, D_MODEL)
    y_sample = y_sample.reshape(nseq, t_new, D_MODEL)

    st_p = (1, batch, N_SSM_GROUPS, SSM_STATE)
    return (y_prompt, y_sample,
            window_out(kp), window_out(vp), sre_p.reshape(st_p), sim_p.reshape(st_p),
            window_out(ks), window_out(vs), state_out(sre_s), state_out(sim_s))
```

```python
import functools
import math

import jax
import jax.numpy as jnp
from jax import lax
from jax.experimental import pallas as pl
from jax.experimental.pallas import tpu as pltpu

F32 = jnp.float32
BF16 = jnp.bfloat16

D_MODEL = 1024
HEAD_DIM = 64
D_ATTN = 512
N_HEADS = 8
N_KV_HEADS = 2
N_REP = 4
D_KV = 128
D_SSM = 512
SSM_GROUP = 16
N_SSM_GROUPS = 32
SSM_STATE = 64
WINDOW = 128
NUM_BUCKETS = 32
MAX_DISTANCE = 128
D_FF = 2816
D_IN = D_ATTN + 2 * D_KV + D_SSM
RMS_EPS = 1e-6
NEG_INF = -1e30

LANES = 128
SUBLANES = 8
MXU_DIM = 256
FF_CHUNKS = D_FF // MXU_DIM
STAGE_SLOTS = 2
FFN_SUB_TILES = 2
HALF_GROUPS = N_SSM_GROUPS // 2
HALF_STATE = HALF_GROUPS * SSM_STATE
HALF_U = HALF_GROUPS * SSM_GROUP
U_SLABS = D_SSM // LANES
PITCH = WINDOW + SUBLANES
SCAN_STEPS = 32
RING = 3
SAMPLE_BLOCKS = 1
VMEM_LIMIT = 60 * 1024 * 1024


def _const_spec(shape):
    nd = len(shape)
    return pl.BlockSpec(shape, lambda *_: (0,) * nd, pipeline_mode=pl.Buffered(1))


def _smem_spec():
    return pl.BlockSpec(memory_space=pltpu.SMEM)


def _rmsnorm(x, g):
    r = lax.rsqrt(jnp.mean(x * x, axis=-1, keepdims=True) + RMS_EPS)
    return (x * r) * g


def _dot(a, b):
    return jnp.dot(a, b, preferred_element_type=F32)


def _dot_nt(a, b):
    return lax.dot_general(a, b, (((1,), (1,)), ((), ())), preferred_element_type=F32)


def _transpose_blocks(a):
    r, c = a.shape
    return jnp.concatenate(
        [jnp.concatenate([a[i * LANES:(i + 1) * LANES, j * LANES:(j + 1) * LANES].T
                          for i in range(r // LANES)], axis=1)
         for j in range(c // LANES)], axis=0)


def _ffn_tile(x, g_ref, wg_ref, wu_ref, wd_ref, fg_ref, final_norm, before_chunk=None):
    h = _rmsnorm(x, g_ref[...]).astype(BF16)
    acc = None
    for c in range(FF_CHUNKS):
        if before_chunk is not None:
            before_chunk(c)
        sl = slice(c * MXU_DIM, (c + 1) * MXU_DIM)
        gate = _dot(h, wg_ref[:, sl])
        up = _dot(h, wu_ref[:, sl])
        a = (gate * jax.nn.sigmoid(gate) * up).astype(BF16)
        part = _dot(a, wd_ref[sl, :])
        acc = part if acc is None else acc + part
    y = x + 0.5 * acc
    if final_norm:
        y = _rmsnorm(y, fg_ref[...])
    return y


def _ffn_kernel(xs_ref, xp_ref, g_ref, wg_hbm, wu_hbm, wd_hbm, fg_ref, os_ref, op_ref,
                wg_b, wu_b, wd_b, wg_st, wu_st, wd_st, sems, *, final_norm, sub_tiles):
    i = pl.program_id(0)
    weights = (g_ref, wg_b, wu_b, wd_b, fg_ref)
    tm = xs_ref.shape[0]

    def chunk_copies(c):
        sl = slice(c * MXU_DIM, (c + 1) * MXU_DIM)
        slot = c % STAGE_SLOTS
        return (pltpu.make_async_copy(wg_hbm.at[:, sl], wg_st.at[slot], sems.at[0, slot]),
                pltpu.make_async_copy(wu_hbm.at[:, sl], wu_st.at[slot], sems.at[1, slot]),
                pltpu.make_async_copy(wd_hbm.at[sl, :], wd_st.at[slot], sems.at[2, slot]))

    @pl.when(i == 0)
    def _():
        for c in range(STAGE_SLOTS):
            for cp in chunk_copies(c):
                cp.start()

        def land_chunk(c):
            sl = slice(c * MXU_DIM, (c + 1) * MXU_DIM)
            slot = c % STAGE_SLOTS
            for cp in chunk_copies(c):
                cp.wait()
            wg_b[:, sl] = wg_st[slot].astype(BF16)
            wu_b[:, sl] = wu_st[slot].astype(BF16)
            wd_b[sl, :] = wd_st[slot].astype(BF16)
            if c + STAGE_SLOTS < FF_CHUNKS:
                for cp in chunk_copies(c + STAGE_SLOTS):
                    cp.start()

        os_ref[...] = _ffn_tile(xs_ref[...], *weights, final_norm, land_chunk)

    @pl.when(i > 0)
    def _():
        for t in range(sub_tiles):
            rows = slice(t * tm, (t + 1) * tm)
            op_ref[rows, :] = _ffn_tile(xp_ref[rows, :], *weights, final_norm)


def _ffn(xp, xs, g, wg, wu, wd, fg, *, final_norm):
    n_p, n_s = xp.shape[0], xs.shape[0]
    tm = FFN_SUB_TILES * n_s
    prompt_steps = n_p // tm
    assert prompt_steps * tm == n_p
    prompt_block = lambda i: (jnp.maximum(i - 1, 0), 0)
    hbm = pl.BlockSpec(memory_space=pl.ANY)
    return pl.pallas_call(
        functools.partial(_ffn_kernel, final_norm=final_norm, sub_tiles=FFN_SUB_TILES),
        grid=(prompt_steps + 1,),
        in_specs=[
            _const_spec((n_s, D_MODEL)),
            pl.BlockSpec((tm, D_MODEL), prompt_block),
            _const_spec((1, D_MODEL)),
            hbm, hbm, hbm,
            _const_spec((1, D_MODEL)),
        ],
        out_specs=[
            pl.BlockSpec((n_s, D_MODEL), lambda i: (0, 0)),
            pl.BlockSpec((tm, D_MODEL), prompt_block),
        ],
        out_shape=[
            jax.ShapeDtypeStruct((n_s, D_MODEL), F32),
            jax.ShapeDtypeStruct((n_p, D_MODEL), F32),
        ],
        scratch_shapes=[
            pltpu.VMEM((D_MODEL, D_FF), BF16),
            pltpu.VMEM((D_MODEL, D_FF), BF16),
            pltpu.VMEM((D_FF, D_MODEL), BF16),
            pltpu.VMEM((STAGE_SLOTS, D_MODEL, MXU_DIM), F32),
            pltpu.VMEM((STAGE_SLOTS, D_MODEL, MXU_DIM), F32),
            pltpu.VMEM((STAGE_SLOTS, MXU_DIM, D_MODEL), F32),
            pltpu.SemaphoreType.DMA((3, STAGE_SLOTS)),
        ],
        compiler_params=pltpu.CompilerParams(
            dimension_semantics=("arbitrary",), vmem_limit_bytes=VMEM_LIMIT),
        name="ffn_final" if final_norm else "ffn",
    )(xs, xp, g, wg, wu, wd, fg)


def _t5_bucket(d):
    d = jnp.maximum(d, 0)
    max_exact = NUM_BUCKETS // 2
    df = jnp.maximum(d, 1).astype(F32)
    large = max_exact + (jnp.log(df / max_exact) / math.log(MAX_DISTANCE / max_exact)
                         * (NUM_BUCKETS - max_exact)).astype(jnp.int32)
    large = jnp.minimum(large, NUM_BUCKETS - 1)
    return jnp.where(d < max_exact, d, large)


def _masked_bias(relb_ref, head, d, valid):
    bucket = _t5_bucket(d)
    b = jnp.zeros(d.shape, F32)
    for k in range(NUM_BUCKETS):
        b = jnp.where(bucket == k, relb_ref[head, k], b)
    return jnp.where(valid, b, NEG_INF)


def _kv_lane_mask(t, g):
    lane = lax.broadcasted_iota(jnp.int32, t.shape, 1)
    return jnp.where((lane >= g * HEAD_DIM) & (lane < (g + 1) * HEAD_DIM), t, jnp.zeros_like(t))


def _stack_queries(q_chunks, g):
    qs = []
    for r in range(N_REP):
        qc = q_chunks[2 * g + r // 2]
        if r % 2 != g:
            qc = pltpu.roll(qc, HEAD_DIM, 1)
        qs.append(qc)
    return (jnp.concatenate(qs, axis=0) * (HEAD_DIM ** -0.5)).astype(BF16)


def _sink_column(sinks_ref, g, rows):
    return jnp.concatenate(
        [jnp.full((rows, 1), sinks_ref[N_REP * g + r], F32) for r in range(N_REP)], axis=0)


def _unstack_heads(o_by_g, rows):
    lane = lax.broadcasted_iota(jnp.int32, (rows, LANES), 1)
    chunks = []
    for c in range(D_ATTN // LANES):
        g = c // 2
        halves = []
        for half in range(2):
            r = 2 * (c % 2) + half
            piece = o_by_g[g][r * rows:(r + 1) * rows]
            if half != g:
                piece = pltpu.roll(piece, HEAD_DIM, 1)
            halves.append(piece)
        chunks.append(jnp.where(lane < HEAD_DIM, halves[0], halves[1]))
    return chunks


def _glu_tail(y, u, dskip_ref, wglu_ref, bglu_ref):
    y = y + dskip_ref[...] * u
    y = jax.nn.gelu(y)
    z = _dot(y.astype(BF16), wglu_ref[...]) + bglu_ref[...]
    return y * jax.nn.sigmoid(z)


def _mixp_kernel(relb_ref, sinks_ref, x_ref, g_ref, win_ref, wout_ref, lam_ref, bd_ref, cd_ref,
                 dskip_ref, wglu_ref, bglu_ref,
                 o_ref, kp_ref, vp_ref, sre_ref, sim_ref,
                 p_s, kband, vband_t, bias_t, us, ys, state, attn_s, *ring_refs, nb, blk):
    i = pl.program_id(0)
    rows = nb * blk
    rings, u_rings = ring_refs[:RING], ring_refs[RING:]

    @pl.when(i == 0)
    def _():
        kband[...] = jnp.zeros(kband.shape, BF16)
        vband_t[...] = jnp.zeros(vband_t.shape, BF16)
        state[...] = jnp.zeros(state.shape, F32)

    @pl.when(i <= 1)
    def _():
        kj = lax.broadcasted_iota(jnp.int32, (2 * blk, blk), 0)
        qi = lax.broadcasted_iota(jnp.int32, (2 * blk, blk), 1)
        d = qi - kj + blk
        first_key = jnp.where(i > 0, 0, blk)
        valid = (d >= 0) & (d < WINDOW) & (kj >= first_key)
        for h in range(N_HEADS):
            r = h % N_REP
            bias_t[h // N_REP, :, r * blk:(r + 1) * blk] = _masked_bias(relb_ref, h, d, valid)

    h = _rmsnorm(x_ref[...].reshape(rows, D_MODEL), g_ref[...]).astype(BF16)
    p_s[...] = _dot(h, win_ref[...])

    kband[:, blk:2 * blk, :] = (
        p_s[:, D_ATTN:D_ATTN + D_KV].reshape(nb, blk, D_KV).astype(BF16))

    @pl.when(i == pl.num_programs(0) - 1)
    def _():
        for b in range(nb):
            rows_b = slice(b * blk, (b + 1) * blk)
            kp_ref[b] = p_s[rows_b, D_ATTN:D_ATTN + D_KV].T
            vp_ref[b] = p_s[rows_b, D_ATTN + D_KV:D_ATTN + 2 * D_KV].T

    u_off = D_ATTN + 2 * D_KV
    for s in range(U_SLABS):
        for b in range(nb):
            us[s, b * PITCH:b * PITCH + blk, :] = (
                p_s[b * blk:(b + 1) * blk, u_off + s * LANES:u_off + (s + 1) * LANES])

    def attention(b):
        rows_b = slice(b * blk, (b + 1) * blk)
        kb = kband[b]
        vband_t[b, :, blk:2 * blk] = p_s[rows_b, D_ATTN + D_KV:D_ATTN + 2 * D_KV].T.astype(BF16)
        vt = vband_t[b]
        feat = lax.broadcasted_iota(jnp.int32, vt.shape, 0)
        q_chunks = [p_s[rows_b, c * LANES:(c + 1) * LANES] for c in range(D_ATTN // LANES)]
        o_by_g = []
        for g in range(N_KV_HEADS):
            q = _stack_queries(q_chunks, g)
            st = _dot_nt(_kv_lane_mask(kb, g), q) + bias_t[g]
            s = jnp.concatenate(
                [jnp.full((1, blk), sinks_ref[N_REP * g + r], F32) for r in range(N_REP)], axis=1)
            m = jnp.maximum(jnp.max(st, axis=0, keepdims=True), s)
            e = jnp.exp(st - m).astype(BF16)
            in_g = (feat >= g * HEAD_DIM) & (feat < (g + 1) * HEAD_DIM)
            ot = _dot(jnp.where(in_g, vt, jnp.ones_like(vt)), e)
            other = (1 - g) * HEAD_DIM
            denom = ot[other:other + 1, :] + jnp.exp(s - m)
            o_by_g.append(ot[g * HEAD_DIM:(g + 1) * HEAD_DIM, :] / denom)
        for c in range(D_ATTN // LANES):
            og = o_by_g[c // 2]
            cc = 2 * (c % 2)
            chunk_t = jnp.concatenate(
                [og[:, cc * blk:(cc + 1) * blk], og[:, (cc + 1) * blk:(cc + 2) * blk]], axis=0)
            attn_s[rows_b, c * LANES:(c + 1) * LANES] = chunk_t.T

    def scan_project_in(j):
        slot = j % RING
        steps = []
        for t in range(SCAN_STEPS):
            steps.append(jnp.concatenate(
                [us[s, pl.ds(j * SCAN_STEPS + t, nb, stride=PITCH), :] for s in range(U_SLABS)],
                axis=1))
        u_chunk = jnp.concatenate(steps, axis=0)
        u_rings[slot][...] = u_chunk
        ub = u_chunk.astype(BF16)
        for hf in range(2):
            rings[slot][:, 2 * hf * HALF_STATE:2 * (hf + 1) * HALF_STATE] = _dot(
                ub[:, hf * HALF_U:(hf + 1) * HALF_U], bd_ref[hf])

    def scan_recurrence(j):
        slot = j % RING
        for hf in range(2):
            lr = jnp.broadcast_to(lam_ref[2 * hf:2 * hf + 1, :], (nb, HALF_STATE))
            li = jnp.broadcast_to(lam_ref[2 * hf + 1:2 * hf + 2, :], (nb, HALF_STATE))
            re = slice(2 * hf * HALF_STATE, (2 * hf + 1) * HALF_STATE)
            im = slice((2 * hf + 1) * HALF_STATE, (2 * hf + 2) * HALF_STATE)
            xr = state[:, re]
            xi = state[:, im]
            for t in range(SCAN_STEPS):
                rows_t = slice(t * nb, (t + 1) * nb)
                xr, xi = (lr * xr - li * xi + rings[slot][rows_t, re],
                          lr * xi + li * xr + rings[slot][rows_t, im])
                rings[slot][rows_t, re] = xr
                rings[slot][rows_t, im] = xi
            state[:, re] = xr
            state[:, im] = xi

    def scan_project_out(j):
        slot = j % RING
        y = jnp.concatenate(
            [_dot(rings[slot][:, 2 * hf * HALF_STATE:2 * (hf + 1) * HALF_STATE].astype(BF16),
                  cd_ref[hf]) for hf in range(2)], axis=1)
        ssm = _glu_tail(y, u_rings[slot][...], dskip_ref, wglu_ref, bglu_ref)
        for t in range(SCAN_STEPS):
            for s in range(U_SLABS):
                ys[s, pl.ds(j * SCAN_STEPS + t, nb, stride=PITCH), :] = (
                    ssm[t * nb:(t + 1) * nb, s * LANES:(s + 1) * LANES])

    n_chunks = blk // SCAN_STEPS
    n_iters = n_chunks + 2
    for jj in range(n_iters):
        if 2 <= jj:
            scan_project_out(jj - 2)
        if 1 <= jj < n_chunks + 1:
            scan_recurrence(jj - 1)
        if jj < n_chunks:
            scan_project_in(jj)
        for b in range(nb):
            if b * n_iters // nb == jj:
                attention(b)

    kband[:, 0:blk, :] = kband[:, blk:2 * blk, :]
    vband_t[:, :, 0:blk] = vband_t[:, :, blk:2 * blk]
    for hf in range(2):
        sre_ref[:, hf * HALF_STATE:(hf + 1) * HALF_STATE] = (
            state[:, 2 * hf * HALF_STATE:(2 * hf + 1) * HALF_STATE])
        sim_ref[:, hf * HALF_STATE:(hf + 1) * HALF_STATE] = (
            state[:, (2 * hf + 1) * HALF_STATE:(2 * hf + 2) * HALF_STATE])

    ssm = jnp.concatenate(
        [jnp.concatenate([ys[s, b * PITCH:b * PITCH + blk, :] for b in range(nb)], axis=0)
         for s in range(U_SLABS)], axis=1)
    mix = (_dot(attn_s[...].astype(BF16), wout_ref[0:D_ATTN, :])
           + _dot(ssm.astype(BF16), wout_ref[D_ATTN:D_ATTN + D_SSM, :]))
    o_ref[...] = x_ref[...] + mix.reshape(nb, blk, D_MODEL)


def _mix_prompt(x, relb, sinks, g, win, wout, lam, bd, cd, dskip, wglu, bglu):
    nb, seq, _ = x.shape
    blk = WINDOW
    rows = nb * blk
    const2 = lambda shape: _const_spec(shape)
    return pl.pallas_call(
        functools.partial(_mixp_kernel, nb=nb, blk=blk),
        grid=(seq // blk,),
        in_specs=[
            _smem_spec(), _smem_spec(),
            pl.BlockSpec((nb, blk, D_MODEL), lambda i: (0, i, 0)),
            const2((1, D_MODEL)),
            const2((D_MODEL, D_IN)),
            const2((D_ATTN + D_SSM, D_MODEL)),
            const2((4, HALF_STATE)),
            const2((2, HALF_U, 2 * HALF_STATE)),
            const2((2, 2 * HALF_STATE, HALF_U)),
            const2((1, D_SSM)),
            const2((D_SSM, D_SSM)),
            const2((1, D_SSM)),
        ],
        out_specs=[
            pl.BlockSpec((nb, blk, D_MODEL), lambda i: (0, i, 0)),
            pl.BlockSpec((nb, blk, D_KV), lambda i: (0, 0, 0)),
            pl.BlockSpec((nb, blk, D_KV), lambda i: (0, 0, 0)),
            pl.BlockSpec((nb, 2 * HALF_STATE), lambda i: (0, 0)),
            pl.BlockSpec((nb, 2 * HALF_STATE), lambda i: (0, 0)),
        ],
        out_shape=[
            jax.ShapeDtypeStruct((nb, seq, D_MODEL), F32),
            jax.ShapeDtypeStruct((nb, blk, D_KV), F32),
            jax.ShapeDtypeStruct((nb, blk, D_KV), F32),
            jax.ShapeDtypeStruct((nb, 2 * HALF_STATE), F32),
            jax.ShapeDtypeStruct((nb, 2 * HALF_STATE), F32),
        ],
        scratch_shapes=[
            pltpu.VMEM((rows, D_IN), F32),
            pltpu.VMEM((nb, 2 * blk, D_KV), BF16),
            pltpu.VMEM((nb, D_KV, 2 * blk), BF16),
            pltpu.VMEM((N_KV_HEADS, 2 * blk, N_REP * blk), F32),
            pltpu.VMEM((U_SLABS, nb * PITCH, LANES), F32),
            pltpu.VMEM((U_SLABS, nb * PITCH, LANES), F32),
            pltpu.VMEM((nb, 4 * HALF_STATE), F32),
            pltpu.VMEM((rows, D_ATTN), F32),
        ] + [pltpu.VMEM((SCAN_STEPS * nb, 4 * HALF_STATE), F32)] * RING
          + [pltpu.VMEM((SCAN_STEPS * nb, D_SSM), F32)] * RING,

        compiler_params=pltpu.CompilerParams(
            dimension_semantics=("arbitrary",), vmem_limit_bytes=VMEM_LIMIT),
        name="mix_prompt",
    )(relb, sinks, x, g, win, wout, lam, bd, cd, dskip, wglu, bglu)


def _mixs_kernel(relb_ref, sinks_ref, x_ref, g_ref, win_ref, wout_ref, ck_ref, cv_ref,
                 sre0_ref, sim0_ref, lam_ref, bd_ref, cd_ref, dskip_ref, wglu_ref, bglu_ref,
                 o_ref, ks_ref, vs_ref, sre_ref, sim_ref,
                 p_s, attn_s, ssm_s, bias_c, bias_n, us,
                 *, nseq, t_new, sb, wbuf):
    i = pl.program_id(0)
    nsteps = pl.num_programs(0)
    rows = nseq * t_new
    brow = sb * t_new
    u_off = D_ATTN + 2 * D_KV
    tshift = t_new.bit_length() - 1
    wshift = wbuf.bit_length() - 1

    @pl.when(i == 0)
    def _():
        x = x_ref[...]
        h = _rmsnorm(x, g_ref[...]).astype(BF16)
        p_s[...] = _dot(h, win_ref[...])

        for s in range(U_SLABS):
            us[s] = p_s[:, u_off + s * LANES:u_off + (s + 1) * LANES]
        uperm = jnp.concatenate(
            [jnp.concatenate([us[s, pl.ds(t, nseq, stride=t_new), :] for t in range(t_new)], axis=0)
             for s in range(U_SLABS)], axis=1)
        yparts = []
        for hf in range(2):
            bu = _dot(uperm[:, hf * HALF_U:(hf + 1) * HALF_U].astype(BF16), bd_ref[hf])
            lr = lam_ref[2 * hf:2 * hf + 1, :]
            li = lam_ref[2 * hf + 1:2 * hf + 2, :]
            xr = _transpose_blocks(sre0_ref[hf * HALF_STATE:(hf + 1) * HALF_STATE, :])
            xi = _transpose_blocks(sim0_ref[hf * HALF_STATE:(hf + 1) * HALF_STATE, :])
            states = []
            for t in range(t_new):
                br = bu[t * nseq:(t + 1) * nseq, 0:HALF_STATE]
                bi = bu[t * nseq:(t + 1) * nseq, HALF_STATE:2 * HALF_STATE]
                xr, xi = lr * xr - li * xi + br, lr * xi + li * xr + bi
                states.append(jnp.concatenate([xr, xi], axis=1).astype(BF16))
            sre_ref[hf * HALF_STATE:(hf + 1) * HALF_STATE, :] = _transpose_blocks(xr)
            sim_ref[hf * HALF_STATE:(hf + 1) * HALF_STATE, :] = _transpose_blocks(xi)
            yparts.append(_dot(jnp.concatenate(states, axis=0), cd_ref[hf]))
        yperm = jnp.concatenate(yparts, axis=1)
        for s in range(U_SLABS):
            for t in range(t_new):
                us[s, pl.ds(t, nseq, stride=t_new), :] = (
                    yperm[t * nseq:(t + 1) * nseq, s * LANES:(s + 1) * LANES])
        y = jnp.concatenate([us[s] for s in range(U_SLABS)], axis=1)
        ssm_s[...] = _glu_tail(y, p_s[:, u_off:u_off + D_SSM], dskip_ref, wglu_ref, bglu_ref)

        ncol = sb * wbuf
        rho = lax.broadcasted_iota(jnp.int32, (brow, ncol), 0)
        kap = lax.broadcasted_iota(jnp.int32, (brow, ncol), 1)
        same_seq = (rho >> tshift) == (kap >> wshift)
        rho_w = lax.broadcasted_iota(jnp.int32, (brow, wbuf), 0)
        d_c = (rho_w & (t_new - 1)) - lax.broadcasted_iota(jnp.int32, (brow, wbuf), 1) + wbuf
        valid_c = (d_c >= 0) & (d_c < WINDOW)
        rho_n = lax.broadcasted_iota(jnp.int32, (brow, LANES), 0)
        kap_n = lax.broadcasted_iota(jnp.int32, (brow, LANES), 1)
        tq_n = rho_n & (t_new - 1)
        d_n = tq_n - (kap_n & (t_new - 1))
        valid_n = ((rho_n >> tshift) == (kap_n >> tshift)) & (d_n >= 0) & (kap_n < brow)
        for hd in range(N_HEADS):
            g, r = hd // N_REP, hd % N_REP
            tile = _masked_bias(relb_ref, hd, d_c, valid_c)
            bias_c[g, r * brow:(r + 1) * brow, :] = jnp.where(
                same_seq, jnp.concatenate([tile] * sb, axis=1), NEG_INF)
            bias_n[g, r * brow:(r + 1) * brow, :] = _masked_bias(relb_ref, hd, d_n, valid_n)

    def seq_block(k):
        first = k * sb
        row0 = pl.multiple_of((i * SAMPLE_BLOCKS + k) * brow, brow)
        kc_t = jnp.concatenate([ck_ref[first + b] for b in range(sb)], axis=1).astype(BF16)
        vc_t = jnp.concatenate([cv_ref[first + b] for b in range(sb)], axis=1).astype(BF16)
        pad = jnp.zeros((LANES - brow, D_KV), F32)
        kn_pad = jnp.concatenate([p_s[pl.ds(row0, brow), D_ATTN:D_ATTN + D_KV], pad], axis=0)
        vn_pad = jnp.concatenate(
            [p_s[pl.ds(row0, brow), D_ATTN + D_KV:D_ATTN + 2 * D_KV], pad], axis=0)
        kn_b = kn_pad.astype(BF16)
        vn_b = vn_pad.astype(BF16)
        q_chunks = [p_s[pl.ds(row0, brow), c * LANES:(c + 1) * LANES]
                    for c in range(D_ATTN // LANES)]
        o_by_g = []
        for g in range(N_KV_HEADS):
            q = _kv_lane_mask(_stack_queries(q_chunks, g), g)
            lc = _dot(q, kc_t) + bias_c[g]
            ln = _dot_nt(q, kn_b) + bias_n[g]
            s = _sink_column(sinks_ref, g, brow)
            m = jnp.maximum(jnp.maximum(jnp.max(lc, axis=-1, keepdims=True),
                                        jnp.max(ln, axis=-1, keepdims=True)), s)
            ec = jnp.exp(lc - m)
            en = jnp.exp(ln - m)
            denom = (jnp.sum(ec, axis=-1, keepdims=True) + jnp.sum(en, axis=-1, keepdims=True)
                     + jnp.exp(s - m))
            o = _dot_nt(ec.astype(BF16), vc_t) + _dot(en.astype(BF16), vn_b)
            o_by_g.append(o / denom)
        for c, chunk in enumerate(_unstack_heads(o_by_g, brow)):
            attn_s[pl.ds(row0, brow), c * LANES:(c + 1) * LANES] = chunk

        lane = lax.broadcasted_iota(jnp.int32, (D_KV, wbuf), 1)
        keep = wbuf - t_new
        for new_pad, old_ref, out_ref in ((kn_pad, ck_ref, ks_ref), (vn_pad, cv_ref, vs_ref)):
            new_t = new_pad.T
            for b in range(sb):
                shifted = pltpu.roll(old_ref[first + b], keep, 1)
                appended = pltpu.roll(new_t, (keep - b * t_new) % wbuf, 1)
                out_ref[first + b] = jnp.where(lane >= keep, appended, shifted)

    for k in range(SAMPLE_BLOCKS):
        seq_block(k)

    @pl.when(i == nsteps - 1)
    def _():
        mix = (_dot(attn_s[...].astype(BF16), wout_ref[0:D_ATTN, :])
               + _dot(ssm_s[...].astype(BF16), wout_ref[D_ATTN:D_ATTN + D_SSM, :]))
        o_ref[...] = x_ref[...] + mix


def _mix_sample(x, relb, sinks, g, win, wout, ck, cv, sre0, sim0, lam, bd, cd, dskip, wglu, bglu,
                *, nseq, t_new):
    rows = nseq * t_new
    wbuf = ck.shape[2]
    assert wbuf == LANES
    sb = SUBLANES
    brow = sb * t_new
    return pl.pallas_call(
        functools.partial(_mixs_kernel, nseq=nseq, t_new=t_new, sb=sb, wbuf=wbuf),
        grid=(nseq // (SAMPLE_BLOCKS * sb),),
        in_specs=[
            _smem_spec(), _smem_spec(),
            _const_spec((rows, D_MODEL)),
            _const_spec((1, D_MODEL)),
            _const_spec((D_MODEL, D_IN)),
            _const_spec((D_ATTN + D_SSM, D_MODEL)),
            pl.BlockSpec((SAMPLE_BLOCKS * sb, D_KV, wbuf), lambda i: (i, 0, 0)),
            pl.BlockSpec((SAMPLE_BLOCKS * sb, D_KV, wbuf), lambda i: (i, 0, 0)),
            _const_spec((2 * HALF_STATE, nseq)),
            _const_spec((2 * HALF_STATE, nseq)),
            _const_spec((4, HALF_STATE)),
            _const_spec((2, HALF_U, 2 * HALF_STATE)),
            _const_spec((2, 2 * HALF_STATE, HALF_U)),
            _const_spec((1, D_SSM)),
            _const_spec((D_SSM, D_SSM)),
            _const_spec((1, D_SSM)),
        ],
        out_specs=[
            pl.BlockSpec((rows, D_MODEL), lambda i: (0, 0)),
            pl.BlockSpec((SAMPLE_BLOCKS * sb, D_KV, wbuf), lambda i: (i, 0, 0)),
            pl.BlockSpec((SAMPLE_BLOCKS * sb, D_KV, wbuf), lambda i: (i, 0, 0)),
            pl.BlockSpec((2 * HALF_STATE, nseq), lambda i: (0, 0)),
            pl.BlockSpec((2 * HALF_STATE, nseq), lambda i: (0, 0)),
        ],
        out_shape=[
            jax.ShapeDtypeStruct((rows, D_MODEL), F32),
            jax.ShapeDtypeStruct((nseq, D_KV, wbuf), F32),
            jax.ShapeDtypeStruct((nseq, D_KV, wbuf), F32),
            jax.ShapeDtypeStruct((2 * HALF_STATE, nseq), F32),
            jax.ShapeDtypeStruct((2 * HALF_STATE, nseq), F32),
        ],
        scratch_shapes=[
            pltpu.VMEM((rows, D_IN), F32),
            pltpu.VMEM((rows, D_ATTN), F32),
            pltpu.VMEM((rows, D_SSM), F32),
            pltpu.VMEM((N_KV_HEADS, N_REP * brow, sb * wbuf), F32),
            pltpu.VMEM((N_KV_HEADS, N_REP * brow, LANES), F32),
            pltpu.VMEM((U_SLABS, rows, LANES), F32),
        ],
        compiler_params=pltpu.CompilerParams(
            dimension_semantics=("arbitrary",), vmem_limit_bytes=VMEM_LIMIT),
        name="mix_sample",
    )(relb, sinks, x, g, win, wout, ck, cv, sre0, sim0, lam, bd, cd, dskip, wglu, bglu)


def _s5_operators(log_dt, a_re, a_im, b_re, b_im, c_re, c_im):
    dt = jnp.exp(log_dt)[:, None]
    mag = jnp.exp(a_re * dt)
    lb_re = mag * jnp.cos(a_im * dt)
    lb_im = mag * jnp.sin(a_im * dt)
    den = a_re * a_re + a_im * a_im
    nr = lb_re - 1.0
    q_re = (nr * a_re + lb_im * a_im) / den
    q_im = (lb_im * a_re - nr * a_im) / den
    bb_re = q_re[..., None] * b_re - q_im[..., None] * b_im
    bb_im = q_re[..., None] * b_im + q_im[..., None] * b_re
    eye = jnp.eye(HALF_GROUPS, dtype=BF16)
    lam = jnp.stack([lb_re.reshape(2, HALF_STATE), lb_im.reshape(2, HALF_STATE)], axis=1)
    bb = jnp.stack([bb_re, bb_im]).astype(BF16).reshape(2, 2, HALF_GROUPS, SSM_STATE, SSM_GROUP)
    bd = jnp.einsum('rhgpc,gk->hgcrkp', bb, eye).reshape(2, HALF_U, 2 * HALF_STATE)
    cc = jnp.stack([c_re, -c_im]).astype(BF16).reshape(2, 2, HALF_GROUPS, SSM_GROUP, SSM_STATE)
    cd = jnp.einsum('rhgcp,gk->hrgpkc', cc, eye).reshape(2, 2 * HALF_STATE, HALF_U)
    return lam.reshape(4, HALF_STATE), bd, cd


def kernel(x_prompt, x_sample, cache_k, cache_v, state_ssm_re, state_ssm_im, rel_bias,
           ffn1_norm, ffn1_w_gate, ffn1_w_up, ffn1_w_down, mix_norm, w_in, sinks,
           log_dt, a_re, a_im, b_re, b_im, c_re, c_im, d_skip, w_glu, b_glu, w_out,
           ffn2_norm, ffn2_w_gate, ffn2_w_up, ffn2_w_down, final_norm):
    depth = w_in.shape[0]
    assert depth == 1
    batch, seq, _ = x_prompt.shape
    nseq, t_new, _ = x_sample.shape
    wbuf = cache_k.shape[2]
    fg = final_norm.reshape(1, D_MODEL)

    l = 0
    ffn1 = (ffn1_norm[l].reshape(1, D_MODEL), ffn1_w_gate[l], ffn1_w_up[l], ffn1_w_down[l], fg)
    ffn2 = (ffn2_norm[l].reshape(1, D_MODEL), ffn2_w_gate[l], ffn2_w_up[l], ffn2_w_down[l], fg)
    lam, bd, cd = _s5_operators(log_dt[l], a_re[l], a_im[l], b_re[l], b_im[l], c_re[l], c_im[l])
    mix_w = (mix_norm[l].reshape(1, D_MODEL), w_in[l].astype(BF16), w_out[l].astype(BF16))
    ssm_w = (lam, bd, cd, d_skip[l].reshape(1, D_SSM), w_glu[l].astype(BF16),
             b_glu[l].reshape(1, D_SSM))
    sinks_l = sinks[l]

    def window_in(c):
        return jnp.transpose(c, (0, 2, 3, 1)).reshape(c.shape[0], D_KV, c.shape[1])

    def window_out(w):
        n, _, pos = w.shape
        return jnp.transpose(w.reshape(n, N_KV_HEADS, HEAD_DIM, pos), (0, 3, 1, 2))[None]

    def state_in(s):
        return jnp.transpose(s, (1, 2, 0)).reshape(2 * HALF_STATE, s.shape[0])

    def state_out(s):
        return jnp.transpose(s.reshape(N_SSM_GROUPS, SSM_STATE, s.shape[1]), (2, 0, 1))[None]

    relb_t = rel_bias.T
    xp = x_prompt.reshape(batch * seq, D_MODEL)
    xs = x_sample.reshape(nseq * t_new, D_MODEL)
    ys, yp = _ffn(xp, xs, *ffn1, final_norm=False)
    yp, kp, vp, sre_p, sim_p = _mix_prompt(
        yp.reshape(batch, seq, D_MODEL), relb_t, sinks_l, *mix_w, *ssm_w)
    ys, ks, vs, sre_s, sim_s = _mix_sample(
        ys, relb_t, sinks_l, *mix_w,
        window_in(cache_k[l]), window_in(cache_v[l]),
        state_in(state_ssm_re[l]), state_in(state_ssm_im[l]),
        *ssm_w, nseq=nseq, t_new=t_new)
    y_sample, y_prompt = _ffn(yp.reshape(batch * seq, D_MODEL), ys, *ffn2, final_norm=True)
    y_prompt = y_prompt.reshape(batch, seq, D_MODEL)
    y_sample = y_sample.reshape(nseq, t_new, D_MODEL)

    st_p = (1, batch, N_SSM_GROUPS, SSM_STATE)
    return (y_prompt, y_sample,
            window_out(kp), window_out(vp), sre_p.reshape(st_p), sim_p.reshape(st_p),
            window_out(ks), window_out(vs), state_out(sre_s), state_out(sim_s))
```

```python
import functools
import math

import jax
import jax.numpy as jnp
from jax import lax
from jax.experimental import pallas as pl
from jax.experimental.pallas import tpu as pltpu

F32 = jnp.float32
BF16 = jnp.bfloat16

D_MODEL = 1024
HEAD_DIM = 64
D_ATTN = 512
N_HEADS = 8
N_KV_HEADS = 2
N_REP = 4
D_KV = 128
D_SSM = 512
SSM_GROUP = 16
N_SSM_GROUPS = 32
SSM_STATE = 64
WINDOW = 128
NUM_BUCKETS = 32
MAX_DISTANCE = 128
D_FF = 2816
D_IN = D_ATTN + 2 * D_KV + D_SSM
RMS_EPS = 1e-6
NEG_INF = -1e30

LANES = 128
SUBLANES = 8
MXU_DIM = 256
FF_CHUNKS = D_FF // MXU_DIM
HALF_GROUPS = N_SSM_GROUPS // 2
HALF_STATE = HALF_GROUPS * SSM_STATE
HALF_U = HALF_GROUPS * SSM_GROUP
U_SLABS = D_SSM // LANES
PITCH = WINDOW + SUBLANES
SCAN_STEPS = 32
RING = 3
SAMPLE_BLOCKS = 1
VMEM_LIMIT = 60 * 1024 * 1024


def _const_spec(shape):
    nd = len(shape)
    return pl.BlockSpec(shape, lambda *_: (0,) * nd, pipeline_mode=pl.Buffered(1))


def _smem_spec():
    return pl.BlockSpec(memory_space=pltpu.SMEM)


def _rmsnorm(x, g):
    r = lax.rsqrt(jnp.mean(x * x, axis=-1, keepdims=True) + RMS_EPS)
    return (x * r) * g


def _dot(a, b):
    return jnp.dot(a, b, preferred_element_type=F32)


def _dot_nt(a, b):
    return lax.dot_general(a, b, (((1,), (1,)), ((), ())), preferred_element_type=F32)


def _transpose_blocks(a):
    r, c = a.shape
    return jnp.concatenate(
        [jnp.concatenate([a[i * LANES:(i + 1) * LANES, j * LANES:(j + 1) * LANES].T
                          for i in range(r // LANES)], axis=1)
         for j in range(c // LANES)], axis=0)


def _ffn_tile(x, g_ref, wg_ref, wu_ref, wd_ref, fg_ref, final_norm, before_chunk=None):
    h = _rmsnorm(x, g_ref[...]).astype(BF16)
    acc = None
    for c in range(FF_CHUNKS):
        if before_chunk is not None:
            before_chunk(c)
        sl = slice(c * MXU_DIM, (c + 1) * MXU_DIM)
        gate = _dot(h, wg_ref[:, sl].astype(BF16))
        up = _dot(h, wu_ref[:, sl].astype(BF16))
        a = (gate * jax.nn.sigmoid(gate) * up).astype(BF16)
        part = _dot(a, wd_ref[sl, :].astype(BF16))
        acc = part if acc is None else acc + part
    y = x + 0.5 * acc
    if final_norm:
        y = _rmsnorm(y, fg_ref[...])
    return y


def _ffn_kernel(xs_ref, xp_ref, g_ref, wg_hbm, wu_hbm, wd_hbm, fg_ref, os_ref, op_ref,
                wg_s, wu_s, wd_s, sems, *, final_norm):
    i = pl.program_id(0)
    weights = (g_ref, wg_s, wu_s, wd_s, fg_ref)

    def chunk_copies(c):
        sl = slice(c * MXU_DIM, (c + 1) * MXU_DIM)
        return (pltpu.make_async_copy(wg_hbm.at[:, sl], wg_s.at[:, sl], sems.at[0, c]),
                pltpu.make_async_copy(wu_hbm.at[:, sl], wu_s.at[:, sl], sems.at[1, c]),
                pltpu.make_async_copy(wd_hbm.at[sl, :], wd_s.at[sl, :], sems.at[2, c]))

    @pl.when(i == 0)
    def _():
        for c in range(FF_CHUNKS):
            for cp in chunk_copies(c):
                cp.start()

        def wait_chunk(c):
            for cp in chunk_copies(c):
                cp.wait()

        os_ref[...] = _ffn_tile(xs_ref[...], *weights, final_norm, wait_chunk)

    @pl.when(i > 0)
    def _():
        op_ref[...] = _ffn_tile(xp_ref[...], *weights, final_norm)


def _ffn(xp, xs, g, wg, wu, wd, fg, *, final_norm):
    n_p, n_s = xp.shape[0], xs.shape[0]
    tm = n_s
    prompt_steps = n_p // tm
    assert prompt_steps * tm == n_p
    prompt_block = lambda i: (jnp.maximum(i - 1, 0), 0)
    hbm = pl.BlockSpec(memory_space=pl.ANY)
    return pl.pallas_call(
        functools.partial(_ffn_kernel, final_norm=final_norm),
        grid=(prompt_steps + 1,),
        in_specs=[
            _const_spec((n_s, D_MODEL)),
            pl.BlockSpec((tm, D_MODEL), prompt_block),
            _const_spec((1, D_MODEL)),
            hbm, hbm, hbm,
            _const_spec((1, D_MODEL)),
        ],
        out_specs=[
            pl.BlockSpec((n_s, D_MODEL), lambda i: (0, 0)),
            pl.BlockSpec((tm, D_MODEL), prompt_block),
        ],
        out_shape=[
            jax.ShapeDtypeStruct((n_s, D_MODEL), F32),
            jax.ShapeDtypeStruct((n_p, D_MODEL), F32),
        ],
        scratch_shapes=[
            pltpu.VMEM((D_MODEL, D_FF), F32),
            pltpu.VMEM((D_MODEL, D_FF), F32),
            pltpu.VMEM((D_FF, D_MODEL), F32),
            pltpu.SemaphoreType.DMA((3, FF_CHUNKS)),
        ],
        compiler_params=pltpu.CompilerParams(
            dimension_semantics=("arbitrary",), vmem_limit_bytes=VMEM_LIMIT),
        name="ffn_final" if final_norm else "ffn",
    )(xs, xp, g, wg, wu, wd, fg)


def _t5_bucket(d):
    d = jnp.maximum(d, 0)
    max_exact = NUM_BUCKETS // 2
    df = jnp.maximum(d, 1).astype(F32)
    large = max_exact + (jnp.log(df / max_exact) / math.log(MAX_DISTANCE / max_exact)
                         * (NUM_BUCKETS - max_exact)).astype(jnp.int32)
    large = jnp.minimum(large, NUM_BUCKETS - 1)
    return jnp.where(d < max_exact, d, large)


def _masked_bias(relb_ref, head, d, valid):
    bucket = _t5_bucket(d)
    b = jnp.zeros(d.shape, F32)
    for k in range(NUM_BUCKETS):
        b = jnp.where(bucket == k, relb_ref[head, k], b)
    return jnp.where(valid, b, NEG_INF)


def _kv_lane_mask(t, g):
    lane = lax.broadcasted_iota(jnp.int32, t.shape, 1)
    return jnp.where((lane >= g * HEAD_DIM) & (lane < (g + 1) * HEAD_DIM), t, jnp.zeros_like(t))


def _stack_queries(q_chunks, g):
    qs = []
    for r in range(N_REP):
        qc = q_chunks[2 * g + r // 2]
        if r % 2 != g:
            qc = pltpu.roll(qc, HEAD_DIM, 1)
        qs.append(qc)
    return (jnp.concatenate(qs, axis=0) * (HEAD_DIM ** -0.5)).astype(BF16)


def _sink_column(sinks_ref, g, rows):
    return jnp.concatenate(
        [jnp.full((rows, 1), sinks_ref[N_REP * g + r], F32) for r in range(N_REP)], axis=0)


def _unstack_heads(o_by_g, rows):
    lane = lax.broadcasted_iota(jnp.int32, (rows, LANES), 1)
    chunks = []
    for c in range(D_ATTN // LANES):
        g = c // 2
        halves = []
        for half in range(2):
            r = 2 * (c % 2) + half
            piece = o_by_g[g][r * rows:(r + 1) * rows]
            if half != g:
                piece = pltpu.roll(piece, HEAD_DIM, 1)
            halves.append(piece)
        chunks.append(jnp.where(lane < HEAD_DIM, halves[0], halves[1]))
    return chunks


def _glu_tail(y, u, dskip_ref, wglu_ref, bglu_ref):
    y = y + dskip_ref[...] * u
    y = jax.nn.gelu(y)
    z = _dot(y.astype(BF16), wglu_ref[...]) + bglu_ref[...]
    return y * jax.nn.sigmoid(z)


def _mixp_kernel(relb_ref, sinks_ref, x_ref, g_ref, win_ref, wout_ref, lam_ref, bd_ref, cd_ref,
                 dskip_ref, wglu_ref, bglu_ref,
                 o_ref, kp_ref, vp_ref, sre_ref, sim_ref,
                 p_s, kband, vband_t, bias_t, bias_later, us, ys, state, attn_s, *ring_refs,
                 nb, blk):
    i = pl.program_id(0)
    rows = nb * blk
    rings, u_rings = ring_refs[:RING], ring_refs[RING:]

    @pl.when(i == 0)
    def _():
        kband[...] = jnp.zeros(kband.shape, BF16)
        vband_t[...] = jnp.zeros(vband_t.shape, BF16)
        state[...] = jnp.zeros(state.shape, F32)

    @pl.when(i == 0)
    def _():
        kj = lax.broadcasted_iota(jnp.int32, (2 * blk, blk), 0)
        qi = lax.broadcasted_iota(jnp.int32, (2 * blk, blk), 1)
        d = qi - kj + blk
        valid = (d >= 0) & (d < WINDOW)
        for h in range(N_HEADS):
            cols = slice((h % N_REP) * blk, (h % N_REP + 1) * blk)
            tile = _masked_bias(relb_ref, h, d, valid)
            bias_later[h // N_REP, :, cols] = tile
            bias_t[h // N_REP, :, cols] = jnp.where(kj >= blk, tile, NEG_INF)

    @pl.when(i == 1)
    def _():
        bias_t[...] = bias_later[...]

    h = _rmsnorm(x_ref[...].reshape(rows, D_MODEL), g_ref[...]).astype(BF16)
    p_s[...] = _dot(h, win_ref[...])

    kband[:, blk:2 * blk, :] = (
        p_s[:, D_ATTN:D_ATTN + D_KV].reshape(nb, blk, D_KV).astype(BF16))

    @pl.when(i == pl.num_programs(0) - 1)
    def _():
        for b in range(nb):
            rows_b = slice(b * blk, (b + 1) * blk)
            kp_ref[b] = p_s[rows_b, D_ATTN:D_ATTN + D_KV].T
            vp_ref[b] = p_s[rows_b, D_ATTN + D_KV:D_ATTN + 2 * D_KV].T

    u_off = D_ATTN + 2 * D_KV
    for s in range(U_SLABS):
        for b in range(nb):
            us[s, b * PITCH:b * PITCH + blk, :] = (
                p_s[b * blk:(b + 1) * blk, u_off + s * LANES:u_off + (s + 1) * LANES])

    def attention(b):
        rows_b = slice(b * blk, (b + 1) * blk)
        kb = kband[b]
        vband_t[b, :, blk:2 * blk] = p_s[rows_b, D_ATTN + D_KV:D_ATTN + 2 * D_KV].T.astype(BF16)
        vt = vband_t[b]
        feat = lax.broadcasted_iota(jnp.int32, vt.shape, 0)
        q_chunks = [p_s[rows_b, c * LANES:(c + 1) * LANES] for c in range(D_ATTN // LANES)]
        o_by_g = []
        for g in range(N_KV_HEADS):
            q = _stack_queries(q_chunks, g)
            st = _dot_nt(_kv_lane_mask(kb, g), q) + bias_t[g]
            s = jnp.concatenate(
                [jnp.full((1, blk), sinks_ref[N_REP * g + r], F32) for r in range(N_REP)], axis=1)
            m = jnp.maximum(jnp.max(st, axis=0, keepdims=True), s)
            e = jnp.exp(st - m).astype(BF16)
            in_g = (feat >= g * HEAD_DIM) & (feat < (g + 1) * HEAD_DIM)
            ot = _dot(jnp.where(in_g, vt, jnp.ones_like(vt)), e)
            other = (1 - g) * HEAD_DIM
            denom = ot[other:other + 1, :] + jnp.exp(s - m)
            o_by_g.append(ot[g * HEAD_DIM:(g + 1) * HEAD_DIM, :] / denom)
        for c in range(D_ATTN // LANES):
            og = o_by_g[c // 2]
            cc = 2 * (c % 2)
            chunk_t = jnp.concatenate(
                [og[:, cc * blk:(cc + 1) * blk], og[:, (cc + 1) * blk:(cc + 2) * blk]], axis=0)
            attn_s[rows_b, c * LANES:(c + 1) * LANES] = chunk_t.T

    def scan_project_in(j):
        slot = j % RING
        steps = []
        for t in range(SCAN_STEPS):
            steps.append(jnp.concatenate(
                [us[s, pl.ds(j * SCAN_STEPS + t, nb, stride=PITCH), :] for s in range(U_SLABS)],
                axis=1))
        u_chunk = jnp.concatenate(steps, axis=0)
        u_rings[slot][...] = u_chunk
        ub = u_chunk.astype(BF16)
        for hf in range(2):
            rings[slot][:, 2 * hf * HALF_STATE:2 * (hf + 1) * HALF_STATE] = _dot(
                ub[:, hf * HALF_U:(hf + 1) * HALF_U], bd_ref[hf])

    def scan_recurrence(j):
        slot = j % RING
        for hf in range(2):
            lr = jnp.broadcast_to(lam_ref[2 * hf:2 * hf + 1, :], (nb, HALF_STATE))
            li = jnp.broadcast_to(lam_ref[2 * hf + 1:2 * hf + 2, :], (nb, HALF_STATE))
            re = slice(2 * hf * HALF_STATE, (2 * hf + 1) * HALF_STATE)
            im = slice((2 * hf + 1) * HALF_STATE, (2 * hf + 2) * HALF_STATE)
            xr = state[:, re]
            xi = state[:, im]
            for t in range(SCAN_STEPS):
                rows_t = slice(t * nb, (t + 1) * nb)
                xr, xi = (lr * xr - li * xi + rings[slot][rows_t, re],
                          lr * xi + li * xr + rings[slot][rows_t, im])
                rings[slot][rows_t, re] = xr
                rings[slot][rows_t, im] = xi
            state[:, re] = xr
            state[:, im] = xi

    def scan_project_out(j):
        slot = j % RING
        y = jnp.concatenate(
            [_dot(rings[slot][:, 2 * hf * HALF_STATE:2 * (hf + 1) * HALF_STATE].astype(BF16),
                  cd_ref[hf]) for hf in range(2)], axis=1)
        ssm = _glu_tail(y, u_rings[slot][...], dskip_ref, wglu_ref, bglu_ref)
        for t in range(SCAN_STEPS):
            for s in range(U_SLABS):
                ys[s, pl.ds(j * SCAN_STEPS + t, nb, stride=PITCH), :] = (
                    ssm[t * nb:(t + 1) * nb, s * LANES:(s + 1) * LANES])

    n_chunks = blk // SCAN_STEPS
    n_iters = n_chunks + 2
    for jj in range(n_iters):
        if 2 <= jj:
            scan_project_out(jj - 2)
        if 1 <= jj < n_chunks + 1:
            scan_recurrence(jj - 1)
        if jj < n_chunks:
            scan_project_in(jj)
        for b in range(nb):
            if b * n_iters // nb == jj:
                attention(b)

    kband[:, 0:blk, :] = kband[:, blk:2 * blk, :]
    vband_t[:, :, 0:blk] = vband_t[:, :, blk:2 * blk]
    for hf in range(2):
        sre_ref[:, hf * HALF_STATE:(hf + 1) * HALF_STATE] = (
            state[:, 2 * hf * HALF_STATE:(2 * hf + 1) * HALF_STATE])
        sim_ref[:, hf * HALF_STATE:(hf + 1) * HALF_STATE] = (
            state[:, (2 * hf + 1) * HALF_STATE:(2 * hf + 2) * HALF_STATE])

    ssm = jnp.concatenate(
        [jnp.concatenate([ys[s, b * PITCH:b * PITCH + blk, :] for b in range(nb)], axis=0)
         for s in range(U_SLABS)], axis=1)
    mix = (_dot(attn_s[...].astype(BF16), wout_ref[0:D_ATTN, :])
           + _dot(ssm.astype(BF16), wout_ref[D_ATTN:D_ATTN + D_SSM, :]))
    o_ref[...] = x_ref[...] + mix.reshape(nb, blk, D_MODEL)


def _mix_prompt(x, relb, sinks, g, win, wout, lam, bd, cd, dskip, wglu, bglu):
    nb, seq, _ = x.shape
    blk = WINDOW
    rows = nb * blk
    const2 = _const_spec
    return pl.pallas_call(
        functools.partial(_mixp_kernel, nb=nb, blk=blk),
        grid=(seq // blk,),
        in_specs=[
            _smem_spec(), _smem_spec(),
            pl.BlockSpec((nb, blk, D_MODEL), lambda i: (0, i, 0)),
            const2((1, D_MODEL)),
            const2((D_MODEL, D_IN)),
            const2((D_ATTN + D_SSM, D_MODEL)),
            const2((4, HALF_STATE)),
            const2((2, HALF_U, 2 * HALF_STATE)),
            const2((2, 2 * HALF_STATE, HALF_U)),
            const2((1, D_SSM)),
            const2((D_SSM, D_SSM)),
            const2((1, D_SSM)),
        ],
        out_specs=[
            pl.BlockSpec((nb, blk, D_MODEL), lambda i: (0, i, 0)),
            pl.BlockSpec((nb, blk, D_KV), lambda i: (0, 0, 0)),
            pl.BlockSpec((nb, blk, D_KV), lambda i: (0, 0, 0)),
            pl.BlockSpec((nb, 2 * HALF_STATE), lambda i: (0, 0)),
            pl.BlockSpec((nb, 2 * HALF_STATE), lambda i: (0, 0)),
        ],
        out_shape=[
            jax.ShapeDtypeStruct((nb, seq, D_MODEL), F32),
            jax.ShapeDtypeStruct((nb, blk, D_KV), F32),
            jax.ShapeDtypeStruct((nb, blk, D_KV), F32),
            jax.ShapeDtypeStruct((nb, 2 * HALF_STATE), F32),
            jax.ShapeDtypeStruct((nb, 2 * HALF_STATE), F32),
        ],
        scratch_shapes=[
            pltpu.VMEM((rows, D_IN), F32),
            pltpu.VMEM((nb, 2 * blk, D_KV), BF16),
            pltpu.VMEM((nb, D_KV, 2 * blk), BF16),
            pltpu.VMEM((N_KV_HEADS, 2 * blk, N_REP * blk), F32),
            pltpu.VMEM((N_KV_HEADS, 2 * blk, N_REP * blk), F32),
            pltpu.VMEM((U_SLABS, nb * PITCH, LANES), F32),
            pltpu.VMEM((U_SLABS, nb * PITCH, LANES), F32),
            pltpu.VMEM((nb, 4 * HALF_STATE), F32),
            pltpu.VMEM((rows, D_ATTN), F32),
        ] + [pltpu.VMEM((SCAN_STEPS * nb, 4 * HALF_STATE), F32)] * RING
          + [pltpu.VMEM((SCAN_STEPS * nb, D_SSM), F32)] * RING,
        compiler_params=pltpu.CompilerParams(
            dimension_semantics=("arbitrary",), vmem_limit_bytes=VMEM_LIMIT),
        name="mix_prompt",
    )(relb, sinks, x, g, win, wout, lam, bd, cd, dskip, wglu, bglu)


def _mixs_kernel(relb_ref, sinks_ref, x_ref, g_ref, win_ref, wout_ref, ck_ref, cv_ref,
                 sre0_ref, sim0_ref, lam_ref, bd_ref, cd_ref, dskip_ref, wglu_ref, bglu_ref,
                 o_ref, ks_ref, vs_ref, sre_ref, sim_ref,
                 p_s, attn_s, ssm_s, bias_c, bias_n, us,
                 *, nseq, t_new, sb, wbuf):
    i = pl.program_id(0)
    nsteps = pl.num_programs(0)
    rows = nseq * t_new
    brow = sb * t_new
    u_off = D_ATTN + 2 * D_KV
    tshift = t_new.bit_length() - 1
    wshift = wbuf.bit_length() - 1

    @pl.when(i == 0)
    def _():
        x = x_ref[...]
        h = _rmsnorm(x, g_ref[...]).astype(BF16)
        p_s[...] = _dot(h, win_ref[...])

        for s in range(U_SLABS):
            us[s] = p_s[:, u_off + s * LANES:u_off + (s + 1) * LANES]
        uperm = jnp.concatenate(
            [jnp.concatenate([us[s, pl.ds(t, nseq, stride=t_new), :] for t in range(t_new)], axis=0)
             for s in range(U_SLABS)], axis=1)
        yparts = []
        for hf in range(2):
            bu = _dot(uperm[:, hf * HALF_U:(hf + 1) * HALF_U].astype(BF16), bd_ref[hf])
            lr = lam_ref[2 * hf:2 * hf + 1, :]
            li = lam_ref[2 * hf + 1:2 * hf + 2, :]
            xr = _transpose_blocks(sre0_ref[hf * HALF_STATE:(hf + 1) * HALF_STATE, :])
            xi = _transpose_blocks(sim0_ref[hf * HALF_STATE:(hf + 1) * HALF_STATE, :])
            states = []
            for t in range(t_new):
                br = bu[t * nseq:(t + 1) * nseq, 0:HALF_STATE]
                bi = bu[t * nseq:(t + 1) * nseq, HALF_STATE:2 * HALF_STATE]
                xr, xi = lr * xr - li * xi + br, lr * xi + li * xr + bi
                states.append(jnp.concatenate([xr, xi], axis=1).astype(BF16))
            sre_ref[hf * HALF_STATE:(hf + 1) * HALF_STATE, :] = _transpose_blocks(xr)
            sim_ref[hf * HALF_STATE:(hf + 1) * HALF_STATE, :] = _transpose_blocks(xi)
            yparts.append(_dot(jnp.concatenate(states, axis=0), cd_ref[hf]))
        yperm = jnp.concatenate(yparts, axis=1)
        for s in range(U_SLABS):
            for t in range(t_new):
                us[s, pl.ds(t, nseq, stride=t_new), :] = (
                    yperm[t * nseq:(t + 1) * nseq, s * LANES:(s + 1) * LANES])
        y = jnp.concatenate([us[s] for s in range(U_SLABS)], axis=1)
        ssm_s[...] = _glu_tail(y, p_s[:, u_off:u_off + D_SSM], dskip_ref, wglu_ref, bglu_ref)

        ncol = sb * wbuf
        rho = lax.broadcasted_iota(jnp.int32, (brow, ncol), 0)
        kap = lax.broadcasted_iota(jnp.int32, (brow, ncol), 1)
        same_seq = (rho >> tshift) == (kap >> wshift)
        rho_w = lax.broadcasted_iota(jnp.int32, (brow, wbuf), 0)
        d_c = (rho_w & (t_new - 1)) - lax.broadcasted_iota(jnp.int32, (brow, wbuf), 1) + wbuf
        valid_c = (d_c >= 0) & (d_c < WINDOW)
        rho_n = lax.broadcasted_iota(jnp.int32, (brow, LANES), 0)
        kap_n = lax.broadcasted_iota(jnp.int32, (brow, LANES), 1)
        tq_n = rho_n & (t_new - 1)
        d_n = tq_n - (kap_n & (t_new - 1))
        valid_n = ((rho_n >> tshift) == (kap_n >> tshift)) & (d_n >= 0) & (kap_n < brow)
        for hd in range(N_HEADS):
            g, r = hd // N_REP, hd % N_REP
            tile = _masked_bias(relb_ref, hd, d_c, valid_c)
            bias_c[g, r * brow:(r + 1) * brow, :] = jnp.where(
                same_seq, jnp.concatenate([tile] * sb, axis=1), NEG_INF)
            bias_n[g, r * brow:(r + 1) * brow, :] = _masked_bias(relb_ref, hd, d_n, valid_n)

    def seq_block(k):
        first = k * sb
        row0 = pl.multiple_of((i * SAMPLE_BLOCKS + k) * brow, brow)
        kc_t = jnp.concatenate([ck_ref[first + b] for b in range(sb)], axis=1).astype(BF16)
        vc_t = jnp.concatenate([cv_ref[first + b] for b in range(sb)], axis=1).astype(BF16)
        pad = jnp.zeros((LANES - brow, D_KV), F32)
        kn_pad = jnp.concatenate([p_s[pl.ds(row0, brow), D_ATTN:D_ATTN + D_KV], pad], axis=0)
        vn_pad = jnp.concatenate(
            [p_s[pl.ds(row0, brow), D_ATTN + D_KV:D_ATTN + 2 * D_KV], pad], axis=0)
        kn_b = kn_pad.astype(BF16)
        vn_b = vn_pad.astype(BF16)
        q_chunks = [p_s[pl.ds(row0, brow), c * LANES:(c + 1) * LANES]
                    for c in range(D_ATTN // LANES)]
        o_by_g = []
        for g in range(N_KV_HEADS):
            q = _kv_lane_mask(_stack_queries(q_chunks, g), g)
            lc = _dot(q, kc_t) + bias_c[g]
            ln = _dot_nt(q, kn_b) + bias_n[g]
            s = _sink_column(sinks_ref, g, brow)
            m = jnp.maximum(jnp.maximum(jnp.max(lc, axis=-1, keepdims=True),
                                        jnp.max(ln, axis=-1, keepdims=True)), s)
            ec = jnp.exp(lc - m)
            en = jnp.exp(ln - m)
            denom = (jnp.sum(ec, axis=-1, keepdims=True) + jnp.sum(en, axis=-1, keepdims=True)
                     + jnp.exp(s - m))
            o = _dot_nt(ec.astype(BF16), vc_t) + _dot(en.astype(BF16), vn_b)
            o_by_g.append(o / denom)
        for c, chunk in enumerate(_unstack_heads(o_by_g, brow)):
            attn_s[pl.ds(row0, brow), c * LANES:(c + 1) * LANES] = chunk

        lane = lax.broadcasted_iota(jnp.int32, (D_KV, wbuf), 1)
        keep = wbuf - t_new
        for new_pad, old_ref, out_ref in ((kn_pad, ck_ref, ks_ref), (vn_pad, cv_ref, vs_ref)):
            new_t = new_pad.T
            for b in range(sb):
                shifted = pltpu.roll(old_ref[first + b], keep, 1)
                appended = pltpu.roll(new_t, (keep - b * t_new) % wbuf, 1)
                out_ref[first + b] = jnp.where(lane >= keep, appended, shifted)

    for k in range(SAMPLE_BLOCKS):
        seq_block(k)

    @pl.when(i == nsteps - 1)
    def _():
        mix = (_dot(attn_s[...].astype(BF16), wout_ref[0:D_ATTN, :])
               + _dot(ssm_s[...].astype(BF16), wout_ref[D_ATTN:D_ATTN + D_SSM, :]))
        o_ref[...] = x_ref[...] + mix


def _mix_sample(x, relb, sinks, g, win, wout, ck, cv, sre0, sim0, lam, bd, cd, dskip, wglu, bglu,
                *, nseq, t_new):
    rows = nseq * t_new
    wbuf = ck.shape[2]
    assert wbuf == LANES
    sb = SUBLANES
    brow = sb * t_new
    return pl.pallas_call(
        functools.partial(_mixs_kernel, nseq=nseq, t_new=t_new, sb=sb, wbuf=wbuf),
        grid=(nseq // (SAMPLE_BLOCKS * sb),),
        in_specs=[
            _smem_spec(), _smem_spec(),
            _const_spec((rows, D_MODEL)),
            _const_spec((1, D_MODEL)),
            _const_spec((D_MODEL, D_IN)),
            _const_spec((D_ATTN + D_SSM, D_MODEL)),
            pl.BlockSpec((SAMPLE_BLOCKS * sb, D_KV, wbuf), lambda i: (i, 0, 0)),
            pl.BlockSpec((SAMPLE_BLOCKS * sb, D_KV, wbuf), lambda i: (i, 0, 0)),
            _const_spec((2 * HALF_STATE, nseq)),
            _const_spec((2 * HALF_STATE, nseq)),
            _const_spec((4, HALF_STATE)),
            _const_spec((2, HALF_U, 2 * HALF_STATE)),
            _const_spec((2, 2 * HALF_STATE, HALF_U)),
            _const_spec((1, D_SSM)),
            _const_spec((D_SSM, D_SSM)),
            _const_spec((1, D_SSM)),
        ],
        out_specs=[
            pl.BlockSpec((rows, D_MODEL), lambda i: (0, 0)),
            pl.BlockSpec((SAMPLE_BLOCKS * sb, D_KV, wbuf), lambda i: (i, 0, 0)),
            pl.BlockSpec((SAMPLE_BLOCKS * sb, D_KV, wbuf), lambda i: (i, 0, 0)),
            pl.BlockSpec((2 * HALF_STATE, nseq), lambda i: (0, 0)),
            pl.BlockSpec((2 * HALF_STATE, nseq), lambda i: (0, 0)),
        ],
        out_shape=[
            jax.ShapeDtypeStruct((rows, D_MODEL), F32),
            jax.ShapeDtypeStruct((nseq, D_KV, wbuf), F32),
            jax.ShapeDtypeStruct((nseq, D_KV, wbuf), F32),
            jax.ShapeDtypeStruct((2 * HALF_STATE, nseq), F32),
            jax.ShapeDtypeStruct((2 * HALF_STATE, nseq), F32),
        ],
        scratch_shapes=[
            pltpu.VMEM((rows, D_IN), F32),
            pltpu.VMEM((rows, D_ATTN), F32),
            pltpu.VMEM((rows, D_SSM), F32),
            pltpu.VMEM((N_KV_HEADS, N_REP * brow, sb * wbuf), F32),
            pltpu.VMEM((N_KV_HEADS, N_REP * brow, LANES), F32),
            pltpu.VMEM((U_SLABS, rows, LANES), F32),
        ],
        compiler_params=pltpu.CompilerParams(
            dimension_semantics=("arbitrary",), vmem_limit_bytes=VMEM_LIMIT),
        name="mix_sample",
    )(relb, sinks, x, g, win, wout, ck, cv, sre0, sim0, lam, bd, cd, dskip, wglu, bglu)


def _s5_operators(log_dt, a_re, a_im, b_re, b_im, c_re, c_im):
    dt = jnp.exp(log_dt)[:, None]
    mag = jnp.exp(a_re * dt)
    lb_re = mag * jnp.cos(a_im * dt)
    lb_im = mag * jnp.sin(a_im * dt)
    den = a_re * a_re + a_im * a_im
    nr = lb_re - 1.0
    q_re = (nr * a_re + lb_im * a_im) / den
    q_im = (lb_im * a_re - nr * a_im) / den
    bb_re = q_re[..., None] * b_re - q_im[..., None] * b_im
    bb_im = q_re[..., None] * b_im + q_im[..., None] * b_re
    eye = jnp.eye(HALF_GROUPS, dtype=BF16)
    lam = jnp.stack([lb_re.reshape(2, HALF_STATE), lb_im.reshape(2, HALF_STATE)], axis=1)
    bb = jnp.stack([bb_re, bb_im]).astype(BF16).reshape(2, 2, HALF_GROUPS, SSM_STATE, SSM_GROUP)
    bd = jnp.einsum('rhgpc,gk->hgcrkp', bb, eye).reshape(2, HALF_U, 2 * HALF_STATE)
    cc = jnp.stack([c_re, -c_im]).astype(BF16).reshape(2, 2, HALF_GROUPS, SSM_GROUP, SSM_STATE)
    cd = jnp.einsum('rhgcp,gk->hrgpkc', cc, eye).reshape(2, 2 * HALF_STATE, HALF_U)
    return lam.reshape(4, HALF_STATE), bd, cd


def kernel(x_prompt, x_sample, cache_k, cache_v, state_ssm_re, state_ssm_im, rel_bias,
           ffn1_norm, ffn1_w_gate, ffn1_w_up, ffn1_w_down, mix_norm, w_in, sinks,
           log_dt, a_re, a_im, b_re, b_im, c_re, c_im, d_skip, w_glu, b_glu, w_out,
           ffn2_norm, ffn2_w_gate, ffn2_w_up, ffn2_w_down, final_norm):
    depth = w_in.shape[0]
    assert depth == 1
    batch, seq, _ = x_prompt.shape
    nseq, t_new, _ = x_sample.shape
    fg = final_norm.reshape(1, D_MODEL)

    l = 0
    ffn1 = (ffn1_norm[l].reshape(1, D_MODEL), ffn1_w_gate[l], ffn1_w_up[l], ffn1_w_down[l], fg)
    ffn2 = (ffn2_norm[l].reshape(1, D_MODEL), ffn2_w_gate[l], ffn2_w_up[l], ffn2_w_down[l], fg)
    lam, bd, cd = _s5_operators(log_dt[l], a_re[l], a_im[l], b_re[l], b_im[l], c_re[l], c_im[l])
    mix_w = (mix_norm[l].reshape(1, D_MODEL), w_in[l].astype(BF16), w_out[l].astype(BF16))
    ssm_w = (lam, bd, cd, d_skip[l].reshape(1, D_SSM), w_glu[l].astype(BF16),
             b_glu[l].reshape(1, D_SSM))
    sinks_l = sinks[l]

    def window_in(c):
        return jnp.transpose(c, (0, 2, 3, 1)).reshape(c.shape[0], D_KV, c.shape[1])

    def window_out(w):
        n, _, pos = w.shape
        return jnp.transpose(w.reshape(n, N_KV_HEADS, HEAD_DIM, pos), (0, 3, 1, 2))[None]

    def state_in(s):
        return jnp.transpose(s, (1, 2, 0)).reshape(2 * HALF_STATE, s.shape[0])

    def state_out(s):
        return jnp.transpose(s.reshape(N_SSM_GROUPS, SSM_STATE, s.shape[1]), (2, 0, 1))[None]

    relb_t = rel_bias.T
    xp = x_prompt.reshape(batch * seq, D_MODEL)
    xs = x_sample.reshape(nseq * t_new, D_MODEL)
    ys, yp = _ffn(xp, xs, *ffn1, final_norm=False)
    yp, kp, vp, sre_p, sim_p = _mix_prompt(
        yp.reshape(batch, seq, D_MODEL), relb_t, sinks_l, *mix_w, *ssm_w)
    ys, ks, vs, sre_s, sim_s = _mix_sample(
        ys, relb_t, sinks_l, *mix_w,
        window_in(cache_k[l]), window_in(cache_v[l]),
        state_in(state_ssm_re[l]), state_in(state_ssm_im[l]),
        *ssm_w, nseq=nseq, t_new=t_new)
    y_sample, y_prompt = _ffn(yp.reshape(batch * seq, D_MODEL), ys, *ffn2, final_norm=True)
    y_prompt = y_prompt.reshape(batch, seq, D_MODEL)
    y_sample = y_sample.reshape(nseq, t_new, D_MODEL)

    st_p = (1, batch, N_SSM_GROUPS, SSM_STATE)
    return (y_prompt, y_sample,
            window_out(kp), window_out(vp), sre_p.reshape(st_p), sim_p.reshape(st_p),
            window_out(ks), window_out(vs), state_out(sre_s), state_out(sim_s))
```

```python
import functools
import math

import jax
import jax.numpy as jnp
from jax import lax
from jax.experimental import pallas as pl
from jax.experimental.pallas import tpu as pltpu

F32 = jnp.float32
BF16 = jnp.bfloat16

D_MODEL = 1024
HEAD_DIM = 64
D_ATTN = 512
N_HEADS = 8
N_KV_HEADS = 2
N_REP = 4
D_KV = 128
D_SSM = 512
SSM_GROUP = 16
N_SSM_GROUPS = 32
SSM_STATE = 64
WINDOW = 128
NUM_BUCKETS = 32
MAX_DISTANCE = 128
D_FF = 2816
D_IN = D_ATTN + 2 * D_KV + D_SSM
RMS_EPS = 1e-6
NEG_INF = -1e30
LOG2E = math.log2(math.e)

LANES = 128
SUBLANES = 8
MXU_DIM = 256
FF_CHUNKS = D_FF // MXU_DIM
HALF_GROUPS = N_SSM_GROUPS // 2
HALF_STATE = HALF_GROUPS * SSM_STATE
HALF_U = HALF_GROUPS * SSM_GROUP
U_SLABS = D_SSM // LANES
PITCH = WINDOW + SUBLANES
SCAN_STEPS = 32
RING = 3
SAMPLE_BLOCKS = 1
VMEM_LIMIT = 60 * 1024 * 1024


def _const_spec(shape):
    nd = len(shape)
    return pl.BlockSpec(shape, lambda *_: (0,) * nd, pipeline_mode=pl.Buffered(1))


def _smem_spec():
    return pl.BlockSpec(memory_space=pltpu.SMEM)


def _rmsnorm(x, g):
    r = lax.rsqrt(jnp.mean(x * x, axis=-1, keepdims=True) + RMS_EPS)
    return (x * r) * g


def _dot(a, b):
    return jnp.dot(a, b, preferred_element_type=F32)


def _dot_nt(a, b):
    return lax.dot_general(a, b, (((1,), (1,)), ((), ())), preferred_element_type=F32)


def _transpose_blocks(a):
    r, c = a.shape
    return jnp.concatenate(
        [jnp.concatenate([a[i * LANES:(i + 1) * LANES, j * LANES:(j + 1) * LANES].T
                          for i in range(r // LANES)], axis=1)
         for j in range(c // LANES)], axis=0)


def _ffn_tile(x, g_ref, wg_ref, wu_ref, wd_ref, fg_ref, final_norm, before_chunk=None):
    h = _rmsnorm(x, g_ref[...]).astype(BF16)
    acc = None
    for c in range(FF_CHUNKS):
        if before_chunk is not None:
            before_chunk(c)
        sl = slice(c * MXU_DIM, (c + 1) * MXU_DIM)
        gate = _dot(h, wg_ref[:, sl].astype(BF16))
        up = _dot(h, wu_ref[:, sl].astype(BF16))
        a = (gate * jax.nn.sigmoid(gate) * up).astype(BF16)
        part = _dot(a, wd_ref[sl, :].astype(BF16))
        acc = part if acc is None else acc + part
    y = x + 0.5 * acc
    if final_norm:
        y = _rmsnorm(y, fg_ref[...])
    return y


def _ffn_kernel(xs_ref, xp_ref, g_ref, wg_hbm, wu_hbm, wd_hbm, fg_ref, os_ref, op_ref,
                wg_s, wu_s, wd_s, sems, *, final_norm):
    i = pl.program_id(0)
    weights = (g_ref, wg_s, wu_s, wd_s, fg_ref)

    def chunk_copies(c):
        sl = slice(c * MXU_DIM, (c + 1) * MXU_DIM)
        return (pltpu.make_async_copy(wg_hbm.at[:, sl], wg_s.at[:, sl], sems.at[0, c]),
                pltpu.make_async_copy(wu_hbm.at[:, sl], wu_s.at[:, sl], sems.at[1, c]),
                pltpu.make_async_copy(wd_hbm.at[sl, :], wd_s.at[sl, :], sems.at[2, c]))

    @pl.when(i == 0)
    def _():
        for c in range(FF_CHUNKS):
            for cp in chunk_copies(c):
                cp.start()

        def wait_chunk(c):
            for cp in chunk_copies(c):
                cp.wait()

        os_ref[...] = _ffn_tile(xs_ref[...], *weights, final_norm, wait_chunk)

    @pl.when(i > 0)
    def _():
        op_ref[...] = _ffn_tile(xp_ref[...], *weights, final_norm)


def _ffn(xp, xs, g, wg, wu, wd, fg, *, final_norm):
    n_p, n_s = xp.shape[0], xs.shape[0]
    tm = n_s
    prompt_steps = n_p // tm
    assert prompt_steps * tm == n_p
    prompt_block = lambda i: (jnp.maximum(i - 1, 0), 0)
    hbm = pl.BlockSpec(memory_space=pl.ANY)
    return pl.pallas_call(
        functools.partial(_ffn_kernel, final_norm=final_norm),
        grid=(prompt_steps + 1,),
        in_specs=[
            _const_spec((n_s, D_MODEL)),
            pl.BlockSpec((tm, D_MODEL), prompt_block),
            _const_spec((1, D_MODEL)),
            hbm, hbm, hbm,
            _const_spec((1, D_MODEL)),
        ],
        out_specs=[
            pl.BlockSpec((n_s, D_MODEL), lambda i: (0, 0)),
            pl.BlockSpec((tm, D_MODEL), prompt_block),
        ],
        out_shape=[
            jax.ShapeDtypeStruct((n_s, D_MODEL), F32),
            jax.ShapeDtypeStruct((n_p, D_MODEL), F32),
        ],
        scratch_shapes=[
            pltpu.VMEM((D_MODEL, D_FF), F32),
            pltpu.VMEM((D_MODEL, D_FF), F32),
            pltpu.VMEM((D_FF, D_MODEL), F32),
            pltpu.SemaphoreType.DMA((3, FF_CHUNKS)),
        ],
        compiler_params=pltpu.CompilerParams(
            dimension_semantics=("arbitrary",), vmem_limit_bytes=VMEM_LIMIT),
        name="ffn_final" if final_norm else "ffn",
    )(xs, xp, g, wg, wu, wd, fg)


def _t5_bucket(d):
    d = jnp.maximum(d, 0)
    max_exact = NUM_BUCKETS // 2
    df = jnp.maximum(d, 1).astype(F32)
    large = max_exact + (jnp.log(df / max_exact) / math.log(MAX_DISTANCE / max_exact)
                         * (NUM_BUCKETS - max_exact)).astype(jnp.int32)
    large = jnp.minimum(large, NUM_BUCKETS - 1)
    return jnp.where(d < max_exact, d, large)


def _masked_bias(relb_ref, head, d, valid, scale=1.0):
    bucket = _t5_bucket(d)
    b = jnp.zeros(d.shape, F32)
    for k in range(NUM_BUCKETS):
        b = jnp.where(bucket == k, relb_ref[head, k] * scale, b)
    return jnp.where(valid, b, NEG_INF)


def _kv_lane_mask(t, g):
    lane = lax.broadcasted_iota(jnp.int32, t.shape, 1)
    return jnp.where((lane >= g * HEAD_DIM) & (lane < (g + 1) * HEAD_DIM), t, jnp.zeros_like(t))


def _stack_queries(q_chunks, g, scale=HEAD_DIM ** -0.5):
    qs = []
    for r in range(N_REP):
        qc = q_chunks[2 * g + r // 2]
        if r % 2 != g:
            qc = pltpu.roll(qc, HEAD_DIM, 1)
        qs.append(qc)
    return (jnp.concatenate(qs, axis=0) * scale).astype(BF16)


def _sink_column(sinks_ref, g, rows):
    return jnp.concatenate(
        [jnp.full((rows, 1), sinks_ref[N_REP * g + r], F32) for r in range(N_REP)], axis=0)


def _unstack_heads(o_by_g, rows):
    lane = lax.broadcasted_iota(jnp.int32, (rows, LANES), 1)
    chunks = []
    for c in range(D_ATTN // LANES):
        g = c // 2
        halves = []
        for half in range(2):
            r = 2 * (c % 2) + half
            piece = o_by_g[g][r * rows:(r + 1) * rows]
            if half != g:
                piece = pltpu.roll(piece, HEAD_DIM, 1)
            halves.append(piece)
        chunks.append(jnp.where(lane < HEAD_DIM, halves[0], halves[1]))
    return chunks


def _glu_tail(y, u, dskip_ref, wglu_ref, bglu_ref):
    y = y + dskip_ref[...] * u
    y = jax.nn.gelu(y)
    z = _dot(y.astype(BF16), wglu_ref[...]) + bglu_ref[...]
    return y * jax.nn.sigmoid(z)


def _mixp_kernel(relb_ref, sinks_ref, x_ref, g_ref, win_ref, wout_ref, lam_ref, bd_ref, cd_ref,
                 dskip_ref, wglu_ref, bglu_ref,
                 o_ref, kp_ref, vp_ref, sre_ref, sim_ref,
                 p_s, kband, vband_t, bias_t, bias_later, us, ys, state, attn_s, *ring_refs,
                 nb, blk):
    i = pl.program_id(0)
    rows = nb * blk
    rings, u_rings = ring_refs[:RING], ring_refs[RING:]

    @pl.when(i == 0)
    def _():
        kband[...] = jnp.zeros(kband.shape, BF16)
        vband_t[...] = jnp.zeros(vband_t.shape, BF16)
        state[...] = jnp.zeros(state.shape, F32)

    @pl.when(i == 0)
    def _():
        kj = lax.broadcasted_iota(jnp.int32, (2 * blk, blk), 0)
        qi = lax.broadcasted_iota(jnp.int32, (2 * blk, blk), 1)
        d = qi - kj + blk
        valid = (d >= 0) & (d < WINDOW)
        for h in range(N_HEADS):
            cols = slice((h % N_REP) * blk, (h % N_REP + 1) * blk)
            tile = _masked_bias(relb_ref, h, d, valid, scale=LOG2E)
            bias_later[h // N_REP, :, cols] = tile
            bias_t[h // N_REP, :, cols] = jnp.where(kj >= blk, tile, NEG_INF)

    @pl.when(i == 1)
    def _():
        bias_t[...] = bias_later[...]

    h = _rmsnorm(x_ref[...].reshape(rows, D_MODEL), g_ref[...]).astype(BF16)
    u_off = D_ATTN + 2 * D_KV
    p_s[...] = _dot(h, win_ref[:, 0:u_off])
    u = _dot(h, win_ref[:, u_off:D_IN])
    for s in range(U_SLABS):
        for b in range(nb):
            us[s, b * PITCH:b * PITCH + blk, :] = u[b * blk:(b + 1) * blk, s * LANES:(s + 1) * LANES]

    kband[:, blk:2 * blk, :] = (
        p_s[:, D_ATTN:D_ATTN + D_KV].reshape(nb, blk, D_KV).astype(BF16))

    @pl.when(i == pl.num_programs(0) - 1)
    def _():
        for b in range(nb):
            rows_b = slice(b * blk, (b + 1) * blk)
            kp_ref[b] = p_s[rows_b, D_ATTN:D_ATTN + D_KV].T
            vp_ref[b] = p_s[rows_b, D_ATTN + D_KV:D_ATTN + 2 * D_KV].T

    def attention(b):
        rows_b = slice(b * blk, (b + 1) * blk)
        kb = kband[b]
        vband_t[b, :, blk:2 * blk] = p_s[rows_b, D_ATTN + D_KV:D_ATTN + 2 * D_KV].T.astype(BF16)
        vt = vband_t[b]
        feat = lax.broadcasted_iota(jnp.int32, vt.shape, 0)
        q_chunks = [p_s[rows_b, c * LANES:(c + 1) * LANES] for c in range(D_ATTN // LANES)]
        o_by_g = []
        for g in range(N_KV_HEADS):
            q = _stack_queries(q_chunks, g, scale=LOG2E * HEAD_DIM ** -0.5)
            st = _dot_nt(_kv_lane_mask(kb, g), q) + bias_t[g]
            s = jnp.concatenate(
                [jnp.full((1, blk), sinks_ref[N_REP * g + r] * LOG2E, F32) for r in range(N_REP)],
                axis=1)
            m = jnp.maximum(jnp.max(st, axis=0, keepdims=True), s)
            e = jnp.exp2(st - m).astype(BF16)
            in_g = (feat >= g * HEAD_DIM) & (feat < (g + 1) * HEAD_DIM)
            ot = _dot(jnp.where(in_g, vt, jnp.ones_like(vt)), e)
            other = (1 - g) * HEAD_DIM
            denom = ot[other:other + 1, :] + jnp.exp2(s - m)
            o_by_g.append(ot[g * HEAD_DIM:(g + 1) * HEAD_DIM, :] / denom)
        for c in range(D_ATTN // LANES):
            og = o_by_g[c // 2]
            cc = 2 * (c % 2)
            chunk_t = jnp.concatenate(
                [og[:, cc * blk:(cc + 1) * blk], og[:, (cc + 1) * blk:(cc + 2) * blk]], axis=0)
            attn_s[rows_b, c * LANES:(c + 1) * LANES] = chunk_t.T

    def scan_project_in(j):
        slot = j % RING
        steps = []
        for t in range(SCAN_STEPS):
            steps.append(jnp.concatenate(
                [us[s, pl.ds(j * SCAN_STEPS + t, nb, stride=PITCH), :] for s in range(U_SLABS)],
                axis=1))
        u_chunk = jnp.concatenate(steps, axis=0)
        u_rings[slot][...] = u_chunk
        ub = u_chunk.astype(BF16)
        for hf in range(2):
            rings[slot][:, 2 * hf * HALF_STATE:2 * (hf + 1) * HALF_STATE] = _dot(
                ub[:, hf * HALF_U:(hf + 1) * HALF_U], bd_ref[hf])

    def scan_recurrence(j):
        slot = j % RING
        for hf in range(2):
            lr = jnp.broadcast_to(lam_ref[2 * hf:2 * hf + 1, :], (nb, HALF_STATE))
            li = jnp.broadcast_to(lam_ref[2 * hf + 1:2 * hf + 2, :], (nb, HALF_STATE))
            re = slice(2 * hf * HALF_STATE, (2 * hf + 1) * HALF_STATE)
            im = slice((2 * hf + 1) * HALF_STATE, (2 * hf + 2) * HALF_STATE)
            xr = state[:, re]
            xi = state[:, im]
            for t in range(SCAN_STEPS):
                rows_t = slice(t * nb, (t + 1) * nb)
                xr, xi = (lr * xr - li * xi + rings[slot][rows_t, re],
                          lr * xi + li * xr + rings[slot][rows_t, im])
                rings[slot][rows_t, re] = xr
                rings[slot][rows_t, im] = xi
            state[:, re] = xr
            state[:, im] = xi

    def scan_project_out(j):
        slot = j % RING
        y = jnp.concatenate(
            [_dot(rings[slot][:, 2 * hf * HALF_STATE:2 * (hf + 1) * HALF_STATE].astype(BF16),
                  cd_ref[hf]) for hf in range(2)], axis=1)
        ssm = _glu_tail(y, u_rings[slot][...], dskip_ref, wglu_ref, bglu_ref)
        for t in range(SCAN_STEPS):
            for s in range(U_SLABS):
                ys[s, pl.ds(j * SCAN_STEPS + t, nb, stride=PITCH), :] = (
                    ssm[t * nb:(t + 1) * nb, s * LANES:(s + 1) * LANES])

    n_chunks = blk // SCAN_STEPS
    n_iters = n_chunks + 2
    for jj in range(n_iters):
        if 2 <= jj:
            scan_project_out(jj - 2)
        if 1 <= jj < n_chunks + 1:
            scan_recurrence(jj - 1)
        if jj < n_chunks:
            scan_project_in(jj)
        for b in range(nb):
            if b * n_iters // nb == jj:
                attention(b)

    kband[:, 0:blk, :] = kband[:, blk:2 * blk, :]
    vband_t[:, :, 0:blk] = vband_t[:, :, blk:2 * blk]
    for hf in range(2):
        sre_ref[:, hf * HALF_STATE:(hf + 1) * HALF_STATE] = (
            state[:, 2 * hf * HALF_STATE:(2 * hf + 1) * HALF_STATE])
        sim_ref[:, hf * HALF_STATE:(hf + 1) * HALF_STATE] = (
            state[:, (2 * hf + 1) * HALF_STATE:(2 * hf + 2) * HALF_STATE])

    ssm = jnp.concatenate(
        [jnp.concatenate([ys[s, b * PITCH:b * PITCH + blk, :] for b in range(nb)], axis=0)
         for s in range(U_SLABS)], axis=1)
    mix = (_dot(attn_s[...].astype(BF16), wout_ref[0:D_ATTN, :])
           + _dot(ssm.astype(BF16), wout_ref[D_ATTN:D_ATTN + D_SSM, :]))
    o_ref[...] = x_ref[...] + mix.reshape(nb, blk, D_MODEL)


def _mix_prompt(x, relb, sinks, g, win, wout, lam, bd, cd, dskip, wglu, bglu):
    nb, seq, _ = x.shape
    blk = WINDOW
    rows = nb * blk
    const2 = _const_spec
    return pl.pallas_call(
        functools.partial(_mixp_kernel, nb=nb, blk=blk),
        grid=(seq // blk,),
        in_specs=[
            _smem_spec(), _smem_spec(),
            pl.BlockSpec((nb, blk, D_MODEL), lambda i: (0, i, 0)),
            const2((1, D_MODEL)),
            const2((D_MODEL, D_IN)),
            const2((D_ATTN + D_SSM, D_MODEL)),
            const2((4, HALF_STATE)),
            const2((2, HALF_U, 2 * HALF_STATE)),
            const2((2, 2 * HALF_STATE, HALF_U)),
            const2((1, D_SSM)),
            const2((D_SSM, D_SSM)),
            const2((1, D_SSM)),
        ],
        out_specs=[
            pl.BlockSpec((nb, blk, D_MODEL), lambda i: (0, i, 0)),
            pl.BlockSpec((nb, blk, D_KV), lambda i: (0, 0, 0)),
            pl.BlockSpec((nb, blk, D_KV), lambda i: (0, 0, 0)),
            pl.BlockSpec((nb, 2 * HALF_STATE), lambda i: (0, 0)),
            pl.BlockSpec((nb, 2 * HALF_STATE), lambda i: (0, 0)),
        ],
        out_shape=[
            jax.ShapeDtypeStruct((nb, seq, D_MODEL), F32),
            jax.ShapeDtypeStruct((nb, blk, D_KV), F32),
            jax.ShapeDtypeStruct((nb, blk, D_KV), F32),
            jax.ShapeDtypeStruct((nb, 2 * HALF_STATE), F32),
            jax.ShapeDtypeStruct((nb, 2 * HALF_STATE), F32),
        ],
        scratch_shapes=[
            pltpu.VMEM((rows, D_ATTN + 2 * D_KV), F32),
            pltpu.VMEM((nb, 2 * blk, D_KV), BF16),
            pltpu.VMEM((nb, D_KV, 2 * blk), BF16),
            pltpu.VMEM((N_KV_HEADS, 2 * blk, N_REP * blk), F32),
            pltpu.VMEM((N_KV_HEADS, 2 * blk, N_REP * blk), F32),
            pltpu.VMEM((U_SLABS, nb * PITCH, LANES), F32),
            pltpu.VMEM((U_SLABS, nb * PITCH, LANES), F32),
            pltpu.VMEM((nb, 4 * HALF_STATE), F32),
            pltpu.VMEM((rows, D_ATTN), F32),
        ] + [pltpu.VMEM((SCAN_STEPS * nb, 4 * HALF_STATE), F32)] * RING
          + [pltpu.VMEM((SCAN_STEPS * nb, D_SSM), F32)] * RING,
        compiler_params=pltpu.CompilerParams(
            dimension_semantics=("arbitrary",), vmem_limit_bytes=VMEM_LIMIT),
        name="mix_prompt",
    )(relb, sinks, x, g, win, wout, lam, bd, cd, dskip, wglu, bglu)


def _mixs_kernel(relb_ref, sinks_ref, x_ref, g_ref, win_ref, wout_ref, ck_ref, cv_ref,
                 sre0_ref, sim0_ref, lam_ref, bd_ref, cd_ref, dskip_ref, wglu_ref, bglu_ref,
                 o_ref, ks_ref, vs_ref, sre_ref, sim_ref,
                 p_s, attn_s, ssm_s, bias_c, bias_n, us,
                 *, nseq, t_new, sb, wbuf):
    i = pl.program_id(0)
    nsteps = pl.num_programs(0)
    rows = nseq * t_new
    brow = sb * t_new
    u_off = D_ATTN + 2 * D_KV
    tshift = t_new.bit_length() - 1
    wshift = wbuf.bit_length() - 1

    @pl.when(i == 0)
    def _():
        x = x_ref[...]
        h = _rmsnorm(x, g_ref[...]).astype(BF16)
        p_s[...] = _dot(h, win_ref[...])

        for s in range(U_SLABS):
            us[s] = p_s[:, u_off + s * LANES:u_off + (s + 1) * LANES]
        uperm = jnp.concatenate(
            [jnp.concatenate([us[s, pl.ds(t, nseq, stride=t_new), :] for t in range(t_new)], axis=0)
             for s in range(U_SLABS)], axis=1)
        yparts = []
        for hf in range(2):
            bu = _dot(uperm[:, hf * HALF_U:(hf + 1) * HALF_U].astype(BF16), bd_ref[hf])
            lr = lam_ref[2 * hf:2 * hf + 1, :]
            li = lam_ref[2 * hf + 1:2 * hf + 2, :]
            xr = _transpose_blocks(sre0_ref[hf * HALF_STATE:(hf + 1) * HALF_STATE, :])
            xi = _transpose_blocks(sim0_ref[hf * HALF_STATE:(hf + 1) * HALF_STATE, :])
            states = []
            for t in range(t_new):
                br = bu[t * nseq:(t + 1) * nseq, 0:HALF_STATE]
                bi = bu[t * nseq:(t + 1) * nseq, HALF_STATE:2 * HALF_STATE]
                xr, xi = lr * xr - li * xi + br, lr * xi + li * xr + bi
                states.append(jnp.concatenate([xr, xi], axis=1).astype(BF16))
            sre_ref[hf * HALF_STATE:(hf + 1) * HALF_STATE, :] = _transpose_blocks(xr)
            sim_ref[hf * HALF_STATE:(hf + 1) * HALF_STATE, :] = _transpose_blocks(xi)
            yparts.append(_dot(jnp.concatenate(states, axis=0), cd_ref[hf]))
        yperm = jnp.concatenate(yparts, axis=1)
        for s in range(U_SLABS):
            for t in range(t_new):
                us[s, pl.ds(t, nseq, stride=t_new), :] = (
                    yperm[t * nseq:(t + 1) * nseq, s * LANES:(s + 1) * LANES])
        y = jnp.concatenate([us[s] for s in range(U_SLABS)], axis=1)
        ssm_s[...] = _glu_tail(y, p_s[:, u_off:u_off + D_SSM], dskip_ref, wglu_ref, bglu_ref)

        ncol = sb * wbuf
        rho = lax.broadcasted_iota(jnp.int32, (brow, ncol), 0)
        kap = lax.broadcasted_iota(jnp.int32, (brow, ncol), 1)
        same_seq = (rho >> tshift) == (kap >> wshift)
        rho_w = lax.broadcasted_iota(jnp.int32, (brow, wbuf), 0)
        d_c = (rho_w & (t_new - 1)) - lax.broadcasted_iota(jnp.int32, (brow, wbuf), 1) + wbuf
        valid_c = (d_c >= 0) & (d_c < WINDOW)
        rho_n = lax.broadcasted_iota(jnp.int32, (brow, LANES), 0)
        kap_n = lax.broadcasted_iota(jnp.int32, (brow, LANES), 1)
        tq_n = rho_n & (t_new - 1)
        d_n = tq_n - (kap_n & (t_new - 1))
        valid_n = ((rho_n >> tshift) == (kap_n >> tshift)) & (d_n >= 0) & (kap_n < brow)
        for hd in range(N_HEADS):
            g, r = hd // N_REP, hd % N_REP
            tile = _masked_bias(relb_ref, hd, d_c, valid_c)
            bias_c[g, r * brow:(r + 1) * brow, :] = jnp.where(
                same_seq, jnp.concatenate([tile] * sb, axis=1), NEG_INF)
            bias_n[g, r * brow:(r + 1) * brow, :] = _masked_bias(relb_ref, hd, d_n, valid_n)

    def seq_block(k):
        first = k * sb
        row0 = pl.multiple_of((i * SAMPLE_BLOCKS + k) * brow, brow)
        kc_t = jnp.concatenate([ck_ref[first + b] for b in range(sb)], axis=1).astype(BF16)
        vc_t = jnp.concatenate([cv_ref[first + b] for b in range(sb)], axis=1).astype(BF16)
        pad = jnp.zeros((LANES - brow, D_KV), F32)
        kn_pad = jnp.concatenate([p_s[pl.ds(row0, brow), D_ATTN:D_ATTN + D_KV], pad], axis=0)
        vn_pad = jnp.concatenate(
            [p_s[pl.ds(row0, brow), D_ATTN + D_KV:D_ATTN + 2 * D_KV], pad], axis=0)
        kn_b = kn_pad.astype(BF16)
        vn_b = vn_pad.astype(BF16)
        q_chunks = [p_s[pl.ds(row0, brow), c * LANES:(c + 1) * LANES]
                    for c in range(D_ATTN // LANES)]
        o_by_g = []
        for g in range(N_KV_HEADS):
            q = _kv_lane_mask(_stack_queries(q_chunks, g), g)
            lc = _dot(q, kc_t) + bias_c[g]
            ln = _dot_nt(q, kn_b) + bias_n[g]
            s = _sink_column(sinks_ref, g, brow)
            m = jnp.maximum(jnp.maximum(jnp.max(lc, axis=-1, keepdims=True),
                                        jnp.max(ln, axis=-1, keepdims=True)), s)
            ec = jnp.exp(lc - m)
            en = jnp.exp(ln - m)
            denom = (jnp.sum(ec, axis=-1, keepdims=True) + jnp.sum(en, axis=-1, keepdims=True)
                     + jnp.exp(s - m))
            o = _dot_nt(ec.astype(BF16), vc_t) + _dot(en.astype(BF16), vn_b)
            o_by_g.append(o / denom)
        for c, chunk in enumerate(_unstack_heads(o_by_g, brow)):
            attn_s[pl.ds(row0, brow), c * LANES:(c + 1) * LANES] = chunk

        lane = lax.broadcasted_iota(jnp.int32, (D_KV, wbuf), 1)
        keep = wbuf - t_new
        for new_pad, old_ref, out_ref in ((kn_pad, ck_ref, ks_ref), (vn_pad, cv_ref, vs_ref)):
            new_t = new_pad.T
            for b in range(sb):
                shifted = pltpu.roll(old_ref[first + b], keep, 1)
                appended = pltpu.roll(new_t, (keep - b * t_new) % wbuf, 1)
                out_ref[first + b] = jnp.where(lane >= keep, appended, shifted)

    for k in range(SAMPLE_BLOCKS):
        seq_block(k)

    @pl.when(i == nsteps - 1)
    def _():
        mix = (_dot(attn_s[...].astype(BF16), wout_ref[0:D_ATTN, :])
               + _dot(ssm_s[...].astype(BF16), wout_ref[D_ATTN:D_ATTN + D_SSM, :]))
        o_ref[...] = x_ref[...] + mix


def _mix_sample(x, relb, sinks, g, win, wout, ck, cv, sre0, sim0, lam, bd, cd, dskip, wglu, bglu,
                *, nseq, t_new):
    rows = nseq * t_new
    wbuf = ck.shape[2]
    assert wbuf == LANES
    sb = SUBLANES
    brow = sb * t_new
    return pl.pallas_call(
        functools.partial(_mixs_kernel, nseq=nseq, t_new=t_new, sb=sb, wbuf=wbuf),
        grid=(nseq // (SAMPLE_BLOCKS * sb),),
        in_specs=[
            _smem_spec(), _smem_spec(),
            _const_spec((rows, D_MODEL)),
            _const_spec((1, D_MODEL)),
            _const_spec((D_MODEL, D_IN)),
            _const_spec((D_ATTN + D_SSM, D_MODEL)),
            pl.BlockSpec((SAMPLE_BLOCKS * sb, D_KV, wbuf), lambda i: (i, 0, 0)),
            pl.BlockSpec((SAMPLE_BLOCKS * sb, D_KV, wbuf), lambda i: (i, 0, 0)),
            _const_spec((2 * HALF_STATE, nseq)),
            _const_spec((2 * HALF_STATE, nseq)),
            _const_spec((4, HALF_STATE)),
            _const_spec((2, HALF_U, 2 * HALF_STATE)),
            _const_spec((2, 2 * HALF_STATE, HALF_U)),
            _const_spec((1, D_SSM)),
            _const_spec((D_SSM, D_SSM)),
            _const_spec((1, D_SSM)),
        ],
        out_specs=[
            pl.BlockSpec((rows, D_MODEL), lambda i: (0, 0)),
            pl.BlockSpec((SAMPLE_BLOCKS * sb, D_KV, wbuf), lambda i: (i, 0, 0)),
            pl.BlockSpec((SAMPLE_BLOCKS * sb, D_KV, wbuf), lambda i: (i, 0, 0)),
            pl.BlockSpec((2 * HALF_STATE, nseq), lambda i: (0, 0)),
            pl.BlockSpec((2 * HALF_STATE, nseq), lambda i: (0, 0)),
        ],
        out_shape=[
            jax.ShapeDtypeStruct((rows, D_MODEL), F32),
            jax.ShapeDtypeStruct((nseq, D_KV, wbuf), F32),
            jax.ShapeDtypeStruct((nseq, D_KV, wbuf), F32),
            jax.ShapeDtypeStruct((2 * HALF_STATE, nseq), F32),
            jax.ShapeDtypeStruct((2 * HALF_STATE, nseq), F32),
        ],
        scratch_shapes=[
            pltpu.VMEM((rows, D_IN), F32),
            pltpu.VMEM((rows, D_ATTN), F32),
            pltpu.VMEM((rows, D_SSM), F32),
            pltpu.VMEM((N_KV_HEADS, N_REP * brow, sb * wbuf), F32),
            pltpu.VMEM((N_KV_HEADS, N_REP * brow, LANES), F32),
            pltpu.VMEM((U_SLABS, rows, LANES), F32),
        ],
        compiler_params=pltpu.CompilerParams(
            dimension_semantics=("arbitrary",), vmem_limit_bytes=VMEM_LIMIT),
        name="mix_sample",
    )(relb, sinks, x, g, win, wout, ck, cv, sre0, sim0, lam, bd, cd, dskip, wglu, bglu)


def _s5_operators(log_dt, a_re, a_im, b_re, b_im, c_re, c_im):
    dt = jnp.exp(log_dt)[:, None]
    mag = jnp.exp(a_re * dt)
    lb_re = mag * jnp.cos(a_im * dt)
    lb_im = mag * jnp.sin(a_im * dt)
    den = a_re * a_re + a_im * a_im
    nr = lb_re - 1.0
    q_re = (nr * a_re + lb_im * a_im) / den
    q_im = (lb_im * a_re - nr * a_im) / den
    bb_re = q_re[..., None] * b_re - q_im[..., None] * b_im
    bb_im = q_re[..., None] * b_im + q_im[..., None] * b_re
    eye = jnp.eye(HALF_GROUPS, dtype=BF16)
    lam = jnp.stack([lb_re.reshape(2, HALF_STATE), lb_im.reshape(2, HALF_STATE)], axis=1)
    bb = jnp.stack([bb_re, bb_im]).astype(BF16).reshape(2, 2, HALF_GROUPS, SSM_STATE, SSM_GROUP)
    bd = jnp.einsum('rhgpc,gk->hgcrkp', bb, eye).reshape(2, HALF_U, 2 * HALF_STATE)
    cc = jnp.stack([c_re, -c_im]).astype(BF16).reshape(2, 2, HALF_GROUPS, SSM_GROUP, SSM_STATE)
    cd = jnp.einsum('rhgcp,gk->hrgpkc', cc, eye).reshape(2, 2 * HALF_STATE, HALF_U)
    return lam.reshape(4, HALF_STATE), bd, cd


def kernel(x_prompt, x_sample, cache_k, cache_v, state_ssm_re, state_ssm_im, rel_bias,
           ffn1_norm, ffn1_w_gate, ffn1_w_up, ffn1_w_down, mix_norm, w_in, sinks,
           log_dt, a_re, a_im, b_re, b_im, c_re, c_im, d_skip, w_glu, b_glu, w_out,
           ffn2_norm, ffn2_w_gate, ffn2_w_up, ffn2_w_down, final_norm):
    depth = w_in.shape[0]
    assert depth == 1
    batch, seq, _ = x_prompt.shape
    nseq, t_new, _ = x_sample.shape
    fg = final_norm.reshape(1, D_MODEL)

    l = 0
    ffn1 = (ffn1_norm[l].reshape(1, D_MODEL), ffn1_w_gate[l], ffn1_w_up[l], ffn1_w_down[l], fg)
    ffn2 = (ffn2_norm[l].reshape(1, D_MODEL), ffn2_w_gate[l], ffn2_w_up[l], ffn2_w_down[l], fg)
    lam, bd, cd = _s5_operators(log_dt[l], a_re[l], a_im[l], b_re[l], b_im[l], c_re[l], c_im[l])
    mix_w = (mix_norm[l].reshape(1, D_MODEL), w_in[l].astype(BF16), w_out[l].astype(BF16))
    ssm_w = (lam, bd, cd, d_skip[l].reshape(1, D_SSM), w_glu[l].astype(BF16),
             b_glu[l].reshape(1, D_SSM))
    sinks_l = sinks[l]

    def window_in(c):
        return jnp.transpose(c, (0, 2, 3, 1)).reshape(c.shape[0], D_KV, c.shape[1])

    def window_out(w):
        n, _, pos = w.shape
        return jnp.transpose(w.reshape(n, N_KV_HEADS, HEAD_DIM, pos), (0, 3, 1, 2))[None]

    def state_in(s):
        return jnp.transpose(s, (1, 2, 0)).reshape(2 * HALF_STATE, s.shape[0])

    def state_out(s):
        return jnp.transpose(s.reshape(N_SSM_GROUPS, SSM_STATE, s.shape[1]), (2, 0, 1))[None]

    relb_t = rel_bias.T
    xp = x_prompt.reshape(batch * seq, D_MODEL)
    xs = x_sample.reshape(nseq * t_new, D_MODEL)
    ys, yp = _ffn(xp, xs, *ffn1, final_norm=False)
    yp, kp, vp, sre_p, sim_p = _mix_prompt(
        yp.reshape(batch, seq, D_MODEL), relb_t, sinks_l, *mix_w, *ssm_w)
    ys, ks, vs, sre_s, sim_s = _mix_sample(
        ys, relb_t, sinks_l, *mix_w,
        window_in(cache_k[l]), window_in(cache_v[l]),
        state_in(state_ssm_re[l]), state_in(state_ssm_im[l]),
        *ssm_w, nseq=nseq, t_new=t_new)
    y_sample, y_prompt = _ffn(yp.reshape(batch * seq, D_MODEL), ys, *ffn2, final_norm=True)
    y_prompt = y_prompt.reshape(batch, seq, D_MODEL)
    y_sample = y_sample.reshape(nseq, t_new, D_MODEL)

    st_p = (1, batch, N_SSM_GROUPS, SSM_STATE)
    return (y_prompt, y_sample,
            window_out(kp), window_out(vp), sre_p.reshape(st_p), sim_p.reshape(st_p),
            window_out(ks), window_out(vs), state_out(sre_s), state_out(sim_s))
```

```python
import functools
import math

import jax
import jax.numpy as jnp
from jax import lax
from jax.experimental import pallas as pl
from jax.experimental.pallas import tpu as pltpu

F32 = jnp.float32
BF16 = jnp.bfloat16

D_MODEL = 1024
HEAD_DIM = 64
D_ATTN = 512
N_HEADS = 8
N_KV_HEADS = 2
N_REP = 4
D_KV = 128
D_SSM = 512
SSM_GROUP = 16
N_SSM_GROUPS = 32
SSM_STATE = 64
WINDOW = 128
NUM_BUCKETS = 32
MAX_DISTANCE = 128
D_FF = 2816
D_IN = D_ATTN + 2 * D_KV + D_SSM
RMS_EPS = 1e-6
NEG_INF = -1e30
LOG2E = math.log2(math.e)

LANES = 128
SUBLANES = 8
MXU_DIM = 256
FF_CHUNKS = D_FF // MXU_DIM
HALF_GROUPS = N_SSM_GROUPS // 2
HALF_STATE = HALF_GROUPS * SSM_STATE
HALF_U = HALF_GROUPS * SSM_GROUP
U_SLABS = D_SSM // LANES
PITCH = WINDOW + SUBLANES
SCAN_STEPS = 128
RING = 1
SAMPLE_BLOCKS = 1
VMEM_LIMIT = 60 * 1024 * 1024


def _const_spec(shape):
    nd = len(shape)
    return pl.BlockSpec(shape, lambda *_: (0,) * nd, pipeline_mode=pl.Buffered(1))


def _smem_spec():
    return pl.BlockSpec(memory_space=pltpu.SMEM)


def _rmsnorm(x, g):
    r = lax.rsqrt(jnp.mean(x * x, axis=-1, keepdims=True) + RMS_EPS)
    return (x * r) * g


def _dot(a, b):
    return jnp.dot(a, b, preferred_element_type=F32)


def _dot_nt(a, b):
    return lax.dot_general(a, b, (((1,), (1,)), ((), ())), preferred_element_type=F32)


def _transpose_blocks(a):
    r, c = a.shape
    return jnp.concatenate(
        [jnp.concatenate([a[i * LANES:(i + 1) * LANES, j * LANES:(j + 1) * LANES].T
                          for i in range(r // LANES)], axis=1)
         for j in range(c // LANES)], axis=0)


def _ffn_tile(x, g_ref, wg_ref, wu_ref, wd_ref, fg_ref, final_norm, before_chunk=None):
    h = _rmsnorm(x, g_ref[...]).astype(BF16)
    acc = None
    for c in range(FF_CHUNKS):
        if before_chunk is not None:
            before_chunk(c)
        sl = slice(c * MXU_DIM, (c + 1) * MXU_DIM)
        gate = _dot(h, wg_ref[:, sl].astype(BF16))
        up = _dot(h, wu_ref[:, sl].astype(BF16))
        a = (gate * jax.nn.sigmoid(gate) * up).astype(BF16)
        part = _dot(a, wd_ref[sl, :].astype(BF16))
        acc = part if acc is None else acc + part
    y = x + 0.5 * acc
    if final_norm:
        y = _rmsnorm(y, fg_ref[...])
    return y


def _ffn_kernel(xs_ref, xp_ref, g_ref, wg_hbm, wu_hbm, wd_hbm, fg_ref, os_ref, op_ref,
                wg_s, wu_s, wd_s, sems, *, final_norm):
    i = pl.program_id(0)
    weights = (g_ref, wg_s, wu_s, wd_s, fg_ref)

    def chunk_copies(c):
        sl = slice(c * MXU_DIM, (c + 1) * MXU_DIM)
        return (pltpu.make_async_copy(wg_hbm.at[:, sl], wg_s.at[:, sl], sems.at[0, c]),
                pltpu.make_async_copy(wu_hbm.at[:, sl], wu_s.at[:, sl], sems.at[1, c]),
                pltpu.make_async_copy(wd_hbm.at[sl, :], wd_s.at[sl, :], sems.at[2, c]))

    @pl.when(i == 0)
    def _():
        for c in range(FF_CHUNKS):
            for cp in chunk_copies(c):
                cp.start()

        def wait_chunk(c):
            for cp in chunk_copies(c):
                cp.wait()

        os_ref[...] = _ffn_tile(xs_ref[...], *weights, final_norm, wait_chunk)

    @pl.when(i > 0)
    def _():
        op_ref[...] = _ffn_tile(xp_ref[...], *weights, final_norm)


def _ffn(xp, xs, g, wg, wu, wd, fg, *, final_norm):
    n_p, n_s = xp.shape[0], xs.shape[0]
    tm = n_s
    prompt_steps = n_p // tm
    assert prompt_steps * tm == n_p
    prompt_block = lambda i: (jnp.maximum(i - 1, 0), 0)
    hbm = pl.BlockSpec(memory_space=pl.ANY)
    return pl.pallas_call(
        functools.partial(_ffn_kernel, final_norm=final_norm),
        grid=(prompt_steps + 1,),
        in_specs=[
            _const_spec((n_s, D_MODEL)),
            pl.BlockSpec((tm, D_MODEL), prompt_block),
            _const_spec((1, D_MODEL)),
            hbm, hbm, hbm,
            _const_spec((1, D_MODEL)),
        ],
        out_specs=[
            pl.BlockSpec((n_s, D_MODEL), lambda i: (0, 0)),
            pl.BlockSpec((tm, D_MODEL), prompt_block),
        ],
        out_shape=[
            jax.ShapeDtypeStruct((n_s, D_MODEL), F32),
            jax.ShapeDtypeStruct((n_p, D_MODEL), F32),
        ],
        scratch_shapes=[
            pltpu.VMEM((D_MODEL, D_FF), F32),
            pltpu.VMEM((D_MODEL, D_FF), F32),
            pltpu.VMEM((D_FF, D_MODEL), F32),
            pltpu.SemaphoreType.DMA((3, FF_CHUNKS)),
        ],
        compiler_params=pltpu.CompilerParams(
            dimension_semantics=("arbitrary",), vmem_limit_bytes=VMEM_LIMIT),
        name="ffn_final" if final_norm else "ffn",
    )(xs, xp, g, wg, wu, wd, fg)


def _t5_bucket(d):
    d = jnp.maximum(d, 0)
    max_exact = NUM_BUCKETS // 2
    df = jnp.maximum(d, 1).astype(F32)
    large = max_exact + (jnp.log(df / max_exact) / math.log(MAX_DISTANCE / max_exact)
                         * (NUM_BUCKETS - max_exact)).astype(jnp.int32)
    large = jnp.minimum(large, NUM_BUCKETS - 1)
    return jnp.where(d < max_exact, d, large)


def _masked_bias(relb_ref, head, d, valid, scale=1.0):
    bucket = _t5_bucket(d)
    b = jnp.zeros(d.shape, F32)
    for k in range(NUM_BUCKETS):
        b = jnp.where(bucket == k, relb_ref[head, k] * scale, b)
    return jnp.where(valid, b, NEG_INF)


def _kv_lane_mask(t, g):
    lane = lax.broadcasted_iota(jnp.int32, t.shape, 1)
    return jnp.where((lane >= g * HEAD_DIM) & (lane < (g + 1) * HEAD_DIM), t, jnp.zeros_like(t))


def _stack_queries(q_chunks, g, scale=HEAD_DIM ** -0.5):
    qs = []
    for r in range(N_REP):
        qc = q_chunks[2 * g + r // 2]
        if r % 2 != g:
            qc = pltpu.roll(qc, HEAD_DIM, 1)
        qs.append(qc)
    return (jnp.concatenate(qs, axis=0) * scale).astype(BF16)


def _sink_column(sinks_ref, g, rows):
    return jnp.concatenate(
        [jnp.full((rows, 1), sinks_ref[N_REP * g + r], F32) for r in range(N_REP)], axis=0)


def _unstack_heads(o_by_g, rows):
    lane = lax.broadcasted_iota(jnp.int32, (rows, LANES), 1)
    chunks = []
    for c in range(D_ATTN // LANES):
        g = c // 2
        halves = []
        for half in range(2):
            r = 2 * (c % 2) + half
            piece = o_by_g[g][r * rows:(r + 1) * rows]
            if half != g:
                piece = pltpu.roll(piece, HEAD_DIM, 1)
            halves.append(piece)
        chunks.append(jnp.where(lane < HEAD_DIM, halves[0], halves[1]))
    return chunks


def _glu_tail(y, u, dskip_ref, wglu_ref, bglu_ref):
    y = y + dskip_ref[...] * u
    y = jax.nn.gelu(y)
    z = _dot(y.astype(BF16), wglu_ref[...]) + bglu_ref[...]
    return y * jax.nn.sigmoid(z)


def _mixp_kernel(relb_ref, sinks_ref, x_ref, g_ref, win_ref, wout_ref, lam_ref, bd_ref, cd_ref,
                 dskip_ref, wglu_ref, bglu_ref,
                 o_ref, kp_ref, vp_ref, sre_ref, sim_ref,
                 p_s, kband, vband_t, bias_t, bias_later, us, ys, state, attn_s, *ring_refs,
                 nb, blk):
    i = pl.program_id(0)
    rows = nb * blk
    rings, u_rings = ring_refs[:RING], ring_refs[RING:]

    @pl.when(i == 0)
    def _():
        kband[...] = jnp.zeros(kband.shape, BF16)
        vband_t[...] = jnp.zeros(vband_t.shape, BF16)
        state[...] = jnp.zeros(state.shape, F32)

    @pl.when(i == 0)
    def _():
        kj = lax.broadcasted_iota(jnp.int32, (2 * blk, blk), 0)
        qi = lax.broadcasted_iota(jnp.int32, (2 * blk, blk), 1)
        d = qi - kj + blk
        valid = (d >= 0) & (d < WINDOW)
        for h in range(N_HEADS):
            cols = slice((h % N_REP) * blk, (h % N_REP + 1) * blk)
            tile = _masked_bias(relb_ref, h, d, valid, scale=LOG2E)
            bias_later[h // N_REP, :, cols] = tile
            bias_t[h // N_REP, :, cols] = jnp.where(kj >= blk, tile, NEG_INF)

    @pl.when(i == 1)
    def _():
        bias_t[...] = bias_later[...]

    h = _rmsnorm(x_ref[...].reshape(rows, D_MODEL), g_ref[...]).astype(BF16)
    u_off = D_ATTN + 2 * D_KV
    p_s[...] = _dot(h, win_ref[:, 0:u_off])
    u = _dot(h, win_ref[:, u_off:D_IN])
    for s in range(U_SLABS):
        for b in range(nb):
            us[s, b * PITCH:b * PITCH + blk, :] = u[b * blk:(b + 1) * blk, s * LANES:(s + 1) * LANES]

    kband[:, blk:2 * blk, :] = (
        p_s[:, D_ATTN:D_ATTN + D_KV].reshape(nb, blk, D_KV).astype(BF16))

    @pl.when(i == pl.num_programs(0) - 1)
    def _():
        for b in range(nb):
            rows_b = slice(b * blk, (b + 1) * blk)
            kp_ref[b] = p_s[rows_b, D_ATTN:D_ATTN + D_KV].T
            vp_ref[b] = p_s[rows_b, D_ATTN + D_KV:D_ATTN + 2 * D_KV].T

    def attention(b):
        rows_b = slice(b * blk, (b + 1) * blk)
        kb = kband[b]
        vband_t[b, :, blk:2 * blk] = p_s[rows_b, D_ATTN + D_KV:D_ATTN + 2 * D_KV].T.astype(BF16)
        vt = vband_t[b]
        feat = lax.broadcasted_iota(jnp.int32, vt.shape, 0)
        q_chunks = [p_s[rows_b, c * LANES:(c + 1) * LANES] for c in range(D_ATTN // LANES)]
        o_by_g = []
        for g in range(N_KV_HEADS):
            q = _stack_queries(q_chunks, g, scale=LOG2E * HEAD_DIM ** -0.5)
            st = _dot_nt(_kv_lane_mask(kb, g), q) + bias_t[g]
            s = jnp.concatenate(
                [jnp.full((1, blk), sinks_ref[N_REP * g + r] * LOG2E, F32) for r in range(N_REP)],
                axis=1)
            m = jnp.maximum(jnp.max(st, axis=0, keepdims=True), s)
            e = jnp.exp2(st - m).astype(BF16)
            in_g = (feat >= g * HEAD_DIM) & (feat < (g + 1) * HEAD_DIM)
            ot = _dot(jnp.where(in_g, vt, jnp.ones_like(vt)), e)
            other = (1 - g) * HEAD_DIM
            denom = ot[other:other + 1, :] + jnp.exp2(s - m)
            o_by_g.append(ot[g * HEAD_DIM:(g + 1) * HEAD_DIM, :] / denom)
        for c in range(D_ATTN // LANES):
            og = o_by_g[c // 2]
            cc = 2 * (c % 2)
            chunk_t = jnp.concatenate(
                [og[:, cc * blk:(cc + 1) * blk], og[:, (cc + 1) * blk:(cc + 2) * blk]], axis=0)
            attn_s[rows_b, c * LANES:(c + 1) * LANES] = chunk_t.T

    def scan_project_in(j):
        slot = j % RING
        steps = []
        for t in range(SCAN_STEPS):
            steps.append(jnp.concatenate(
                [us[s, pl.ds(j * SCAN_STEPS + t, nb, stride=PITCH), :] for s in range(U_SLABS)],
                axis=1))
        u_chunk = jnp.concatenate(steps, axis=0)
        u_rings[slot][...] = u_chunk
        ub = u_chunk.astype(BF16)
        for hf in range(2):
            rings[slot][:, 2 * hf * HALF_STATE:2 * (hf + 1) * HALF_STATE] = _dot(
                ub[:, hf * HALF_U:(hf + 1) * HALF_U], bd_ref[hf])

    def scan_recurrence(j):
        slot = j % RING
        for hf in range(2):
            lr = jnp.broadcast_to(lam_ref[2 * hf:2 * hf + 1, :], (nb, HALF_STATE))
            li = jnp.broadcast_to(lam_ref[2 * hf + 1:2 * hf + 2, :], (nb, HALF_STATE))
            re = slice(2 * hf * HALF_STATE, (2 * hf + 1) * HALF_STATE)
            im = slice((2 * hf + 1) * HALF_STATE, (2 * hf + 2) * HALF_STATE)
            xr = state[:, re]
            xi = state[:, im]
            for t in range(SCAN_STEPS):
                rows_t = slice(t * nb, (t + 1) * nb)
                xr, xi = (lr * xr - li * xi + rings[slot][rows_t, re],
                          lr * xi + li * xr + rings[slot][rows_t, im])
                rings[slot][rows_t, re] = xr
                rings[slot][rows_t, im] = xi
            state[:, re] = xr
            state[:, im] = xi

    def scan_project_out(j):
        slot = j % RING
        y = jnp.concatenate(
            [_dot(rings[slot][:, 2 * hf * HALF_STATE:2 * (hf + 1) * HALF_STATE].astype(BF16),
                  cd_ref[hf]) for hf in range(2)], axis=1)
        ssm = _glu_tail(y, u_rings[slot][...], dskip_ref, wglu_ref, bglu_ref)
        for t in range(SCAN_STEPS):
            for s in range(U_SLABS):
                ys[s, pl.ds(j * SCAN_STEPS + t, nb, stride=PITCH), :] = (
                    ssm[t * nb:(t + 1) * nb, s * LANES:(s + 1) * LANES])

    n_chunks = blk // SCAN_STEPS
    n_iters = n_chunks + 2
    for b in range(nb):
        attention(b)
    for jj in range(n_iters):
        if jj < n_chunks:
            scan_project_in(jj)
        if 1 <= jj < n_chunks + 1:
            scan_recurrence(jj - 1)
        if 2 <= jj:
            scan_project_out(jj - 2)

    kband[:, 0:blk, :] = kband[:, blk:2 * blk, :]
    vband_t[:, :, 0:blk] = vband_t[:, :, blk:2 * blk]
    for hf in range(2):
        sre_ref[:, hf * HALF_STATE:(hf + 1) * HALF_STATE] = (
            state[:, 2 * hf * HALF_STATE:(2 * hf + 1) * HALF_STATE])
        sim_ref[:, hf * HALF_STATE:(hf + 1) * HALF_STATE] = (
            state[:, (2 * hf + 1) * HALF_STATE:(2 * hf + 2) * HALF_STATE])

    ssm = jnp.concatenate(
        [jnp.concatenate([ys[s, b * PITCH:b * PITCH + blk, :] for b in range(nb)], axis=0)
         for s in range(U_SLABS)], axis=1)
    mix = (_dot(attn_s[...].astype(BF16), wout_ref[0:D_ATTN, :])
           + _dot(ssm.astype(BF16), wout_ref[D_ATTN:D_ATTN + D_SSM, :]))
    o_ref[...] = x_ref[...] + mix.reshape(nb, blk, D_MODEL)


def _mix_prompt(x, relb, sinks, g, win, wout, lam, bd, cd, dskip, wglu, bglu):
    nb, seq, _ = x.shape
    blk = WINDOW
    rows = nb * blk
    const2 = _const_spec
    return pl.pallas_call(
        functools.partial(_mixp_kernel, nb=nb, blk=blk),
        grid=(seq // blk,),
        in_specs=[
            _smem_spec(), _smem_spec(),
            pl.BlockSpec((nb, blk, D_MODEL), lambda i: (0, i, 0)),
            const2((1, D_MODEL)),
            const2((D_MODEL, D_IN)),
            const2((D_ATTN + D_SSM, D_MODEL)),
            const2((4, HALF_STATE)),
            const2((2, HALF_U, 2 * HALF_STATE)),
            const2((2, 2 * HALF_STATE, HALF_U)),
            const2((1, D_SSM)),
            const2((D_SSM, D_SSM)),
            const2((1, D_SSM)),
        ],
        out_specs=[
            pl.BlockSpec((nb, blk, D_MODEL), lambda i: (0, i, 0)),
            pl.BlockSpec((nb, blk, D_KV), lambda i: (0, 0, 0)),
            pl.BlockSpec((nb, blk, D_KV), lambda i: (0, 0, 0)),
            pl.BlockSpec((nb, 2 * HALF_STATE), lambda i: (0, 0)),
            pl.BlockSpec((nb, 2 * HALF_STATE), lambda i: (0, 0)),
        ],
        out_shape=[
            jax.ShapeDtypeStruct((nb, seq, D_MODEL), F32),
            jax.ShapeDtypeStruct((nb, blk, D_KV), F32),
            jax.ShapeDtypeStruct((nb, blk, D_KV), F32),
            jax.ShapeDtypeStruct((nb, 2 * HALF_STATE), F32),
            jax.ShapeDtypeStruct((nb, 2 * HALF_STATE), F32),
        ],
        scratch_shapes=[
            pltpu.VMEM((rows, D_ATTN + 2 * D_KV), F32),
            pltpu.VMEM((nb, 2 * blk, D_KV), BF16),
            pltpu.VMEM((nb, D_KV, 2 * blk), BF16),
            pltpu.VMEM((N_KV_HEADS, 2 * blk, N_REP * blk), F32),
            pltpu.VMEM((N_KV_HEADS, 2 * blk, N_REP * blk), F32),
            pltpu.VMEM((U_SLABS, nb * PITCH, LANES), F32),
            pltpu.VMEM((U_SLABS, nb * PITCH, LANES), F32),
            pltpu.VMEM((nb, 4 * HALF_STATE), F32),
            pltpu.VMEM((rows, D_ATTN), F32),
        ] + [pltpu.VMEM((SCAN_STEPS * nb, 4 * HALF_STATE), F32)] * RING
          + [pltpu.VMEM((SCAN_STEPS * nb, D_SSM), F32)] * RING,
        compiler_params=pltpu.CompilerParams(
            dimension_semantics=("arbitrary",), vmem_limit_bytes=VMEM_LIMIT),
        name="mix_prompt",
    )(relb, sinks, x, g, win, wout, lam, bd, cd, dskip, wglu, bglu)


def _mixs_kernel(relb_ref, sinks_ref, x_ref, g_ref, win_ref, wout_ref, ck_ref, cv_ref,
                 sre0_ref, sim0_ref, lam_ref, bd_ref, cd_ref, dskip_ref, wglu_ref, bglu_ref,
                 o_ref, ks_ref, vs_ref, sre_ref, sim_ref,
                 p_s, attn_s, ssm_s, bias_c, bias_n, us,
                 *, nseq, t_new, sb, wbuf):
    i = pl.program_id(0)
    nsteps = pl.num_programs(0)
    rows = nseq * t_new
    brow = sb * t_new
    u_off = D_ATTN + 2 * D_KV
    tshift = t_new.bit_length() - 1
    wshift = wbuf.bit_length() - 1

    @pl.when(i == 0)
    def _():
        x = x_ref[...]
        h = _rmsnorm(x, g_ref[...]).astype(BF16)
        p_s[...] = _dot(h, win_ref[...])

        for s in range(U_SLABS):
            us[s] = p_s[:, u_off + s * LANES:u_off + (s + 1) * LANES]
        uperm = jnp.concatenate(
            [jnp.concatenate([us[s, pl.ds(t, nseq, stride=t_new), :] for t in range(t_new)], axis=0)
             for s in range(U_SLABS)], axis=1)
        yparts = []
        for hf in range(2):
            bu = _dot(uperm[:, hf * HALF_U:(hf + 1) * HALF_U].astype(BF16), bd_ref[hf])
            lr = lam_ref[2 * hf:2 * hf + 1, :]
            li = lam_ref[2 * hf + 1:2 * hf + 2, :]
            xr = _transpose_blocks(sre0_ref[hf * HALF_STATE:(hf + 1) * HALF_STATE, :])
            xi = _transpose_blocks(sim0_ref[hf * HALF_STATE:(hf + 1) * HALF_STATE, :])
            states = []
            for t in range(t_new):
                br = bu[t * nseq:(t + 1) * nseq, 0:HALF_STATE]
                bi = bu[t * nseq:(t + 1) * nseq, HALF_STATE:2 * HALF_STATE]
                xr, xi = lr * xr - li * xi + br, lr * xi + li * xr + bi
                states.append(jnp.concatenate([xr, xi], axis=1).astype(BF16))
            sre_ref[hf * HALF_STATE:(hf + 1) * HALF_STATE, :] = _transpose_blocks(xr)
            sim_ref[hf * HALF_STATE:(hf + 1) * HALF_STATE, :] = _transpose_blocks(xi)
            yparts.append(_dot(jnp.concatenate(states, axis=0), cd_ref[hf]))
        yperm = jnp.concatenate(yparts, axis=1)
        for s in range(U_SLABS):
            for t in range(t_new):
                us[s, pl.ds(t, nseq, stride=t_new), :] = (
                    yperm[t * nseq:(t + 1) * nseq, s * LANES:(s + 1) * LANES])
        y = jnp.concatenate([us[s] for s in range(U_SLABS)], axis=1)
        ssm_s[...] = _glu_tail(y, p_s[:, u_off:u_off + D_SSM], dskip_ref, wglu_ref, bglu_ref)

        ncol = sb * wbuf
        rho = lax.broadcasted_iota(jnp.int32, (brow, ncol), 0)
        kap = lax.broadcasted_iota(jnp.int32, (brow, ncol), 1)
        same_seq = (rho >> tshift) == (kap >> wshift)
        rho_w = lax.broadcasted_iota(jnp.int32, (brow, wbuf), 0)
        d_c = (rho_w & (t_new - 1)) - lax.broadcasted_iota(jnp.int32, (brow, wbuf), 1) + wbuf
        valid_c = (d_c >= 0) & (d_c < WINDOW)
        rho_n = lax.broadcasted_iota(jnp.int32, (brow, LANES), 0)
        kap_n = lax.broadcasted_iota(jnp.int32, (brow, LANES), 1)
        tq_n = rho_n & (t_new - 1)
        d_n = tq_n - (kap_n & (t_new - 1))
        valid_n = ((rho_n >> tshift) == (kap_n >> tshift)) & (d_n >= 0) & (kap_n < brow)
        for hd in range(N_HEADS):
            g, r = hd // N_REP, hd % N_REP
            tile = _masked_bias(relb_ref, hd, d_c, valid_c)
            bias_c[g, r * brow:(r + 1) * brow, :] = jnp.where(
                same_seq, jnp.concatenate([tile] * sb, axis=1), NEG_INF)
            bias_n[g, r * brow:(r + 1) * brow, :] = _masked_bias(relb_ref, hd, d_n, valid_n)

    def seq_block(k):
        first = k * sb
        row0 = pl.multiple_of((i * SAMPLE_BLOCKS + k) * brow, brow)
        kc_t = jnp.concatenate([ck_ref[first + b] for b in range(sb)], axis=1).astype(BF16)
        vc_t = jnp.concatenate([cv_ref[first + b] for b in range(sb)], axis=1).astype(BF16)
        pad = jnp.zeros((LANES - brow, D_KV), F32)
        kn_pad = jnp.concatenate([p_s[pl.ds(row0, brow), D_ATTN:D_ATTN + D_KV], pad], axis=0)
        vn_pad = jnp.concatenate(
            [p_s[pl.ds(row0, brow), D_ATTN + D_KV:D_ATTN + 2 * D_KV], pad], axis=0)
        kn_b = kn_pad.astype(BF16)
        vn_b = vn_pad.astype(BF16)
        q_chunks = [p_s[pl.ds(row0, brow), c * LANES:(c + 1) * LANES]
                    for c in range(D_ATTN // LANES)]
        o_by_g = []
        for g in range(N_KV_HEADS):
            q = _kv_lane_mask(_stack_queries(q_chunks, g), g)
            lc = _dot(q, kc_t) + bias_c[g]
            ln = _dot_nt(q, kn_b) + bias_n[g]
            s = _sink_column(sinks_ref, g, brow)
            m = jnp.maximum(jnp.maximum(jnp.max(lc, axis=-1, keepdims=True),
                                        jnp.max(ln, axis=-1, keepdims=True)), s)
            ec = jnp.exp(lc - m)
            en = jnp.exp(ln - m)
            denom = (jnp.sum(ec, axis=-1, keepdims=True) + jnp.sum(en, axis=-1, keepdims=True)
                     + jnp.exp(s - m))
            o = _dot_nt(ec.astype(BF16), vc_t) + _dot(en.astype(BF16), vn_b)
            o_by_g.append(o / denom)
        for c, chunk in enumerate(_unstack_heads(o_by_g, brow)):
            attn_s[pl.ds(row0, brow), c * LANES:(c + 1) * LANES] = chunk

        lane = lax.broadcasted_iota(jnp.int32, (D_KV, wbuf), 1)
        keep = wbuf - t_new
        for new_pad, old_ref, out_ref in ((kn_pad, ck_ref, ks_ref), (vn_pad, cv_ref, vs_ref)):
            new_t = new_pad.T
            for b in range(sb):
                shifted = pltpu.roll(old_ref[first + b], keep, 1)
                appended = pltpu.roll(new_t, (keep - b * t_new) % wbuf, 1)
                out_ref[first + b] = jnp.where(lane >= keep, appended, shifted)

    for k in range(SAMPLE_BLOCKS):
        seq_block(k)

    @pl.when(i == nsteps - 1)
    def _():
        mix = (_dot(attn_s[...].astype(BF16), wout_ref[0:D_ATTN, :])
               + _dot(ssm_s[...].astype(BF16), wout_ref[D_ATTN:D_ATTN + D_SSM, :]))
        o_ref[...] = x_ref[...] + mix


def _mix_sample(x, relb, sinks, g, win, wout, ck, cv, sre0, sim0, lam, bd, cd, dskip, wglu, bglu,
                *, nseq, t_new):
    rows = nseq * t_new
    wbuf = ck.shape[2]
    assert wbuf == LANES
    sb = SUBLANES
    brow = sb * t_new
    return pl.pallas_call(
        functools.partial(_mixs_kernel, nseq=nseq, t_new=t_new, sb=sb, wbuf=wbuf),
        grid=(nseq // (SAMPLE_BLOCKS * sb),),
        in_specs=[
            _smem_spec(), _smem_spec(),
            _const_spec((rows, D_MODEL)),
            _const_spec((1, D_MODEL)),
            _const_spec((D_MODEL, D_IN)),
            _const_spec((D_ATTN + D_SSM, D_MODEL)),
            pl.BlockSpec((SAMPLE_BLOCKS * sb, D_KV, wbuf), lambda i: (i, 0, 0)),
            pl.BlockSpec((SAMPLE_BLOCKS * sb, D_KV, wbuf), lambda i: (i, 0, 0)),
            _const_spec((2 * HALF_STATE, nseq)),
            _const_spec((2 * HALF_STATE, nseq)),
            _const_spec((4, HALF_STATE)),
            _const_spec((2, HALF_U, 2 * HALF_STATE)),
            _const_spec((2, 2 * HALF_STATE, HALF_U)),
            _const_spec((1, D_SSM)),
            _const_spec((D_SSM, D_SSM)),
            _const_spec((1, D_SSM)),
        ],
        out_specs=[
            pl.BlockSpec((rows, D_MODEL), lambda i: (0, 0)),
            pl.BlockSpec((SAMPLE_BLOCKS * sb, D_KV, wbuf), lambda i: (i, 0, 0)),
            pl.BlockSpec((SAMPLE_BLOCKS * sb, D_KV, wbuf), lambda i: (i, 0, 0)),
            pl.BlockSpec((2 * HALF_STATE, nseq), lambda i: (0, 0)),
            pl.BlockSpec((2 * HALF_STATE, nseq), lambda i: (0, 0)),
        ],
        out_shape=[
            jax.ShapeDtypeStruct((rows, D_MODEL), F32),
            jax.ShapeDtypeStruct((nseq, D_KV, wbuf), F32),
            jax.ShapeDtypeStruct((nseq, D_KV, wbuf), F32),
            jax.ShapeDtypeStruct((2 * HALF_STATE, nseq), F32),
            jax.ShapeDtypeStruct((2 * HALF_STATE, nseq), F32),
        ],
        scratch_shapes=[
            pltpu.VMEM((rows, D_IN), F32),
            pltpu.VMEM((rows, D_ATTN), F32),
            pltpu.VMEM((rows, D_SSM), F32),
            pltpu.VMEM((N_KV_HEADS, N_REP * brow, sb * wbuf), F32),
            pltpu.VMEM((N_KV_HEADS, N_REP * brow, LANES), F32),
            pltpu.VMEM((U_SLABS, rows, LANES), F32),
        ],
        compiler_params=pltpu.CompilerParams(
            dimension_semantics=("arbitrary",), vmem_limit_bytes=VMEM_LIMIT),
        name="mix_sample",
    )(relb, sinks, x, g, win, wout, ck, cv, sre0, sim0, lam, bd, cd, dskip, wglu, bglu)


def _s5_operators(log_dt, a_re, a_im, b_re, b_im, c_re, c_im):
    dt = jnp.exp(log_dt)[:, None]
    mag = jnp.exp(a_re * dt)
    lb_re = mag * jnp.cos(a_im * dt)
    lb_im = mag * jnp.sin(a_im * dt)
    den = a_re * a_re + a_im * a_im
    nr = lb_re - 1.0
    q_re = (nr * a_re + lb_im * a_im) / den
    q_im = (lb_im * a_re - nr * a_im) / den
    bb_re = q_re[..., None] * b_re - q_im[..., None] * b_im
    bb_im = q_re[..., None] * b_im + q_im[..., None] * b_re
    eye = jnp.eye(HALF_GROUPS, dtype=BF16)
    lam = jnp.stack([lb_re.reshape(2, HALF_STATE), lb_im.reshape(2, HALF_STATE)], axis=1)
    bb = jnp.stack([bb_re, bb_im]).astype(BF16).reshape(2, 2, HALF_GROUPS, SSM_STATE, SSM_GROUP)
    bd = jnp.einsum('rhgpc,gk->hgcrkp', bb, eye).reshape(2, HALF_U, 2 * HALF_STATE)
    cc = jnp.stack([c_re, -c_im]).astype(BF16).reshape(2, 2, HALF_GROUPS, SSM_GROUP, SSM_STATE)
    cd = jnp.einsum('rhgcp,gk->hrgpkc', cc, eye).reshape(2, 2 * HALF_STATE, HALF_U)
    return lam.reshape(4, HALF_STATE), bd, cd


def kernel(x_prompt, x_sample, cache_k, cache_v, state_ssm_re, state_ssm_im, rel_bias,
           ffn1_norm, ffn1_w_gate, ffn1_w_up, ffn1_w_down, mix_norm, w_in, sinks,
           log_dt, a_re, a_im, b_re, b_im, c_re, c_im, d_skip, w_glu, b_glu, w_out,
           ffn2_norm, ffn2_w_gate, ffn2_w_up, ffn2_w_down, final_norm):
    depth = w_in.shape[0]
    assert depth == 1
    batch, seq, _ = x_prompt.shape
    nseq, t_new, _ = x_sample.shape
    fg = final_norm.reshape(1, D_MODEL)

    l = 0
    ffn1 = (ffn1_norm[l].reshape(1, D_MODEL), ffn1_w_gate[l], ffn1_w_up[l], ffn1_w_down[l], fg)
    ffn2 = (ffn2_norm[l].reshape(1, D_MODEL), ffn2_w_gate[l], ffn2_w_up[l], ffn2_w_down[l], fg)
    lam, bd, cd = _s5_operators(log_dt[l], a_re[l], a_im[l], b_re[l], b_im[l], c_re[l], c_im[l])
    mix_w = (mix_norm[l].reshape(1, D_MODEL), w_in[l].astype(BF16), w_out[l].astype(BF16))
    ssm_w = (lam, bd, cd, d_skip[l].reshape(1, D_SSM), w_glu[l].astype(BF16),
             b_glu[l].reshape(1, D_SSM))
    sinks_l = sinks[l]

    def window_in(c):
        return jnp.transpose(c, (0, 2, 3, 1)).reshape(c.shape[0], D_KV, c.shape[1])

    def window_out(w):
        n, _, pos = w.shape
        return jnp.transpose(w.reshape(n, N_KV_HEADS, HEAD_DIM, pos), (0, 3, 1, 2))[None]

    def state_in(s):
        return jnp.transpose(s, (1, 2, 0)).reshape(2 * HALF_STATE, s.shape[0])

    def state_out(s):
        return jnp.transpose(s.reshape(N_SSM_GROUPS, SSM_STATE, s.shape[1]), (2, 0, 1))[None]

    relb_t = rel_bias.T
    xp = x_prompt.reshape(batch * seq, D_MODEL)
    xs = x_sample.reshape(nseq * t_new, D_MODEL)
    ys, yp = _ffn(xp, xs, *ffn1, final_norm=False)
    yp, kp, vp, sre_p, sim_p = _mix_prompt(
        yp.reshape(batch, seq, D_MODEL), relb_t, sinks_l, *mix_w, *ssm_w)
    ys, ks, vs, sre_s, sim_s = _mix_sample(
        ys, relb_t, sinks_l, *mix_w,
        window_in(cache_k[l]), window_in(cache_v[l]),
        state_in(state_ssm_re[l]), state_in(state_ssm_im[l]),
        *ssm_w, nseq=nseq, t_new=t_new)
    y_sample, y_prompt = _ffn(yp.reshape(batch * seq, D_MODEL), ys, *ffn2, final_norm=True)
    y_prompt = y_prompt.reshape(batch, seq, D_MODEL)
    y_sample = y_sample.reshape(nseq, t_new, D_MODEL)

    st_p = (1, batch, N_SSM_GROUPS, SSM_STATE)
    return (y_prompt, y_sample,
            window_out(kp), window_out(vp), sre_p.reshape(st_p), sim_p.reshape(st_p),
            window_out(ks), window_out(vs), state_out(sre_s), state_out(sim_s))
```

```python
import functools
import math

import jax
import jax.numpy as jnp
from jax import lax
from jax.experimental import pallas as pl
from jax.experimental.pallas import tpu as pltpu

F32 = jnp.float32
BF16 = jnp.bfloat16

D_MODEL = 1024
HEAD_DIM = 64
D_ATTN = 512
N_HEADS = 8
N_KV_HEADS = 2
N_REP = 4
D_KV = 128
D_SSM = 512
SSM_GROUP = 16
N_SSM_GROUPS = 32
SSM_STATE = 64
WINDOW = 128
NUM_BUCKETS = 32
MAX_DISTANCE = 128
D_FF = 2816
D_IN = D_ATTN + 2 * D_KV + D_SSM
RMS_EPS = 1e-6
NEG_INF = -1e30
LOG2E = math.log2(math.e)

LANES = 128
SUBLANES = 8
MXU_DIM = 256
FF_CHUNKS = D_FF // MXU_DIM
HALF_GROUPS = N_SSM_GROUPS // 2
HALF_STATE = HALF_GROUPS * SSM_STATE
HALF_U = HALF_GROUPS * SSM_GROUP
U_SLABS = D_SSM // LANES
PITCH = WINDOW + SUBLANES
SCAN_STEPS = 128
RING = 1
SAMPLE_BLOCKS = 1
VMEM_LIMIT = 60 * 1024 * 1024


def _const_spec(shape):
    nd = len(shape)
    return pl.BlockSpec(shape, lambda *_: (0,) * nd, pipeline_mode=pl.Buffered(1))


def _smem_spec():
    return pl.BlockSpec(memory_space=pltpu.SMEM)


def _rmsnorm(x, g):
    r = lax.rsqrt(jnp.mean(x * x, axis=-1, keepdims=True) + RMS_EPS)
    return (x * r) * g


def _dot(a, b):
    return jnp.dot(a, b, preferred_element_type=F32)


def _dot_nt(a, b):
    return lax.dot_general(a, b, (((1,), (1,)), ((), ())), preferred_element_type=F32)


def _transpose_blocks(a):
    r, c = a.shape
    return jnp.concatenate(
        [jnp.concatenate([a[i * LANES:(i + 1) * LANES, j * LANES:(j + 1) * LANES].T
                          for i in range(r // LANES)], axis=1)
         for j in range(c // LANES)], axis=0)


def _ffn_tile(x, g_ref, wg_ref, wu_ref, wd_ref, fg_ref, final_norm, before_chunk=None):
    h = _rmsnorm(x, g_ref[...]).astype(BF16)
    acc = None
    for c in range(FF_CHUNKS):
        if before_chunk is not None:
            before_chunk(c)
        sl = slice(c * MXU_DIM, (c + 1) * MXU_DIM)
        gate = _dot(h, wg_ref[:, sl].astype(BF16))
        up = _dot(h, wu_ref[:, sl].astype(BF16))
        a = (gate * jax.nn.sigmoid(gate) * up).astype(BF16)
        part = _dot(a, wd_ref[sl, :].astype(BF16))
        acc = part if acc is None else acc + part
    y = x + 0.5 * acc
    if final_norm:
        y = _rmsnorm(y, fg_ref[...])
    return y


def _ffn_kernel(xs_ref, xp_ref, g_ref, wg_hbm, wu_hbm, wd_hbm, fg_ref, os_ref, op_ref,
                wg_s, wu_s, wd_s, sems, *, final_norm):
    i = pl.program_id(0)
    weights = (g_ref, wg_s, wu_s, wd_s, fg_ref)

    def chunk_copies(c):
        sl = slice(c * MXU_DIM, (c + 1) * MXU_DIM)
        return (pltpu.make_async_copy(wg_hbm.at[:, sl], wg_s.at[:, sl], sems.at[0, c]),
                pltpu.make_async_copy(wu_hbm.at[:, sl], wu_s.at[:, sl], sems.at[1, c]),
                pltpu.make_async_copy(wd_hbm.at[sl, :], wd_s.at[sl, :], sems.at[2, c]))

    @pl.when(i == 0)
    def _():
        for c in range(FF_CHUNKS):
            for cp in chunk_copies(c):
                cp.start()

        def wait_chunk(c):
            for cp in chunk_copies(c):
                cp.wait()

        os_ref[...] = _ffn_tile(xs_ref[...], *weights, final_norm, wait_chunk)

    @pl.when(i > 0)
    def _():
        op_ref[...] = _ffn_tile(xp_ref[...], *weights, final_norm)


def _ffn(xp, xs, g, wg, wu, wd, fg, *, final_norm):
    n_p, n_s = xp.shape[0], xs.shape[0]
    tm = n_s
    prompt_steps = n_p // tm
    assert prompt_steps * tm == n_p
    prompt_block = lambda i: (jnp.maximum(i - 1, 0), 0)
    hbm = pl.BlockSpec(memory_space=pl.ANY)
    return pl.pallas_call(
        functools.partial(_ffn_kernel, final_norm=final_norm),
        grid=(prompt_steps + 1,),
        in_specs=[
            _const_spec((n_s, D_MODEL)),
            pl.BlockSpec((tm, D_MODEL), prompt_block),
            _const_spec((1, D_MODEL)),
            hbm, hbm, hbm,
            _const_spec((1, D_MODEL)),
        ],
        out_specs=[
            pl.BlockSpec((n_s, D_MODEL), lambda i: (0, 0)),
            pl.BlockSpec((tm, D_MODEL), prompt_block),
        ],
        out_shape=[
            jax.ShapeDtypeStruct((n_s, D_MODEL), F32),
            jax.ShapeDtypeStruct((n_p, D_MODEL), F32),
        ],
        scratch_shapes=[
            pltpu.VMEM((D_MODEL, D_FF), F32),
            pltpu.VMEM((D_MODEL, D_FF), F32),
            pltpu.VMEM((D_FF, D_MODEL), F32),
            pltpu.SemaphoreType.DMA((3, FF_CHUNKS)),
        ],
        compiler_params=pltpu.CompilerParams(
            dimension_semantics=("arbitrary",), vmem_limit_bytes=VMEM_LIMIT),
        name="ffn_final" if final_norm else "ffn",
    )(xs, xp, g, wg, wu, wd, fg)


def _t5_bucket(d):
    d = jnp.maximum(d, 0)
    max_exact = NUM_BUCKETS // 2
    df = jnp.maximum(d, 1).astype(F32)
    large = max_exact + (jnp.log(df / max_exact) / math.log(MAX_DISTANCE / max_exact)
                         * (NUM_BUCKETS - max_exact)).astype(jnp.int32)
    large = jnp.minimum(large, NUM_BUCKETS - 1)
    return jnp.where(d < max_exact, d, large)


def _masked_bias(relb_ref, head, d, valid, scale=1.0):
    bucket = _t5_bucket(d)
    b = jnp.zeros(d.shape, F32)
    for k in range(NUM_BUCKETS):
        b = jnp.where(bucket == k, relb_ref[head, k] * scale, b)
    return jnp.where(valid, b, NEG_INF)


def _kv_lane_mask(t, g):
    lane = lax.broadcasted_iota(jnp.int32, t.shape, 1)
    return jnp.where((lane >= g * HEAD_DIM) & (lane < (g + 1) * HEAD_DIM), t, jnp.zeros_like(t))


def _stack_queries(q_chunks, g, scale=HEAD_DIM ** -0.5):
    qs = []
    for r in range(N_REP):
        qc = q_chunks[2 * g + r // 2]
        if r % 2 != g:
            qc = pltpu.roll(qc, HEAD_DIM, 1)
        qs.append(qc)
    return (jnp.concatenate(qs, axis=0) * scale).astype(BF16)


def _sink_column(sinks_ref, g, rows):
    return jnp.concatenate(
        [jnp.full((rows, 1), sinks_ref[N_REP * g + r], F32) for r in range(N_REP)], axis=0)


def _unstack_heads(o_by_g, rows):
    lane = lax.broadcasted_iota(jnp.int32, (rows, LANES), 1)
    chunks = []
    for c in range(D_ATTN // LANES):
        g = c // 2
        halves = []
        for half in range(2):
            r = 2 * (c % 2) + half
            piece = o_by_g[g][r * rows:(r + 1) * rows]
            if half != g:
                piece = pltpu.roll(piece, HEAD_DIM, 1)
            halves.append(piece)
        chunks.append(jnp.where(lane < HEAD_DIM, halves[0], halves[1]))
    return chunks


def _glu_tail(y, u, dskip_ref, wglu_ref, bglu_ref):
    y = y + dskip_ref[...] * u
    y = jax.nn.gelu(y)
    z = _dot(y.astype(BF16), wglu_ref[...]) + bglu_ref[...]
    return y * jax.nn.sigmoid(z)


def _mixp_kernel(relb_ref, sinks_ref, x_ref, g_ref, win_ref, wout_ref, lam_ref, bd_ref, cd_ref,
                 dskip_ref, wglu_ref, bglu_ref,
                 o_ref, kp_ref, vp_ref, sre_ref, sim_ref,
                 p_s, kband, vband_t, bias_t, bias_later, us, ys, state, attn_s, *ring_refs,
                 nb, blk):
    i = pl.program_id(0)
    rows = nb * blk
    rings, u_rings = ring_refs[:RING], ring_refs[RING:]

    @pl.when(i == 0)
    def _():
        kband[...] = jnp.zeros(kband.shape, BF16)
        vband_t[...] = jnp.zeros(vband_t.shape, BF16)
        state[...] = jnp.zeros(state.shape, F32)

    @pl.when(i == 0)
    def _():
        kj = lax.broadcasted_iota(jnp.int32, (2 * blk, blk), 0)
        qi = lax.broadcasted_iota(jnp.int32, (2 * blk, blk), 1)
        d = qi - kj + blk
        valid = (d >= 0) & (d < WINDOW)
        for h in range(N_HEADS):
            cols = slice((h % N_REP) * blk, (h % N_REP + 1) * blk)
            tile = _masked_bias(relb_ref, h, d, valid, scale=LOG2E)
            bias_later[h // N_REP, :, cols] = tile
            bias_t[h // N_REP, :, cols] = jnp.where(kj >= blk, tile, NEG_INF)

    @pl.when(i == 1)
    def _():
        bias_t[...] = bias_later[...]

    h = _rmsnorm(x_ref[...].reshape(rows, D_MODEL), g_ref[...]).astype(BF16)
    u_off = D_ATTN + 2 * D_KV
    p = _dot(h, win_ref[...])
    p_s[...] = p[:, 0:u_off]
    for s in range(U_SLABS):
        for b in range(nb):
            us[s, b * PITCH:b * PITCH + blk, :] = (
                p[b * blk:(b + 1) * blk, u_off + s * LANES:u_off + (s + 1) * LANES])

    kband[:, blk:2 * blk, :] = (
        p_s[:, D_ATTN:D_ATTN + D_KV].reshape(nb, blk, D_KV).astype(BF16))

    def attention(b):
        rows_b = slice(b * blk, (b + 1) * blk)
        kb = kband[b]
        vband_t[b, :, blk:2 * blk] = p_s[rows_b, D_ATTN + D_KV:D_ATTN + 2 * D_KV].T.astype(BF16)
        vt = vband_t[b]
        feat = lax.broadcasted_iota(jnp.int32, vt.shape, 0)
        q_chunks = [p_s[rows_b, c * LANES:(c + 1) * LANES] for c in range(D_ATTN // LANES)]
        o_by_g = []
        for g in range(N_KV_HEADS):
            q = _stack_queries(q_chunks, g, scale=LOG2E * HEAD_DIM ** -0.5)
            st = _dot_nt(_kv_lane_mask(kb, g), q) + bias_t[g]
            s = jnp.concatenate(
                [jnp.full((1, blk), sinks_ref[N_REP * g + r] * LOG2E, F32) for r in range(N_REP)],
                axis=1)
            m = jnp.maximum(jnp.max(st, axis=0, keepdims=True), s)
            e = jnp.exp2(st - m).astype(BF16)
            in_g = (feat >= g * HEAD_DIM) & (feat < (g + 1) * HEAD_DIM)
            ot = _dot(jnp.where(in_g, vt, jnp.ones_like(vt)), e)
            other = (1 - g) * HEAD_DIM
            denom = ot[other:other + 1, :] + jnp.exp2(s - m)
            o_by_g.append(ot[g * HEAD_DIM:(g + 1) * HEAD_DIM, :] / denom)
        for c in range(D_ATTN // LANES):
            og = o_by_g[c // 2]
            cc = 2 * (c % 2)
            chunk_t = jnp.concatenate(
                [og[:, cc * blk:(cc + 1) * blk], og[:, (cc + 1) * blk:(cc + 2) * blk]], axis=0)
            attn_s[rows_b, c * LANES:(c + 1) * LANES] = chunk_t.T

    def scan_gather():
        steps = []
        for t in range(blk):
            steps.append(jnp.concatenate(
                [us[s, pl.ds(t, nb, stride=PITCH), :] for s in range(U_SLABS)], axis=1))
        u_block = jnp.concatenate(steps, axis=0)
        u_rings[0][...] = u_block
        return u_block.astype(BF16)

    def scan_half(ub, hf):
        ring = rings[0]
        re = slice(2 * hf * HALF_STATE, (2 * hf + 1) * HALF_STATE)
        im = slice((2 * hf + 1) * HALF_STATE, (2 * hf + 2) * HALF_STATE)
        both = slice(2 * hf * HALF_STATE, 2 * (hf + 1) * HALF_STATE)
        lr = jnp.broadcast_to(lam_ref[2 * hf:2 * hf + 1, :], (nb, HALF_STATE))
        li = jnp.broadcast_to(lam_ref[2 * hf + 1:2 * hf + 2, :], (nb, HALF_STATE))
        xr = state[:, re]
        xi = state[:, im]
        for t in range(blk):
            rows_t = slice(t * nb, (t + 1) * nb)
            xr, xi = (lr * xr - li * xi + ring[rows_t, re],
                      lr * xi + li * xr + ring[rows_t, im])
            ring[rows_t, re] = xr
            ring[rows_t, im] = xi
        state[:, re] = xr
        state[:, im] = xi
        return _dot(ring[:, both].astype(BF16), cd_ref[hf])

    for b in range(nb):
        attention(b)
    ub = scan_gather()
    for hf in range(2):
        rings[0][:, 2 * hf * HALF_STATE:2 * (hf + 1) * HALF_STATE] = _dot(
            ub[:, hf * HALF_U:(hf + 1) * HALF_U], bd_ref[hf])
    y = jnp.concatenate([scan_half(ub, hf) for hf in range(2)], axis=1)
    ssm_t = _glu_tail(y, u_rings[0][...], dskip_ref, wglu_ref, bglu_ref)
    for t in range(blk):
        for s in range(U_SLABS):
            ys[s, pl.ds(t, nb, stride=PITCH), :] = ssm_t[t * nb:(t + 1) * nb, s * LANES:(s + 1) * LANES]

    kband[:, 0:blk, :] = kband[:, blk:2 * blk, :]
    vband_t[:, :, 0:blk] = vband_t[:, :, blk:2 * blk]
    for hf in range(2):
        sre_ref[:, hf * HALF_STATE:(hf + 1) * HALF_STATE] = (
            state[:, 2 * hf * HALF_STATE:(2 * hf + 1) * HALF_STATE])
        sim_ref[:, hf * HALF_STATE:(hf + 1) * HALF_STATE] = (
            state[:, (2 * hf + 1) * HALF_STATE:(2 * hf + 2) * HALF_STATE])

    ssm = jnp.concatenate(
        [jnp.concatenate([ys[s, b * PITCH:b * PITCH + blk, :] for b in range(nb)], axis=0)
         for s in range(U_SLABS)], axis=1)
    mix = (_dot(attn_s[...].astype(BF16), wout_ref[0:D_ATTN, :])
           + _dot(ssm.astype(BF16), wout_ref[D_ATTN:D_ATTN + D_SSM, :]))
    o_ref[...] = x_ref[...] + mix.reshape(nb, blk, D_MODEL)

    @pl.when(i == pl.num_programs(0) - 1)
    def _():
        for b in range(nb):
            rows_b = slice(b * blk, (b + 1) * blk)
            kp_ref[b] = p_s[rows_b, D_ATTN:D_ATTN + D_KV].T
            vp_ref[b] = p_s[rows_b, D_ATTN + D_KV:D_ATTN + 2 * D_KV].T


def _mix_prompt(x, relb, sinks, g, win, wout, lam, bd, cd, dskip, wglu, bglu):
    nb, seq, _ = x.shape
    blk = WINDOW
    rows = nb * blk
    const2 = _const_spec
    return pl.pallas_call(
        functools.partial(_mixp_kernel, nb=nb, blk=blk),
        grid=(seq // blk,),
        in_specs=[
            _smem_spec(), _smem_spec(),
            pl.BlockSpec((nb, blk, D_MODEL), lambda i: (0, i, 0)),
            const2((1, D_MODEL)),
            const2((D_MODEL, D_IN)),
            const2((D_ATTN + D_SSM, D_MODEL)),
            const2((4, HALF_STATE)),
            const2((2, HALF_U, 2 * HALF_STATE)),
            const2((2, 2 * HALF_STATE, HALF_U)),
            const2((1, D_SSM)),
            const2((D_SSM, D_SSM)),
            const2((1, D_SSM)),
        ],
        out_specs=[
            pl.BlockSpec((nb, blk, D_MODEL), lambda i: (0, i, 0)),
            pl.BlockSpec((nb, blk, D_KV), lambda i: (0, 0, 0)),
            pl.BlockSpec((nb, blk, D_KV), lambda i: (0, 0, 0)),
            pl.BlockSpec((nb, 2 * HALF_STATE), lambda i: (0, 0)),
            pl.BlockSpec((nb, 2 * HALF_STATE), lambda i: (0, 0)),
        ],
        out_shape=[
            jax.ShapeDtypeStruct((nb, seq, D_MODEL), F32),
            jax.ShapeDtypeStruct((nb, blk, D_KV), F32),
            jax.ShapeDtypeStruct((nb, blk, D_KV), F32),
            jax.ShapeDtypeStruct((nb, 2 * HALF_STATE), F32),
            jax.ShapeDtypeStruct((nb, 2 * HALF_STATE), F32),
        ],
        scratch_shapes=[
            pltpu.VMEM((rows, D_ATTN + 2 * D_KV), F32),
            pltpu.VMEM((nb, 2 * blk, D_KV), BF16),
            pltpu.VMEM((nb, D_KV, 2 * blk), BF16),
            pltpu.VMEM((N_KV_HEADS, 2 * blk, N_REP * blk), F32),
            pltpu.VMEM((N_KV_HEADS, 2 * blk, N_REP * blk), F32),
            pltpu.VMEM((U_SLABS, nb * PITCH, LANES), F32),
            pltpu.VMEM((U_SLABS, nb * PITCH, LANES), F32),
            pltpu.VMEM((nb, 4 * HALF_STATE), F32),
            pltpu.VMEM((rows, D_ATTN), F32),
        ] + [pltpu.VMEM((SCAN_STEPS * nb, 4 * HALF_STATE), F32)] * RING
          + [pltpu.VMEM((SCAN_STEPS * nb, D_SSM), F32)] * RING,
        compiler_params=pltpu.CompilerParams(
            dimension_semantics=("arbitrary",), vmem_limit_bytes=VMEM_LIMIT),
        name="mix_prompt",
    )(relb, sinks, x, g, win, wout, lam, bd, cd, dskip, wglu, bglu)


def _mixs_kernel(relb_ref, sinks_ref, x_ref, g_ref, win_ref, wout_ref, ck_ref, cv_ref,
                 sre0_ref, sim0_ref, lam_ref, bd_ref, cd_ref, dskip_ref, wglu_ref, bglu_ref,
                 o_ref, ks_ref, vs_ref, sre_ref, sim_ref,
                 p_s, attn_s, ssm_s, bias_c, bias_n, us,
                 *, nseq, t_new, sb, wbuf):
    i = pl.program_id(0)
    nsteps = pl.num_programs(0)
    rows = nseq * t_new
    brow = sb * t_new
    u_off = D_ATTN + 2 * D_KV
    tshift = t_new.bit_length() - 1
    wshift = wbuf.bit_length() - 1

    @pl.when(i == 0)
    def _():
        x = x_ref[...]
        h = _rmsnorm(x, g_ref[...]).astype(BF16)
        p_s[...] = _dot(h, win_ref[...])

        for s in range(U_SLABS):
            us[s] = p_s[:, u_off + s * LANES:u_off + (s + 1) * LANES]
        uperm = jnp.concatenate(
            [jnp.concatenate([us[s, pl.ds(t, nseq, stride=t_new), :] for t in range(t_new)], axis=0)
             for s in range(U_SLABS)], axis=1)
        yparts = []
        for hf in range(2):
            bu = _dot(uperm[:, hf * HALF_U:(hf + 1) * HALF_U].astype(BF16), bd_ref[hf])
            lr = lam_ref[2 * hf:2 * hf + 1, :]
            li = lam_ref[2 * hf + 1:2 * hf + 2, :]
            xr = _transpose_blocks(sre0_ref[hf * HALF_STATE:(hf + 1) * HALF_STATE, :])
            xi = _transpose_blocks(sim0_ref[hf * HALF_STATE:(hf + 1) * HALF_STATE, :])
            states = []
            for t in range(t_new):
                br = bu[t * nseq:(t + 1) * nseq, 0:HALF_STATE]
                bi = bu[t * nseq:(t + 1) * nseq, HALF_STATE:2 * HALF_STATE]
                xr, xi = lr * xr - li * xi + br, lr * xi + li * xr + bi
                states.append(jnp.concatenate([xr, xi], axis=1).astype(BF16))
            sre_ref[hf * HALF_STATE:(hf + 1) * HALF_STATE, :] = _transpose_blocks(xr)
            sim_ref[hf * HALF_STATE:(hf + 1) * HALF_STATE, :] = _transpose_blocks(xi)
            yparts.append(_dot(jnp.concatenate(states, axis=0), cd_ref[hf]))
        yperm = jnp.concatenate(yparts, axis=1)
        for s in range(U_SLABS):
            for t in range(t_new):
                us[s, pl.ds(t, nseq, stride=t_new), :] = (
                    yperm[t * nseq:(t + 1) * nseq, s * LANES:(s + 1) * LANES])
        y = jnp.concatenate([us[s] for s in range(U_SLABS)], axis=1)
        ssm_s[...] = _glu_tail(y, p_s[:, u_off:u_off + D_SSM], dskip_ref, wglu_ref, bglu_ref)

        ncol = sb * wbuf
        rho = lax.broadcasted_iota(jnp.int32, (brow, ncol), 0)
        kap = lax.broadcasted_iota(jnp.int32, (brow, ncol), 1)
        same_seq = (rho >> tshift) == (kap >> wshift)
        rho_w = lax.broadcasted_iota(jnp.int32, (brow, wbuf), 0)
        d_c = (rho_w & (t_new - 1)) - lax.broadcasted_iota(jnp.int32, (brow, wbuf), 1) + wbuf
        valid_c = (d_c >= 0) & (d_c < WINDOW)
        rho_n = lax.broadcasted_iota(jnp.int32, (brow, LANES), 0)
        kap_n = lax.broadcasted_iota(jnp.int32, (brow, LANES), 1)
        tq_n = rho_n & (t_new - 1)
        d_n = tq_n - (kap_n & (t_new - 1))
        valid_n = ((rho_n >> tshift) == (kap_n >> tshift)) & (d_n >= 0) & (kap_n < brow)
        for hd in range(N_HEADS):
            g, r = hd // N_REP, hd % N_REP
            tile = _masked_bias(relb_ref, hd, d_c, valid_c)
            bias_c[g, r * brow:(r + 1) * brow, :] = jnp.where(
                same_seq, jnp.concatenate([tile] * sb, axis=1), NEG_INF)
            bias_n[g, r * brow:(r + 1) * brow, :] = _masked_bias(relb_ref, hd, d_n, valid_n)

    def seq_block(k):
        first = k * sb
        row0 = pl.multiple_of((i * SAMPLE_BLOCKS + k) * brow, brow)
        kc_t = jnp.concatenate([ck_ref[first + b] for b in range(sb)], axis=1).astype(BF16)
        vc_t = jnp.concatenate([cv_ref[first + b] for b in range(sb)], axis=1).astype(BF16)
        pad = jnp.zeros((LANES - brow, D_KV), F32)
        kn_pad = jnp.concatenate([p_s[pl.ds(row0, brow), D_ATTN:D_ATTN + D_KV], pad], axis=0)
        vn_pad = jnp.concatenate(
            [p_s[pl.ds(row0, brow), D_ATTN + D_KV:D_ATTN + 2 * D_KV], pad], axis=0)
        kn_b = kn_pad.astype(BF16)
        vn_b = vn_pad.astype(BF16)
        q_chunks = [p_s[pl.ds(row0, brow), c * LANES:(c + 1) * LANES]
                    for c in range(D_ATTN // LANES)]
        o_by_g = []
        for g in range(N_KV_HEADS):
            q = _kv_lane_mask(_stack_queries(q_chunks, g), g)
            lc = _dot(q, kc_t) + bias_c[g]
            ln = _dot_nt(q, kn_b) + bias_n[g]
            s = _sink_column(sinks_ref, g, brow)
            m = jnp.maximum(jnp.maximum(jnp.max(lc, axis=-1, keepdims=True),
                                        jnp.max(ln, axis=-1, keepdims=True)), s)
            ec = jnp.exp(lc - m)
            en = jnp.exp(ln - m)
            denom = (jnp.sum(ec, axis=-1, keepdims=True) + jnp.sum(en, axis=-1, keepdims=True)
                     + jnp.exp(s - m))
            o = _dot_nt(ec.astype(BF16), vc_t) + _dot(en.astype(BF16), vn_b)
            o_by_g.append(o / denom)
        for c, chunk in enumerate(_unstack_heads(o_by_g, brow)):
            attn_s[pl.ds(row0, brow), c * LANES:(c + 1) * LANES] = chunk

        lane = lax.broadcasted_iota(jnp.int32, (D_KV, wbuf), 1)
        keep = wbuf - t_new
        for new_pad, old_ref, out_ref in ((kn_pad, ck_ref, ks_ref), (vn_pad, cv_ref, vs_ref)):
            new_t = new_pad.T
            for b in range(sb):
                shifted = pltpu.roll(old_ref[first + b], keep, 1)
                appended = pltpu.roll(new_t, (keep - b * t_new) % wbuf, 1)
                out_ref[first + b] = jnp.where(lane >= keep, appended, shifted)

    for k in range(SAMPLE_BLOCKS):
        seq_block(k)

    @pl.when(i == nsteps - 1)
    def _():
        mix = (_dot(attn_s[...].astype(BF16), wout_ref[0:D_ATTN, :])
               + _dot(ssm_s[...].astype(BF16), wout_ref[D_ATTN:D_ATTN + D_SSM, :]))
        o_ref[...] = x_ref[...] + mix


def _mix_sample(x, relb, sinks, g, win, wout, ck, cv, sre0, sim0, lam, bd, cd, dskip, wglu, bglu,
                *, nseq, t_new):
    rows = nseq * t_new
    wbuf = ck.shape[2]
    assert wbuf == LANES
    sb = SUBLANES
    brow = sb * t_new
    return pl.pallas_call(
        functools.partial(_mixs_kernel, nseq=nseq, t_new=t_new, sb=sb, wbuf=wbuf),
        grid=(nseq // (SAMPLE_BLOCKS * sb),),
        in_specs=[
            _smem_spec(), _smem_spec(),
            _const_spec((rows, D_MODEL)),
            _const_spec((1, D_MODEL)),
            _const_spec((D_MODEL, D_IN)),
            _const_spec((D_ATTN + D_SSM, D_MODEL)),
            pl.BlockSpec((SAMPLE_BLOCKS * sb, D_KV, wbuf), lambda i: (i, 0, 0)),
            pl.BlockSpec((SAMPLE_BLOCKS * sb, D_KV, wbuf), lambda i: (i, 0, 0)),
            _const_spec((2 * HALF_STATE, nseq)),
            _const_spec((2 * HALF_STATE, nseq)),
            _const_spec((4, HALF_STATE)),
            _const_spec((2, HALF_U, 2 * HALF_STATE)),
            _const_spec((2, 2 * HALF_STATE, HALF_U)),
            _const_spec((1, D_SSM)),
            _const_spec((D_SSM, D_SSM)),
            _const_spec((1, D_SSM)),
        ],
        out_specs=[
            pl.BlockSpec((rows, D_MODEL), lambda i: (0, 0)),
            pl.BlockSpec((SAMPLE_BLOCKS * sb, D_KV, wbuf), lambda i: (i, 0, 0)),
            pl.BlockSpec((SAMPLE_BLOCKS * sb, D_KV, wbuf), lambda i: (i, 0, 0)),
            pl.BlockSpec((2 * HALF_STATE, nseq), lambda i: (0, 0)),
            pl.BlockSpec((2 * HALF_STATE, nseq), lambda i: (0, 0)),
        ],
        out_shape=[
            jax.ShapeDtypeStruct((rows, D_MODEL), F32),
            jax.ShapeDtypeStruct((nseq, D_KV, wbuf), F32),
            jax.ShapeDtypeStruct((nseq, D_KV, wbuf), F32),
            jax.ShapeDtypeStruct((2 * HALF_STATE, nseq), F32),
            jax.ShapeDtypeStruct((2 * HALF_STATE, nseq), F32),
        ],
        scratch_shapes=[
            pltpu.VMEM((rows, D_IN), F32),
            pltpu.VMEM((rows, D_ATTN), F32),
            pltpu.VMEM((rows, D_SSM), F32),
            pltpu.VMEM((N_KV_HEADS, N_REP * brow, sb * wbuf), F32),
            pltpu.VMEM((N_KV_HEADS, N_REP * brow, LANES), F32),
            pltpu.VMEM((U_SLABS, rows, LANES), F32),
        ],
        compiler_params=pltpu.CompilerParams(
            dimension_semantics=("arbitrary",), vmem_limit_bytes=VMEM_LIMIT),
        name="mix_sample",
    )(relb, sinks, x, g, win, wout, ck, cv, sre0, sim0, lam, bd, cd, dskip, wglu, bglu)


def _s5_operators(log_dt, a_re, a_im, b_re, b_im, c_re, c_im):
    dt = jnp.exp(log_dt)[:, None]
    mag = jnp.exp(a_re * dt)
    lb_re = mag * jnp.cos(a_im * dt)
    lb_im = mag * jnp.sin(a_im * dt)
    den = a_re * a_re + a_im * a_im
    nr = lb_re - 1.0
    q_re = (nr * a_re + lb_im * a_im) / den
    q_im = (lb_im * a_re - nr * a_im) / den
    bb_re = q_re[..., None] * b_re - q_im[..., None] * b_im
    bb_im = q_re[..., None] * b_im + q_im[..., None] * b_re
    eye = jnp.eye(HALF_GROUPS, dtype=BF16)
    lam = jnp.stack([lb_re.reshape(2, HALF_STATE), lb_im.reshape(2, HALF_STATE)], axis=1)
    bb = jnp.stack([bb_re, bb_im]).astype(BF16).reshape(2, 2, HALF_GROUPS, SSM_STATE, SSM_GROUP)
    bd = jnp.einsum('rhgpc,gk->hgcrkp', bb, eye).reshape(2, HALF_U, 2 * HALF_STATE)
    cc = jnp.stack([c_re, -c_im]).astype(BF16).reshape(2, 2, HALF_GROUPS, SSM_GROUP, SSM_STATE)
    cd = jnp.einsum('rhgcp,gk->hrgpkc', cc, eye).reshape(2, 2 * HALF_STATE, HALF_U)
    return lam.reshape(4, HALF_STATE), bd, cd


def kernel(x_prompt, x_sample, cache_k, cache_v, state_ssm_re, state_ssm_im, rel_bias,
           ffn1_norm, ffn1_w_gate, ffn1_w_up, ffn1_w_down, mix_norm, w_in, sinks,
           log_dt, a_re, a_im, b_re, b_im, c_re, c_im, d_skip, w_glu, b_glu, w_out,
           ffn2_norm, ffn2_w_gate, ffn2_w_up, ffn2_w_down, final_norm):
    depth = w_in.shape[0]
    assert depth == 1
    batch, seq, _ = x_prompt.shape
    nseq, t_new, _ = x_sample.shape
    fg = final_norm.reshape(1, D_MODEL)

    l = 0
    ffn1 = (ffn1_norm[l].reshape(1, D_MODEL), ffn1_w_gate[l], ffn1_w_up[l], ffn1_w_down[l], fg)
    ffn2 = (ffn2_norm[l].reshape(1, D_MODEL), ffn2_w_gate[l], ffn2_w_up[l], ffn2_w_down[l], fg)
    lam, bd, cd = _s5_operators(log_dt[l], a_re[l], a_im[l], b_re[l], b_im[l], c_re[l], c_im[l])
    mix_w = (mix_norm[l].reshape(1, D_MODEL), w_in[l].astype(BF16), w_out[l].astype(BF16))
    ssm_w = (lam, bd, cd, d_skip[l].reshape(1, D_SSM), w_glu[l].astype(BF16),
             b_glu[l].reshape(1, D_SSM))
    sinks_l = sinks[l]

    def window_in(c):
        return jnp.transpose(c, (0, 2, 3, 1)).reshape(c.shape[0], D_KV, c.shape[1])

    def window_out(w):
        n, _, pos = w.shape
        return jnp.transpose(w.reshape(n, N_KV_HEADS, HEAD_DIM, pos), (0, 3, 1, 2))[None]

    def state_in(s):
        return jnp.transpose(s, (1, 2, 0)).reshape(2 * HALF_STATE, s.shape[0])

    def state_out(s):
        return jnp.transpose(s.reshape(N_SSM_GROUPS, SSM_STATE, s.shape[1]), (2, 0, 1))[None]

    relb_t = rel_bias.T
    xp = x_prompt.reshape(batch * seq, D_MODEL)
    xs = x_sample.reshape(nseq * t_new, D_MODEL)
    ys, yp = _ffn(xp, xs, *ffn1, final_norm=False)
    yp, kp, vp, sre_p, sim_p = _mix_prompt(
        yp.reshape(batch, seq, D_MODEL), relb_t, sinks_l, *mix_w, *ssm_w)
    ys, ks, vs, sre_s, sim_s = _mix_sample(
        ys, relb_t, sinks_l, *mix_w,
        window_in(cache_k[l]), window_in(cache_v[l]),
        state_in(state_ssm_re[l]), state_in(state_ssm_im[l]),
        *ssm_w, nseq=nseq, t_new=t_new)
    y_sample, y_prompt = _ffn(yp.reshape(batch * seq, D_MODEL), ys, *ffn2, final_norm=True)
    y_prompt = y_prompt.reshape(batch, seq, D_MODEL)
    y_sample = y_sample.reshape(nseq, t_new, D_MODEL)

    st_p = (1, batch, N_SSM_GROUPS, SSM_STATE)
    return (y_prompt, y_sample,
            window_out(kp), window_out(vp), sre_p.reshape(st_p), sim_p.reshape(st_p),
            window_out(ks), window_out(vs), state_out(sre_s), state_out(sim_s))
```

```python
import functools
import math

import jax
import jax.numpy as jnp
from jax import lax
from jax.experimental import pallas as pl
from jax.experimental.pallas import tpu as pltpu

F32 = jnp.float32
BF16 = jnp.bfloat16

D_MODEL = 1024
HEAD_DIM = 64
D_ATTN = 512
N_HEADS = 8
N_KV_HEADS = 2
N_REP = 4
D_KV = 128
D_SSM = 512
SSM_GROUP = 16
N_SSM_GROUPS = 32
SSM_STATE = 64
WINDOW = 128
NUM_BUCKETS = 32
MAX_DISTANCE = 128
D_FF = 2816
D_IN = D_ATTN + 2 * D_KV + D_SSM
RMS_EPS = 1e-6
NEG_INF = -1e30
LOG2E = math.log2(math.e)

LANES = 128
SUBLANES = 8
MXU_DIM = 256
FF_CHUNKS = D_FF // MXU_DIM
HALF_GROUPS = N_SSM_GROUPS // 2
HALF_STATE = HALF_GROUPS * SSM_STATE
HALF_U = HALF_GROUPS * SSM_GROUP
U_SLABS = D_SSM // LANES
PITCH = WINDOW + SUBLANES
SCAN_STEPS = 128
RING = 1
SAMPLE_BLOCKS = 1
VMEM_LIMIT = 60 * 1024 * 1024


def _const_spec(shape):
    nd = len(shape)
    return pl.BlockSpec(shape, lambda *_: (0,) * nd, pipeline_mode=pl.Buffered(1))


def _smem_spec():
    return pl.BlockSpec(memory_space=pltpu.SMEM)


def _rmsnorm(x, g):
    r = lax.rsqrt(jnp.mean(x * x, axis=-1, keepdims=True) + RMS_EPS)
    return (x * r) * g


def _dot(a, b):
    return jnp.dot(a, b, preferred_element_type=F32)


def _dot_nt(a, b):
    return lax.dot_general(a, b, (((1,), (1,)), ((), ())), preferred_element_type=F32)


def _transpose_blocks(a):
    r, c = a.shape
    return jnp.concatenate(
        [jnp.concatenate([a[i * LANES:(i + 1) * LANES, j * LANES:(j + 1) * LANES].T
                          for i in range(r // LANES)], axis=1)
         for j in range(c // LANES)], axis=0)


def _ffn_tile(x, g_ref, wg_ref, wu_ref, wd_ref, fg_ref, final_norm, before_chunk=None):
    h = _rmsnorm(x, g_ref[...]).astype(BF16)
    acc = None
    for c in range(FF_CHUNKS):
        if before_chunk is not None:
            before_chunk(c)
        sl = slice(c * MXU_DIM, (c + 1) * MXU_DIM)
        gate = _dot(h, wg_ref[:, sl].astype(BF16))
        up = _dot(h, wu_ref[:, sl].astype(BF16))
        a = (gate * jax.nn.sigmoid(gate) * up).astype(BF16)
        part = _dot(a, wd_ref[sl, :].astype(BF16))
        acc = part if acc is None else acc + part
    y = x + 0.5 * acc
    if final_norm:
        y = _rmsnorm(y, fg_ref[...])
    return y


def _ffn_kernel(xs_ref, xp_ref, g_ref, wg_hbm, wu_hbm, wd_hbm, fg_ref, os_ref, op_ref,
                wg_s, wu_s, wd_s, sems, *, final_norm):
    i = pl.program_id(0)
    weights = (g_ref, wg_s, wu_s, wd_s, fg_ref)

    def chunk_copies(c):
        sl = slice(c * MXU_DIM, (c + 1) * MXU_DIM)
        return (pltpu.make_async_copy(wg_hbm.at[:, sl], wg_s.at[:, sl], sems.at[0, c]),
                pltpu.make_async_copy(wu_hbm.at[:, sl], wu_s.at[:, sl], sems.at[1, c]),
                pltpu.make_async_copy(wd_hbm.at[sl, :], wd_s.at[sl, :], sems.at[2, c]))

    @pl.when(i == 0)
    def _():
        for c in range(FF_CHUNKS):
            for cp in chunk_copies(c):
                cp.start()

        def wait_chunk(c):
            for cp in chunk_copies(c):
                cp.wait()

        os_ref[...] = _ffn_tile(xs_ref[...], *weights, final_norm, wait_chunk)

    @pl.when(i > 0)
    def _():
        op_ref[...] = _ffn_tile(xp_ref[...], *weights, final_norm)


def _ffn(xp, xs, g, wg, wu, wd, fg, *, final_norm):
    n_p, n_s = xp.shape[0], xs.shape[0]
    tm = n_s
    prompt_steps = n_p // tm
    assert prompt_steps * tm == n_p
    prompt_block = lambda i: (jnp.maximum(i - 1, 0), 0)
    hbm = pl.BlockSpec(memory_space=pl.ANY)
    return pl.pallas_call(
        functools.partial(_ffn_kernel, final_norm=final_norm),
        grid=(prompt_steps + 1,),
        in_specs=[
            _const_spec((n_s, D_MODEL)),
            pl.BlockSpec((tm, D_MODEL), prompt_block),
            _const_spec((1, D_MODEL)),
            hbm, hbm, hbm,
            _const_spec((1, D_MODEL)),
        ],
        out_specs=[
            pl.BlockSpec((n_s, D_MODEL), lambda i: (0, 0)),
            pl.BlockSpec((tm, D_MODEL), prompt_block),
        ],
        out_shape=[
            jax.ShapeDtypeStruct((n_s, D_MODEL), F32),
            jax.ShapeDtypeStruct((n_p, D_MODEL), F32),
        ],
        scratch_shapes=[
            pltpu.VMEM((D_MODEL, D_FF), F32),
            pltpu.VMEM((D_MODEL, D_FF), F32),
            pltpu.VMEM((D_FF, D_MODEL), F32),
            pltpu.SemaphoreType.DMA((3, FF_CHUNKS)),
        ],
        compiler_params=pltpu.CompilerParams(
            dimension_semantics=("arbitrary",), vmem_limit_bytes=VMEM_LIMIT),
        name="ffn_final" if final_norm else "ffn",
    )(xs, xp, g, wg, wu, wd, fg)


def _t5_bucket(d):
    d = jnp.maximum(d, 0)
    max_exact = NUM_BUCKETS // 2
    df = jnp.maximum(d, 1).astype(F32)
    large = max_exact + (jnp.log(df / max_exact) / math.log(MAX_DISTANCE / max_exact)
                         * (NUM_BUCKETS - max_exact)).astype(jnp.int32)
    large = jnp.minimum(large, NUM_BUCKETS - 1)
    return jnp.where(d < max_exact, d, large)


def _masked_bias(relb_ref, head, d, valid, scale=1.0):
    bucket = _t5_bucket(d)
    b = jnp.zeros(d.shape, F32)
    for k in range(NUM_BUCKETS):
        b = jnp.where(bucket == k, relb_ref[head, k] * scale, b)
    return jnp.where(valid, b, NEG_INF)


def _kv_lane_mask(t, g):
    lane = lax.broadcasted_iota(jnp.int32, t.shape, 1)
    return jnp.where((lane >= g * HEAD_DIM) & (lane < (g + 1) * HEAD_DIM), t, jnp.zeros_like(t))


def _stack_queries(q_chunks, g, scale=HEAD_DIM ** -0.5):
    qs = []
    for r in range(N_REP):
        qc = q_chunks[2 * g + r // 2]
        if r % 2 != g:
            qc = pltpu.roll(qc, HEAD_DIM, 1)
        qs.append(qc)
    return (jnp.concatenate(qs, axis=0) * scale).astype(BF16)


def _sink_column(sinks_ref, g, rows):
    return jnp.concatenate(
        [jnp.full((rows, 1), sinks_ref[N_REP * g + r], F32) for r in range(N_REP)], axis=0)


def _unstack_heads(o_by_g, rows):
    lane = lax.broadcasted_iota(jnp.int32, (rows, LANES), 1)
    chunks = []
    for c in range(D_ATTN // LANES):
        g = c // 2
        halves = []
        for half in range(2):
            r = 2 * (c % 2) + half
            piece = o_by_g[g][r * rows:(r + 1) * rows]
            if half != g:
                piece = pltpu.roll(piece, HEAD_DIM, 1)
            halves.append(piece)
        chunks.append(jnp.where(lane < HEAD_DIM, halves[0], halves[1]))
    return chunks


def _glu_tail(y, u, dskip_ref, wglu_ref, bglu_ref):
    y = y + dskip_ref[...] * u
    y = jax.nn.gelu(y)
    z = _dot(y.astype(BF16), wglu_ref[...]) + bglu_ref[...]
    return y * jax.nn.sigmoid(z)


def _mixp_kernel(relb_ref, sinks_ref, x_ref, g_ref, win_ref, wout_ref, lam_ref, bd_ref, cd_ref,
                 dskip_ref, wglu_ref, bglu_ref,
                 o_ref, kp_ref, vp_ref, sre_ref, sim_ref,
                 p_s, kband, vband_t, bias_t, bias_later, us, ys, state, attn_s, *ring_refs,
                 nb, blk):
    i = pl.program_id(0)
    rows = nb * blk
    rings, u_rings = ring_refs[:RING], ring_refs[RING:]

    @pl.when(i == 0)
    def _():
        kband[...] = jnp.zeros(kband.shape, BF16)
        vband_t[...] = jnp.zeros(vband_t.shape, BF16)
        state[...] = jnp.zeros(state.shape, F32)

    @pl.when(i == 0)
    def _():
        kj = lax.broadcasted_iota(jnp.int32, (2 * blk, blk), 0)
        qi = lax.broadcasted_iota(jnp.int32, (2 * blk, blk), 1)
        d = qi - kj + blk
        valid = (d >= 0) & (d < WINDOW)
        for h in range(N_HEADS):
            cols = slice((h % N_REP) * blk, (h % N_REP + 1) * blk)
            tile = _masked_bias(relb_ref, h, d, valid, scale=LOG2E)
            bias_later[h // N_REP, :, cols] = tile
            bias_t[h // N_REP, :, cols] = jnp.where(kj >= blk, tile, NEG_INF)

    @pl.when(i == 1)
    def _():
        bias_t[...] = bias_later[...]

    h = _rmsnorm(x_ref[...].reshape(rows, D_MODEL), g_ref[...]).astype(BF16)
    u_off = D_ATTN + 2 * D_KV
    p = _dot(h, win_ref[...])
    p_s[...] = p[:, 0:u_off]
    for s in range(U_SLABS):
        for b in range(nb):
            us[s, b * PITCH:b * PITCH + blk, :] = (
                p[b * blk:(b + 1) * blk, u_off + s * LANES:u_off + (s + 1) * LANES])

    kband[:, blk:2 * blk, :] = (
        p_s[:, D_ATTN:D_ATTN + D_KV].reshape(nb, blk, D_KV).astype(BF16))

    def attention_all():
        for b in range(nb):
            vband_t[b, :, blk:2 * blk] = (
                p_s[b * blk:(b + 1) * blk, D_ATTN + D_KV:D_ATTN + 2 * D_KV].T.astype(BF16))
        kb = kband[...]
        vt = vband_t[...]
        feat = lax.broadcasted_iota(jnp.int32, vt.shape, 1)
        lane = lax.broadcasted_iota(jnp.int32, kb.shape, 2)
        q_chunks = [p_s[:, c * LANES:(c + 1) * LANES].reshape(nb, blk, LANES)
                    for c in range(D_ATTN // LANES)]
        o_by_g = []
        for g in range(N_KV_HEADS):
            qs = []
            for r in range(N_REP):
                qc = q_chunks[2 * g + r // 2]
                if r % 2 != g:
                    qc = pltpu.roll(qc, HEAD_DIM, 2)
                qs.append(qc)
            q = (jnp.concatenate(qs, axis=1) * (LOG2E * HEAD_DIM ** -0.5)).astype(BF16)
            kg = jnp.where((lane >= g * HEAD_DIM) & (lane < (g + 1) * HEAD_DIM), kb,
                           jnp.zeros_like(kb))
            st = lax.dot_general(kg, q, (((2,), (2,)), ((0,), (0,))),
                                 preferred_element_type=F32) + bias_t[g]
            s = jnp.concatenate(
                [jnp.full((1, 1, blk), sinks_ref[N_REP * g + r] * LOG2E, F32)
                 for r in range(N_REP)], axis=2)
            m = jnp.maximum(jnp.max(st, axis=1, keepdims=True), s)
            e = jnp.exp2(st - m).astype(BF16)
            in_g = (feat >= g * HEAD_DIM) & (feat < (g + 1) * HEAD_DIM)
            ot = lax.dot_general(jnp.where(in_g, vt, jnp.ones_like(vt)), e,
                                 (((2,), (1,)), ((0,), (0,))),
                                 preferred_element_type=F32)
            other = (1 - g) * HEAD_DIM
            denom = ot[:, other:other + 1, :] + jnp.exp2(s - m)
            o_by_g.append(ot[:, g * HEAD_DIM:(g + 1) * HEAD_DIM, :] / denom)
        for b in range(nb):
            for c in range(D_ATTN // LANES):
                og = o_by_g[c // 2][b]
                cc = 2 * (c % 2)
                chunk_t = jnp.concatenate(
                    [og[:, cc * blk:(cc + 1) * blk], og[:, (cc + 1) * blk:(cc + 2) * blk]], axis=0)
                attn_s[b * blk:(b + 1) * blk, c * LANES:(c + 1) * LANES] = chunk_t.T

    def scan_gather():
        steps = []
        for t in range(blk):
            steps.append(jnp.concatenate(
                [us[s, pl.ds(t, nb, stride=PITCH), :] for s in range(U_SLABS)], axis=1))
        u_block = jnp.concatenate(steps, axis=0)
        u_rings[0][...] = u_block
        return u_block.astype(BF16)

    def scan_half(ub, hf):
        ring = rings[0]
        re = slice(2 * hf * HALF_STATE, (2 * hf + 1) * HALF_STATE)
        im = slice((2 * hf + 1) * HALF_STATE, (2 * hf + 2) * HALF_STATE)
        both = slice(2 * hf * HALF_STATE, 2 * (hf + 1) * HALF_STATE)
        lr = jnp.broadcast_to(lam_ref[2 * hf:2 * hf + 1, :], (nb, HALF_STATE))
        li = jnp.broadcast_to(lam_ref[2 * hf + 1:2 * hf + 2, :], (nb, HALF_STATE))
        xr = state[:, re]
        xi = state[:, im]
        for t in range(blk):
            rows_t = slice(t * nb, (t + 1) * nb)
            xr, xi = (lr * xr - li * xi + ring[rows_t, re],
                      lr * xi + li * xr + ring[rows_t, im])
            ring[rows_t, re] = xr
            ring[rows_t, im] = xi
        state[:, re] = xr
        state[:, im] = xi
        return _dot(ring[:, both].astype(BF16), cd_ref[hf])

    attention_all()
    ub = scan_gather()
    for hf in range(2):
        rings[0][:, 2 * hf * HALF_STATE:2 * (hf + 1) * HALF_STATE] = _dot(
            ub[:, hf * HALF_U:(hf + 1) * HALF_U], bd_ref[hf])
    y = jnp.concatenate([scan_half(ub, hf) for hf in range(2)], axis=1)
    ssm_t = _glu_tail(y, u_rings[0][...], dskip_ref, wglu_ref, bglu_ref)
    for t in range(blk):
        for s in range(U_SLABS):
            ys[s, pl.ds(t, nb, stride=PITCH), :] = ssm_t[t * nb:(t + 1) * nb, s * LANES:(s + 1) * LANES]

    kband[:, 0:blk, :] = kband[:, blk:2 * blk, :]
    vband_t[:, :, 0:blk] = vband_t[:, :, blk:2 * blk]
    for hf in range(2):
        sre_ref[:, hf * HALF_STATE:(hf + 1) * HALF_STATE] = (
            state[:, 2 * hf * HALF_STATE:(2 * hf + 1) * HALF_STATE])
        sim_ref[:, hf * HALF_STATE:(hf + 1) * HALF_STATE] = (
            state[:, (2 * hf + 1) * HALF_STATE:(2 * hf + 2) * HALF_STATE])

    ssm = jnp.concatenate(
        [jnp.concatenate([ys[s, b * PITCH:b * PITCH + blk, :] for b in range(nb)], axis=0)
         for s in range(U_SLABS)], axis=1)
    mix = (_dot(attn_s[...].astype(BF16), wout_ref[0:D_ATTN, :])
           + _dot(ssm.astype(BF16), wout_ref[D_ATTN:D_ATTN + D_SSM, :]))
    o_ref[...] = x_ref[...] + mix.reshape(nb, blk, D_MODEL)

    @pl.when(i == pl.num_programs(0) - 1)
    def _():
        for b in range(nb):
            rows_b = slice(b * blk, (b + 1) * blk)
            kp_ref[b] = p_s[rows_b, D_ATTN:D_ATTN + D_KV].T
            vp_ref[b] = p_s[rows_b, D_ATTN + D_KV:D_ATTN + 2 * D_KV].T


def _mix_prompt(x, relb, sinks, g, win, wout, lam, bd, cd, dskip, wglu, bglu):
    nb, seq, _ = x.shape
    blk = WINDOW
    rows = nb * blk
    const2 = _const_spec
    return pl.pallas_call(
        functools.partial(_mixp_kernel, nb=nb, blk=blk),
        grid=(seq // blk,),
        in_specs=[
            _smem_spec(), _smem_spec(),
            pl.BlockSpec((nb, blk, D_MODEL), lambda i: (0, i, 0)),
            const2((1, D_MODEL)),
            const2((D_MODEL, D_IN)),
            const2((D_ATTN + D_SSM, D_MODEL)),
            const2((4, HALF_STATE)),
            const2((2, HALF_U, 2 * HALF_STATE)),
            const2((2, 2 * HALF_STATE, HALF_U)),
            const2((1, D_SSM)),
            const2((D_SSM, D_SSM)),
            const2((1, D_SSM)),
        ],
        out_specs=[
            pl.BlockSpec((nb, blk, D_MODEL), lambda i: (0, i, 0)),
            pl.BlockSpec((nb, blk, D_KV), lambda i: (0, 0, 0)),
            pl.BlockSpec((nb, blk, D_KV), lambda i: (0, 0, 0)),
            pl.BlockSpec((nb, 2 * HALF_STATE), lambda i: (0, 0)),
            pl.BlockSpec((nb, 2 * HALF_STATE), lambda i: (0, 0)),
        ],
        out_shape=[
            jax.ShapeDtypeStruct((nb, seq, D_MODEL), F32),
            jax.ShapeDtypeStruct((nb, blk, D_KV), F32),
            jax.ShapeDtypeStruct((nb, blk, D_KV), F32),
            jax.ShapeDtypeStruct((nb, 2 * HALF_STATE), F32),
            jax.ShapeDtypeStruct((nb, 2 * HALF_STATE), F32),
        ],
        scratch_shapes=[
            pltpu.VMEM((rows, D_ATTN + 2 * D_KV), F32),
            pltpu.VMEM((nb, 2 * blk, D_KV), BF16),
            pltpu.VMEM((nb, D_KV, 2 * blk), BF16),
            pltpu.VMEM((N_KV_HEADS, 2 * blk, N_REP * blk), F32),
            pltpu.VMEM((N_KV_HEADS, 2 * blk, N_REP * blk), F32),
            pltpu.VMEM((U_SLABS, nb * PITCH, LANES), F32),
            pltpu.VMEM((U_SLABS, nb * PITCH, LANES), F32),
            pltpu.VMEM((nb, 4 * HALF_STATE), F32),
            pltpu.VMEM((rows, D_ATTN), F32),
        ] + [pltpu.VMEM((SCAN_STEPS * nb, 4 * HALF_STATE), F32)] * RING
          + [pltpu.VMEM((SCAN_STEPS * nb, D_SSM), F32)] * RING,
        compiler_params=pltpu.CompilerParams(
            dimension_semantics=("arbitrary",), vmem_limit_bytes=VMEM_LIMIT),
        name="mix_prompt",
    )(relb, sinks, x, g, win, wout, lam, bd, cd, dskip, wglu, bglu)


def _mixs_kernel(relb_ref, sinks_ref, x_ref, g_ref, win_ref, wout_ref, ck_ref, cv_ref,
                 sre0_ref, sim0_ref, lam_ref, bd_ref, cd_ref, dskip_ref, wglu_ref, bglu_ref,
                 o_ref, ks_ref, vs_ref, sre_ref, sim_ref,
                 p_s, attn_s, ssm_s, bias_c, bias_n, us,
                 *, nseq, t_new, sb, wbuf):
    i = pl.program_id(0)
    nsteps = pl.num_programs(0)
    rows = nseq * t_new
    brow = sb * t_new
    u_off = D_ATTN + 2 * D_KV
    tshift = t_new.bit_length() - 1
    wshift = wbuf.bit_length() - 1

    @pl.when(i == 0)
    def _():
        x = x_ref[...]
        h = _rmsnorm(x, g_ref[...]).astype(BF16)
        p_s[...] = _dot(h, win_ref[...])

        for s in range(U_SLABS):
            us[s] = p_s[:, u_off + s * LANES:u_off + (s + 1) * LANES]
        uperm = jnp.concatenate(
            [jnp.concatenate([us[s, pl.ds(t, nseq, stride=t_new), :] for t in range(t_new)], axis=0)
             for s in range(U_SLABS)], axis=1)
        yparts = []
        for hf in range(2):
            bu = _dot(uperm[:, hf * HALF_U:(hf + 1) * HALF_U].astype(BF16), bd_ref[hf])
            lr = lam_ref[2 * hf:2 * hf + 1, :]
            li = lam_ref[2 * hf + 1:2 * hf + 2, :]
            xr = _transpose_blocks(sre0_ref[hf * HALF_STATE:(hf + 1) * HALF_STATE, :])
            xi = _transpose_blocks(sim0_ref[hf * HALF_STATE:(hf + 1) * HALF_STATE, :])
            states = []
            for t in range(t_new):
                br = bu[t * nseq:(t + 1) * nseq, 0:HALF_STATE]
                bi = bu[t * nseq:(t + 1) * nseq, HALF_STATE:2 * HALF_STATE]
                xr, xi = lr * xr - li * xi + br, lr * xi + li * xr + bi
                states.append(jnp.concatenate([xr, xi], axis=1).astype(BF16))
            sre_ref[hf * HALF_STATE:(hf + 1) * HALF_STATE, :] = _transpose_blocks(xr)
            sim_ref[hf * HALF_STATE:(hf + 1) * HALF_STATE, :] = _transpose_blocks(xi)
            yparts.append(_dot(jnp.concatenate(states, axis=0), cd_ref[hf]))
        yperm = jnp.concatenate(yparts, axis=1)
        for s in range(U_SLABS):
            for t in range(t_new):
                us[s, pl.ds(t, nseq, stride=t_new), :] = (
                    yperm[t * nseq:(t + 1) * nseq, s * LANES:(s + 1) * LANES])
        y = jnp.concatenate([us[s] for s in range(U_SLABS)], axis=1)
        ssm_s[...] = _glu_tail(y, p_s[:, u_off:u_off + D_SSM], dskip_ref, wglu_ref, bglu_ref)

        ncol = sb * wbuf
        rho = lax.broadcasted_iota(jnp.int32, (brow, ncol), 0)
        kap = lax.broadcasted_iota(jnp.int32, (brow, ncol), 1)
        same_seq = (rho >> tshift) == (kap >> wshift)
        rho_w = lax.broadcasted_iota(jnp.int32, (brow, wbuf), 0)
        d_c = (rho_w & (t_new - 1)) - lax.broadcasted_iota(jnp.int32, (brow, wbuf), 1) + wbuf
        valid_c = (d_c >= 0) & (d_c < WINDOW)
        rho_n = lax.broadcasted_iota(jnp.int32, (brow, LANES), 0)
        kap_n = lax.broadcasted_iota(jnp.int32, (brow, LANES), 1)
        tq_n = rho_n & (t_new - 1)
        d_n = tq_n - (kap_n & (t_new - 1))
        valid_n = ((rho_n >> tshift) == (kap_n >> tshift)) & (d_n >= 0) & (kap_n < brow)
        for hd in range(N_HEADS):
            g, r = hd // N_REP, hd % N_REP
            tile = _masked_bias(relb_ref, hd, d_c, valid_c)
            bias_c[g, r * brow:(r + 1) * brow, :] = jnp.where(
                same_seq, jnp.concatenate([tile] * sb, axis=1), NEG_INF)
            bias_n[g, r * brow:(r + 1) * brow, :] = _masked_bias(relb_ref, hd, d_n, valid_n)

    def seq_block(k):
        first = k * sb
        row0 = pl.multiple_of((i * SAMPLE_BLOCKS + k) * brow, brow)
        kc_t = jnp.concatenate([ck_ref[first + b] for b in range(sb)], axis=1).astype(BF16)
        vc_t = jnp.concatenate([cv_ref[first + b] for b in range(sb)], axis=1).astype(BF16)
        pad = jnp.zeros((LANES - brow, D_KV), F32)
        kn_pad = jnp.concatenate([p_s[pl.ds(row0, brow), D_ATTN:D_ATTN + D_KV], pad], axis=0)
        vn_pad = jnp.concatenate(
            [p_s[pl.ds(row0, brow), D_ATTN + D_KV:D_ATTN + 2 * D_KV], pad], axis=0)
        kn_b = kn_pad.astype(BF16)
        vn_b = vn_pad.astype(BF16)
        q_chunks = [p_s[pl.ds(row0, brow), c * LANES:(c + 1) * LANES]
                    for c in range(D_ATTN // LANES)]
        o_by_g = []
        for g in range(N_KV_HEADS):
            q = _kv_lane_mask(_stack_queries(q_chunks, g), g)
            lc = _dot(q, kc_t) + bias_c[g]
            ln = _dot_nt(q, kn_b) + bias_n[g]
            s = _sink_column(sinks_ref, g, brow)
            m = jnp.maximum(jnp.maximum(jnp.max(lc, axis=-1, keepdims=True),
                                        jnp.max(ln, axis=-1, keepdims=True)), s)
            ec = jnp.exp(lc - m)
            en = jnp.exp(ln - m)
            denom = (jnp.sum(ec, axis=-1, keepdims=True) + jnp.sum(en, axis=-1, keepdims=True)
                     + jnp.exp(s - m))
            o = _dot_nt(ec.astype(BF16), vc_t) + _dot(en.astype(BF16), vn_b)
            o_by_g.append(o / denom)
        for c, chunk in enumerate(_unstack_heads(o_by_g, brow)):
            attn_s[pl.ds(row0, brow), c * LANES:(c + 1) * LANES] = chunk

        lane = lax.broadcasted_iota(jnp.int32, (D_KV, wbuf), 1)
        keep = wbuf - t_new
        for new_pad, old_ref, out_ref in ((kn_pad, ck_ref, ks_ref), (vn_pad, cv_ref, vs_ref)):
            new_t = new_pad.T
            for b in range(sb):
                shifted = pltpu.roll(old_ref[first + b], keep, 1)
                appended = pltpu.roll(new_t, (keep - b * t_new) % wbuf, 1)
                out_ref[first + b] = jnp.where(lane >= keep, appended, shifted)

    for k in range(SAMPLE_BLOCKS):
        seq_block(k)

    @pl.when(i == nsteps - 1)
    def _():
        mix = (_dot(attn_s[...].astype(BF16), wout_ref[0:D_ATTN, :])
               + _dot(ssm_s[...].astype(BF16), wout_ref[D_ATTN:D_ATTN + D_SSM, :]))
        o_ref[...] = x_ref[...] + mix


def _mix_sample(x, relb, sinks, g, win, wout, ck, cv, sre0, sim0, lam, bd, cd, dskip, wglu, bglu,
                *, nseq, t_new):
    rows = nseq * t_new
    wbuf = ck.shape[2]
    assert wbuf == LANES
    sb = SUBLANES
    brow = sb * t_new
    return pl.pallas_call(
        functools.partial(_mixs_kernel, nseq=nseq, t_new=t_new, sb=sb, wbuf=wbuf),
        grid=(nseq // (SAMPLE_BLOCKS * sb),),
        in_specs=[
            _smem_spec(), _smem_spec(),
            _const_spec((rows, D_MODEL)),
            _const_spec((1, D_MODEL)),
            _const_spec((D_MODEL, D_IN)),
            _const_spec((D_ATTN + D_SSM, D_MODEL)),
            pl.BlockSpec((SAMPLE_BLOCKS * sb, D_KV, wbuf), lambda i: (i, 0, 0)),
            pl.BlockSpec((SAMPLE_BLOCKS * sb, D_KV, wbuf), lambda i: (i, 0, 0)),
            _const_spec((2 * HALF_STATE, nseq)),
            _const_spec((2 * HALF_STATE, nseq)),
            _const_spec((4, HALF_STATE)),
            _const_spec((2, HALF_U, 2 * HALF_STATE)),
            _const_spec((2, 2 * HALF_STATE, HALF_U)),
            _const_spec((1, D_SSM)),
            _const_spec((D_SSM, D_SSM)),
            _const_spec((1, D_SSM)),
        ],
        out_specs=[
            pl.BlockSpec((rows, D_MODEL), lambda i: (0, 0)),
            pl.BlockSpec((SAMPLE_BLOCKS * sb, D_KV, wbuf), lambda i: (i, 0, 0)),
            pl.BlockSpec((SAMPLE_BLOCKS * sb, D_KV, wbuf), lambda i: (i, 0, 0)),
            pl.BlockSpec((2 * HALF_STATE, nseq), lambda i: (0, 0)),
            pl.BlockSpec((2 * HALF_STATE, nseq), lambda i: (0, 0)),
        ],
        out_shape=[
            jax.ShapeDtypeStruct((rows, D_MODEL), F32),
            jax.ShapeDtypeStruct((nseq, D_KV, wbuf), F32),
            jax.ShapeDtypeStruct((nseq, D_KV, wbuf), F32),
            jax.ShapeDtypeStruct((2 * HALF_STATE, nseq), F32),
            jax.ShapeDtypeStruct((2 * HALF_STATE, nseq), F32),
        ],
        scratch_shapes=[
            pltpu.VMEM((rows, D_IN), F32),
            pltpu.VMEM((rows, D_ATTN), F32),
            pltpu.VMEM((rows, D_SSM), F32),
            pltpu.VMEM((N_KV_HEADS, N_REP * brow, sb * wbuf), F32),
            pltpu.VMEM((N_KV_HEADS, N_REP * brow, LANES), F32),
            pltpu.VMEM((U_SLABS, rows, LANES), F32),
        ],
        compiler_params=pltpu.CompilerParams(
            dimension_semantics=("arbitrary",), vmem_limit_bytes=VMEM_LIMIT),
        name="mix_sample",
    )(relb, sinks, x, g, win, wout, ck, cv, sre0, sim0, lam, bd, cd, dskip, wglu, bglu)


def _s5_operators(log_dt, a_re, a_im, b_re, b_im, c_re, c_im):
    dt = jnp.exp(log_dt)[:, None]
    mag = jnp.exp(a_re * dt)
    lb_re = mag * jnp.cos(a_im * dt)
    lb_im = mag * jnp.sin(a_im * dt)
    den = a_re * a_re + a_im * a_im
    nr = lb_re - 1.0
    q_re = (nr * a_re + lb_im * a_im) / den
    q_im = (lb_im * a_re - nr * a_im) / den
    bb_re = q_re[..., None] * b_re - q_im[..., None] * b_im
    bb_im = q_re[..., None] * b_im + q_im[..., None] * b_re
    eye = jnp.eye(HALF_GROUPS, dtype=BF16)
    lam = jnp.stack([lb_re.reshape(2, HALF_STATE), lb_im.reshape(2, HALF_STATE)], axis=1)
    bb = jnp.stack([bb_re, bb_im]).astype(BF16).reshape(2, 2, HALF_GROUPS, SSM_STATE, SSM_GROUP)
    bd = jnp.einsum('rhgpc,gk->hgcrkp', bb, eye).reshape(2, HALF_U, 2 * HALF_STATE)
    cc = jnp.stack([c_re, -c_im]).astype(BF16).reshape(2, 2, HALF_GROUPS, SSM_GROUP, SSM_STATE)
    cd = jnp.einsum('rhgcp,gk->hrgpkc', cc, eye).reshape(2, 2 * HALF_STATE, HALF_U)
    return lam.reshape(4, HALF_STATE), bd, cd


def kernel(x_prompt, x_sample, cache_k, cache_v, state_ssm_re, state_ssm_im, rel_bias,
           ffn1_norm, ffn1_w_gate, ffn1_w_up, ffn1_w_down, mix_norm, w_in, sinks,
           log_dt, a_re, a_im, b_re, b_im, c_re, c_im, d_skip, w_glu, b_glu, w_out,
           ffn2_norm, ffn2_w_gate, ffn2_w_up, ffn2_w_down, final_norm):
    depth = w_in.shape[0]
    assert depth == 1
    batch, seq, _ = x_prompt.shape
    nseq, t_new, _ = x_sample.shape
    fg = final_norm.reshape(1, D_MODEL)

    l = 0
    ffn1 = (ffn1_norm[l].reshape(1, D_MODEL), ffn1_w_gate[l], ffn1_w_up[l], ffn1_w_down[l], fg)
    ffn2 = (ffn2_norm[l].reshape(1, D_MODEL), ffn2_w_gate[l], ffn2_w_up[l], ffn2_w_down[l], fg)
    lam, bd, cd = _s5_operators(log_dt[l], a_re[l], a_im[l], b_re[l], b_im[l], c_re[l], c_im[l])
    mix_w = (mix_norm[l].reshape(1, D_MODEL), w_in[l].astype(BF16), w_out[l].astype(BF16))
    ssm_w = (lam, bd, cd, d_skip[l].reshape(1, D_SSM), w_glu[l].astype(BF16),
             b_glu[l].reshape(1, D_SSM))
    sinks_l = sinks[l]

    def window_in(c):
        return jnp.transpose(c, (0, 2, 3, 1)).reshape(c.shape[0], D_KV, c.shape[1])

    def window_out(w):
        n, _, pos = w.shape
        return jnp.transpose(w.reshape(n, N_KV_HEADS, HEAD_DIM, pos), (0, 3, 1, 2))[None]

    def state_in(s):
        return jnp.transpose(s, (1, 2, 0)).reshape(2 * HALF_STATE, s.shape[0])

    def state_out(s):
        return jnp.transpose(s.reshape(N_SSM_GROUPS, SSM_STATE, s.shape[1]), (2, 0, 1))[None]

    relb_t = rel_bias.T
    xp = x_prompt.reshape(batch * seq, D_MODEL)
    xs = x_sample.reshape(nseq * t_new, D_MODEL)
    ys, yp = _ffn(xp, xs, *ffn1, final_norm=False)
    yp, kp, vp, sre_p, sim_p = _mix_prompt(
        yp.reshape(batch, seq, D_MODEL), relb_t, sinks_l, *mix_w, *ssm_w)
    ys, ks, vs, sre_s, sim_s = _mix_sample(
        ys, relb_t, sinks_l, *mix_w,
        window_in(cache_k[l]), window_in(cache_v[l]),
        state_in(state_ssm_re[l]), state_in(state_ssm_im[l]),
        *ssm_w, nseq=nseq, t_new=t_new)
    y_sample, y_prompt = _ffn(yp.reshape(batch * seq, D_MODEL), ys, *ffn2, final_norm=True)
    y_prompt = y_prompt.reshape(batch, seq, D_MODEL)
    y_sample = y_sample.reshape(nseq, t_new, D_MODEL)

    st_p = (1, batch, N_SSM_GROUPS, SSM_STATE)
    return (y_prompt, y_sample,
            window_out(kp), window_out(vp), sre_p.reshape(st_p), sim_p.reshape(st_p),
            window_out(ks), window_out(vs), state_out(sre_s), state_out(sim_s))
```

```python
import functools
import math

import jax
import jax.numpy as jnp
from jax import lax
from jax.experimental import pallas as pl
from jax.experimental.pallas import tpu as pltpu

F32 = jnp.float32
BF16 = jnp.bfloat16

D_MODEL = 1024
HEAD_DIM = 64
D_ATTN = 512
N_HEADS = 8
N_KV_HEADS = 2
N_REP = 4
D_KV = 128
D_SSM = 512
SSM_GROUP = 16
N_SSM_GROUPS = 32
SSM_STATE = 64
WINDOW = 128
NUM_BUCKETS = 32
MAX_DISTANCE = 128
D_FF = 2816
D_IN = D_ATTN + 2 * D_KV + D_SSM
RMS_EPS = 1e-6
NEG_INF = -1e30
LOG2E = math.log2(math.e)

LANES = 128
SUBLANES = 8
MXU_DIM = 256
FF_CHUNKS = D_FF // MXU_DIM
HALF_GROUPS = N_SSM_GROUPS // 2
HALF_STATE = HALF_GROUPS * SSM_STATE
HALF_U = HALF_GROUPS * SSM_GROUP
U_SLABS = D_SSM // LANES
PITCH = WINDOW + SUBLANES
SCAN_STEPS = 128
RING = 1
SAMPLE_BLOCKS = 4
VMEM_LIMIT = 60 * 1024 * 1024


def _const_spec(shape):
    nd = len(shape)
    return pl.BlockSpec(shape, lambda *_: (0,) * nd, pipeline_mode=pl.Buffered(1))


def _smem_spec():
    return pl.BlockSpec(memory_space=pltpu.SMEM)


def _rmsnorm(x, g):
    r = lax.rsqrt(jnp.mean(x * x, axis=-1, keepdims=True) + RMS_EPS)
    return (x * r) * g


def _dot(a, b):
    return jnp.dot(a, b, preferred_element_type=F32)


def _dot_nt(a, b):
    return lax.dot_general(a, b, (((1,), (1,)), ((), ())), preferred_element_type=F32)


def _transpose_blocks(a):
    r, c = a.shape
    return jnp.concatenate(
        [jnp.concatenate([a[i * LANES:(i + 1) * LANES, j * LANES:(j + 1) * LANES].T
                          for i in range(r // LANES)], axis=1)
         for j in range(c // LANES)], axis=0)


def _ffn_tile(x, g_ref, wg_ref, wu_ref, wd_ref, fg_ref, final_norm, before_chunk=None):
    h = _rmsnorm(x, g_ref[...]).astype(BF16)
    acc = None
    for c in range(FF_CHUNKS):
        if before_chunk is not None:
            before_chunk(c)
        sl = slice(c * MXU_DIM, (c + 1) * MXU_DIM)
        gate = _dot(h, wg_ref[:, sl].astype(BF16))
        up = _dot(h, wu_ref[:, sl].astype(BF16))
        a = (gate * jax.nn.sigmoid(gate) * up).astype(BF16)
        part = _dot(a, wd_ref[sl, :].astype(BF16))
        acc = part if acc is None else acc + part
    y = x + 0.5 * acc
    if final_norm:
        y = _rmsnorm(y, fg_ref[...])
    return y


def _ffn_kernel(xs_ref, xp_ref, g_ref, wg_hbm, wu_hbm, wd_hbm, fg_ref, os_ref, op_ref,
                wg_s, wu_s, wd_s, sems, *, final_norm):
    i = pl.program_id(0)
    weights = (g_ref, wg_s, wu_s, wd_s, fg_ref)

    def chunk_copies(c):
        sl = slice(c * MXU_DIM, (c + 1) * MXU_DIM)
        return (pltpu.make_async_copy(wg_hbm.at[:, sl], wg_s.at[:, sl], sems.at[0, c]),
                pltpu.make_async_copy(wu_hbm.at[:, sl], wu_s.at[:, sl], sems.at[1, c]),
                pltpu.make_async_copy(wd_hbm.at[sl, :], wd_s.at[sl, :], sems.at[2, c]))

    @pl.when(i == 0)
    def _():
        for c in range(FF_CHUNKS):
            for cp in chunk_copies(c):
                cp.start()

        def wait_chunk(c):
            for cp in chunk_copies(c):
                cp.wait()

        os_ref[...] = _ffn_tile(xs_ref[...], *weights, final_norm, wait_chunk)

    @pl.when(i > 0)
    def _():
        op_ref[...] = _ffn_tile(xp_ref[...], *weights, final_norm)


def _ffn(xp, xs, g, wg, wu, wd, fg, *, final_norm):
    n_p, n_s = xp.shape[0], xs.shape[0]
    tm = n_s
    prompt_steps = n_p // tm
    assert prompt_steps * tm == n_p
    prompt_block = lambda i: (jnp.maximum(i - 1, 0), 0)
    hbm = pl.BlockSpec(memory_space=pl.ANY)
    return pl.pallas_call(
        functools.partial(_ffn_kernel, final_norm=final_norm),
        grid=(prompt_steps + 1,),
        in_specs=[
            _const_spec((n_s, D_MODEL)),
            pl.BlockSpec((tm, D_MODEL), prompt_block),
            _const_spec((1, D_MODEL)),
            hbm, hbm, hbm,
            _const_spec((1, D_MODEL)),
        ],
        out_specs=[
            pl.BlockSpec((n_s, D_MODEL), lambda i: (0, 0)),
            pl.BlockSpec((tm, D_MODEL), prompt_block),
        ],
        out_shape=[
            jax.ShapeDtypeStruct((n_s, D_MODEL), F32),
            jax.ShapeDtypeStruct((n_p, D_MODEL), F32),
        ],
        scratch_shapes=[
            pltpu.VMEM((D_MODEL, D_FF), F32),
            pltpu.VMEM((D_MODEL, D_FF), F32),
            pltpu.VMEM((D_FF, D_MODEL), F32),
            pltpu.SemaphoreType.DMA((3, FF_CHUNKS)),
        ],
        compiler_params=pltpu.CompilerParams(
            dimension_semantics=("arbitrary",), vmem_limit_bytes=VMEM_LIMIT),
        name="ffn_final" if final_norm else "ffn",
    )(xs, xp, g, wg, wu, wd, fg)


def _t5_bucket(d):
    d = jnp.maximum(d, 0)
    max_exact = NUM_BUCKETS // 2
    df = jnp.maximum(d, 1).astype(F32)
    large = max_exact + (jnp.log(df / max_exact) / math.log(MAX_DISTANCE / max_exact)
                         * (NUM_BUCKETS - max_exact)).astype(jnp.int32)
    large = jnp.minimum(large, NUM_BUCKETS - 1)
    return jnp.where(d < max_exact, d, large)


def _masked_bias(relb_ref, head, d, valid, scale=1.0):
    bucket = _t5_bucket(d)
    b = jnp.zeros(d.shape, F32)
    for k in range(NUM_BUCKETS):
        b = jnp.where(bucket == k, relb_ref[head, k] * scale, b)
    return jnp.where(valid, b, NEG_INF)


def _kv_lane_mask(t, g):
    lane = lax.broadcasted_iota(jnp.int32, t.shape, 1)
    return jnp.where((lane >= g * HEAD_DIM) & (lane < (g + 1) * HEAD_DIM), t, jnp.zeros_like(t))


def _stack_queries(q_chunks, g, scale=HEAD_DIM ** -0.5):
    qs = []
    for r in range(N_REP):
        qc = q_chunks[2 * g + r // 2]
        if r % 2 != g:
            qc = pltpu.roll(qc, HEAD_DIM, 1)
        qs.append(qc)
    return (jnp.concatenate(qs, axis=0) * scale).astype(BF16)


def _sink_column(sinks_ref, g, rows):
    return jnp.concatenate(
        [jnp.full((rows, 1), sinks_ref[N_REP * g + r], F32) for r in range(N_REP)], axis=0)


def _unstack_heads(o_by_g, rows):
    lane = lax.broadcasted_iota(jnp.int32, (rows, LANES), 1)
    chunks = []
    for c in range(D_ATTN // LANES):
        g = c // 2
        halves = []
        for half in range(2):
            r = 2 * (c % 2) + half
            piece = o_by_g[g][r * rows:(r + 1) * rows]
            if half != g:
                piece = pltpu.roll(piece, HEAD_DIM, 1)
            halves.append(piece)
        chunks.append(jnp.where(lane < HEAD_DIM, halves[0], halves[1]))
    return chunks


def _glu_tail(y, u, dskip_ref, wglu_ref, bglu_ref):
    y = y + dskip_ref[...] * u
    y = jax.nn.gelu(y)
    z = _dot(y.astype(BF16), wglu_ref[...]) + bglu_ref[...]
    return y * jax.nn.sigmoid(z)


def _mixp_kernel(relb_ref, sinks_ref, x_ref, g_ref, win_ref, wout_ref, lam_ref, bd_ref, cd_ref,
                 dskip_ref, wglu_ref, bglu_ref,
                 o_ref, kp_ref, vp_ref, sre_ref, sim_ref,
                 p_s, kband, vband_t, bias_t, bias_later, us, ys, state, attn_s, *ring_refs,
                 nb, blk):
    i = pl.program_id(0)
    rows = nb * blk
    rings, u_rings = ring_refs[:RING], ring_refs[RING:]

    @pl.when(i == 0)
    def _():
        kband[...] = jnp.zeros(kband.shape, BF16)
        vband_t[...] = jnp.zeros(vband_t.shape, BF16)
        state[...] = jnp.zeros(state.shape, F32)

    @pl.when(i == 0)
    def _():
        kj = lax.broadcasted_iota(jnp.int32, (2 * blk, blk), 0)
        qi = lax.broadcasted_iota(jnp.int32, (2 * blk, blk), 1)
        d = qi - kj + blk
        valid = (d >= 0) & (d < WINDOW)
        for h in range(N_HEADS):
            cols = slice((h % N_REP) * blk, (h % N_REP + 1) * blk)
            tile = _masked_bias(relb_ref, h, d, valid, scale=LOG2E)
            bias_later[h // N_REP, :, cols] = tile
            bias_t[h // N_REP, :, cols] = jnp.where(kj >= blk, tile, NEG_INF)

    @pl.when(i == 1)
    def _():
        bias_t[...] = bias_later[...]

    h = _rmsnorm(x_ref[...].reshape(rows, D_MODEL), g_ref[...]).astype(BF16)
    u_off = D_ATTN + 2 * D_KV
    p = _dot(h, win_ref[...])
    p_s[...] = p[:, 0:u_off]
    for s in range(U_SLABS):
        for b in range(nb):
            us[s, b * PITCH:b * PITCH + blk, :] = (
                p[b * blk:(b + 1) * blk, u_off + s * LANES:u_off + (s + 1) * LANES])

    kband[:, blk:2 * blk, :] = (
        p_s[:, D_ATTN:D_ATTN + D_KV].reshape(nb, blk, D_KV).astype(BF16))

    def attention_all():
        for b in range(nb):
            vband_t[b, :, blk:2 * blk] = (
                p_s[b * blk:(b + 1) * blk, D_ATTN + D_KV:D_ATTN + 2 * D_KV].T.astype(BF16))
        kb = kband[...]
        vt = vband_t[...]
        feat = lax.broadcasted_iota(jnp.int32, vt.shape, 1)
        lane = lax.broadcasted_iota(jnp.int32, kb.shape, 2)
        q_chunks = [p_s[:, c * LANES:(c + 1) * LANES].reshape(nb, blk, LANES)
                    for c in range(D_ATTN // LANES)]
        o_by_g = []
        for g in range(N_KV_HEADS):
            qs = []
            for r in range(N_REP):
                qc = q_chunks[2 * g + r // 2]
                if r % 2 != g:
                    qc = pltpu.roll(qc, HEAD_DIM, 2)
                qs.append(qc)
            q = (jnp.concatenate(qs, axis=1) * (LOG2E * HEAD_DIM ** -0.5)).astype(BF16)
            kg = jnp.where((lane >= g * HEAD_DIM) & (lane < (g + 1) * HEAD_DIM), kb,
                           jnp.zeros_like(kb))
            st = lax.dot_general(kg, q, (((2,), (2,)), ((0,), (0,))),
                                 preferred_element_type=F32) + bias_t[g]
            s = jnp.concatenate(
                [jnp.full((1, 1, blk), sinks_ref[N_REP * g + r] * LOG2E, F32)
                 for r in range(N_REP)], axis=2)
            m = jnp.maximum(jnp.max(st, axis=1, keepdims=True), s)
            e = jnp.exp2(st - m).astype(BF16)
            in_g = (feat >= g * HEAD_DIM) & (feat < (g + 1) * HEAD_DIM)
            ot = lax.dot_general(jnp.where(in_g, vt, jnp.ones_like(vt)), e,
                                 (((2,), (1,)), ((0,), (0,))),
                                 preferred_element_type=F32)
            other = (1 - g) * HEAD_DIM
            denom = ot[:, other:other + 1, :] + jnp.exp2(s - m)
            o_by_g.append(ot[:, g * HEAD_DIM:(g + 1) * HEAD_DIM, :] / denom)
        for b in range(nb):
            for c in range(D_ATTN // LANES):
                og = o_by_g[c // 2][b]
                cc = 2 * (c % 2)
                chunk_t = jnp.concatenate(
                    [og[:, cc * blk:(cc + 1) * blk], og[:, (cc + 1) * blk:(cc + 2) * blk]], axis=0)
                attn_s[b * blk:(b + 1) * blk, c * LANES:(c + 1) * LANES] = chunk_t.T

    def scan_gather():
        steps = []
        for t in range(blk):
            steps.append(jnp.concatenate(
                [us[s, pl.ds(t, nb, stride=PITCH), :] for s in range(U_SLABS)], axis=1))
        u_block = jnp.concatenate(steps, axis=0)
        u_rings[0][...] = u_block
        return u_block.astype(BF16)

    def scan_half(ub, hf):
        ring = rings[0]
        re = slice(2 * hf * HALF_STATE, (2 * hf + 1) * HALF_STATE)
        im = slice((2 * hf + 1) * HALF_STATE, (2 * hf + 2) * HALF_STATE)
        both = slice(2 * hf * HALF_STATE, 2 * (hf + 1) * HALF_STATE)
        lr = jnp.broadcast_to(lam_ref[2 * hf:2 * hf + 1, :], (nb, HALF_STATE))
        li = jnp.broadcast_to(lam_ref[2 * hf + 1:2 * hf + 2, :], (nb, HALF_STATE))
        xr = state[:, re]
        xi = state[:, im]
        for t in range(blk):
            rows_t = slice(t * nb, (t + 1) * nb)
            xr, xi = (lr * xr - li * xi + ring[rows_t, re],
                      lr * xi + li * xr + ring[rows_t, im])
            ring[rows_t, re] = xr
            ring[rows_t, im] = xi
        state[:, re] = xr
        state[:, im] = xi
        return _dot(ring[:, both].astype(BF16), cd_ref[hf])

    attention_all()
    ub = scan_gather()
    for hf in range(2):
        rings[0][:, 2 * hf * HALF_STATE:2 * (hf + 1) * HALF_STATE] = _dot(
            ub[:, hf * HALF_U:(hf + 1) * HALF_U], bd_ref[hf])
    y = jnp.concatenate([scan_half(ub, hf) for hf in range(2)], axis=1)
    ssm_t = _glu_tail(y, u_rings[0][...], dskip_ref, wglu_ref, bglu_ref)
    for t in range(blk):
        for s in range(U_SLABS):
            ys[s, pl.ds(t, nb, stride=PITCH), :] = ssm_t[t * nb:(t + 1) * nb, s * LANES:(s + 1) * LANES]

    kband[:, 0:blk, :] = kband[:, blk:2 * blk, :]
    vband_t[:, :, 0:blk] = vband_t[:, :, blk:2 * blk]
    for hf in range(2):
        sre_ref[:, hf * HALF_STATE:(hf + 1) * HALF_STATE] = (
            state[:, 2 * hf * HALF_STATE:(2 * hf + 1) * HALF_STATE])
        sim_ref[:, hf * HALF_STATE:(hf + 1) * HALF_STATE] = (
            state[:, (2 * hf + 1) * HALF_STATE:(2 * hf + 2) * HALF_STATE])

    ssm = jnp.concatenate(
        [jnp.concatenate([ys[s, b * PITCH:b * PITCH + blk, :] for b in range(nb)], axis=0)
         for s in range(U_SLABS)], axis=1)
    mix = (_dot(attn_s[...].astype(BF16), wout_ref[0:D_ATTN, :])
           + _dot(ssm.astype(BF16), wout_ref[D_ATTN:D_ATTN + D_SSM, :]))
    o_ref[...] = x_ref[...] + mix.reshape(nb, blk, D_MODEL)

    @pl.when(i == pl.num_programs(0) - 1)
    def _():
        for b in range(nb):
            rows_b = slice(b * blk, (b + 1) * blk)
            kp_ref[b] = p_s[rows_b, D_ATTN:D_ATTN + D_KV].T
            vp_ref[b] = p_s[rows_b, D_ATTN + D_KV:D_ATTN + 2 * D_KV].T


def _mix_prompt(x, relb, sinks, g, win, wout, lam, bd, cd, dskip, wglu, bglu):
    nb, seq, _ = x.shape
    blk = WINDOW
    rows = nb * blk
    const2 = _const_spec
    return pl.pallas_call(
        functools.partial(_mixp_kernel, nb=nb, blk=blk),
        grid=(seq // blk,),
        in_specs=[
            _smem_spec(), _smem_spec(),
            pl.BlockSpec((nb, blk, D_MODEL), lambda i: (0, i, 0)),
            const2((1, D_MODEL)),
            const2((D_MODEL, D_IN)),
            const2((D_ATTN + D_SSM, D_MODEL)),
            const2((4, HALF_STATE)),
            const2((2, HALF_U, 2 * HALF_STATE)),
            const2((2, 2 * HALF_STATE, HALF_U)),
            const2((1, D_SSM)),
            const2((D_SSM, D_SSM)),
            const2((1, D_SSM)),
        ],
        out_specs=[
            pl.BlockSpec((nb, blk, D_MODEL), lambda i: (0, i, 0)),
            pl.BlockSpec((nb, blk, D_KV), lambda i: (0, 0, 0)),
            pl.BlockSpec((nb, blk, D_KV), lambda i: (0, 0, 0)),
            pl.BlockSpec((nb, 2 * HALF_STATE), lambda i: (0, 0)),
            pl.BlockSpec((nb, 2 * HALF_STATE), lambda i: (0, 0)),
        ],
        out_shape=[
            jax.ShapeDtypeStruct((nb, seq, D_MODEL), F32),
            jax.ShapeDtypeStruct((nb, blk, D_KV), F32),
            jax.ShapeDtypeStruct((nb, blk, D_KV), F32),
            jax.ShapeDtypeStruct((nb, 2 * HALF_STATE), F32),
            jax.ShapeDtypeStruct((nb, 2 * HALF_STATE), F32),
        ],
        scratch_shapes=[
            pltpu.VMEM((rows, D_ATTN + 2 * D_KV), F32),
            pltpu.VMEM((nb, 2 * blk, D_KV), BF16),
            pltpu.VMEM((nb, D_KV, 2 * blk), BF16),
            pltpu.VMEM((N_KV_HEADS, 2 * blk, N_REP * blk), F32),
            pltpu.VMEM((N_KV_HEADS, 2 * blk, N_REP * blk), F32),
            pltpu.VMEM((U_SLABS, nb * PITCH, LANES), F32),
            pltpu.VMEM((U_SLABS, nb * PITCH, LANES), F32),
            pltpu.VMEM((nb, 4 * HALF_STATE), F32),
            pltpu.VMEM((rows, D_ATTN), F32),
        ] + [pltpu.VMEM((SCAN_STEPS * nb, 4 * HALF_STATE), F32)] * RING
          + [pltpu.VMEM((SCAN_STEPS * nb, D_SSM), F32)] * RING,
        compiler_params=pltpu.CompilerParams(
            dimension_semantics=("arbitrary",), vmem_limit_bytes=VMEM_LIMIT),
        name="mix_prompt",
    )(relb, sinks, x, g, win, wout, lam, bd, cd, dskip, wglu, bglu)


def _mixs_kernel(relb_ref, sinks_ref, x_ref, g_ref, win_ref, wout_ref, ck_ref, cv_ref,
                 sre0_ref, sim0_ref, lam_ref, bd_ref, cd_ref, dskip_ref, wglu_ref, bglu_ref,
                 o_ref, ks_ref, vs_ref, sre_ref, sim_ref,
                 p_s, attn_s, ssm_s, bias_c, bias_n, us,
                 *, nseq, t_new, sb, wbuf):
    i = pl.program_id(0)
    nsteps = pl.num_programs(0)
    rows = nseq * t_new
    brow = sb * t_new
    u_off = D_ATTN + 2 * D_KV
    tshift = t_new.bit_length() - 1
    wshift = wbuf.bit_length() - 1

    @pl.when(i == 0)
    def _():
        x = x_ref[...]
        h = _rmsnorm(x, g_ref[...]).astype(BF16)
        p_s[...] = _dot(h, win_ref[...])

        for s in range(U_SLABS):
            us[s] = p_s[:, u_off + s * LANES:u_off + (s + 1) * LANES]
        uperm = jnp.concatenate(
            [jnp.concatenate([us[s, pl.ds(t, nseq, stride=t_new), :] for t in range(t_new)], axis=0)
             for s in range(U_SLABS)], axis=1)
        yparts = []
        for hf in range(2):
            bu = _dot(uperm[:, hf * HALF_U:(hf + 1) * HALF_U].astype(BF16), bd_ref[hf])
            lr = lam_ref[2 * hf:2 * hf + 1, :]
            li = lam_ref[2 * hf + 1:2 * hf + 2, :]
            xr = _transpose_blocks(sre0_ref[hf * HALF_STATE:(hf + 1) * HALF_STATE, :])
            xi = _transpose_blocks(sim0_ref[hf * HALF_STATE:(hf + 1) * HALF_STATE, :])
            states = []
            for t in range(t_new):
                br = bu[t * nseq:(t + 1) * nseq, 0:HALF_STATE]
                bi = bu[t * nseq:(t + 1) * nseq, HALF_STATE:2 * HALF_STATE]
                xr, xi = lr * xr - li * xi + br, lr * xi + li * xr + bi
                states.append(jnp.concatenate([xr, xi], axis=1).astype(BF16))
            sre_ref[hf * HALF_STATE:(hf + 1) * HALF_STATE, :] = _transpose_blocks(xr)
            sim_ref[hf * HALF_STATE:(hf + 1) * HALF_STATE, :] = _transpose_blocks(xi)
            yparts.append(_dot(jnp.concatenate(states, axis=0), cd_ref[hf]))
        yperm = jnp.concatenate(yparts, axis=1)
        for s in range(U_SLABS):
            for t in range(t_new):
                us[s, pl.ds(t, nseq, stride=t_new), :] = (
                    yperm[t * nseq:(t + 1) * nseq, s * LANES:(s + 1) * LANES])
        y = jnp.concatenate([us[s] for s in range(U_SLABS)], axis=1)
        ssm_s[...] = _glu_tail(y, p_s[:, u_off:u_off + D_SSM], dskip_ref, wglu_ref, bglu_ref)

        ncol = sb * wbuf
        rho = lax.broadcasted_iota(jnp.int32, (brow, ncol), 0)
        kap = lax.broadcasted_iota(jnp.int32, (brow, ncol), 1)
        same_seq = (rho >> tshift) == (kap >> wshift)
        rho_w = lax.broadcasted_iota(jnp.int32, (brow, wbuf), 0)
        d_c = (rho_w & (t_new - 1)) - lax.broadcasted_iota(jnp.int32, (brow, wbuf), 1) + wbuf
        valid_c = (d_c >= 0) & (d_c < WINDOW)
        rho_n = lax.broadcasted_iota(jnp.int32, (brow, LANES), 0)
        kap_n = lax.broadcasted_iota(jnp.int32, (brow, LANES), 1)
        tq_n = rho_n & (t_new - 1)
        d_n = tq_n - (kap_n & (t_new - 1))
        valid_n = ((rho_n >> tshift) == (kap_n >> tshift)) & (d_n >= 0) & (kap_n < brow)
        for hd in range(N_HEADS):
            g, r = hd // N_REP, hd % N_REP
            tile = _masked_bias(relb_ref, hd, d_c, valid_c)
            bias_c[g, r * brow:(r + 1) * brow, :] = jnp.where(
                same_seq, jnp.concatenate([tile] * sb, axis=1), NEG_INF)
            bias_n[g, r * brow:(r + 1) * brow, :] = _masked_bias(relb_ref, hd, d_n, valid_n)

    blocks = range(SAMPLE_BLOCKS)
    row0 = [pl.multiple_of((i * SAMPLE_BLOCKS + k) * brow, brow) for k in blocks]
    pad = jnp.zeros((LANES - brow, D_KV), F32)
    kn_pad = [jnp.concatenate([p_s[pl.ds(row0[k], brow), D_ATTN:D_ATTN + D_KV], pad], axis=0)
              for k in blocks]
    vn_pad = [jnp.concatenate(
        [p_s[pl.ds(row0[k], brow), D_ATTN + D_KV:D_ATTN + 2 * D_KV], pad], axis=0) for k in blocks]
    kc_t = jnp.stack([jnp.concatenate([ck_ref[k * sb + b] for b in range(sb)], axis=1)
                      for k in blocks]).astype(BF16)
    vc_t = jnp.stack([jnp.concatenate([cv_ref[k * sb + b] for b in range(sb)], axis=1)
                      for k in blocks]).astype(BF16)
    kn_b = jnp.stack(kn_pad).astype(BF16)
    vn_b = jnp.stack(vn_pad).astype(BF16)
    q_chunks = [[p_s[pl.ds(row0[k], brow), c * LANES:(c + 1) * LANES]
                 for c in range(D_ATTN // LANES)] for k in blocks]
    batch = ((0,), (0,))
    o_by_g = []
    for g in range(N_KV_HEADS):
        q = jnp.stack([_kv_lane_mask(_stack_queries(q_chunks[k], g), g) for k in blocks])
        lc = lax.dot_general(q, kc_t, (((2,), (1,)), batch), preferred_element_type=F32) + bias_c[g]
        ln = lax.dot_general(q, kn_b, (((2,), (2,)), batch), preferred_element_type=F32) + bias_n[g]
        s = _sink_column(sinks_ref, g, brow)
        m = jnp.maximum(jnp.maximum(jnp.max(lc, axis=-1, keepdims=True),
                                    jnp.max(ln, axis=-1, keepdims=True)), s)
        ec = jnp.exp(lc - m)
        en = jnp.exp(ln - m)
        denom = (jnp.sum(ec, axis=-1, keepdims=True) + jnp.sum(en, axis=-1, keepdims=True)
                 + jnp.exp(s - m))
        o = (lax.dot_general(ec.astype(BF16), vc_t, (((2,), (2,)), batch),
                             preferred_element_type=F32)
             + lax.dot_general(en.astype(BF16), vn_b, (((2,), (1,)), batch),
                               preferred_element_type=F32))
        o_by_g.append(o / denom)
    for k in blocks:
        for c, chunk in enumerate(_unstack_heads([o[k] for o in o_by_g], brow)):
            attn_s[pl.ds(row0[k], brow), c * LANES:(c + 1) * LANES] = chunk

    lane = lax.broadcasted_iota(jnp.int32, (D_KV, wbuf), 1)
    keep = wbuf - t_new
    for new_pads, old_ref, out_ref in ((kn_pad, ck_ref, ks_ref), (vn_pad, cv_ref, vs_ref)):
        for k in blocks:
            new_t = new_pads[k].T
            for b in range(sb):
                shifted = pltpu.roll(old_ref[k * sb + b], keep, 1)
                appended = pltpu.roll(new_t, (keep - b * t_new) % wbuf, 1)
                out_ref[k * sb + b] = jnp.where(lane >= keep, appended, shifted)

    @pl.when(i == nsteps - 1)
    def _():
        mix = (_dot(attn_s[...].astype(BF16), wout_ref[0:D_ATTN, :])
               + _dot(ssm_s[...].astype(BF16), wout_ref[D_ATTN:D_ATTN + D_SSM, :]))
        o_ref[...] = x_ref[...] + mix


def _mix_sample(x, relb, sinks, g, win, wout, ck, cv, sre0, sim0, lam, bd, cd, dskip, wglu, bglu,
                *, nseq, t_new):
    rows = nseq * t_new
    wbuf = ck.shape[2]
    assert wbuf == LANES
    sb = SUBLANES
    brow = sb * t_new
    return pl.pallas_call(
        functools.partial(_mixs_kernel, nseq=nseq, t_new=t_new, sb=sb, wbuf=wbuf),
        grid=(nseq // (SAMPLE_BLOCKS * sb),),
        in_specs=[
            _smem_spec(), _smem_spec(),
            _const_spec((rows, D_MODEL)),
            _const_spec((1, D_MODEL)),
            _const_spec((D_MODEL, D_IN)),
            _const_spec((D_ATTN + D_SSM, D_MODEL)),
            pl.BlockSpec((SAMPLE_BLOCKS * sb, D_KV, wbuf), lambda i: (i, 0, 0)),
            pl.BlockSpec((SAMPLE_BLOCKS * sb, D_KV, wbuf), lambda i: (i, 0, 0)),
            _const_spec((2 * HALF_STATE, nseq)),
            _const_spec((2 * HALF_STATE, nseq)),
            _const_spec((4, HALF_STATE)),
            _const_spec((2, HALF_U, 2 * HALF_STATE)),
            _const_spec((2, 2 * HALF_STATE, HALF_U)),
            _const_spec((1, D_SSM)),
            _const_spec((D_SSM, D_SSM)),
            _const_spec((1, D_SSM)),
        ],
        out_specs=[
            pl.BlockSpec((rows, D_MODEL), lambda i: (0, 0)),
            pl.BlockSpec((SAMPLE_BLOCKS * sb, D_KV, wbuf), lambda i: (i, 0, 0)),
            pl.BlockSpec((SAMPLE_BLOCKS * sb, D_KV, wbuf), lambda i: (i, 0, 0)),
            pl.BlockSpec((2 * HALF_STATE, nseq), lambda i: (0, 0)),
            pl.BlockSpec((2 * HALF_STATE, nseq), lambda i: (0, 0)),
        ],
        out_shape=[
            jax.ShapeDtypeStruct((rows, D_MODEL), F32),
            jax.ShapeDtypeStruct((nseq, D_KV, wbuf), F32),
            jax.ShapeDtypeStruct((nseq, D_KV, wbuf), F32),
            jax.ShapeDtypeStruct((2 * HALF_STATE, nseq), F32),
            jax.ShapeDtypeStruct((2 * HALF_STATE, nseq), F32),
        ],
        scratch_shapes=[
            pltpu.VMEM((rows, D_IN), F32),
            pltpu.VMEM((rows, D_ATTN), F32),
            pltpu.VMEM((rows, D_SSM), F32),
            pltpu.VMEM((N_KV_HEADS, N_REP * brow, sb * wbuf), F32),
            pltpu.VMEM((N_KV_HEADS, N_REP * brow, LANES), F32),
            pltpu.VMEM((U_SLABS, rows, LANES), F32),
        ],
        compiler_params=pltpu.CompilerParams(
            dimension_semantics=("arbitrary",), vmem_limit_bytes=VMEM_LIMIT),
        name="mix_sample",
    )(relb, sinks, x, g, win, wout, ck, cv, sre0, sim0, lam, bd, cd, dskip, wglu, bglu)


def _s5_operators(log_dt, a_re, a_im, b_re, b_im, c_re, c_im):
    dt = jnp.exp(log_dt)[:, None]
    mag = jnp.exp(a_re * dt)
    lb_re = mag * jnp.cos(a_im * dt)
    lb_im = mag * jnp.sin(a_im * dt)
    den = a_re * a_re + a_im * a_im
    nr = lb_re - 1.0
    q_re = (nr * a_re + lb_im * a_im) / den
    q_im = (lb_im * a_re - nr * a_im) / den
    bb_re = q_re[..., None] * b_re - q_im[..., None] * b_im
    bb_im = q_re[..., None] * b_im + q_im[..., None] * b_re
    eye = jnp.eye(HALF_GROUPS, dtype=BF16)
    lam = jnp.stack([lb_re.reshape(2, HALF_STATE), lb_im.reshape(2, HALF_STATE)], axis=1)
    bb = jnp.stack([bb_re, bb_im]).astype(BF16).reshape(2, 2, HALF_GROUPS, SSM_STATE, SSM_GROUP)
    bd = jnp.einsum('rhgpc,gk->hgcrkp', bb, eye).reshape(2, HALF_U, 2 * HALF_STATE)
    cc = jnp.stack([c_re, -c_im]).astype(BF16).reshape(2, 2, HALF_GROUPS, SSM_GROUP, SSM_STATE)
    cd = jnp.einsum('rhgcp,gk->hrgpkc', cc, eye).reshape(2, 2 * HALF_STATE, HALF_U)
    return lam.reshape(4, HALF_STATE), bd, cd


def kernel(x_prompt, x_sample, cache_k, cache_v, state_ssm_re, state_ssm_im, rel_bias,
           ffn1_norm, ffn1_w_gate, ffn1_w_up, ffn1_w_down, mix_norm, w_in, sinks,
           log_dt, a_re, a_im, b_re, b_im, c_re, c_im, d_skip, w_glu, b_glu, w_out,
           ffn2_norm, ffn2_w_gate, ffn2_w_up, ffn2_w_down, final_norm):
    depth = w_in.shape[0]
    assert depth == 1
    batch, seq, _ = x_prompt.shape
    nseq, t_new, _ = x_sample.shape
    fg = final_norm.reshape(1, D_MODEL)

    l = 0
    ffn1 = (ffn1_norm[l].reshape(1, D_MODEL), ffn1_w_gate[l], ffn1_w_up[l], ffn1_w_down[l], fg)
    ffn2 = (ffn2_norm[l].reshape(1, D_MODEL), ffn2_w_gate[l], ffn2_w_up[l], ffn2_w_down[l], fg)
    lam, bd, cd = _s5_operators(log_dt[l], a_re[l], a_im[l], b_re[l], b_im[l], c_re[l], c_im[l])
    mix_w = (mix_norm[l].reshape(1, D_MODEL), w_in[l].astype(BF16), w_out[l].astype(BF16))
    ssm_w = (lam, bd, cd, d_skip[l].reshape(1, D_SSM), w_glu[l].astype(BF16),
             b_glu[l].reshape(1, D_SSM))
    sinks_l = sinks[l]

    def window_in(c):
        return jnp.transpose(c, (0, 2, 3, 1)).reshape(c.shape[0], D_KV, c.shape[1])

    def window_out(w):
        n, _, pos = w.shape
        return jnp.transpose(w.reshape(n, N_KV_HEADS, HEAD_DIM, pos), (0, 3, 1, 2))[None]

    def state_in(s):
        return jnp.transpose(s, (1, 2, 0)).reshape(2 * HALF_STATE, s.shape[0])

    def state_out(s):
        return jnp.transpose(s.reshape(N_SSM_GROUPS, SSM_STATE, s.shape[1]), (2, 0, 1))[None]

    relb_t = rel_bias.T
    xp = x_prompt.reshape(batch * seq, D_MODEL)
    xs = x_sample.reshape(nseq * t_new, D_MODEL)
    ys, yp = _ffn(xp, xs, *ffn1, final_norm=False)
    yp, kp, vp, sre_p, sim_p = _mix_prompt(
        yp.reshape(batch, seq, D_MODEL), relb_t, sinks_l, *mix_w, *ssm_w)
    ys, ks, vs, sre_s, sim_s = _mix_sample(
        ys, relb_t, sinks_l, *mix_w,
        window_in(cache_k[l]), window_in(cache_v[l]),
        state_in(state_ssm_re[l]), state_in(state_ssm_im[l]),
        *ssm_w, nseq=nseq, t_new=t_new)
    y_sample, y_prompt = _ffn(yp.reshape(batch * seq, D_MODEL), ys, *ffn2, final_norm=True)
    y_prompt = y_prompt.reshape(batch, seq, D_MODEL)
    y_sample = y_sample.reshape(nseq, t_new, D_MODEL)

    st_p = (1, batch, N_SSM_GROUPS, SSM_STATE)
    return (y_prompt, y_sample,
            window_out(kp), window_out(vp), sre_p.reshape(st_p), sim_p.reshape(st_p),
            window_out(ks), window_out(vs), state_out(sre_s), state_out(sim_s))
```

```python
import functools
import math

import jax
import jax.numpy as jnp
from jax import lax
from jax.experimental import pallas as pl
from jax.experimental.pallas import tpu as pltpu

F32 = jnp.float32
BF16 = jnp.bfloat16

D_MODEL = 1024
HEAD_DIM = 64
D_ATTN = 512
N_HEADS = 8
N_KV_HEADS = 2
N_REP = 4
D_KV = 128
D_SSM = 512
SSM_GROUP = 16
N_SSM_GROUPS = 32
SSM_STATE = 64
WINDOW = 128
NUM_BUCKETS = 32
MAX_DISTANCE = 128
D_FF = 2816
D_IN = D_ATTN + 2 * D_KV + D_SSM
RMS_EPS = 1e-6
NEG_INF = -1e30
LOG2E = math.log2(math.e)

LANES = 128
SUBLANES = 8
MXU_DIM = 256
FF_CHUNKS = D_FF // MXU_DIM
HALF_GROUPS = N_SSM_GROUPS // 2
HALF_STATE = HALF_GROUPS * SSM_STATE
HALF_U = HALF_GROUPS * SSM_GROUP
U_SLABS = D_SSM // LANES
PITCH = WINDOW + SUBLANES
SAMPLE_BLOCKS = 4
VMEM_LIMIT = 60 * 1024 * 1024


def _const_spec(shape):
    nd = len(shape)
    return pl.BlockSpec(shape, lambda *_: (0,) * nd, pipeline_mode=pl.Buffered(1))


def _smem_spec():
    return pl.BlockSpec(memory_space=pltpu.SMEM)


def _rmsnorm(x, g):
    r = lax.rsqrt(jnp.mean(x * x, axis=-1, keepdims=True) + RMS_EPS)
    return (x * r) * g


def _dot(a, b):
    return jnp.dot(a, b, preferred_element_type=F32)


def _dot_nt(a, b):
    return lax.dot_general(a, b, (((1,), (1,)), ((), ())), preferred_element_type=F32)


def _transpose_blocks(a):
    r, c = a.shape
    return jnp.concatenate(
        [jnp.concatenate([a[i * LANES:(i + 1) * LANES, j * LANES:(j + 1) * LANES].T
                          for i in range(r // LANES)], axis=1)
         for j in range(c // LANES)], axis=0)


def _ffn_tile(x, g_ref, wg_ref, wu_ref, wd_ref, fg_ref, final_norm, before_chunk=None):
    h = _rmsnorm(x, g_ref[...]).astype(BF16)
    acc = None
    for c in range(FF_CHUNKS):
        if before_chunk is not None:
            before_chunk(c)
        sl = slice(c * MXU_DIM, (c + 1) * MXU_DIM)
        gate = _dot(h, wg_ref[:, sl].astype(BF16))
        up = _dot(h, wu_ref[:, sl].astype(BF16))
        a = (gate * jax.nn.sigmoid(gate) * up).astype(BF16)
        part = _dot(a, wd_ref[sl, :].astype(BF16))
        acc = part if acc is None else acc + part
    y = x + 0.5 * acc
    if final_norm:
        y = _rmsnorm(y, fg_ref[...])
    return y


def _ffn_kernel(xs_ref, xp_ref, g_ref, wg_hbm, wu_hbm, wd_hbm, fg_ref, os_ref, op_ref,
                wg_s, wu_s, wd_s, sems, *, final_norm):
    i = pl.program_id(0)
    weights = (g_ref, wg_s, wu_s, wd_s, fg_ref)

    def chunk_copies(c):
        sl = slice(c * MXU_DIM, (c + 1) * MXU_DIM)
        return (pltpu.make_async_copy(wg_hbm.at[:, sl], wg_s.at[:, sl], sems.at[0, c]),
                pltpu.make_async_copy(wu_hbm.at[:, sl], wu_s.at[:, sl], sems.at[1, c]),
                pltpu.make_async_copy(wd_hbm.at[sl, :], wd_s.at[sl, :], sems.at[2, c]))

    @pl.when(i == 0)
    def _():
        for c in range(FF_CHUNKS):
            for cp in chunk_copies(c):
                cp.start()

        def wait_chunk(c):
            for cp in chunk_copies(c):
                cp.wait()

        os_ref[...] = _ffn_tile(xs_ref[...], *weights, final_norm, wait_chunk)

    @pl.when(i > 0)
    def _():
        op_ref[...] = _ffn_tile(xp_ref[...], *weights, final_norm)


def _ffn(xp, xs, g, wg, wu, wd, fg, *, final_norm):
    n_p, n_s = xp.shape[0], xs.shape[0]
    tm = n_s
    prompt_steps = n_p // tm
    assert prompt_steps * tm == n_p
    prompt_block = lambda i: (jnp.maximum(i - 1, 0), 0)
    hbm = pl.BlockSpec(memory_space=pl.ANY)
    return pl.pallas_call(
        functools.partial(_ffn_kernel, final_norm=final_norm),
        grid=(prompt_steps + 1,),
        in_specs=[
            _const_spec((n_s, D_MODEL)),
            pl.BlockSpec((tm, D_MODEL), prompt_block),
            _const_spec((1, D_MODEL)),
            hbm, hbm, hbm,
            _const_spec((1, D_MODEL)),
        ],
        out_specs=[
            pl.BlockSpec((n_s, D_MODEL), lambda i: (0, 0)),
            pl.BlockSpec((tm, D_MODEL), prompt_block),
        ],
        out_shape=[
            jax.ShapeDtypeStruct((n_s, D_MODEL), F32),
            jax.ShapeDtypeStruct((n_p, D_MODEL), F32),
        ],
        scratch_shapes=[
            pltpu.VMEM((D_MODEL, D_FF), F32),
            pltpu.VMEM((D_MODEL, D_FF), F32),
            pltpu.VMEM((D_FF, D_MODEL), F32),
            pltpu.SemaphoreType.DMA((3, FF_CHUNKS)),
        ],
        compiler_params=pltpu.CompilerParams(
            dimension_semantics=("arbitrary",), vmem_limit_bytes=VMEM_LIMIT),
        name="ffn_final" if final_norm else "ffn",
    )(xs, xp, g, wg, wu, wd, fg)


def _t5_bucket(d):
    d = jnp.maximum(d, 0)
    max_exact = NUM_BUCKETS // 2
    df = jnp.maximum(d, 1).astype(F32)
    large = max_exact + (jnp.log(df / max_exact) / math.log(MAX_DISTANCE / max_exact)
                         * (NUM_BUCKETS - max_exact)).astype(jnp.int32)
    large = jnp.minimum(large, NUM_BUCKETS - 1)
    return jnp.where(d < max_exact, d, large)


def _masked_bias(relb_ref, head, d, valid, scale=1.0):
    bucket = _t5_bucket(d)
    b = jnp.zeros(d.shape, F32)
    for k in range(NUM_BUCKETS):
        b = jnp.where(bucket == k, relb_ref[head, k] * scale, b)
    return jnp.where(valid, b, NEG_INF)


def _kv_lane_mask(t, g):
    lane = lax.broadcasted_iota(jnp.int32, t.shape, 1)
    return jnp.where((lane >= g * HEAD_DIM) & (lane < (g + 1) * HEAD_DIM), t, jnp.zeros_like(t))


def _stack_queries(q_chunks, g, scale=HEAD_DIM ** -0.5):
    qs = []
    for r in range(N_REP):
        qc = q_chunks[2 * g + r // 2]
        if r % 2 != g:
            qc = pltpu.roll(qc, HEAD_DIM, 1)
        qs.append(qc)
    return (jnp.concatenate(qs, axis=0) * scale).astype(BF16)


def _sink_column(sinks_ref, g, rows):
    return jnp.concatenate(
        [jnp.full((rows, 1), sinks_ref[N_REP * g + r], F32) for r in range(N_REP)], axis=0)


def _unstack_heads(o_by_g, rows):
    lane = lax.broadcasted_iota(jnp.int32, (rows, LANES), 1)
    chunks = []
    for c in range(D_ATTN // LANES):
        g = c // 2
        halves = []
        for half in range(2):
            r = 2 * (c % 2) + half
            piece = o_by_g[g][r * rows:(r + 1) * rows]
            if half != g:
                piece = pltpu.roll(piece, HEAD_DIM, 1)
            halves.append(piece)
        chunks.append(jnp.where(lane < HEAD_DIM, halves[0], halves[1]))
    return chunks


def _glu_tail(y, u, dskip_ref, wglu_ref, bglu_ref):
    y = y + dskip_ref[...] * u
    y = jax.nn.gelu(y)
    z = _dot(y.astype(BF16), wglu_ref[...]) + bglu_ref[...]
    return y * jax.nn.sigmoid(z)


def _mixp_kernel(relb_ref, sinks_ref, x_ref, g_ref, win_ref, wout_ref, lam_ref, bd_ref, cd_ref,
                 dskip_ref, wglu_ref, bglu_ref,
                 o_ref, kp_ref, vp_ref, sre_ref, sim_ref,
                 p_s, kband, vband_t, bias_t, bias_later, us, ys, state, attn_s, bx_s, ut_s,
                 *, nb, blk):
    i = pl.program_id(0)
    rows = nb * blk

    @pl.when(i == 0)
    def _():
        kband[...] = jnp.zeros(kband.shape, BF16)
        vband_t[...] = jnp.zeros(vband_t.shape, BF16)
        state[...] = jnp.zeros(state.shape, F32)

    @pl.when(i == 0)
    def _():
        kj = lax.broadcasted_iota(jnp.int32, (2 * blk, blk), 0)
        qi = lax.broadcasted_iota(jnp.int32, (2 * blk, blk), 1)
        d = qi - kj + blk
        valid = (d >= 0) & (d < WINDOW)
        for h in range(N_HEADS):
            cols = slice((h % N_REP) * blk, (h % N_REP + 1) * blk)
            tile = _masked_bias(relb_ref, h, d, valid, scale=LOG2E)
            bias_later[h // N_REP, :, cols] = tile
            bias_t[h // N_REP, :, cols] = jnp.where(kj >= blk, tile, NEG_INF)

    @pl.when(i == 1)
    def _():
        bias_t[...] = bias_later[...]

    h = _rmsnorm(x_ref[...].reshape(rows, D_MODEL), g_ref[...]).astype(BF16)
    u_off = D_ATTN + 2 * D_KV
    p = _dot(h, win_ref[...])
    p_s[...] = p[:, 0:u_off]
    for s in range(U_SLABS):
        for b in range(nb):
            us[s, b * PITCH:b * PITCH + blk, :] = (
                p[b * blk:(b + 1) * blk, u_off + s * LANES:u_off + (s + 1) * LANES])

    kband[:, blk:2 * blk, :] = (
        p_s[:, D_ATTN:D_ATTN + D_KV].reshape(nb, blk, D_KV).astype(BF16))

    def attention_all():
        for b in range(nb):
            vband_t[b, :, blk:2 * blk] = (
                p_s[b * blk:(b + 1) * blk, D_ATTN + D_KV:D_ATTN + 2 * D_KV].T.astype(BF16))
        kb = kband[...]
        vt = vband_t[...]
        feat = lax.broadcasted_iota(jnp.int32, vt.shape, 1)
        lane = lax.broadcasted_iota(jnp.int32, kb.shape, 2)
        q_chunks = [p_s[:, c * LANES:(c + 1) * LANES].reshape(nb, blk, LANES)
                    for c in range(D_ATTN // LANES)]
        o_by_g = []
        for g in range(N_KV_HEADS):
            qs = []
            for r in range(N_REP):
                qc = q_chunks[2 * g + r // 2]
                if r % 2 != g:
                    qc = pltpu.roll(qc, HEAD_DIM, 2)
                qs.append(qc)
            q = (jnp.concatenate(qs, axis=1) * (LOG2E * HEAD_DIM ** -0.5)).astype(BF16)
            kg = jnp.where((lane >= g * HEAD_DIM) & (lane < (g + 1) * HEAD_DIM), kb,
                           jnp.zeros_like(kb))
            st = lax.dot_general(kg, q, (((2,), (2,)), ((0,), (0,))),
                                 preferred_element_type=F32) + bias_t[g]
            s = jnp.concatenate(
                [jnp.full((1, 1, blk), sinks_ref[N_REP * g + r] * LOG2E, F32)
                 for r in range(N_REP)], axis=2)
            m = jnp.maximum(jnp.max(st, axis=1, keepdims=True), s)
            e = jnp.exp2(st - m).astype(BF16)
            in_g = (feat >= g * HEAD_DIM) & (feat < (g + 1) * HEAD_DIM)
            ot = lax.dot_general(jnp.where(in_g, vt, jnp.ones_like(vt)), e,
                                 (((2,), (1,)), ((0,), (0,))),
                                 preferred_element_type=F32)
            other = (1 - g) * HEAD_DIM
            denom = ot[:, other:other + 1, :] + jnp.exp2(s - m)
            o_by_g.append(ot[:, g * HEAD_DIM:(g + 1) * HEAD_DIM, :] / denom)
        for b in range(nb):
            for c in range(D_ATTN // LANES):
                og = o_by_g[c // 2][b]
                cc = 2 * (c % 2)
                chunk_t = jnp.concatenate(
                    [og[:, cc * blk:(cc + 1) * blk], og[:, (cc + 1) * blk:(cc + 2) * blk]], axis=0)
                attn_s[b * blk:(b + 1) * blk, c * LANES:(c + 1) * LANES] = chunk_t.T

    def scan_gather():
        steps = []
        for t in range(blk):
            steps.append(jnp.concatenate(
                [us[s, pl.ds(t, nb, stride=PITCH), :] for s in range(U_SLABS)], axis=1))
        u_block = jnp.concatenate(steps, axis=0)
        ut_s[...] = u_block
        return u_block.astype(BF16)

    def scan_half(hf):
        ring = bx_s
        re = slice(2 * hf * HALF_STATE, (2 * hf + 1) * HALF_STATE)
        im = slice((2 * hf + 1) * HALF_STATE, (2 * hf + 2) * HALF_STATE)
        both = slice(2 * hf * HALF_STATE, 2 * (hf + 1) * HALF_STATE)
        lr = jnp.broadcast_to(lam_ref[2 * hf:2 * hf + 1, :], (nb, HALF_STATE))
        li = jnp.broadcast_to(lam_ref[2 * hf + 1:2 * hf + 2, :], (nb, HALF_STATE))
        xr = state[:, re]
        xi = state[:, im]
        for t in range(blk):
            rows_t = slice(t * nb, (t + 1) * nb)
            xr, xi = (lr * xr - li * xi + ring[rows_t, re],
                      lr * xi + li * xr + ring[rows_t, im])
            ring[rows_t, re] = xr
            ring[rows_t, im] = xi
        state[:, re] = xr
        state[:, im] = xi
        return _dot(ring[:, both].astype(BF16), cd_ref[hf])

    attention_all()
    ub = scan_gather()
    for hf in range(2):
        bx_s[:, 2 * hf * HALF_STATE:2 * (hf + 1) * HALF_STATE] = _dot(
            ub[:, hf * HALF_U:(hf + 1) * HALF_U], bd_ref[hf])
    y = jnp.concatenate([scan_half(hf) for hf in range(2)], axis=1)
    ssm_t = _glu_tail(y, ut_s[...], dskip_ref, wglu_ref, bglu_ref)
    for t in range(blk):
        for s in range(U_SLABS):
            ys[s, pl.ds(t, nb, stride=PITCH), :] = ssm_t[t * nb:(t + 1) * nb, s * LANES:(s + 1) * LANES]

    kband[:, 0:blk, :] = kband[:, blk:2 * blk, :]
    vband_t[:, :, 0:blk] = vband_t[:, :, blk:2 * blk]
    for hf in range(2):
        sre_ref[:, hf * HALF_STATE:(hf + 1) * HALF_STATE] = (
            state[:, 2 * hf * HALF_STATE:(2 * hf + 1) * HALF_STATE])
        sim_ref[:, hf * HALF_STATE:(hf + 1) * HALF_STATE] = (
            state[:, (2 * hf + 1) * HALF_STATE:(2 * hf + 2) * HALF_STATE])

    ssm = jnp.concatenate(
        [jnp.concatenate([ys[s, b * PITCH:b * PITCH + blk, :] for b in range(nb)], axis=0)
         for s in range(U_SLABS)], axis=1)
    mix = (_dot(attn_s[...].astype(BF16), wout_ref[0:D_ATTN, :])
           + _dot(ssm.astype(BF16), wout_ref[D_ATTN:D_ATTN + D_SSM, :]))
    o_ref[...] = x_ref[...] + mix.reshape(nb, blk, D_MODEL)

    @pl.when(i == pl.num_programs(0) - 1)
    def _():
        for b in range(nb):
            rows_b = slice(b * blk, (b + 1) * blk)
            kp_ref[b] = p_s[rows_b, D_ATTN:D_ATTN + D_KV].T
            vp_ref[b] = p_s[rows_b, D_ATTN + D_KV:D_ATTN + 2 * D_KV].T


def _mix_prompt(x, relb, sinks, g, win, wout, lam, bd, cd, dskip, wglu, bglu):
    nb, seq, _ = x.shape
    blk = WINDOW
    rows = nb * blk
    const2 = _const_spec
    return pl.pallas_call(
        functools.partial(_mixp_kernel, nb=nb, blk=blk),
        grid=(seq // blk,),
        in_specs=[
            _smem_spec(), _smem_spec(),
            pl.BlockSpec((nb, blk, D_MODEL), lambda i: (0, i, 0)),
            const2((1, D_MODEL)),
            const2((D_MODEL, D_IN)),
            const2((D_ATTN + D_SSM, D_MODEL)),
            const2((4, HALF_STATE)),
            const2((2, HALF_U, 2 * HALF_STATE)),
            const2((2, 2 * HALF_STATE, HALF_U)),
            const2((1, D_SSM)),
            const2((D_SSM, D_SSM)),
            const2((1, D_SSM)),
        ],
        out_specs=[
            pl.BlockSpec((nb, blk, D_MODEL), lambda i: (0, i, 0)),
            pl.BlockSpec((nb, blk, D_KV), lambda i: (0, 0, 0)),
            pl.BlockSpec((nb, blk, D_KV), lambda i: (0, 0, 0)),
            pl.BlockSpec((nb, 2 * HALF_STATE), lambda i: (0, 0)),
            pl.BlockSpec((nb, 2 * HALF_STATE), lambda i: (0, 0)),
        ],
        out_shape=[
            jax.ShapeDtypeStruct((nb, seq, D_MODEL), F32),
            jax.ShapeDtypeStruct((nb, blk, D_KV), F32),
            jax.ShapeDtypeStruct((nb, blk, D_KV), F32),
            jax.ShapeDtypeStruct((nb, 2 * HALF_STATE), F32),
            jax.ShapeDtypeStruct((nb, 2 * HALF_STATE), F32),
        ],
        scratch_shapes=[
            pltpu.VMEM((rows, D_ATTN + 2 * D_KV), F32),
            pltpu.VMEM((nb, 2 * blk, D_KV), BF16),
            pltpu.VMEM((nb, D_KV, 2 * blk), BF16),
            pltpu.VMEM((N_KV_HEADS, 2 * blk, N_REP * blk), F32),
            pltpu.VMEM((N_KV_HEADS, 2 * blk, N_REP * blk), F32),
            pltpu.VMEM((U_SLABS, nb * PITCH, LANES), F32),
            pltpu.VMEM((U_SLABS, nb * PITCH, LANES), F32),
            pltpu.VMEM((nb, 4 * HALF_STATE), F32),
            pltpu.VMEM((rows, D_ATTN), F32),
            pltpu.VMEM((rows, 4 * HALF_STATE), F32),
            pltpu.VMEM((rows, D_SSM), F32),
        ],
        compiler_params=pltpu.CompilerParams(
            dimension_semantics=("arbitrary",), vmem_limit_bytes=VMEM_LIMIT),
        name="mix_prompt",
    )(relb, sinks, x, g, win, wout, lam, bd, cd, dskip, wglu, bglu)


def _mixs_kernel(relb_ref, sinks_ref, x_ref, g_ref, win_ref, wout_ref, ck_ref, cv_ref,
                 sre0_ref, sim0_ref, lam_ref, bd_ref, cd_ref, dskip_ref, wglu_ref, bglu_ref,
                 o_ref, ks_ref, vs_ref, sre_ref, sim_ref,
                 p_s, attn_s, ssm_s, bias_c, bias_n, us,
                 *, nseq, t_new, sb, wbuf):
    i = pl.program_id(0)
    nsteps = pl.num_programs(0)
    rows = nseq * t_new
    brow = sb * t_new
    u_off = D_ATTN + 2 * D_KV
    tshift = t_new.bit_length() - 1
    wshift = wbuf.bit_length() - 1

    @pl.when(i == 0)
    def _():
        x = x_ref[...]
        h = _rmsnorm(x, g_ref[...]).astype(BF16)
        p_s[...] = _dot(h, win_ref[...])

        for s in range(U_SLABS):
            us[s] = p_s[:, u_off + s * LANES:u_off + (s + 1) * LANES]
        uperm = jnp.concatenate(
            [jnp.concatenate([us[s, pl.ds(t, nseq, stride=t_new), :] for t in range(t_new)], axis=0)
             for s in range(U_SLABS)], axis=1)
        yparts = []
        bus = [_dot(uperm[:, hf * HALF_U:(hf + 1) * HALF_U].astype(BF16), bd_ref[hf])
               for hf in range(2)]
        for hf in range(2):
            bu = bus[hf]
            lr = lam_ref[2 * hf:2 * hf + 1, :]
            li = lam_ref[2 * hf + 1:2 * hf + 2, :]
            xr = _transpose_blocks(sre0_ref[hf * HALF_STATE:(hf + 1) * HALF_STATE, :])
            xi = _transpose_blocks(sim0_ref[hf * HALF_STATE:(hf + 1) * HALF_STATE, :])
            states = []
            for t in range(t_new):
                br = bu[t * nseq:(t + 1) * nseq, 0:HALF_STATE]
                bi = bu[t * nseq:(t + 1) * nseq, HALF_STATE:2 * HALF_STATE]
                xr, xi = lr * xr - li * xi + br, lr * xi + li * xr + bi
                states.append(jnp.concatenate([xr, xi], axis=1).astype(BF16))
            sre_ref[hf * HALF_STATE:(hf + 1) * HALF_STATE, :] = _transpose_blocks(xr)
            sim_ref[hf * HALF_STATE:(hf + 1) * HALF_STATE, :] = _transpose_blocks(xi)
            yparts.append(_dot(jnp.concatenate(states, axis=0), cd_ref[hf]))
        yperm = jnp.concatenate(yparts, axis=1)
        for s in range(U_SLABS):
            for t in range(t_new):
                us[s, pl.ds(t, nseq, stride=t_new), :] = (
                    yperm[t * nseq:(t + 1) * nseq, s * LANES:(s + 1) * LANES])
        y = jnp.concatenate([us[s] for s in range(U_SLABS)], axis=1)
        ssm_s[...] = _glu_tail(y, p_s[:, u_off:u_off + D_SSM], dskip_ref, wglu_ref, bglu_ref)

        ncol = sb * wbuf
        rho = lax.broadcasted_iota(jnp.int32, (brow, ncol), 0)
        kap = lax.broadcasted_iota(jnp.int32, (brow, ncol), 1)
        same_seq = (rho >> tshift) == (kap >> wshift)
        rho_w = lax.broadcasted_iota(jnp.int32, (brow, wbuf), 0)
        d_c = (rho_w & (t_new - 1)) - lax.broadcasted_iota(jnp.int32, (brow, wbuf), 1) + wbuf
        valid_c = (d_c >= 0) & (d_c < WINDOW)
        rho_n = lax.broadcasted_iota(jnp.int32, (brow, LANES), 0)
        kap_n = lax.broadcasted_iota(jnp.int32, (brow, LANES), 1)
        tq_n = rho_n & (t_new - 1)
        d_n = tq_n - (kap_n & (t_new - 1))
        valid_n = ((rho_n >> tshift) == (kap_n >> tshift)) & (d_n >= 0) & (kap_n < brow)
        for hd in range(N_HEADS):
            g, r = hd // N_REP, hd % N_REP
            tile = _masked_bias(relb_ref, hd, d_c, valid_c)
            bias_c[g, r * brow:(r + 1) * brow, :] = jnp.where(
                same_seq, jnp.concatenate([tile] * sb, axis=1), NEG_INF)
            bias_n[g, r * brow:(r + 1) * brow, :] = _masked_bias(relb_ref, hd, d_n, valid_n)

    blocks = range(SAMPLE_BLOCKS)
    row0 = [pl.multiple_of((i * SAMPLE_BLOCKS + k) * brow, brow) for k in blocks]
    pad = jnp.zeros((LANES - brow, D_KV), F32)
    kn_pad = [jnp.concatenate([p_s[pl.ds(row0[k], brow), D_ATTN:D_ATTN + D_KV], pad], axis=0)
              for k in blocks]
    vn_pad = [jnp.concatenate(
        [p_s[pl.ds(row0[k], brow), D_ATTN + D_KV:D_ATTN + 2 * D_KV], pad], axis=0) for k in blocks]
    kc_t = jnp.stack([jnp.concatenate([ck_ref[k * sb + b] for b in range(sb)], axis=1)
                      for k in blocks]).astype(BF16)
    vc_t = jnp.stack([jnp.concatenate([cv_ref[k * sb + b] for b in range(sb)], axis=1)
                      for k in blocks]).astype(BF16)
    kn_b = jnp.stack(kn_pad).astype(BF16)
    vn_b = jnp.stack(vn_pad).astype(BF16)
    q_chunks = [[p_s[pl.ds(row0[k], brow), c * LANES:(c + 1) * LANES]
                 for c in range(D_ATTN // LANES)] for k in blocks]
    batch = ((0,), (0,))
    o_by_g = []
    for g in range(N_KV_HEADS):
        q = jnp.stack([_kv_lane_mask(_stack_queries(q_chunks[k], g), g) for k in blocks])
        lc = lax.dot_general(q, kc_t, (((2,), (1,)), batch), preferred_element_type=F32) + bias_c[g]
        ln = lax.dot_general(q, kn_b, (((2,), (2,)), batch), preferred_element_type=F32) + bias_n[g]
        s = _sink_column(sinks_ref, g, brow)
        m = jnp.maximum(jnp.maximum(jnp.max(lc, axis=-1, keepdims=True),
                                    jnp.max(ln, axis=-1, keepdims=True)), s)
        ec = jnp.exp(lc - m)
        en = jnp.exp(ln - m)
        denom = (jnp.sum(ec, axis=-1, keepdims=True) + jnp.sum(en, axis=-1, keepdims=True)
                 + jnp.exp(s - m))
        o = (lax.dot_general(ec.astype(BF16), vc_t, (((2,), (2,)), batch),
                             preferred_element_type=F32)
             + lax.dot_general(en.astype(BF16), vn_b, (((2,), (1,)), batch),
                               preferred_element_type=F32))
        o_by_g.append(o / denom)
    for k in blocks:
        for c, chunk in enumerate(_unstack_heads([o[k] for o in o_by_g], brow)):
            attn_s[pl.ds(row0[k], brow), c * LANES:(c + 1) * LANES] = chunk

    lane = lax.broadcasted_iota(jnp.int32, (D_KV, wbuf), 1)
    keep = wbuf - t_new
    for new_pads, old_ref, out_ref in ((kn_pad, ck_ref, ks_ref), (vn_pad, cv_ref, vs_ref)):
        for k in blocks:
            new_t = new_pads[k].T
            for b in range(sb):
                shifted = pltpu.roll(old_ref[k * sb + b], keep, 1)
                appended = pltpu.roll(new_t, (keep - b * t_new) % wbuf, 1)
                out_ref[k * sb + b] = jnp.where(lane >= keep, appended, shifted)

    @pl.when(i == nsteps - 1)
    def _():
        mix = (_dot(attn_s[...].astype(BF16), wout_ref[0:D_ATTN, :])
               + _dot(ssm_s[...].astype(BF16), wout_ref[D_ATTN:D_ATTN + D_SSM, :]))
        o_ref[...] = x_ref[...] + mix


def _mix_sample(x, relb, sinks, g, win, wout, ck, cv, sre0, sim0, lam, bd, cd, dskip, wglu, bglu,
                *, nseq, t_new):
    rows = nseq * t_new
    wbuf = ck.shape[2]
    assert wbuf == LANES
    sb = SUBLANES
    brow = sb * t_new
    return pl.pallas_call(
        functools.partial(_mixs_kernel, nseq=nseq, t_new=t_new, sb=sb, wbuf=wbuf),
        grid=(nseq // (SAMPLE_BLOCKS * sb),),
        in_specs=[
            _smem_spec(), _smem_spec(),
            _const_spec((rows, D_MODEL)),
            _const_spec((1, D_MODEL)),
            _const_spec((D_MODEL, D_IN)),
            _const_spec((D_ATTN + D_SSM, D_MODEL)),
            pl.BlockSpec((SAMPLE_BLOCKS * sb, D_KV, wbuf), lambda i: (i, 0, 0)),
            pl.BlockSpec((SAMPLE_BLOCKS * sb, D_KV, wbuf), lambda i: (i, 0, 0)),
            _const_spec((2 * HALF_STATE, nseq)),
            _const_spec((2 * HALF_STATE, nseq)),
            _const_spec((4, HALF_STATE)),
            _const_spec((2, HALF_U, 2 * HALF_STATE)),
            _const_spec((2, 2 * HALF_STATE, HALF_U)),
            _const_spec((1, D_SSM)),
            _const_spec((D_SSM, D_SSM)),
            _const_spec((1, D_SSM)),
        ],
        out_specs=[
            pl.BlockSpec((rows, D_MODEL), lambda i: (0, 0)),
            pl.BlockSpec((SAMPLE_BLOCKS * sb, D_KV, wbuf), lambda i: (i, 0, 0)),
            pl.BlockSpec((SAMPLE_BLOCKS * sb, D_KV, wbuf), lambda i: (i, 0, 0)),
            pl.BlockSpec((2 * HALF_STATE, nseq), lambda i: (0, 0)),
            pl.BlockSpec((2 * HALF_STATE, nseq), lambda i: (0, 0)),
        ],
        out_shape=[
            jax.ShapeDtypeStruct((rows, D_MODEL), F32),
            jax.ShapeDtypeStruct((nseq, D_KV, wbuf), F32),
            jax.ShapeDtypeStruct((nseq, D_KV, wbuf), F32),
            jax.ShapeDtypeStruct((2 * HALF_STATE, nseq), F32),
            jax.ShapeDtypeStruct((2 * HALF_STATE, nseq), F32),
        ],
        scratch_shapes=[
            pltpu.VMEM((rows, D_IN), F32),
            pltpu.VMEM((rows, D_ATTN), F32),
            pltpu.VMEM((rows, D_SSM), F32),
            pltpu.VMEM((N_KV_HEADS, N_REP * brow, sb * wbuf), F32),
            pltpu.VMEM((N_KV_HEADS, N_REP * brow, LANES), F32),
            pltpu.VMEM((U_SLABS, rows, LANES), F32),
        ],
        compiler_params=pltpu.CompilerParams(
            dimension_semantics=("arbitrary",), vmem_limit_bytes=VMEM_LIMIT),
        name="mix_sample",
    )(relb, sinks, x, g, win, wout, ck, cv, sre0, sim0, lam, bd, cd, dskip, wglu, bglu)


def _s5_operators(log_dt, a_re, a_im, b_re, b_im, c_re, c_im):
    dt = jnp.exp(log_dt)[:, None]
    mag = jnp.exp(a_re * dt)
    lb_re = mag * jnp.cos(a_im * dt)
    lb_im = mag * jnp.sin(a_im * dt)
    den = a_re * a_re + a_im * a_im
    nr = lb_re - 1.0
    q_re = (nr * a_re + lb_im * a_im) / den
    q_im = (lb_im * a_re - nr * a_im) / den
    bb_re = q_re[..., None] * b_re - q_im[..., None] * b_im
    bb_im = q_re[..., None] * b_im + q_im[..., None] * b_re
    eye = jnp.eye(HALF_GROUPS, dtype=BF16)
    lam = jnp.stack([lb_re.reshape(2, HALF_STATE), lb_im.reshape(2, HALF_STATE)], axis=1)
    bb = jnp.stack([bb_re, bb_im]).astype(BF16).reshape(2, 2, HALF_GROUPS, SSM_STATE, SSM_GROUP)
    bd = jnp.einsum('rhgpc,gk->hgcrkp', bb, eye).reshape(2, HALF_U, 2 * HALF_STATE)
    cc = jnp.stack([c_re, -c_im]).astype(BF16).reshape(2, 2, HALF_GROUPS, SSM_GROUP, SSM_STATE)
    cd = jnp.einsum('rhgcp,gk->hrgpkc', cc, eye).reshape(2, 2 * HALF_STATE, HALF_U)
    return lam.reshape(4, HALF_STATE), bd, cd


def kernel(x_prompt, x_sample, cache_k, cache_v, state_ssm_re, state_ssm_im, rel_bias,
           ffn1_norm, ffn1_w_gate, ffn1_w_up, ffn1_w_down, mix_norm, w_in, sinks,
           log_dt, a_re, a_im, b_re, b_im, c_re, c_im, d_skip, w_glu, b_glu, w_out,
           ffn2_norm, ffn2_w_gate, ffn2_w_up, ffn2_w_down, final_norm):
    depth = w_in.shape[0]
    assert depth == 1
    batch, seq, _ = x_prompt.shape
    nseq, t_new, _ = x_sample.shape
    fg = final_norm.reshape(1, D_MODEL)

    l = 0
    ffn1 = (ffn1_norm[l].reshape(1, D_MODEL), ffn1_w_gate[l], ffn1_w_up[l], ffn1_w_down[l], fg)
    ffn2 = (ffn2_norm[l].reshape(1, D_MODEL), ffn2_w_gate[l], ffn2_w_up[l], ffn2_w_down[l], fg)
    lam, bd, cd = _s5_operators(log_dt[l], a_re[l], a_im[l], b_re[l], b_im[l], c_re[l], c_im[l])
    mix_w = (mix_norm[l].reshape(1, D_MODEL), w_in[l].astype(BF16), w_out[l].astype(BF16))
    ssm_w = (lam, bd, cd, d_skip[l].reshape(1, D_SSM), w_glu[l].astype(BF16),
             b_glu[l].reshape(1, D_SSM))
    sinks_l = sinks[l]

    def window_in(c):
        return jnp.transpose(c, (0, 2, 3, 1)).reshape(c.shape[0], D_KV, c.shape[1])

    def window_out(w):
        n, _, pos = w.shape
        return jnp.transpose(w.reshape(n, N_KV_HEADS, HEAD_DIM, pos), (0, 3, 1, 2))[None]

    def state_in(s):
        return jnp.transpose(s, (1, 2, 0)).reshape(2 * HALF_STATE, s.shape[0])

    def state_out(s):
        return jnp.transpose(s.reshape(N_SSM_GROUPS, SSM_STATE, s.shape[1]), (2, 0, 1))[None]

    relb_t = rel_bias.T
    xp = x_prompt.reshape(batch * seq, D_MODEL)
    xs = x_sample.reshape(nseq * t_new, D_MODEL)
    ys, yp = _ffn(xp, xs, *ffn1, final_norm=False)
    yp, kp, vp, sre_p, sim_p = _mix_prompt(
        yp.reshape(batch, seq, D_MODEL), relb_t, sinks_l, *mix_w, *ssm_w)
    ys, ks, vs, sre_s, sim_s = _mix_sample(
        ys, relb_t, sinks_l, *mix_w,
        window_in(cache_k[l]), window_in(cache_v[l]),
        state_in(state_ssm_re[l]), state_in(state_ssm_im[l]),
        *ssm_w, nseq=nseq, t_new=t_new)
    y_sample, y_prompt = _ffn(yp.reshape(batch * seq, D_MODEL), ys, *ffn2, final_norm=True)
    y_prompt = y_prompt.reshape(batch, seq, D_MODEL)
    y_sample = y_sample.reshape(nseq, t_new, D_MODEL)

    st_p = (1, batch, N_SSM_GROUPS, SSM_STATE)
    return (y_prompt, y_sample,
            window_out(kp), window_out(vp), sre_p.reshape(st_p), sim_p.reshape(st_p),
            window_out(ks), window_out(vs), state_out(sre_s), state_out(sim_s))
```

```python
import functools
import math

import jax
import jax.numpy as jnp
from jax import lax
from jax.experimental import pallas as pl
from jax.experimental.pallas import tpu as pltpu

F32 = jnp.float32
BF16 = jnp.bfloat16

D_MODEL = 1024
HEAD_DIM = 64
D_ATTN = 512
N_HEADS = 8
N_KV_HEADS = 2
N_REP = 4
D_KV = 128
D_SSM = 512
SSM_GROUP = 16
N_SSM_GROUPS = 32
SSM_STATE = 64
WINDOW = 128
NUM_BUCKETS = 32
MAX_DISTANCE = 128
D_FF = 2816
D_IN = D_ATTN + 2 * D_KV + D_SSM
RMS_EPS = 1e-6
NEG_INF = -1e30
LOG2E = math.log2(math.e)

LANES = 128
SUBLANES = 8
MXU_DIM = 256
FF_CHUNKS = D_FF // MXU_DIM
HALF_GROUPS = N_SSM_GROUPS // 2
HALF_STATE = HALF_GROUPS * SSM_STATE
HALF_U = HALF_GROUPS * SSM_GROUP
U_SLABS = D_SSM // LANES
PITCH = WINDOW + SUBLANES
SAMPLE_BLOCKS = 4
VMEM_LIMIT = 60 * 1024 * 1024


def _const_spec(shape):
    nd = len(shape)
    return pl.BlockSpec(shape, lambda *_: (0,) * nd, pipeline_mode=pl.Buffered(1))


def _smem_spec():
    return pl.BlockSpec(memory_space=pltpu.SMEM)


def _rmsnorm(x, g):
    r = lax.rsqrt(jnp.mean(x * x, axis=-1, keepdims=True) + RMS_EPS)
    return (x * r) * g


def _dot(a, b):
    return jnp.dot(a, b, preferred_element_type=F32)


def _dot_nt(a, b):
    return lax.dot_general(a, b, (((1,), (1,)), ((), ())), preferred_element_type=F32)


def _transpose_blocks(a):
    r, c = a.shape
    return jnp.concatenate(
        [jnp.concatenate([a[i * LANES:(i + 1) * LANES, j * LANES:(j + 1) * LANES].T
                          for i in range(r // LANES)], axis=1)
         for j in range(c // LANES)], axis=0)


def _ffn_tile(x, g_ref, wg_ref, wu_ref, wd_ref, fg_ref, final_norm, before_chunk=None):
    h = _rmsnorm(x, g_ref[...]).astype(BF16)
    acc = None
    for c in range(FF_CHUNKS):
        if before_chunk is not None:
            before_chunk(c)
        sl = slice(c * MXU_DIM, (c + 1) * MXU_DIM)
        gate = _dot(h, wg_ref[:, sl].astype(BF16))
        up = _dot(h, wu_ref[:, sl].astype(BF16))
        a = (gate * jax.nn.sigmoid(gate) * up).astype(BF16)
        part = _dot(a, wd_ref[sl, :].astype(BF16))
        acc = part if acc is None else acc + part
    y = x + 0.5 * acc
    if final_norm:
        y = _rmsnorm(y, fg_ref[...])
    return y


def _ffn_kernel(xs_ref, xp_ref, g_ref, wg_hbm, wu_hbm, wd_hbm, fg_ref, os_ref, op_ref,
                wg_s, wu_s, wd_s, sems, *, final_norm):
    i = pl.program_id(0)
    weights = (g_ref, wg_s, wu_s, wd_s, fg_ref)

    def chunk_copies(c):
        sl = slice(c * MXU_DIM, (c + 1) * MXU_DIM)
        return (pltpu.make_async_copy(wg_hbm.at[:, sl], wg_s.at[:, sl], sems.at[0, c]),
                pltpu.make_async_copy(wu_hbm.at[:, sl], wu_s.at[:, sl], sems.at[1, c]),
                pltpu.make_async_copy(wd_hbm.at[sl, :], wd_s.at[sl, :], sems.at[2, c]))

    @pl.when(i == 0)
    def _():
        for c in range(FF_CHUNKS):
            for cp in chunk_copies(c):
                cp.start()

        def wait_chunk(c):
            for cp in chunk_copies(c):
                cp.wait()

        xs = xs_ref[...].reshape(-1, D_MODEL)
        os_ref[...] = _ffn_tile(xs, *weights, final_norm, wait_chunk).reshape(os_ref.shape)

    @pl.when(i > 0)
    def _():
        op_ref[...] = _ffn_tile(xp_ref[...], *weights, final_norm)


def _ffn(xp, xs, g, wg, wu, wd, fg, *, final_norm, ys_shape):
    n_p, n_s = xp.shape[0], math.prod(xs.shape[:-1])
    zeros = lambda nd: (lambda i: (0,) * nd)
    tm = n_s
    prompt_steps = n_p // tm
    assert prompt_steps * tm == n_p
    prompt_block = lambda i: (jnp.maximum(i - 1, 0), 0)
    hbm = pl.BlockSpec(memory_space=pl.ANY)
    return pl.pallas_call(
        functools.partial(_ffn_kernel, final_norm=final_norm),
        grid=(prompt_steps + 1,),
        in_specs=[
            _const_spec(xs.shape),
            pl.BlockSpec((tm, D_MODEL), prompt_block),
            _const_spec((1, D_MODEL)),
            hbm, hbm, hbm,
            _const_spec((1, D_MODEL)),
        ],
        out_specs=[
            pl.BlockSpec(ys_shape, zeros(len(ys_shape))),
            pl.BlockSpec((tm, D_MODEL), prompt_block),
        ],
        out_shape=[
            jax.ShapeDtypeStruct(ys_shape, F32),
            jax.ShapeDtypeStruct((n_p, D_MODEL), F32),
        ],
        scratch_shapes=[
            pltpu.VMEM((D_MODEL, D_FF), F32),
            pltpu.VMEM((D_MODEL, D_FF), F32),
            pltpu.VMEM((D_FF, D_MODEL), F32),
            pltpu.SemaphoreType.DMA((3, FF_CHUNKS)),
        ],
        compiler_params=pltpu.CompilerParams(
            dimension_semantics=("arbitrary",), vmem_limit_bytes=VMEM_LIMIT),
        name="ffn_final" if final_norm else "ffn",
    )(xs, xp, g, wg, wu, wd, fg)


def _t5_bucket(d):
    d = jnp.maximum(d, 0)
    max_exact = NUM_BUCKETS // 2
    df = jnp.maximum(d, 1).astype(F32)
    large = max_exact + (jnp.log(df / max_exact) / math.log(MAX_DISTANCE / max_exact)
                         * (NUM_BUCKETS - max_exact)).astype(jnp.int32)
    large = jnp.minimum(large, NUM_BUCKETS - 1)
    return jnp.where(d < max_exact, d, large)


def _masked_bias(relb_ref, head, d, valid, scale=1.0):
    bucket = _t5_bucket(d)
    b = jnp.zeros(d.shape, F32)
    for k in range(NUM_BUCKETS):
        b = jnp.where(bucket == k, relb_ref[head, k] * scale, b)
    return jnp.where(valid, b, NEG_INF)


def _kv_lane_mask(t, g):
    lane = lax.broadcasted_iota(jnp.int32, t.shape, 1)
    return jnp.where((lane >= g * HEAD_DIM) & (lane < (g + 1) * HEAD_DIM), t, jnp.zeros_like(t))


def _stack_queries(q_chunks, g, scale=HEAD_DIM ** -0.5):
    qs = []
    for r in range(N_REP):
        qc = q_chunks[2 * g + r // 2]
        if r % 2 != g:
            qc = pltpu.roll(qc, HEAD_DIM, 1)
        qs.append(qc)
    return (jnp.concatenate(qs, axis=0) * scale).astype(BF16)


def _sink_column(sinks_ref, g, rows):
    return jnp.concatenate(
        [jnp.full((rows, 1), sinks_ref[N_REP * g + r], F32) for r in range(N_REP)], axis=0)


def _unstack_heads(o_by_g, rows):
    lane = lax.broadcasted_iota(jnp.int32, (rows, LANES), 1)
    chunks = []
    for c in range(D_ATTN // LANES):
        g = c // 2
        halves = []
        for half in range(2):
            r = 2 * (c % 2) + half
            piece = o_by_g[g][r * rows:(r + 1) * rows]
            if half != g:
                piece = pltpu.roll(piece, HEAD_DIM, 1)
            halves.append(piece)
        chunks.append(jnp.where(lane < HEAD_DIM, halves[0], halves[1]))
    return chunks


def _glu_tail(y, u, dskip_ref, wglu_ref, bglu_ref):
    y = y + dskip_ref[...] * u
    y = jax.nn.gelu(y)
    z = _dot(y.astype(BF16), wglu_ref[...]) + bglu_ref[...]
    return y * jax.nn.sigmoid(z)


def _mixp_kernel(relb_ref, sinks_ref, x_ref, g_ref, win_ref, wout_ref, lam_ref, bd_ref, cd_ref,
                 dskip_ref, wglu_ref, bglu_ref,
                 o_ref, kp_ref, vp_ref, sre_ref, sim_ref,
                 p_s, kband, vband_t, bias_t, bias_later, us, ys, state, attn_s, bx_s, ut_s,
                 *, nb, blk):
    i = pl.program_id(0)
    rows = nb * blk

    @pl.when(i == 0)
    def _():
        kband[...] = jnp.zeros(kband.shape, BF16)
        vband_t[...] = jnp.zeros(vband_t.shape, BF16)
        state[...] = jnp.zeros(state.shape, F32)

    @pl.when(i == 0)
    def _():
        kj = lax.broadcasted_iota(jnp.int32, (2 * blk, blk), 0)
        qi = lax.broadcasted_iota(jnp.int32, (2 * blk, blk), 1)
        d = qi - kj + blk
        valid = (d >= 0) & (d < WINDOW)
        for h in range(N_HEADS):
            cols = slice((h % N_REP) * blk, (h % N_REP + 1) * blk)
            tile = _masked_bias(relb_ref, h, d, valid, scale=LOG2E)
            bias_later[h // N_REP, :, cols] = tile
            bias_t[h // N_REP, :, cols] = jnp.where(kj >= blk, tile, NEG_INF)

    @pl.when(i == 1)
    def _():
        bias_t[...] = bias_later[...]

    h = _rmsnorm(x_ref[...].reshape(rows, D_MODEL), g_ref[...]).astype(BF16)
    u_off = D_ATTN + 2 * D_KV
    p = _dot(h, win_ref[...])
    p_s[...] = p[:, 0:u_off]
    for s in range(U_SLABS):
        for b in range(nb):
            us[s, b * PITCH:b * PITCH + blk, :] = (
                p[b * blk:(b + 1) * blk, u_off + s * LANES:u_off + (s + 1) * LANES])

    kband[:, blk:2 * blk, :] = (
        p_s[:, D_ATTN:D_ATTN + D_KV].reshape(nb, blk, D_KV).astype(BF16))

    def attention_all():
        for b in range(nb):
            vband_t[b, :, blk:2 * blk] = (
                p_s[b * blk:(b + 1) * blk, D_ATTN + D_KV:D_ATTN + 2 * D_KV].T.astype(BF16))
        kb = kband[...]
        vt = vband_t[...]
        feat = lax.broadcasted_iota(jnp.int32, vt.shape, 1)
        lane = lax.broadcasted_iota(jnp.int32, kb.shape, 2)
        q_chunks = [p_s[:, c * LANES:(c + 1) * LANES].reshape(nb, blk, LANES)
                    for c in range(D_ATTN // LANES)]
        o_by_g = []
        for g in range(N_KV_HEADS):
            qs = []
            for r in range(N_REP):
                qc = q_chunks[2 * g + r // 2]
                if r % 2 != g:
                    qc = pltpu.roll(qc, HEAD_DIM, 2)
                qs.append(qc)
            q = (jnp.concatenate(qs, axis=1) * (LOG2E * HEAD_DIM ** -0.5)).astype(BF16)
            kg = jnp.where((lane >= g * HEAD_DIM) & (lane < (g + 1) * HEAD_DIM), kb,
                           jnp.zeros_like(kb))
            st = lax.dot_general(kg, q, (((2,), (2,)), ((0,), (0,))),
                                 preferred_element_type=F32) + bias_t[g]
            s = jnp.concatenate(
                [jnp.full((1, 1, blk), sinks_ref[N_REP * g + r] * LOG2E, F32)
                 for r in range(N_REP)], axis=2)
            m = jnp.maximum(jnp.max(st, axis=1, keepdims=True), s)
            e = jnp.exp2(st - m).astype(BF16)
            in_g = (feat >= g * HEAD_DIM) & (feat < (g + 1) * HEAD_DIM)
            ot = lax.dot_general(jnp.where(in_g, vt, jnp.ones_like(vt)), e,
                                 (((2,), (1,)), ((0,), (0,))),
                                 preferred_element_type=F32)
            other = (1 - g) * HEAD_DIM
            denom = ot[:, other:other + 1, :] + jnp.exp2(s - m)
            o_by_g.append(ot[:, g * HEAD_DIM:(g + 1) * HEAD_DIM, :] / denom)
        for b in range(nb):
            for c in range(D_ATTN // LANES):
                og = o_by_g[c // 2][b]
                cc = 2 * (c % 2)
                chunk_t = jnp.concatenate(
                    [og[:, cc * blk:(cc + 1) * blk], og[:, (cc + 1) * blk:(cc + 2) * blk]], axis=0)
                attn_s[b * blk:(b + 1) * blk, c * LANES:(c + 1) * LANES] = chunk_t.T

    def scan_gather():
        steps = []
        for t in range(blk):
            steps.append(jnp.concatenate(
                [us[s, pl.ds(t, nb, stride=PITCH), :] for s in range(U_SLABS)], axis=1))
        u_block = jnp.concatenate(steps, axis=0)
        ut_s[...] = u_block
        return u_block.astype(BF16)

    def scan_half(hf):
        ring = bx_s
        re = slice(2 * hf * HALF_STATE, (2 * hf + 1) * HALF_STATE)
        im = slice((2 * hf + 1) * HALF_STATE, (2 * hf + 2) * HALF_STATE)
        both = slice(2 * hf * HALF_STATE, 2 * (hf + 1) * HALF_STATE)
        lr = jnp.broadcast_to(lam_ref[2 * hf:2 * hf + 1, :], (nb, HALF_STATE))
        li = jnp.broadcast_to(lam_ref[2 * hf + 1:2 * hf + 2, :], (nb, HALF_STATE))
        xr = state[:, re]
        xi = state[:, im]
        for t in range(blk):
            rows_t = slice(t * nb, (t + 1) * nb)
            xr, xi = (lr * xr - li * xi + ring[rows_t, re],
                      lr * xi + li * xr + ring[rows_t, im])
            ring[rows_t, re] = xr
            ring[rows_t, im] = xi
        state[:, re] = xr
        state[:, im] = xi
        return _dot(ring[:, both].astype(BF16), cd_ref[hf])

    attention_all()
    ub = scan_gather()
    for hf in range(2):
        bx_s[:, 2 * hf * HALF_STATE:2 * (hf + 1) * HALF_STATE] = _dot(
            ub[:, hf * HALF_U:(hf + 1) * HALF_U], bd_ref[hf])
    y = jnp.concatenate([scan_half(hf) for hf in range(2)], axis=1)
    ssm_t = _glu_tail(y, ut_s[...], dskip_ref, wglu_ref, bglu_ref)
    for t in range(blk):
        for s in range(U_SLABS):
            ys[s, pl.ds(t, nb, stride=PITCH), :] = ssm_t[t * nb:(t + 1) * nb, s * LANES:(s + 1) * LANES]

    kband[:, 0:blk, :] = kband[:, blk:2 * blk, :]
    vband_t[:, :, 0:blk] = vband_t[:, :, blk:2 * blk]
    for hf in range(2):
        sre_ref[:, hf * HALF_STATE:(hf + 1) * HALF_STATE] = (
            state[:, 2 * hf * HALF_STATE:(2 * hf + 1) * HALF_STATE])
        sim_ref[:, hf * HALF_STATE:(hf + 1) * HALF_STATE] = (
            state[:, (2 * hf + 1) * HALF_STATE:(2 * hf + 2) * HALF_STATE])

    ssm = jnp.concatenate(
        [jnp.concatenate([ys[s, b * PITCH:b * PITCH + blk, :] for b in range(nb)], axis=0)
         for s in range(U_SLABS)], axis=1)
    mix = (_dot(attn_s[...].astype(BF16), wout_ref[0:D_ATTN, :])
           + _dot(ssm.astype(BF16), wout_ref[D_ATTN:D_ATTN + D_SSM, :]))
    o_ref[...] = x_ref[...] + mix.reshape(nb, blk, D_MODEL)

    @pl.when(i == pl.num_programs(0) - 1)
    def _():
        for b in range(nb):
            rows_b = slice(b * blk, (b + 1) * blk)
            kp_ref[b] = p_s[rows_b, D_ATTN:D_ATTN + D_KV].T
            vp_ref[b] = p_s[rows_b, D_ATTN + D_KV:D_ATTN + 2 * D_KV].T


def _mix_prompt(x, relb, sinks, g, win, wout, lam, bd, cd, dskip, wglu, bglu):
    nb, seq, _ = x.shape
    blk = WINDOW
    rows = nb * blk
    const2 = _const_spec
    return pl.pallas_call(
        functools.partial(_mixp_kernel, nb=nb, blk=blk),
        grid=(seq // blk,),
        in_specs=[
            _smem_spec(), _smem_spec(),
            pl.BlockSpec((nb, blk, D_MODEL), lambda i: (0, i, 0)),
            const2((1, D_MODEL)),
            const2((D_MODEL, D_IN)),
            const2((D_ATTN + D_SSM, D_MODEL)),
            const2((4, HALF_STATE)),
            const2((2, HALF_U, 2 * HALF_STATE)),
            const2((2, 2 * HALF_STATE, HALF_U)),
            const2((1, D_SSM)),
            const2((D_SSM, D_SSM)),
            const2((1, D_SSM)),
        ],
        out_specs=[
            pl.BlockSpec((nb, blk, D_MODEL), lambda i: (0, i, 0)),
            pl.BlockSpec((nb, blk, D_KV), lambda i: (0, 0, 0)),
            pl.BlockSpec((nb, blk, D_KV), lambda i: (0, 0, 0)),
            pl.BlockSpec((nb, 2 * HALF_STATE), lambda i: (0, 0)),
            pl.BlockSpec((nb, 2 * HALF_STATE), lambda i: (0, 0)),
        ],
        out_shape=[
            jax.ShapeDtypeStruct((nb, seq, D_MODEL), F32),
            jax.ShapeDtypeStruct((nb, blk, D_KV), F32),
            jax.ShapeDtypeStruct((nb, blk, D_KV), F32),
            jax.ShapeDtypeStruct((nb, 2 * HALF_STATE), F32),
            jax.ShapeDtypeStruct((nb, 2 * HALF_STATE), F32),
        ],
        scratch_shapes=[
            pltpu.VMEM((rows, D_ATTN + 2 * D_KV), F32),
            pltpu.VMEM((nb, 2 * blk, D_KV), BF16),
            pltpu.VMEM((nb, D_KV, 2 * blk), BF16),
            pltpu.VMEM((N_KV_HEADS, 2 * blk, N_REP * blk), F32),
            pltpu.VMEM((N_KV_HEADS, 2 * blk, N_REP * blk), F32),
            pltpu.VMEM((U_SLABS, nb * PITCH, LANES), F32),
            pltpu.VMEM((U_SLABS, nb * PITCH, LANES), F32),
            pltpu.VMEM((nb, 4 * HALF_STATE), F32),
            pltpu.VMEM((rows, D_ATTN), F32),
            pltpu.VMEM((rows, 4 * HALF_STATE), F32),
            pltpu.VMEM((rows, D_SSM), F32),
        ],
        compiler_params=pltpu.CompilerParams(
            dimension_semantics=("arbitrary",), vmem_limit_bytes=VMEM_LIMIT),
        name="mix_prompt",
    )(relb, sinks, x, g, win, wout, lam, bd, cd, dskip, wglu, bglu)


def _mixs_kernel(relb_ref, sinks_ref, x_ref, g_ref, win_ref, wout_ref, ck_ref, cv_ref,
                 sre0_ref, sim0_ref, lam_ref, bd_ref, cd_ref, dskip_ref, wglu_ref, bglu_ref,
                 o_ref, ks_ref, vs_ref, sre_ref, sim_ref,
                 p_s, attn_s, ssm_s, bias_c, bias_n, us,
                 *, nseq, t_new, sb, wbuf):
    i = pl.program_id(0)
    nsteps = pl.num_programs(0)
    rows = nseq * t_new
    brow = sb * t_new
    u_off = D_ATTN + 2 * D_KV
    tshift = t_new.bit_length() - 1
    wshift = wbuf.bit_length() - 1

    @pl.when(i == 0)
    def _():
        x = x_ref[...]
        h = _rmsnorm(x, g_ref[...]).astype(BF16)
        p_s[...] = _dot(h, win_ref[...])

        for s in range(U_SLABS):
            us[s] = p_s[:, u_off + s * LANES:u_off + (s + 1) * LANES]
        uperm = jnp.concatenate(
            [jnp.concatenate([us[s, pl.ds(t, nseq, stride=t_new), :] for t in range(t_new)], axis=0)
             for s in range(U_SLABS)], axis=1)
        yparts = []
        bus = [_dot(uperm[:, hf * HALF_U:(hf + 1) * HALF_U].astype(BF16), bd_ref[hf])
               for hf in range(2)]
        for hf in range(2):
            bu = bus[hf]
            lr = lam_ref[2 * hf:2 * hf + 1, :]
            li = lam_ref[2 * hf + 1:2 * hf + 2, :]
            xr = _transpose_blocks(sre0_ref[hf * HALF_STATE:(hf + 1) * HALF_STATE, :])
            xi = _transpose_blocks(sim0_ref[hf * HALF_STATE:(hf + 1) * HALF_STATE, :])
            states = []
            for t in range(t_new):
                br = bu[t * nseq:(t + 1) * nseq, 0:HALF_STATE]
                bi = bu[t * nseq:(t + 1) * nseq, HALF_STATE:2 * HALF_STATE]
                xr, xi = lr * xr - li * xi + br, lr * xi + li * xr + bi
                states.append(jnp.concatenate([xr, xi], axis=1).astype(BF16))
            sre_ref[hf * HALF_STATE:(hf + 1) * HALF_STATE, :] = _transpose_blocks(xr)
            sim_ref[hf * HALF_STATE:(hf + 1) * HALF_STATE, :] = _transpose_blocks(xi)
            yparts.append(_dot(jnp.concatenate(states, axis=0), cd_ref[hf]))
        yperm = jnp.concatenate(yparts, axis=1)
        for s in range(U_SLABS):
            for t in range(t_new):
                us[s, pl.ds(t, nseq, stride=t_new), :] = (
                    yperm[t * nseq:(t + 1) * nseq, s * LANES:(s + 1) * LANES])
        y = jnp.concatenate([us[s] for s in range(U_SLABS)], axis=1)
        ssm_s[...] = _glu_tail(y, p_s[:, u_off:u_off + D_SSM], dskip_ref, wglu_ref, bglu_ref)

        ncol = sb * wbuf
        rho = lax.broadcasted_iota(jnp.int32, (brow, ncol), 0)
        kap = lax.broadcasted_iota(jnp.int32, (brow, ncol), 1)
        same_seq = (rho >> tshift) == (kap >> wshift)
        rho_w = lax.broadcasted_iota(jnp.int32, (brow, wbuf), 0)
        d_c = (rho_w & (t_new - 1)) - lax.broadcasted_iota(jnp.int32, (brow, wbuf), 1) + wbuf
        valid_c = (d_c >= 0) & (d_c < WINDOW)
        rho_n = lax.broadcasted_iota(jnp.int32, (brow, LANES), 0)
        kap_n = lax.broadcasted_iota(jnp.int32, (brow, LANES), 1)
        tq_n = rho_n & (t_new - 1)
        d_n = tq_n - (kap_n & (t_new - 1))
        valid_n = ((rho_n >> tshift) == (kap_n >> tshift)) & (d_n >= 0) & (kap_n < brow)
        for hd in range(N_HEADS):
            g, r = hd // N_REP, hd % N_REP
            tile = _masked_bias(relb_ref, hd, d_c, valid_c)
            bias_c[g, r * brow:(r + 1) * brow, :] = jnp.where(
                same_seq, jnp.concatenate([tile] * sb, axis=1), NEG_INF)
            bias_n[g, r * brow:(r + 1) * brow, :] = _masked_bias(relb_ref, hd, d_n, valid_n)

    blocks = range(SAMPLE_BLOCKS)
    row0 = [pl.multiple_of((i * SAMPLE_BLOCKS + k) * brow, brow) for k in blocks]
    pad = jnp.zeros((LANES - brow, D_KV), F32)
    kn_pad = [jnp.concatenate([p_s[pl.ds(row0[k], brow), D_ATTN:D_ATTN + D_KV], pad], axis=0)
              for k in blocks]
    vn_pad = [jnp.concatenate(
        [p_s[pl.ds(row0[k], brow), D_ATTN + D_KV:D_ATTN + 2 * D_KV], pad], axis=0) for k in blocks]
    kc_t = jnp.stack([jnp.concatenate([ck_ref[k * sb + b] for b in range(sb)], axis=1)
                      for k in blocks]).astype(BF16)
    vc_t = jnp.stack([jnp.concatenate([cv_ref[k * sb + b] for b in range(sb)], axis=1)
                      for k in blocks]).astype(BF16)
    kn_b = jnp.stack(kn_pad).astype(BF16)
    vn_b = jnp.stack(vn_pad).astype(BF16)
    q_chunks = [[p_s[pl.ds(row0[k], brow), c * LANES:(c + 1) * LANES]
                 for c in range(D_ATTN // LANES)] for k in blocks]
    batch = ((0,), (0,))
    o_by_g = []
    for g in range(N_KV_HEADS):
        q = jnp.stack([_kv_lane_mask(_stack_queries(q_chunks[k], g), g) for k in blocks])
        lc = lax.dot_general(q, kc_t, (((2,), (1,)), batch), preferred_element_type=F32) + bias_c[g]
        ln = lax.dot_general(q, kn_b, (((2,), (2,)), batch), preferred_element_type=F32) + bias_n[g]
        s = _sink_column(sinks_ref, g, brow)
        m = jnp.maximum(jnp.maximum(jnp.max(lc, axis=-1, keepdims=True),
                                    jnp.max(ln, axis=-1, keepdims=True)), s)
        ec = jnp.exp(lc - m)
        en = jnp.exp(ln - m)
        denom = (jnp.sum(ec, axis=-1, keepdims=True) + jnp.sum(en, axis=-1, keepdims=True)
                 + jnp.exp(s - m))
        o = (lax.dot_general(ec.astype(BF16), vc_t, (((2,), (2,)), batch),
                             preferred_element_type=F32)
             + lax.dot_general(en.astype(BF16), vn_b, (((2,), (1,)), batch),
                               preferred_element_type=F32))
        o_by_g.append(o / denom)
    for k in blocks:
        for c, chunk in enumerate(_unstack_heads([o[k] for o in o_by_g], brow)):
            attn_s[pl.ds(row0[k], brow), c * LANES:(c + 1) * LANES] = chunk

    lane = lax.broadcasted_iota(jnp.int32, (D_KV, wbuf), 1)
    keep = wbuf - t_new
    for new_pads, old_ref, out_ref in ((kn_pad, ck_ref, ks_ref), (vn_pad, cv_ref, vs_ref)):
        for k in blocks:
            new_t = new_pads[k].T
            for b in range(sb):
                shifted = pltpu.roll(old_ref[k * sb + b], keep, 1)
                appended = pltpu.roll(new_t, (keep - b * t_new) % wbuf, 1)
                out_ref[k * sb + b] = jnp.where(lane >= keep, appended, shifted)

    @pl.when(i == nsteps - 1)
    def _():
        mix = (_dot(attn_s[...].astype(BF16), wout_ref[0:D_ATTN, :])
               + _dot(ssm_s[...].astype(BF16), wout_ref[D_ATTN:D_ATTN + D_SSM, :]))
        o_ref[...] = x_ref[...] + mix


def _mix_sample(x, relb, sinks, g, win, wout, ck, cv, sre0, sim0, lam, bd, cd, dskip, wglu, bglu,
                *, nseq, t_new):
    rows = nseq * t_new
    wbuf = ck.shape[2]
    assert wbuf == LANES
    sb = SUBLANES
    brow = sb * t_new
    return pl.pallas_call(
        functools.partial(_mixs_kernel, nseq=nseq, t_new=t_new, sb=sb, wbuf=wbuf),
        grid=(nseq // (SAMPLE_BLOCKS * sb),),
        in_specs=[
            _smem_spec(), _smem_spec(),
            _const_spec((rows, D_MODEL)),
            _const_spec((1, D_MODEL)),
            _const_spec((D_MODEL, D_IN)),
            _const_spec((D_ATTN + D_SSM, D_MODEL)),
            pl.BlockSpec((SAMPLE_BLOCKS * sb, D_KV, wbuf), lambda i: (i, 0, 0)),
            pl.BlockSpec((SAMPLE_BLOCKS * sb, D_KV, wbuf), lambda i: (i, 0, 0)),
            _const_spec((2 * HALF_STATE, nseq)),
            _const_spec((2 * HALF_STATE, nseq)),
            _const_spec((4, HALF_STATE)),
            _const_spec((2, HALF_U, 2 * HALF_STATE)),
            _const_spec((2, 2 * HALF_STATE, HALF_U)),
            _const_spec((1, D_SSM)),
            _const_spec((D_SSM, D_SSM)),
            _const_spec((1, D_SSM)),
        ],
        out_specs=[
            pl.BlockSpec((rows, D_MODEL), lambda i: (0, 0)),
            pl.BlockSpec((SAMPLE_BLOCKS * sb, D_KV, wbuf), lambda i: (i, 0, 0)),
            pl.BlockSpec((SAMPLE_BLOCKS * sb, D_KV, wbuf), lambda i: (i, 0, 0)),
            pl.BlockSpec((2 * HALF_STATE, nseq), lambda i: (0, 0)),
            pl.BlockSpec((2 * HALF_STATE, nseq), lambda i: (0, 0)),
        ],
        out_shape=[
            jax.ShapeDtypeStruct((rows, D_MODEL), F32),
            jax.ShapeDtypeStruct((nseq, D_KV, wbuf), F32),
            jax.ShapeDtypeStruct((nseq, D_KV, wbuf), F32),
            jax.ShapeDtypeStruct((2 * HALF_STATE, nseq), F32),
            jax.ShapeDtypeStruct((2 * HALF_STATE, nseq), F32),
        ],
        scratch_shapes=[
            pltpu.VMEM((rows, D_IN), F32),
            pltpu.VMEM((rows, D_ATTN), F32),
            pltpu.VMEM((rows, D_SSM), F32),
            pltpu.VMEM((N_KV_HEADS, N_REP * brow, sb * wbuf), F32),
            pltpu.VMEM((N_KV_HEADS, N_REP * brow, LANES), F32),
            pltpu.VMEM((U_SLABS, rows, LANES), F32),
        ],
        compiler_params=pltpu.CompilerParams(
            dimension_semantics=("arbitrary",), vmem_limit_bytes=VMEM_LIMIT),
        name="mix_sample",
    )(relb, sinks, x, g, win, wout, ck, cv, sre0, sim0, lam, bd, cd, dskip, wglu, bglu)


def _s5_operators(log_dt, a_re, a_im, b_re, b_im, c_re, c_im):
    dt = jnp.exp(log_dt)[:, None]
    mag = jnp.exp(a_re * dt)
    lb_re = mag * jnp.cos(a_im * dt)
    lb_im = mag * jnp.sin(a_im * dt)
    den = a_re * a_re + a_im * a_im
    nr = lb_re - 1.0
    q_re = (nr * a_re + lb_im * a_im) / den
    q_im = (lb_im * a_re - nr * a_im) / den
    bb_re = q_re[..., None] * b_re - q_im[..., None] * b_im
    bb_im = q_re[..., None] * b_im + q_im[..., None] * b_re
    eye = jnp.eye(HALF_GROUPS, dtype=BF16)
    lam = jnp.stack([lb_re.reshape(2, HALF_STATE), lb_im.reshape(2, HALF_STATE)], axis=1)
    bb = jnp.stack([bb_re, bb_im]).astype(BF16).reshape(2, 2, HALF_GROUPS, SSM_STATE, SSM_GROUP)
    bd = jnp.einsum('rhgpc,gk->hgcrkp', bb, eye).reshape(2, HALF_U, 2 * HALF_STATE)
    cc = jnp.stack([c_re, -c_im]).astype(BF16).reshape(2, 2, HALF_GROUPS, SSM_GROUP, SSM_STATE)
    cd = jnp.einsum('rhgcp,gk->hrgpkc', cc, eye).reshape(2, 2 * HALF_STATE, HALF_U)
    return lam.reshape(4, HALF_STATE), bd, cd


def kernel(x_prompt, x_sample, cache_k, cache_v, state_ssm_re, state_ssm_im, rel_bias,
           ffn1_norm, ffn1_w_gate, ffn1_w_up, ffn1_w_down, mix_norm, w_in, sinks,
           log_dt, a_re, a_im, b_re, b_im, c_re, c_im, d_skip, w_glu, b_glu, w_out,
           ffn2_norm, ffn2_w_gate, ffn2_w_up, ffn2_w_down, final_norm):
    depth = w_in.shape[0]
    assert depth == 1
    batch, seq, _ = x_prompt.shape
    nseq, t_new, _ = x_sample.shape
    fg = final_norm.reshape(1, D_MODEL)

    l = 0
    ffn1 = (ffn1_norm[l].reshape(1, D_MODEL), ffn1_w_gate[l], ffn1_w_up[l], ffn1_w_down[l], fg)
    ffn2 = (ffn2_norm[l].reshape(1, D_MODEL), ffn2_w_gate[l], ffn2_w_up[l], ffn2_w_down[l], fg)
    lam, bd, cd = _s5_operators(log_dt[l], a_re[l], a_im[l], b_re[l], b_im[l], c_re[l], c_im[l])
    mix_w = (mix_norm[l].reshape(1, D_MODEL), w_in[l].astype(BF16), w_out[l].astype(BF16))
    ssm_w = (lam, bd, cd, d_skip[l].reshape(1, D_SSM), w_glu[l].astype(BF16),
             b_glu[l].reshape(1, D_SSM))
    sinks_l = sinks[l]

    def window_in(c):
        return jnp.transpose(c, (0, 2, 3, 1)).reshape(c.shape[0], D_KV, c.shape[1])

    def window_out(w):
        n, _, pos = w.shape
        return jnp.transpose(w.reshape(n, N_KV_HEADS, HEAD_DIM, pos), (0, 3, 1, 2))[None]

    def state_in(s):
        return jnp.transpose(s, (1, 2, 0)).reshape(2 * HALF_STATE, s.shape[0])

    def state_out(s):
        return jnp.transpose(s.reshape(N_SSM_GROUPS, SSM_STATE, s.shape[1]), (2, 0, 1))[None]

    relb_t = rel_bias.T
    xp = x_prompt.reshape(batch * seq, D_MODEL)
    ys, yp = _ffn(xp, x_sample, *ffn1, final_norm=False, ys_shape=(nseq * t_new, D_MODEL))
    yp, kp, vp, sre_p, sim_p = _mix_prompt(
        yp.reshape(batch, seq, D_MODEL), relb_t, sinks_l, *mix_w, *ssm_w)
    ys, ks, vs, sre_s, sim_s = _mix_sample(
        ys, relb_t, sinks_l, *mix_w,
        window_in(cache_k[l]), window_in(cache_v[l]),
        state_in(state_ssm_re[l]), state_in(state_ssm_im[l]),
        *ssm_w, nseq=nseq, t_new=t_new)
    y_sample, y_prompt = _ffn(yp.reshape(batch * seq, D_MODEL), ys, *ffn2, final_norm=True,
                              ys_shape=(nseq, t_new, D_MODEL))
    y_prompt = y_prompt.reshape(batch, seq, D_MODEL)

    st_p = (1, batch, N_SSM_GROUPS, SSM_STATE)
    return (y_prompt, y_sample,
            window_out(kp), window_out(vp), sre_p.reshape(st_p), sim_p.reshape(st_p),
            window_out(ks), window_out(vs), state_out(sre_s), state_out(sim_s))
```

```python
import functools
import math

import jax
import jax.numpy as jnp
from jax import lax
from jax.experimental import pallas as pl
from jax.experimental.pallas import tpu as pltpu

F32 = jnp.float32
BF16 = jnp.bfloat16

D_MODEL = 1024
HEAD_DIM = 64
D_ATTN = 512
N_HEADS = 8
N_KV_HEADS = 2
N_REP = 4
D_KV = 128
D_SSM = 512
SSM_GROUP = 16
N_SSM_GROUPS = 32
SSM_STATE = 64
WINDOW = 128
NUM_BUCKETS = 32
MAX_DISTANCE = 128
D_FF = 2816
D_IN = D_ATTN + 2 * D_KV + D_SSM
RMS_EPS = 1e-6
NEG_INF = -1e30
LOG2E = math.log2(math.e)

LANES = 128
SUBLANES = 8
MXU_DIM = 256
FF_CHUNKS = D_FF // MXU_DIM
HALF_GROUPS = N_SSM_GROUPS // 2
HALF_STATE = HALF_GROUPS * SSM_STATE
HALF_U = HALF_GROUPS * SSM_GROUP
U_SLABS = D_SSM // LANES
PITCH = WINDOW + SUBLANES
SAMPLE_BLOCKS = 4
VMEM_LIMIT = 60 * 1024 * 1024


def _const_spec(shape):
    nd = len(shape)
    return pl.BlockSpec(shape, lambda *_: (0,) * nd, pipeline_mode=pl.Buffered(1))


def _smem_spec():
    return pl.BlockSpec(memory_space=pltpu.SMEM)


def _rmsnorm(x, g):
    r = lax.rsqrt(jnp.mean(x * x, axis=-1, keepdims=True) + RMS_EPS)
    return (x * r) * g


def _dot(a, b):
    return jnp.dot(a, b, preferred_element_type=F32)


def _dot_nt(a, b):
    return lax.dot_general(a, b, (((1,), (1,)), ((), ())), preferred_element_type=F32)


def _transpose_blocks(a):
    r, c = a.shape
    return jnp.concatenate(
        [jnp.concatenate([a[i * LANES:(i + 1) * LANES, j * LANES:(j + 1) * LANES].T
                          for i in range(r // LANES)], axis=1)
         for j in range(c // LANES)], axis=0)


def _ffn_tile(x, g_ref, wg_ref, wu_ref, wd_ref, fg_ref, final_norm, before_chunk=None):
    h = _rmsnorm(x, g_ref[...]).astype(BF16)
    acc = None
    for c in range(FF_CHUNKS):
        if before_chunk is not None:
            before_chunk(c)
        sl = slice(c * MXU_DIM, (c + 1) * MXU_DIM)
        gate = _dot(h, wg_ref[:, sl].astype(BF16))
        up = _dot(h, wu_ref[:, sl].astype(BF16))
        a = (gate * jax.nn.sigmoid(gate) * up).astype(BF16)
        part = _dot(a, wd_ref[sl, :].astype(BF16))
        acc = part if acc is None else acc + part
    y = x + 0.5 * acc
    if final_norm:
        y = _rmsnorm(y, fg_ref[...])
    return y


def _ffn_kernel(xs_ref, xp_ref, g_ref, wg_hbm, wu_hbm, wd_hbm, fg_ref, os_ref, op_ref,
                wg_s, wu_s, wd_s, sems, *, final_norm):
    i = pl.program_id(0)
    weights = (g_ref, wg_s, wu_s, wd_s, fg_ref)

    def chunk_copies(c):
        sl = slice(c * MXU_DIM, (c + 1) * MXU_DIM)
        return (pltpu.make_async_copy(wg_hbm.at[:, sl], wg_s.at[:, sl], sems.at[0, c]),
                pltpu.make_async_copy(wu_hbm.at[:, sl], wu_s.at[:, sl], sems.at[1, c]),
                pltpu.make_async_copy(wd_hbm.at[sl, :], wd_s.at[sl, :], sems.at[2, c]))

    @pl.when(i == 0)
    def _():
        for c in range(FF_CHUNKS):
            for cp in chunk_copies(c):
                cp.start()

        def wait_chunk(c):
            for cp in chunk_copies(c):
                cp.wait()

        xs = xs_ref[...].reshape(-1, D_MODEL)
        os_ref[...] = _ffn_tile(xs, *weights, final_norm, wait_chunk).reshape(os_ref.shape)

    @pl.when(i > 0)
    def _():
        op_ref[...] = _ffn_tile(xp_ref[...], *weights, final_norm)


def _ffn(xp, xs, g, wg, wu, wd, fg, *, final_norm, ys_shape):
    n_p, n_s = xp.shape[0], math.prod(xs.shape[:-1])
    zeros = lambda nd: (lambda i: (0,) * nd)
    tm = n_s
    prompt_steps = n_p // tm
    assert prompt_steps * tm == n_p
    prompt_block = lambda i: (jnp.maximum(i - 1, 0), 0)
    hbm = pl.BlockSpec(memory_space=pl.ANY)
    return pl.pallas_call(
        functools.partial(_ffn_kernel, final_norm=final_norm),
        grid=(prompt_steps + 1,),
        in_specs=[
            _const_spec(xs.shape),
            pl.BlockSpec((tm, D_MODEL), prompt_block),
            _const_spec((1, D_MODEL)),
            hbm, hbm, hbm,
            _const_spec((1, D_MODEL)),
        ],
        out_specs=[
            pl.BlockSpec(ys_shape, zeros(len(ys_shape))),
            pl.BlockSpec((tm, D_MODEL), prompt_block),
        ],
        out_shape=[
            jax.ShapeDtypeStruct(ys_shape, F32),
            jax.ShapeDtypeStruct((n_p, D_MODEL), F32),
        ],
        scratch_shapes=[
            pltpu.VMEM((D_MODEL, D_FF), F32),
            pltpu.VMEM((D_MODEL, D_FF), F32),
            pltpu.VMEM((D_FF, D_MODEL), F32),
            pltpu.SemaphoreType.DMA((3, FF_CHUNKS)),
        ],
        compiler_params=pltpu.CompilerParams(
            dimension_semantics=("arbitrary",), vmem_limit_bytes=VMEM_LIMIT),
        name="ffn_final" if final_norm else "ffn",
    )(xs, xp, g, wg, wu, wd, fg)


def _t5_bucket(d):
    d = jnp.maximum(d, 0)
    max_exact = NUM_BUCKETS // 2
    df = jnp.maximum(d, 1).astype(F32)
    large = max_exact + (jnp.log(df / max_exact) / math.log(MAX_DISTANCE / max_exact)
                         * (NUM_BUCKETS - max_exact)).astype(jnp.int32)
    large = jnp.minimum(large, NUM_BUCKETS - 1)
    return jnp.where(d < max_exact, d, large)


def _masked_bias(relb_ref, head, d, valid, scale=1.0):
    bucket = _t5_bucket(d)
    b = jnp.zeros(d.shape, F32)
    for k in range(NUM_BUCKETS):
        b = jnp.where(bucket == k, relb_ref[head, k] * scale, b)
    return jnp.where(valid, b, NEG_INF)


def _kv_lane_mask(t, g):
    lane = lax.broadcasted_iota(jnp.int32, t.shape, 1)
    return jnp.where((lane >= g * HEAD_DIM) & (lane < (g + 1) * HEAD_DIM), t, jnp.zeros_like(t))


def _stack_queries(q_chunks, g, scale=HEAD_DIM ** -0.5):
    qs = []
    for r in range(N_REP):
        qc = q_chunks[2 * g + r // 2]
        if r % 2 != g:
            qc = pltpu.roll(qc, HEAD_DIM, 1)
        qs.append(qc)
    return (jnp.concatenate(qs, axis=0) * scale).astype(BF16)


def _sink_column(sinks_ref, g, rows):
    return jnp.concatenate(
        [jnp.full((rows, 1), sinks_ref[N_REP * g + r], F32) for r in range(N_REP)], axis=0)


def _unstack_heads(o_by_g, rows):
    lane = lax.broadcasted_iota(jnp.int32, (rows, LANES), 1)
    chunks = []
    for c in range(D_ATTN // LANES):
        g = c // 2
        halves = []
        for half in range(2):
            r = 2 * (c % 2) + half
            piece = o_by_g[g][r * rows:(r + 1) * rows]
            if half != g:
                piece = pltpu.roll(piece, HEAD_DIM, 1)
            halves.append(piece)
        chunks.append(jnp.where(lane < HEAD_DIM, halves[0], halves[1]))
    return chunks


def _glu_tail(y, u, dskip_ref, wglu_ref, bglu_ref):
    y = y + dskip_ref[...] * u
    y = jax.nn.gelu(y)
    z = _dot(y.astype(BF16), wglu_ref[...]) + bglu_ref[...]
    return y * jax.nn.sigmoid(z)


def _mixp_kernel(relb_ref, sinks_ref, x_ref, g_ref, win_ref, wout_ref, lam_ref, bd_ref, cd_ref,
                 dskip_ref, wglu_ref, bglu_ref,
                 o_ref, kp_ref, vp_ref, sre_ref, sim_ref,
                 p_s, kband, vband_t, bias_t, bias_later, us, ys, state, attn_s, bx_s, ut_s,
                 *, nb, blk):
    i = pl.program_id(0)
    rows = nb * blk

    @pl.when(i == 0)
    def _():
        kband[...] = jnp.zeros(kband.shape, BF16)
        vband_t[...] = jnp.zeros(vband_t.shape, BF16)
        state[...] = jnp.zeros(state.shape, F32)

    @pl.when(i == 0)
    def _():
        kj = lax.broadcasted_iota(jnp.int32, (2 * blk, blk), 0)
        qi = lax.broadcasted_iota(jnp.int32, (2 * blk, blk), 1)
        d = qi - kj + blk
        valid = (d >= 0) & (d < WINDOW)
        for h in range(N_HEADS):
            cols = slice((h % N_REP) * blk, (h % N_REP + 1) * blk)
            tile = _masked_bias(relb_ref, h, d, valid, scale=LOG2E)
            bias_later[h // N_REP, :, cols] = tile
            bias_t[h // N_REP, :, cols] = jnp.where(kj >= blk, tile, NEG_INF)

    @pl.when(i == 1)
    def _():
        bias_t[...] = bias_later[...]

    h = _rmsnorm(x_ref[...].reshape(rows, D_MODEL), g_ref[...]).astype(BF16)
    u_off = D_ATTN + 2 * D_KV
    p = _dot(h, win_ref[...])
    p_s[...] = p[:, 0:u_off]
    for s in range(U_SLABS):
        for b in range(nb):
            us[s, b * PITCH:b * PITCH + blk, :] = (
                p[b * blk:(b + 1) * blk, u_off + s * LANES:u_off + (s + 1) * LANES])

    kband[:, blk:2 * blk, :] = (
        p_s[:, D_ATTN:D_ATTN + D_KV].reshape(nb, blk, D_KV).astype(BF16))

    def attention_all():
        for b in range(nb):
            vband_t[b, :, blk:2 * blk] = (
                p_s[b * blk:(b + 1) * blk, D_ATTN + D_KV:D_ATTN + 2 * D_KV].T.astype(BF16))
        kb = kband[...]
        vt = vband_t[...]
        feat = lax.broadcasted_iota(jnp.int32, vt.shape, 1)
        lane = lax.broadcasted_iota(jnp.int32, kb.shape, 2)
        q_chunks = [p_s[:, c * LANES:(c + 1) * LANES].reshape(nb, blk, LANES)
                    for c in range(D_ATTN // LANES)]
        o_by_g = []
        for g in range(N_KV_HEADS):
            qs = []
            for r in range(N_REP):
                qc = q_chunks[2 * g + r // 2]
                if r % 2 != g:
                    qc = pltpu.roll(qc, HEAD_DIM, 2)
                qs.append(qc)
            q = (jnp.concatenate(qs, axis=1) * (LOG2E * HEAD_DIM ** -0.5)).astype(BF16)
            kg = jnp.where((lane >= g * HEAD_DIM) & (lane < (g + 1) * HEAD_DIM), kb,
                           jnp.zeros_like(kb))
            st = lax.dot_general(kg, q, (((2,), (2,)), ((0,), (0,))),
                                 preferred_element_type=F32) + bias_t[g]
            s = jnp.concatenate(
                [jnp.full((1, 1, blk), sinks_ref[N_REP * g + r] * LOG2E, F32)
                 for r in range(N_REP)], axis=2)
            m = jnp.maximum(jnp.max(st, axis=1, keepdims=True), s)
            e = jnp.exp2(st - m).astype(BF16)
            in_g = (feat >= g * HEAD_DIM) & (feat < (g + 1) * HEAD_DIM)
            ot = lax.dot_general(jnp.where(in_g, vt, jnp.ones_like(vt)), e,
                                 (((2,), (1,)), ((0,), (0,))),
                                 preferred_element_type=F32)
            other = (1 - g) * HEAD_DIM
            denom = ot[:, other:other + 1, :] + jnp.exp2(s - m)
            o_by_g.append(ot[:, g * HEAD_DIM:(g + 1) * HEAD_DIM, :] / denom)
        for b in range(nb):
            for c in range(D_ATTN // LANES):
                og = o_by_g[c // 2][b]
                cc = 2 * (c % 2)
                chunk_t = jnp.concatenate(
                    [og[:, cc * blk:(cc + 1) * blk], og[:, (cc + 1) * blk:(cc + 2) * blk]], axis=0)
                attn_s[b * blk:(b + 1) * blk, c * LANES:(c + 1) * LANES] = chunk_t.T

    def scan_gather():
        steps = []
        for t in range(blk):
            steps.append(jnp.concatenate(
                [us[s, pl.ds(t, nb, stride=PITCH), :] for s in range(U_SLABS)], axis=1))
        u_block = jnp.concatenate(steps, axis=0)
        ut_s[...] = u_block
        return u_block.astype(BF16)

    def scan_half(hf):
        ring = bx_s
        re = slice(2 * hf * HALF_STATE, (2 * hf + 1) * HALF_STATE)
        im = slice((2 * hf + 1) * HALF_STATE, (2 * hf + 2) * HALF_STATE)
        both = slice(2 * hf * HALF_STATE, 2 * (hf + 1) * HALF_STATE)
        lr = jnp.broadcast_to(lam_ref[2 * hf:2 * hf + 1, :], (nb, HALF_STATE))
        li = jnp.broadcast_to(lam_ref[2 * hf + 1:2 * hf + 2, :], (nb, HALF_STATE))
        xr = state[:, re]
        xi = state[:, im]
        for t in range(blk):
            rows_t = slice(t * nb, (t + 1) * nb)
            xr, xi = (lr * xr - li * xi + ring[rows_t, re],
                      lr * xi + li * xr + ring[rows_t, im])
            ring[rows_t, re] = xr
            ring[rows_t, im] = xi
        state[:, re] = xr
        state[:, im] = xi
        return _dot(ring[:, both].astype(BF16), cd_ref[hf])

    attention_all()
    ub = scan_gather()
    for hf in range(2):
        bx_s[:, 2 * hf * HALF_STATE:2 * (hf + 1) * HALF_STATE] = _dot(
            ub[:, hf * HALF_U:(hf + 1) * HALF_U], bd_ref[hf])
    y = jnp.concatenate([scan_half(hf) for hf in range(2)], axis=1)
    ssm_t = _glu_tail(y, ut_s[...], dskip_ref, wglu_ref, bglu_ref)
    for t in range(blk):
        for s in range(U_SLABS):
            ys[s, pl.ds(t, nb, stride=PITCH), :] = ssm_t[t * nb:(t + 1) * nb, s * LANES:(s + 1) * LANES]

    kband[:, 0:blk, :] = kband[:, blk:2 * blk, :]
    vband_t[:, :, 0:blk] = vband_t[:, :, blk:2 * blk]
    for hf in range(2):
        sre_ref[:, hf * HALF_STATE:(hf + 1) * HALF_STATE] = (
            state[:, 2 * hf * HALF_STATE:(2 * hf + 1) * HALF_STATE])
        sim_ref[:, hf * HALF_STATE:(hf + 1) * HALF_STATE] = (
            state[:, (2 * hf + 1) * HALF_STATE:(2 * hf + 2) * HALF_STATE])

    ssm = jnp.concatenate(
        [jnp.concatenate([ys[s, b * PITCH:b * PITCH + blk, :] for b in range(nb)], axis=0)
         for s in range(U_SLABS)], axis=1)
    mix = (_dot(attn_s[...].astype(BF16), wout_ref[0:D_ATTN, :])
           + _dot(ssm.astype(BF16), wout_ref[D_ATTN:D_ATTN + D_SSM, :]))
    o_ref[...] = x_ref[...] + mix.reshape(nb, blk, D_MODEL)

    @pl.when(i == pl.num_programs(0) - 1)
    def _():
        for b in range(nb):
            rows_b = slice(b * blk, (b + 1) * blk)
            kp_ref[b] = p_s[rows_b, D_ATTN:D_ATTN + D_KV].T
            vp_ref[b] = p_s[rows_b, D_ATTN + D_KV:D_ATTN + 2 * D_KV].T


def _mix_prompt(x, relb, sinks, g, win, wout, lam, bd, cd, dskip, wglu, bglu):
    nb, seq, _ = x.shape
    blk = WINDOW
    rows = nb * blk
    const2 = _const_spec
    return pl.pallas_call(
        functools.partial(_mixp_kernel, nb=nb, blk=blk),
        grid=(seq // blk,),
        in_specs=[
            _smem_spec(), _smem_spec(),
            pl.BlockSpec((nb, blk, D_MODEL), lambda i: (0, i, 0)),
            const2((1, D_MODEL)),
            const2((D_MODEL, D_IN)),
            const2((D_ATTN + D_SSM, D_MODEL)),
            const2((4, HALF_STATE)),
            const2((2, HALF_U, 2 * HALF_STATE)),
            const2((2, 2 * HALF_STATE, HALF_U)),
            const2((1, D_SSM)),
            const2((D_SSM, D_SSM)),
            const2((1, D_SSM)),
        ],
        out_specs=[
            pl.BlockSpec((nb, blk, D_MODEL), lambda i: (0, i, 0)),
            pl.BlockSpec((nb, blk, D_KV), lambda i: (0, 0, 0)),
            pl.BlockSpec((nb, blk, D_KV), lambda i: (0, 0, 0)),
            pl.BlockSpec((nb, 2 * HALF_STATE), lambda i: (0, 0)),
            pl.BlockSpec((nb, 2 * HALF_STATE), lambda i: (0, 0)),
        ],
        out_shape=[
            jax.ShapeDtypeStruct((nb, seq, D_MODEL), F32),
            jax.ShapeDtypeStruct((nb, blk, D_KV), F32),
            jax.ShapeDtypeStruct((nb, blk, D_KV), F32),
            jax.ShapeDtypeStruct((nb, 2 * HALF_STATE), F32),
            jax.ShapeDtypeStruct((nb, 2 * HALF_STATE), F32),
        ],
        scratch_shapes=[
            pltpu.VMEM((rows, D_ATTN + 2 * D_KV), F32),
            pltpu.VMEM((nb, 2 * blk, D_KV), BF16),
            pltpu.VMEM((nb, D_KV, 2 * blk), BF16),
            pltpu.VMEM((N_KV_HEADS, 2 * blk, N_REP * blk), F32),
            pltpu.VMEM((N_KV_HEADS, 2 * blk, N_REP * blk), F32),
            pltpu.VMEM((U_SLABS, nb * PITCH, LANES), F32),
            pltpu.VMEM((U_SLABS, nb * PITCH, LANES), F32),
            pltpu.VMEM((nb, 4 * HALF_STATE), F32),
            pltpu.VMEM((rows, D_ATTN), F32),
            pltpu.VMEM((rows, 4 * HALF_STATE), F32),
            pltpu.VMEM((rows, D_SSM), F32),
        ],
        compiler_params=pltpu.CompilerParams(
            dimension_semantics=("arbitrary",), vmem_limit_bytes=VMEM_LIMIT),
        name="mix_prompt",
    )(relb, sinks, x, g, win, wout, lam, bd, cd, dskip, wglu, bglu)


def _mixs_kernel(relb_ref, sinks_ref, x_ref, g_ref, win_ref, wout_hbm, ck_ref, cv_ref,
                 sre0_ref, sim0_ref, lam_ref, bd_hbm, cd_hbm, dskip_ref, wglu_ref, bglu_ref,
                 o_ref, ks_ref, vs_ref, sre_ref, sim_ref,
                 p_s, attn_s, ssm_s, bias_c, bias_n, us, wout_ref, bd_ref, cd_ref, sems,
                 *, nseq, t_new, sb, wbuf):
    i = pl.program_id(0)
    nsteps = pl.num_programs(0)
    late_copies = {
        "bd": pltpu.make_async_copy(bd_hbm, bd_ref, sems.at[0]),
        "cd": pltpu.make_async_copy(cd_hbm, cd_ref, sems.at[1]),
        "wout": pltpu.make_async_copy(wout_hbm, wout_ref, sems.at[2]),
    }
    rows = nseq * t_new
    brow = sb * t_new
    u_off = D_ATTN + 2 * D_KV
    tshift = t_new.bit_length() - 1
    wshift = wbuf.bit_length() - 1

    @pl.when(i == 0)
    def _():
        for cp in late_copies.values():
            cp.start()
        x = x_ref[...]
        h = _rmsnorm(x, g_ref[...]).astype(BF16)
        p_s[...] = _dot(h, win_ref[...])

        for s in range(U_SLABS):
            us[s] = p_s[:, u_off + s * LANES:u_off + (s + 1) * LANES]
        uperm = jnp.concatenate(
            [jnp.concatenate([us[s, pl.ds(t, nseq, stride=t_new), :] for t in range(t_new)], axis=0)
             for s in range(U_SLABS)], axis=1)
        yparts = []
        late_copies["bd"].wait()
        bus = [_dot(uperm[:, hf * HALF_U:(hf + 1) * HALF_U].astype(BF16), bd_ref[hf])
               for hf in range(2)]
        late_copies["cd"].wait()
        for hf in range(2):
            bu = bus[hf]
            lr = lam_ref[2 * hf:2 * hf + 1, :]
            li = lam_ref[2 * hf + 1:2 * hf + 2, :]
            xr = _transpose_blocks(sre0_ref[hf * HALF_STATE:(hf + 1) * HALF_STATE, :])
            xi = _transpose_blocks(sim0_ref[hf * HALF_STATE:(hf + 1) * HALF_STATE, :])
            states = []
            for t in range(t_new):
                br = bu[t * nseq:(t + 1) * nseq, 0:HALF_STATE]
                bi = bu[t * nseq:(t + 1) * nseq, HALF_STATE:2 * HALF_STATE]
                xr, xi = lr * xr - li * xi + br, lr * xi + li * xr + bi
                states.append(jnp.concatenate([xr, xi], axis=1).astype(BF16))
            sre_ref[hf * HALF_STATE:(hf + 1) * HALF_STATE, :] = _transpose_blocks(xr)
            sim_ref[hf * HALF_STATE:(hf + 1) * HALF_STATE, :] = _transpose_blocks(xi)
            yparts.append(_dot(jnp.concatenate(states, axis=0), cd_ref[hf]))
        yperm = jnp.concatenate(yparts, axis=1)
        for s in range(U_SLABS):
            for t in range(t_new):
                us[s, pl.ds(t, nseq, stride=t_new), :] = (
                    yperm[t * nseq:(t + 1) * nseq, s * LANES:(s + 1) * LANES])
        y = jnp.concatenate([us[s] for s in range(U_SLABS)], axis=1)
        ssm_s[...] = _glu_tail(y, p_s[:, u_off:u_off + D_SSM], dskip_ref, wglu_ref, bglu_ref)

        ncol = sb * wbuf
        rho = lax.broadcasted_iota(jnp.int32, (brow, ncol), 0)
        kap = lax.broadcasted_iota(jnp.int32, (brow, ncol), 1)
        same_seq = (rho >> tshift) == (kap >> wshift)
        rho_w = lax.broadcasted_iota(jnp.int32, (brow, wbuf), 0)
        d_c = (rho_w & (t_new - 1)) - lax.broadcasted_iota(jnp.int32, (brow, wbuf), 1) + wbuf
        valid_c = (d_c >= 0) & (d_c < WINDOW)
        rho_n = lax.broadcasted_iota(jnp.int32, (brow, LANES), 0)
        kap_n = lax.broadcasted_iota(jnp.int32, (brow, LANES), 1)
        tq_n = rho_n & (t_new - 1)
        d_n = tq_n - (kap_n & (t_new - 1))
        valid_n = ((rho_n >> tshift) == (kap_n >> tshift)) & (d_n >= 0) & (kap_n < brow)
        for hd in range(N_HEADS):
            g, r = hd // N_REP, hd % N_REP
            tile = _masked_bias(relb_ref, hd, d_c, valid_c)
            bias_c[g, r * brow:(r + 1) * brow, :] = jnp.where(
                same_seq, jnp.concatenate([tile] * sb, axis=1), NEG_INF)
            bias_n[g, r * brow:(r + 1) * brow, :] = _masked_bias(relb_ref, hd, d_n, valid_n)

    blocks = range(SAMPLE_BLOCKS)
    row0 = [pl.multiple_of((i * SAMPLE_BLOCKS + k) * brow, brow) for k in blocks]
    pad = jnp.zeros((LANES - brow, D_KV), F32)
    kn_pad = [jnp.concatenate([p_s[pl.ds(row0[k], brow), D_ATTN:D_ATTN + D_KV], pad], axis=0)
              for k in blocks]
    vn_pad = [jnp.concatenate(
        [p_s[pl.ds(row0[k], brow), D_ATTN + D_KV:D_ATTN + 2 * D_KV], pad], axis=0) for k in blocks]
    kc_t = jnp.stack([jnp.concatenate([ck_ref[k * sb + b] for b in range(sb)], axis=1)
                      for k in blocks]).astype(BF16)
    vc_t = jnp.stack([jnp.concatenate([cv_ref[k * sb + b] for b in range(sb)], axis=1)
                      for k in blocks]).astype(BF16)
    kn_b = jnp.stack(kn_pad).astype(BF16)
    vn_b = jnp.stack(vn_pad).astype(BF16)
    q_chunks = [[p_s[pl.ds(row0[k], brow), c * LANES:(c + 1) * LANES]
                 for c in range(D_ATTN // LANES)] for k in blocks]
    batch = ((0,), (0,))
    o_by_g = []
    for g in range(N_KV_HEADS):
        q = jnp.stack([_kv_lane_mask(_stack_queries(q_chunks[k], g), g) for k in blocks])
        lc = lax.dot_general(q, kc_t, (((2,), (1,)), batch), preferred_element_type=F32) + bias_c[g]
        ln = lax.dot_general(q, kn_b, (((2,), (2,)), batch), preferred_element_type=F32) + bias_n[g]
        s = _sink_column(sinks_ref, g, brow)
        m = jnp.maximum(jnp.maximum(jnp.max(lc, axis=-1, keepdims=True),
                                    jnp.max(ln, axis=-1, keepdims=True)), s)
        ec = jnp.exp(lc - m)
        en = jnp.exp(ln - m)
        denom = (jnp.sum(ec, axis=-1, keepdims=True) + jnp.sum(en, axis=-1, keepdims=True)
                 + jnp.exp(s - m))
        o = (lax.dot_general(ec.astype(BF16), vc_t, (((2,), (2,)), batch),
                             preferred_element_type=F32)
             + lax.dot_general(en.astype(BF16), vn_b, (((2,), (1,)), batch),
                               preferred_element_type=F32))
        o_by_g.append(o / denom)
    for k in blocks:
        for c, chunk in enumerate(_unstack_heads([o[k] for o in o_by_g], brow)):
            attn_s[pl.ds(row0[k], brow), c * LANES:(c + 1) * LANES] = chunk

    lane = lax.broadcasted_iota(jnp.int32, (D_KV, wbuf), 1)
    keep = wbuf - t_new
    for new_pads, old_ref, out_ref in ((kn_pad, ck_ref, ks_ref), (vn_pad, cv_ref, vs_ref)):
        for k in blocks:
            new_t = new_pads[k].T
            for b in range(sb):
                shifted = pltpu.roll(old_ref[k * sb + b], keep, 1)
                appended = pltpu.roll(new_t, (keep - b * t_new) % wbuf, 1)
                out_ref[k * sb + b] = jnp.where(lane >= keep, appended, shifted)

    @pl.when(i == nsteps - 1)
    def _():
        late_copies["wout"].wait()
        mix = (_dot(attn_s[...].astype(BF16), wout_ref[0:D_ATTN, :])
               + _dot(ssm_s[...].astype(BF16), wout_ref[D_ATTN:D_ATTN + D_SSM, :]))
        o_ref[...] = x_ref[...] + mix


def _mix_sample(x, relb, sinks, g, win, wout, ck, cv, sre0, sim0, lam, bd, cd, dskip, wglu, bglu,
                *, nseq, t_new):
    rows = nseq * t_new
    wbuf = ck.shape[2]
    assert wbuf == LANES
    sb = SUBLANES
    brow = sb * t_new
    hbm = pl.BlockSpec(memory_space=pl.ANY)
    return pl.pallas_call(
        functools.partial(_mixs_kernel, nseq=nseq, t_new=t_new, sb=sb, wbuf=wbuf),
        grid=(nseq // (SAMPLE_BLOCKS * sb),),
        in_specs=[
            _smem_spec(), _smem_spec(),
            _const_spec((rows, D_MODEL)),
            _const_spec((1, D_MODEL)),
            _const_spec((D_MODEL, D_IN)),
            hbm,
            pl.BlockSpec((SAMPLE_BLOCKS * sb, D_KV, wbuf), lambda i: (i, 0, 0)),
            pl.BlockSpec((SAMPLE_BLOCKS * sb, D_KV, wbuf), lambda i: (i, 0, 0)),
            _const_spec((2 * HALF_STATE, nseq)),
            _const_spec((2 * HALF_STATE, nseq)),
            _const_spec((4, HALF_STATE)),
            hbm,
            hbm,
            _const_spec((1, D_SSM)),
            _const_spec((D_SSM, D_SSM)),
            _const_spec((1, D_SSM)),
        ],
        out_specs=[
            pl.BlockSpec((rows, D_MODEL), lambda i: (0, 0)),
            pl.BlockSpec((SAMPLE_BLOCKS * sb, D_KV, wbuf), lambda i: (i, 0, 0)),
            pl.BlockSpec((SAMPLE_BLOCKS * sb, D_KV, wbuf), lambda i: (i, 0, 0)),
            pl.BlockSpec((2 * HALF_STATE, nseq), lambda i: (0, 0)),
            pl.BlockSpec((2 * HALF_STATE, nseq), lambda i: (0, 0)),
        ],
        out_shape=[
            jax.ShapeDtypeStruct((rows, D_MODEL), F32),
            jax.ShapeDtypeStruct((nseq, D_KV, wbuf), F32),
            jax.ShapeDtypeStruct((nseq, D_KV, wbuf), F32),
            jax.ShapeDtypeStruct((2 * HALF_STATE, nseq), F32),
            jax.ShapeDtypeStruct((2 * HALF_STATE, nseq), F32),
        ],
        scratch_shapes=[
            pltpu.VMEM((rows, D_IN), F32),
            pltpu.VMEM((rows, D_ATTN), F32),
            pltpu.VMEM((rows, D_SSM), F32),
            pltpu.VMEM((N_KV_HEADS, N_REP * brow, sb * wbuf), F32),
            pltpu.VMEM((N_KV_HEADS, N_REP * brow, LANES), F32),
            pltpu.VMEM((U_SLABS, rows, LANES), F32),
            pltpu.VMEM((D_ATTN + D_SSM, D_MODEL), BF16),
            pltpu.VMEM((2, HALF_U, 2 * HALF_STATE), BF16),
            pltpu.VMEM((2, 2 * HALF_STATE, HALF_U), BF16),
            pltpu.SemaphoreType.DMA((3,)),
        ],
        compiler_params=pltpu.CompilerParams(
            dimension_semantics=("arbitrary",), vmem_limit_bytes=VMEM_LIMIT),
        name="mix_sample",
    )(relb, sinks, x, g, win, wout, ck, cv, sre0, sim0, lam, bd, cd, dskip, wglu, bglu)


def _s5_operators(log_dt, a_re, a_im, b_re, b_im, c_re, c_im):
    dt = jnp.exp(log_dt)[:, None]
    mag = jnp.exp(a_re * dt)
    lb_re = mag * jnp.cos(a_im * dt)
    lb_im = mag * jnp.sin(a_im * dt)
    den = a_re * a_re + a_im * a_im
    nr = lb_re - 1.0
    q_re = (nr * a_re + lb_im * a_im) / den
    q_im = (lb_im * a_re - nr * a_im) / den
    bb_re = q_re[..., None] * b_re - q_im[..., None] * b_im
    bb_im = q_re[..., None] * b_im + q_im[..., None] * b_re
    eye = jnp.eye(HALF_GROUPS, dtype=BF16)
    lam = jnp.stack([lb_re.reshape(2, HALF_STATE), lb_im.reshape(2, HALF_STATE)], axis=1)
    bb = jnp.stack([bb_re, bb_im]).astype(BF16).reshape(2, 2, HALF_GROUPS, SSM_STATE, SSM_GROUP)
    bd = jnp.einsum('rhgpc,gk->hgcrkp', bb, eye).reshape(2, HALF_U, 2 * HALF_STATE)
    cc = jnp.stack([c_re, -c_im]).astype(BF16).reshape(2, 2, HALF_GROUPS, SSM_GROUP, SSM_STATE)
    cd = jnp.einsum('rhgcp,gk->hrgpkc', cc, eye).reshape(2, 2 * HALF_STATE, HALF_U)
    return lam.reshape(4, HALF_STATE), bd, cd


def kernel(x_prompt, x_sample, cache_k, cache_v, state_ssm_re, state_ssm_im, rel_bias,
           ffn1_norm, ffn1_w_gate, ffn1_w_up, ffn1_w_down, mix_norm, w_in, sinks,
           log_dt, a_re, a_im, b_re, b_im, c_re, c_im, d_skip, w_glu, b_glu, w_out,
           ffn2_norm, ffn2_w_gate, ffn2_w_up, ffn2_w_down, final_norm):
    depth = w_in.shape[0]
    assert depth == 1
    batch, seq, _ = x_prompt.shape
    nseq, t_new, _ = x_sample.shape
    fg = final_norm.reshape(1, D_MODEL)

    l = 0
    ffn1 = (ffn1_norm[l].reshape(1, D_MODEL), ffn1_w_gate[l], ffn1_w_up[l], ffn1_w_down[l], fg)
    ffn2 = (ffn2_norm[l].reshape(1, D_MODEL), ffn2_w_gate[l], ffn2_w_up[l], ffn2_w_down[l], fg)
    lam, bd, cd = _s5_operators(log_dt[l], a_re[l], a_im[l], b_re[l], b_im[l], c_re[l], c_im[l])
    mix_w = (mix_norm[l].reshape(1, D_MODEL), w_in[l].astype(BF16), w_out[l].astype(BF16))
    ssm_w = (lam, bd, cd, d_skip[l].reshape(1, D_SSM), w_glu[l].astype(BF16),
             b_glu[l].reshape(1, D_SSM))
    sinks_l = sinks[l]

    def window_in(c):
        return jnp.transpose(c, (0, 2, 3, 1)).reshape(c.shape[0], D_KV, c.shape[1])

    def window_out(w):
        n, _, pos = w.shape
        return jnp.transpose(w.reshape(n, N_KV_HEADS, HEAD_DIM, pos), (0, 3, 1, 2))[None]

    def state_in(s):
        return jnp.transpose(s, (1, 2, 0)).reshape(2 * HALF_STATE, s.shape[0])

    def state_out(s):
        return jnp.transpose(s.reshape(N_SSM_GROUPS, SSM_STATE, s.shape[1]), (2, 0, 1))[None]

    relb_t = rel_bias.T
    xp = x_prompt.reshape(batch * seq, D_MODEL)
    ys, yp = _ffn(xp, x_sample, *ffn1, final_norm=False, ys_shape=(nseq * t_new, D_MODEL))
    yp, kp, vp, sre_p, sim_p = _mix_prompt(
        yp.reshape(batch, seq, D_MODEL), relb_t, sinks_l, *mix_w, *ssm_w)
    ys, ks, vs, sre_s, sim_s = _mix_sample(
        ys, relb_t, sinks_l, *mix_w,
        window_in(cache_k[l]), window_in(cache_v[l]),
        state_in(state_ssm_re[l]), state_in(state_ssm_im[l]),
        *ssm_w, nseq=nseq, t_new=t_new)
    y_sample, y_prompt = _ffn(yp.reshape(batch * seq, D_MODEL), ys, *ffn2, final_norm=True,
                              ys_shape=(nseq, t_new, D_MODEL))
    y_prompt = y_prompt.reshape(batch, seq, D_MODEL)

    st_p = (1, batch, N_SSM_GROUPS, SSM_STATE)
    return (y_prompt, y_sample,
            window_out(kp), window_out(vp), sre_p.reshape(st_p), sim_p.reshape(st_p),
            window_out(ks), window_out(vs), state_out(sre_s), state_out(sim_s))
```

```python
import functools
import math

import jax
import jax.numpy as jnp
from jax import lax
from jax.experimental import pallas as pl
from jax.experimental.pallas import tpu as pltpu

F32 = jnp.float32
BF16 = jnp.bfloat16

D_MODEL = 1024
HEAD_DIM = 64
D_ATTN = 512
N_HEADS = 8
N_KV_HEADS = 2
N_REP = 4
D_KV = 128
D_SSM = 512
SSM_GROUP = 16
N_SSM_GROUPS = 32
SSM_STATE = 64
WINDOW = 128
NUM_BUCKETS = 32
MAX_DISTANCE = 128
D_FF = 2816
D_IN = D_ATTN + 2 * D_KV + D_SSM
RMS_EPS = 1e-6
NEG_INF = -1e30
LOG2E = math.log2(math.e)

LANES = 128
SUBLANES = 8
MXU_DIM = 256
FF_CHUNKS = D_FF // MXU_DIM
HALF_GROUPS = N_SSM_GROUPS // 2
HALF_STATE = HALF_GROUPS * SSM_STATE
HALF_U = HALF_GROUPS * SSM_GROUP
U_SLABS = D_SSM // LANES
PITCH = WINDOW + SUBLANES
SAMPLE_BLOCKS = 4
VMEM_LIMIT = 60 * 1024 * 1024


def _const_spec(shape):
    nd = len(shape)
    return pl.BlockSpec(shape, lambda *_: (0,) * nd, pipeline_mode=pl.Buffered(1))


def _smem_spec():
    return pl.BlockSpec(memory_space=pltpu.SMEM)


def _rmsnorm(x, g):
    r = lax.rsqrt(jnp.mean(x * x, axis=-1, keepdims=True) + RMS_EPS)
    return (x * r) * g


def _dot(a, b):
    return jnp.dot(a, b, preferred_element_type=F32)


def _dot_nt(a, b):
    return lax.dot_general(a, b, (((1,), (1,)), ((), ())), preferred_element_type=F32)


def _transpose_blocks(a):
    r, c = a.shape
    return jnp.concatenate(
        [jnp.concatenate([a[i * LANES:(i + 1) * LANES, j * LANES:(j + 1) * LANES].T
                          for i in range(r // LANES)], axis=1)
         for j in range(c // LANES)], axis=0)


def _ffn_tile(x, g_ref, wg_ref, wu_ref, wd_ref, fg_ref, final_norm, before_chunk=None):
    h = _rmsnorm(x, g_ref[...]).astype(BF16)
    acc = None
    for c in range(FF_CHUNKS):
        if before_chunk is not None:
            before_chunk(c)
        sl = slice(c * MXU_DIM, (c + 1) * MXU_DIM)
        gate = _dot(h, wg_ref[:, sl].astype(BF16))
        up = _dot(h, wu_ref[:, sl].astype(BF16))
        a = (gate * jax.nn.sigmoid(gate) * up).astype(BF16)
        part = _dot(a, wd_ref[sl, :].astype(BF16))
        acc = part if acc is None else acc + part
    y = x + 0.5 * acc
    if final_norm:
        y = _rmsnorm(y, fg_ref[...])
    return y


def _ffn_kernel(xs_ref, xp_ref, g_ref, wg_hbm, wu_hbm, wd_hbm, fg_ref, os_ref, op_ref,
                wg_s, wu_s, wd_s, sems, *, final_norm):
    i = pl.program_id(0)
    weights = (g_ref, wg_s, wu_s, wd_s, fg_ref)

    def chunk_copies(c):
        sl = slice(c * MXU_DIM, (c + 1) * MXU_DIM)
        return (pltpu.make_async_copy(wg_hbm.at[:, sl], wg_s.at[:, sl], sems.at[0, c]),
                pltpu.make_async_copy(wu_hbm.at[:, sl], wu_s.at[:, sl], sems.at[1, c]),
                pltpu.make_async_copy(wd_hbm.at[sl, :], wd_s.at[sl, :], sems.at[2, c]))

    @pl.when(i == 0)
    def _():
        for c in range(FF_CHUNKS):
            for k, cp in enumerate(chunk_copies(c)):
                cp.start(priority=(3 * c + k) % 2)

        def wait_chunk(c):
            for cp in chunk_copies(c):
                cp.wait()

        xs = xs_ref[...].reshape(-1, D_MODEL)
        os_ref[...] = _ffn_tile(xs, *weights, final_norm, wait_chunk).reshape(os_ref.shape)

    @pl.when(i > 0)
    def _():
        op_ref[...] = _ffn_tile(xp_ref[...], *weights, final_norm)


def _ffn(xp, xs, g, wg, wu, wd, fg, *, final_norm, ys_shape):
    n_p, n_s = xp.shape[0], math.prod(xs.shape[:-1])
    zeros = lambda nd: (lambda i: (0,) * nd)
    tm = n_s
    prompt_steps = n_p // tm
    assert prompt_steps * tm == n_p
    prompt_block = lambda i: (jnp.maximum(i - 1, 0), 0)
    hbm = pl.BlockSpec(memory_space=pl.ANY)
    return pl.pallas_call(
        functools.partial(_ffn_kernel, final_norm=final_norm),
        grid=(prompt_steps + 1,),
        in_specs=[
            _const_spec(xs.shape),
            pl.BlockSpec((tm, D_MODEL), prompt_block),
            _const_spec((1, D_MODEL)),
            hbm, hbm, hbm,
            _const_spec((1, D_MODEL)),
        ],
        out_specs=[
            pl.BlockSpec(ys_shape, zeros(len(ys_shape))),
            pl.BlockSpec((tm, D_MODEL), prompt_block),
        ],
        out_shape=[
            jax.ShapeDtypeStruct(ys_shape, F32),
            jax.ShapeDtypeStruct((n_p, D_MODEL), F32),
        ],
        scratch_shapes=[
            pltpu.VMEM((D_MODEL, D_FF), F32),
            pltpu.VMEM((D_MODEL, D_FF), F32),
            pltpu.VMEM((D_FF, D_MODEL), F32),
            pltpu.SemaphoreType.DMA((3, FF_CHUNKS)),
        ],
        compiler_params=pltpu.CompilerParams(
            dimension_semantics=("arbitrary",), vmem_limit_bytes=VMEM_LIMIT),
        name="ffn_final" if final_norm else "ffn",
    )(xs, xp, g, wg, wu, wd, fg)


def _t5_bucket(d):
    d = jnp.maximum(d, 0)
    max_exact = NUM_BUCKETS // 2
    df = jnp.maximum(d, 1).astype(F32)
    large = max_exact + (jnp.log(df / max_exact) / math.log(MAX_DISTANCE / max_exact)
                         * (NUM_BUCKETS - max_exact)).astype(jnp.int32)
    large = jnp.minimum(large, NUM_BUCKETS - 1)
    return jnp.where(d < max_exact, d, large)


def _masked_bias(relb_ref, head, d, valid, scale=1.0):
    bucket = _t5_bucket(d)
    b = jnp.zeros(d.shape, F32)
    for k in range(NUM_BUCKETS):
        b = jnp.where(bucket == k, relb_ref[head, k] * scale, b)
    return jnp.where(valid, b, NEG_INF)


def _kv_lane_mask(t, g):
    lane = lax.broadcasted_iota(jnp.int32, t.shape, 1)
    return jnp.where((lane >= g * HEAD_DIM) & (lane < (g + 1) * HEAD_DIM), t, jnp.zeros_like(t))


def _stack_queries(q_chunks, g, scale=HEAD_DIM ** -0.5):
    qs = []
    for r in range(N_REP):
        qc = q_chunks[2 * g + r // 2]
        if r % 2 != g:
            qc = pltpu.roll(qc, HEAD_DIM, 1)
        qs.append(qc)
    return (jnp.concatenate(qs, axis=0) * scale).astype(BF16)


def _sink_column(sinks_ref, g, rows):
    return jnp.concatenate(
        [jnp.full((rows, 1), sinks_ref[N_REP * g + r], F32) for r in range(N_REP)], axis=0)


def _unstack_heads(o_by_g, rows):
    lane = lax.broadcasted_iota(jnp.int32, (rows, LANES), 1)
    chunks = []
    for c in range(D_ATTN // LANES):
        g = c // 2
        halves = []
        for half in range(2):
            r = 2 * (c % 2) + half
            piece = o_by_g[g][r * rows:(r + 1) * rows]
            if half != g:
                piece = pltpu.roll(piece, HEAD_DIM, 1)
            halves.append(piece)
        chunks.append(jnp.where(lane < HEAD_DIM, halves[0], halves[1]))
    return chunks


def _glu_tail(y, u, dskip_ref, wglu_ref, bglu_ref):
    y = y + dskip_ref[...] * u
    y = jax.nn.gelu(y)
    z = _dot(y.astype(BF16), wglu_ref[...]) + bglu_ref[...]
    return y * jax.nn.sigmoid(z)


def _mixp_kernel(relb_ref, sinks_ref, x_ref, g_ref, win_ref, wout_ref, lam_ref, bd_ref, cd_ref,
                 dskip_ref, wglu_ref, bglu_ref,
                 o_ref, kp_ref, vp_ref, sre_ref, sim_ref,
                 p_s, kband, vband_t, bias_t, bias_later, us, ys, state, attn_s, bx_s, ut_s,
                 *, nb, blk):
    i = pl.program_id(0)
    rows = nb * blk

    @pl.when(i == 0)
    def _():
        kband[...] = jnp.zeros(kband.shape, BF16)
        vband_t[...] = jnp.zeros(vband_t.shape, BF16)
        state[...] = jnp.zeros(state.shape, F32)

    @pl.when(i == 0)
    def _():
        kj = lax.broadcasted_iota(jnp.int32, (2 * blk, blk), 0)
        qi = lax.broadcasted_iota(jnp.int32, (2 * blk, blk), 1)
        d = qi - kj + blk
        valid = (d >= 0) & (d < WINDOW)
        for h in range(N_HEADS):
            cols = slice((h % N_REP) * blk, (h % N_REP + 1) * blk)
            tile = _masked_bias(relb_ref, h, d, valid, scale=LOG2E)
            bias_later[h // N_REP, :, cols] = tile
            bias_t[h // N_REP, :, cols] = jnp.where(kj >= blk, tile, NEG_INF)

    @pl.when(i == 1)
    def _():
        bias_t[...] = bias_later[...]

    h = _rmsnorm(x_ref[...].reshape(rows, D_MODEL), g_ref[...]).astype(BF16)
    u_off = D_ATTN + 2 * D_KV
    p = _dot(h, win_ref[...])
    p_s[...] = p[:, 0:u_off]
    for s in range(U_SLABS):
        for b in range(nb):
            us[s, b * PITCH:b * PITCH + blk, :] = (
                p[b * blk:(b + 1) * blk, u_off + s * LANES:u_off + (s + 1) * LANES])

    kband[:, blk:2 * blk, :] = (
        p_s[:, D_ATTN:D_ATTN + D_KV].reshape(nb, blk, D_KV).astype(BF16))

    def attention_all():
        for b in range(nb):
            vband_t[b, :, blk:2 * blk] = (
                p_s[b * blk:(b + 1) * blk, D_ATTN + D_KV:D_ATTN + 2 * D_KV].T.astype(BF16))
        kb = kband[...]
        vt = vband_t[...]
        feat = lax.broadcasted_iota(jnp.int32, vt.shape, 1)
        lane = lax.broadcasted_iota(jnp.int32, kb.shape, 2)
        q_chunks = [p_s[:, c * LANES:(c + 1) * LANES].reshape(nb, blk, LANES)
                    for c in range(D_ATTN // LANES)]
        o_by_g = []
        for g in range(N_KV_HEADS):
            qs = []
            for r in range(N_REP):
                qc = q_chunks[2 * g + r // 2]
                if r % 2 != g:
                    qc = pltpu.roll(qc, HEAD_DIM, 2)
                qs.append(qc)
            q = (jnp.concatenate(qs, axis=1) * (LOG2E * HEAD_DIM ** -0.5)).astype(BF16)
            kg = jnp.where((lane >= g * HEAD_DIM) & (lane < (g + 1) * HEAD_DIM), kb,
                           jnp.zeros_like(kb))
            st = lax.dot_general(kg, q, (((2,), (2,)), ((0,), (0,))),
                                 preferred_element_type=F32) + bias_t[g]
            s = jnp.concatenate(
                [jnp.full((1, 1, blk), sinks_ref[N_REP * g + r] * LOG2E, F32)
                 for r in range(N_REP)], axis=2)
            m = jnp.maximum(jnp.max(st, axis=1, keepdims=True), s)
            e = jnp.exp2(st - m).astype(BF16)
            in_g = (feat >= g * HEAD_DIM) & (feat < (g + 1) * HEAD_DIM)
            ot = lax.dot_general(jnp.where(in_g, vt, jnp.ones_like(vt)), e,
                                 (((2,), (1,)), ((0,), (0,))),
                                 preferred_element_type=F32)
            other = (1 - g) * HEAD_DIM
            denom = ot[:, other:other + 1, :] + jnp.exp2(s - m)
            o_by_g.append(ot[:, g * HEAD_DIM:(g + 1) * HEAD_DIM, :] / denom)
        for b in range(nb):
            for c in range(D_ATTN // LANES):
                og = o_by_g[c // 2][b]
                cc = 2 * (c % 2)
                chunk_t = jnp.concatenate(
                    [og[:, cc * blk:(cc + 1) * blk], og[:, (cc + 1) * blk:(cc + 2) * blk]], axis=0)
                attn_s[b * blk:(b + 1) * blk, c * LANES:(c + 1) * LANES] = chunk_t.T

    def scan_gather():
        steps = []
        for t in range(blk):
            steps.append(jnp.concatenate(
                [us[s, pl.ds(t, nb, stride=PITCH), :] for s in range(U_SLABS)], axis=1))
        u_block = jnp.concatenate(steps, axis=0)
        ut_s[...] = u_block
        return u_block.astype(BF16)

    def scan_half(hf):
        ring = bx_s
        re = slice(2 * hf * HALF_STATE, (2 * hf + 1) * HALF_STATE)
        im = slice((2 * hf + 1) * HALF_STATE, (2 * hf + 2) * HALF_STATE)
        both = slice(2 * hf * HALF_STATE, 2 * (hf + 1) * HALF_STATE)
        lr = jnp.broadcast_to(lam_ref[2 * hf:2 * hf + 1, :], (nb, HALF_STATE))
        li = jnp.broadcast_to(lam_ref[2 * hf + 1:2 * hf + 2, :], (nb, HALF_STATE))
        xr = state[:, re]
        xi = state[:, im]
        for t in range(blk):
            rows_t = slice(t * nb, (t + 1) * nb)
            xr, xi = (lr * xr - li * xi + ring[rows_t, re],
                      lr * xi + li * xr + ring[rows_t, im])
            ring[rows_t, re] = xr
            ring[rows_t, im] = xi
        state[:, re] = xr
        state[:, im] = xi
        return _dot(ring[:, both].astype(BF16), cd_ref[hf])

    attention_all()
    ub = scan_gather()
    for hf in range(2):
        bx_s[:, 2 * hf * HALF_STATE:2 * (hf + 1) * HALF_STATE] = _dot(
            ub[:, hf * HALF_U:(hf + 1) * HALF_U], bd_ref[hf])
    y = jnp.concatenate([scan_half(hf) for hf in range(2)], axis=1)
    ssm_t = _glu_tail(y, ut_s[...], dskip_ref, wglu_ref, bglu_ref)
    for t in range(blk):
        for s in range(U_SLABS):
            ys[s, pl.ds(t, nb, stride=PITCH), :] = ssm_t[t * nb:(t + 1) * nb, s * LANES:(s + 1) * LANES]

    kband[:, 0:blk, :] = kband[:, blk:2 * blk, :]
    vband_t[:, :, 0:blk] = vband_t[:, :, blk:2 * blk]
    for hf in range(2):
        sre_ref[:, hf * HALF_STATE:(hf + 1) * HALF_STATE] = (
            state[:, 2 * hf * HALF_STATE:(2 * hf + 1) * HALF_STATE])
        sim_ref[:, hf * HALF_STATE:(hf + 1) * HALF_STATE] = (
            state[:, (2 * hf + 1) * HALF_STATE:(2 * hf + 2) * HALF_STATE])

    ssm = jnp.concatenate(
        [jnp.concatenate([ys[s, b * PITCH:b * PITCH + blk, :] for b in range(nb)], axis=0)
         for s in range(U_SLABS)], axis=1)
    mix = (_dot(attn_s[...].astype(BF16), wout_ref[0:D_ATTN, :])
           + _dot(ssm.astype(BF16), wout_ref[D_ATTN:D_ATTN + D_SSM, :]))
    o_ref[...] = x_ref[...] + mix.reshape(nb, blk, D_MODEL)

    @pl.when(i == pl.num_programs(0) - 1)
    def _():
        for b in range(nb):
            rows_b = slice(b * blk, (b + 1) * blk)
            kp_ref[b] = p_s[rows_b, D_ATTN:D_ATTN + D_KV].T
            vp_ref[b] = p_s[rows_b, D_ATTN + D_KV:D_ATTN + 2 * D_KV].T


def _mix_prompt(x, relb, sinks, g, win, wout, lam, bd, cd, dskip, wglu, bglu):
    nb, seq, _ = x.shape
    blk = WINDOW
    rows = nb * blk
    const2 = _const_spec
    return pl.pallas_call(
        functools.partial(_mixp_kernel, nb=nb, blk=blk),
        grid=(seq // blk,),
        in_specs=[
            _smem_spec(), _smem_spec(),
            pl.BlockSpec((nb, blk, D_MODEL), lambda i: (0, i, 0)),
            const2((1, D_MODEL)),
            const2((D_MODEL, D_IN)),
            const2((D_ATTN + D_SSM, D_MODEL)),
            const2((4, HALF_STATE)),
            const2((2, HALF_U, 2 * HALF_STATE)),
            const2((2, 2 * HALF_STATE, HALF_U)),
            const2((1, D_SSM)),
            const2((D_SSM, D_SSM)),
            const2((1, D_SSM)),
        ],
        out_specs=[
            pl.BlockSpec((nb, blk, D_MODEL), lambda i: (0, i, 0)),
            pl.BlockSpec((nb, blk, D_KV), lambda i: (0, 0, 0)),
            pl.BlockSpec((nb, blk, D_KV), lambda i: (0, 0, 0)),
            pl.BlockSpec((nb, 2 * HALF_STATE), lambda i: (0, 0)),
            pl.BlockSpec((nb, 2 * HALF_STATE), lambda i: (0, 0)),
        ],
        out_shape=[
            jax.ShapeDtypeStruct((nb, seq, D_MODEL), F32),
            jax.ShapeDtypeStruct((nb, blk, D_KV), F32),
            jax.ShapeDtypeStruct((nb, blk, D_KV), F32),
            jax.ShapeDtypeStruct((nb, 2 * HALF_STATE), F32),
            jax.ShapeDtypeStruct((nb, 2 * HALF_STATE), F32),
        ],
        scratch_shapes=[
            pltpu.VMEM((rows, D_ATTN + 2 * D_KV), F32),
            pltpu.VMEM((nb, 2 * blk, D_KV), BF16),
            pltpu.VMEM((nb, D_KV, 2 * blk), BF16),
            pltpu.VMEM((N_KV_HEADS, 2 * blk, N_REP * blk), F32),
            pltpu.VMEM((N_KV_HEADS, 2 * blk, N_REP * blk), F32),
            pltpu.VMEM((U_SLABS, nb * PITCH, LANES), F32),
            pltpu.VMEM((U_SLABS, nb * PITCH, LANES), F32),
            pltpu.VMEM((nb, 4 * HALF_STATE), F32),
            pltpu.VMEM((rows, D_ATTN), F32),
            pltpu.VMEM((rows, 4 * HALF_STATE), F32),
            pltpu.VMEM((rows, D_SSM), F32),
        ],
        compiler_params=pltpu.CompilerParams(
            dimension_semantics=("arbitrary",), vmem_limit_bytes=VMEM_LIMIT),
        name="mix_prompt",
    )(relb, sinks, x, g, win, wout, lam, bd, cd, dskip, wglu, bglu)


def _mixs_kernel(relb_ref, sinks_ref, x_ref, g_ref, win_ref, wout_ref, ck_ref, cv_ref,
                 sre0_ref, sim0_ref, lam_ref, bd_ref, cd_ref, dskip_ref, wglu_ref, bglu_ref,
                 o_ref, ks_ref, vs_ref, sre_ref, sim_ref,
                 p_s, attn_s, ssm_s, bias_c, bias_n, us,
                 *, nseq, t_new, sb, wbuf):
    i = pl.program_id(0)
    nsteps = pl.num_programs(0)
    rows = nseq * t_new
    brow = sb * t_new
    u_off = D_ATTN + 2 * D_KV
    tshift = t_new.bit_length() - 1
    wshift = wbuf.bit_length() - 1

    @pl.when(i == 0)
    def _():
        x = x_ref[...]
        h = _rmsnorm(x, g_ref[...]).astype(BF16)
        p_s[...] = _dot(h, win_ref[...])

        for s in range(U_SLABS):
            us[s] = p_s[:, u_off + s * LANES:u_off + (s + 1) * LANES]
        uperm = jnp.concatenate(
            [jnp.concatenate([us[s, pl.ds(t, nseq, stride=t_new), :] for t in range(t_new)], axis=0)
             for s in range(U_SLABS)], axis=1)
        yparts = []
        bus = [_dot(uperm[:, hf * HALF_U:(hf + 1) * HALF_U].astype(BF16), bd_ref[hf])
               for hf in range(2)]
        for hf in range(2):
            bu = bus[hf]
            lr = lam_ref[2 * hf:2 * hf + 1, :]
            li = lam_ref[2 * hf + 1:2 * hf + 2, :]
            xr = _transpose_blocks(sre0_ref[hf * HALF_STATE:(hf + 1) * HALF_STATE, :])
            xi = _transpose_blocks(sim0_ref[hf * HALF_STATE:(hf + 1) * HALF_STATE, :])
            states = []
            for t in range(t_new):
                br = bu[t * nseq:(t + 1) * nseq, 0:HALF_STATE]
                bi = bu[t * nseq:(t + 1) * nseq, HALF_STATE:2 * HALF_STATE]
                xr, xi = lr * xr - li * xi + br, lr * xi + li * xr + bi
                states.append(jnp.concatenate([xr, xi], axis=1).astype(BF16))
            sre_ref[hf * HALF_STATE:(hf + 1) * HALF_STATE, :] = _transpose_blocks(xr)
            sim_ref[hf * HALF_STATE:(hf + 1) * HALF_STATE, :] = _transpose_blocks(xi)
            yparts.append(_dot(jnp.concatenate(states, axis=0), cd_ref[hf]))
        yperm = jnp.concatenate(yparts, axis=1)
        for s in range(U_SLABS):
            for t in range(t_new):
                us[s, pl.ds(t, nseq, stride=t_new), :] = (
                    yperm[t * nseq:(t + 1) * nseq, s * LANES:(s + 1) * LANES])
        y = jnp.concatenate([us[s] for s in range(U_SLABS)], axis=1)
        ssm_s[...] = _glu_tail(y, p_s[:, u_off:u_off + D_SSM], dskip_ref, wglu_ref, bglu_ref)

        ncol = sb * wbuf
        rho = lax.broadcasted_iota(jnp.int32, (brow, ncol), 0)
        kap = lax.broadcasted_iota(jnp.int32, (brow, ncol), 1)
        same_seq = (rho >> tshift) == (kap >> wshift)
        rho_w = lax.broadcasted_iota(jnp.int32, (brow, wbuf), 0)
        d_c = (rho_w & (t_new - 1)) - lax.broadcasted_iota(jnp.int32, (brow, wbuf), 1) + wbuf
        valid_c = (d_c >= 0) & (d_c < WINDOW)
        rho_n = lax.broadcasted_iota(jnp.int32, (brow, LANES), 0)
        kap_n = lax.broadcasted_iota(jnp.int32, (brow, LANES), 1)
        tq_n = rho_n & (t_new - 1)
        d_n = tq_n - (kap_n & (t_new - 1))
        valid_n = ((rho_n >> tshift) == (kap_n >> tshift)) & (d_n >= 0) & (kap_n < brow)
        for hd in range(N_HEADS):
            g, r = hd // N_REP, hd % N_REP
            tile = _masked_bias(relb_ref, hd, d_c, valid_c)
            bias_c[g, r * brow:(r + 1) * brow, :] = jnp.where(
                same_seq, jnp.concatenate([tile] * sb, axis=1), NEG_INF)
            bias_n[g, r * brow:(r + 1) * brow, :] = _masked_bias(relb_ref, hd, d_n, valid_n)

    blocks = range(SAMPLE_BLOCKS)
    row0 = [pl.multiple_of((i * SAMPLE_BLOCKS + k) * brow, brow) for k in blocks]
    pad = jnp.zeros((LANES - brow, D_KV), F32)
    kn_pad = [jnp.concatenate([p_s[pl.ds(row0[k], brow), D_ATTN:D_ATTN + D_KV], pad], axis=0)
              for k in blocks]
    vn_pad = [jnp.concatenate(
        [p_s[pl.ds(row0[k], brow), D_ATTN + D_KV:D_ATTN + 2 * D_KV], pad], axis=0) for k in blocks]
    kc_t = jnp.stack([jnp.concatenate([ck_ref[k * sb + b] for b in range(sb)], axis=1)
                      for k in blocks]).astype(BF16)
    vc_t = jnp.stack([jnp.concatenate([cv_ref[k * sb + b] for b in range(sb)], axis=1)
                      for k in blocks]).astype(BF16)
    kn_b = jnp.stack(kn_pad).astype(BF16)
    vn_b = jnp.stack(vn_pad).astype(BF16)
    q_chunks = [[p_s[pl.ds(row0[k], brow), c * LANES:(c + 1) * LANES]
                 for c in range(D_ATTN // LANES)] for k in blocks]
    batch = ((0,), (0,))
    o_by_g = []
    for g in range(N_KV_HEADS):
        q = jnp.stack([_kv_lane_mask(_stack_queries(q_chunks[k], g), g) for k in blocks])
        lc = lax.dot_general(q, kc_t, (((2,), (1,)), batch), preferred_element_type=F32) + bias_c[g]
        ln = lax.dot_general(q, kn_b, (((2,), (2,)), batch), preferred_element_type=F32) + bias_n[g]
        s = _sink_column(sinks_ref, g, brow)
        m = jnp.maximum(jnp.maximum(jnp.max(lc, axis=-1, keepdims=True),
                                    jnp.max(ln, axis=-1, keepdims=True)), s)
        ec = jnp.exp(lc - m)
        en = jnp.exp(ln - m)
        denom = (jnp.sum(ec, axis=-1, keepdims=True) + jnp.sum(en, axis=-1, keepdims=True)
                 + jnp.exp(s - m))
        o = (lax.dot_general(ec.astype(BF16), vc_t, (((2,), (2,)), batch),
                             preferred_element_type=F32)
             + lax.dot_general(en.astype(BF16), vn_b, (((2,), (1,)), batch),
                               preferred_element_type=F32))
        o_by_g.append(o / denom)
    for k in blocks:
        for c, chunk in enumerate(_unstack_heads([o[k] for o in o_by_g], brow)):
            attn_s[pl.ds(row0[k], brow), c * LANES:(c + 1) * LANES] = chunk

    lane = lax.broadcasted_iota(jnp.int32, (D_KV, wbuf), 1)
    keep = wbuf - t_new
    for new_pads, old_ref, out_ref in ((kn_pad, ck_ref, ks_ref), (vn_pad, cv_ref, vs_ref)):
        for k in blocks:
            new_t = new_pads[k].T
            for b in range(sb):
                shifted = pltpu.roll(old_ref[k * sb + b], keep, 1)
                appended = pltpu.roll(new_t, (keep - b * t_new) % wbuf, 1)
                out_ref[k * sb + b] = jnp.where(lane >= keep, appended, shifted)

    @pl.when(i == nsteps - 1)
    def _():
        mix = (_dot(attn_s[...].astype(BF16), wout_ref[0:D_ATTN, :])
               + _dot(ssm_s[...].astype(BF16), wout_ref[D_ATTN:D_ATTN + D_SSM, :]))
        o_ref[...] = x_ref[...] + mix


def _mix_sample(x, relb, sinks, g, win, wout, ck, cv, sre0, sim0, lam, bd, cd, dskip, wglu, bglu,
                *, nseq, t_new):
    rows = nseq * t_new
    wbuf = ck.shape[2]
    assert wbuf == LANES
    sb = SUBLANES
    brow = sb * t_new
    return pl.pallas_call(
        functools.partial(_mixs_kernel, nseq=nseq, t_new=t_new, sb=sb, wbuf=wbuf),
        grid=(nseq // (SAMPLE_BLOCKS * sb),),
        in_specs=[
            _smem_spec(), _smem_spec(),
            _const_spec((rows, D_MODEL)),
            _const_spec((1, D_MODEL)),
            _const_spec((D_MODEL, D_IN)),
            _const_spec((D_ATTN + D_SSM, D_MODEL)),
            pl.BlockSpec((SAMPLE_BLOCKS * sb, D_KV, wbuf), lambda i: (i, 0, 0)),
            pl.BlockSpec((SAMPLE_BLOCKS * sb, D_KV, wbuf), lambda i: (i, 0, 0)),
            _const_spec((2 * HALF_STATE, nseq)),
            _const_spec((2 * HALF_STATE, nseq)),
            _const_spec((4, HALF_STATE)),
            _const_spec((2, HALF_U, 2 * HALF_STATE)),
            _const_spec((2, 2 * HALF_STATE, HALF_U)),
            _const_spec((1, D_SSM)),
            _const_spec((D_SSM, D_SSM)),
            _const_spec((1, D_SSM)),
        ],
        out_specs=[
            pl.BlockSpec((rows, D_MODEL), lambda i: (0, 0)),
            pl.BlockSpec((SAMPLE_BLOCKS * sb, D_KV, wbuf), lambda i: (i, 0, 0)),
            pl.BlockSpec((SAMPLE_BLOCKS * sb, D_KV, wbuf), lambda i: (i, 0, 0)),
            pl.BlockSpec((2 * HALF_STATE, nseq), lambda i: (0, 0)),
            pl.BlockSpec((2 * HALF_STATE, nseq), lambda i: (0, 0)),
        ],
        out_shape=[
            jax.ShapeDtypeStruct((rows, D_MODEL), F32),
            jax.ShapeDtypeStruct((nseq, D_KV, wbuf), F32),
            jax.ShapeDtypeStruct((nseq, D_KV, wbuf), F32),
            jax.ShapeDtypeStruct((2 * HALF_STATE, nseq), F32),
            jax.ShapeDtypeStruct((2 * HALF_STATE, nseq), F32),
        ],
        scratch_shapes=[
            pltpu.VMEM((rows, D_IN), F32),
            pltpu.VMEM((rows, D_ATTN), F32),
            pltpu.VMEM((rows, D_SSM), F32),
            pltpu.VMEM((N_KV_HEADS, N_REP * brow, sb * wbuf), F32),
            pltpu.VMEM((N_KV_HEADS, N_REP * brow, LANES), F32),
            pltpu.VMEM((U_SLABS, rows, LANES), F32),
        ],
        compiler_params=pltpu.CompilerParams(
            dimension_semantics=("arbitrary",), vmem_limit_bytes=VMEM_LIMIT),
        name="mix_sample",
    )(relb, sinks, x, g, win, wout, ck, cv, sre0, sim0, lam, bd, cd, dskip, wglu, bglu)


def _s5_operators(log_dt, a_re, a_im, b_re, b_im, c_re, c_im):
    dt = jnp.exp(log_dt)[:, None]
    mag = jnp.exp(a_re * dt)
    lb_re = mag * jnp.cos(a_im * dt)
    lb_im = mag * jnp.sin(a_im * dt)
    den = a_re * a_re + a_im * a_im
    nr = lb_re - 1.0
    q_re = (nr * a_re + lb_im * a_im) / den
    q_im = (lb_im * a_re - nr * a_im) / den
    bb_re = q_re[..., None] * b_re - q_im[..., None] * b_im
    bb_im = q_re[..., None] * b_im + q_im[..., None] * b_re
    eye = jnp.eye(HALF_GROUPS, dtype=BF16)
    lam = jnp.stack([lb_re.reshape(2, HALF_STATE), lb_im.reshape(2, HALF_STATE)], axis=1)
    bb = jnp.stack([bb_re, bb_im]).astype(BF16).reshape(2, 2, HALF_GROUPS, SSM_STATE, SSM_GROUP)
    bd = jnp.einsum('rhgpc,gk->hgcrkp', bb, eye).reshape(2, HALF_U, 2 * HALF_STATE)
    cc = jnp.stack([c_re, -c_im]).astype(BF16).reshape(2, 2, HALF_GROUPS, SSM_GROUP, SSM_STATE)
    cd = jnp.einsum('rhgcp,gk->hrgpkc', cc, eye).reshape(2, 2 * HALF_STATE, HALF_U)
    return lam.reshape(4, HALF_STATE), bd, cd


def kernel(x_prompt, x_sample, cache_k, cache_v, state_ssm_re, state_ssm_im, rel_bias,
           ffn1_norm, ffn1_w_gate, ffn1_w_up, ffn1_w_down, mix_norm, w_in, sinks,
           log_dt, a_re, a_im, b_re, b_im, c_re, c_im, d_skip, w_glu, b_glu, w_out,
           ffn2_norm, ffn2_w_gate, ffn2_w_up, ffn2_w_down, final_norm):
    depth = w_in.shape[0]
    assert depth == 1
    batch, seq, _ = x_prompt.shape
    nseq, t_new, _ = x_sample.shape
    fg = final_norm.reshape(1, D_MODEL)

    l = 0
    ffn1 = (ffn1_norm[l].reshape(1, D_MODEL), ffn1_w_gate[l], ffn1_w_up[l], ffn1_w_down[l], fg)
    ffn2 = (ffn2_norm[l].reshape(1, D_MODEL), ffn2_w_gate[l], ffn2_w_up[l], ffn2_w_down[l], fg)
    lam, bd, cd = _s5_operators(log_dt[l], a_re[l], a_im[l], b_re[l], b_im[l], c_re[l], c_im[l])
    mix_w = (mix_norm[l].reshape(1, D_MODEL), w_in[l].astype(BF16), w_out[l].astype(BF16))
    ssm_w = (lam, bd, cd, d_skip[l].reshape(1, D_SSM), w_glu[l].astype(BF16),
             b_glu[l].reshape(1, D_SSM))
    sinks_l = sinks[l]

    def window_in(c):
        return jnp.transpose(c, (0, 2, 3, 1)).reshape(c.shape[0], D_KV, c.shape[1])

    def window_out(w):
        n, _, pos = w.shape
        return jnp.transpose(w.reshape(n, N_KV_HEADS, HEAD_DIM, pos), (0, 3, 1, 2))[None]

    def state_in(s):
        return jnp.transpose(s, (1, 2, 0)).reshape(2 * HALF_STATE, s.shape[0])

    def state_out(s):
        return jnp.transpose(s.reshape(N_SSM_GROUPS, SSM_STATE, s.shape[1]), (2, 0, 1))[None]

    relb_t = rel_bias.T
    xp = x_prompt.reshape(batch * seq, D_MODEL)
    ys, yp = _ffn(xp, x_sample, *ffn1, final_norm=False, ys_shape=(nseq * t_new, D_MODEL))
    yp, kp, vp, sre_p, sim_p = _mix_prompt(
        yp.reshape(batch, seq, D_MODEL), relb_t, sinks_l, *mix_w, *ssm_w)
    ys, ks, vs, sre_s, sim_s = _mix_sample(
        ys, relb_t, sinks_l, *mix_w,
        window_in(cache_k[l]), window_in(cache_v[l]),
        state_in(state_ssm_re[l]), state_in(state_ssm_im[l]),
        *ssm_w, nseq=nseq, t_new=t_new)
    y_sample, y_prompt = _ffn(yp.reshape(batch * seq, D_MODEL), ys, *ffn2, final_norm=True,
                              ys_shape=(nseq, t_new, D_MODEL))
    y_prompt = y_prompt.reshape(batch, seq, D_MODEL)

    st_p = (1, batch, N_SSM_GROUPS, SSM_STATE)
    return (y_prompt, y_sample,
            window_out(kp), window_out(vp), sre_p.reshape(st_p), sim_p.reshape(st_p),
            window_out(ks), window_out(vs), state_out(sre_s), state_out(sim_s))
```
